```python
import math
import jax, jax.numpy as jnp
from jax import lax
import numpy as np

D_MODEL = 1024
BATCH = 16
SEQ = 256
DEPTH = 2
DEC_BATCH = 4
DEC_SEQ = 1024
PAST_LEN = 256

GRID_W = 64
D_FF = 2816
N_MODS = 9
CONV_DIM = 512
CONV_WIDTH = 31
RET_HEADS = 4
RET_DK = 128
RET_DV = 256
RET_CHUNK = 128
MLA_HEADS = 8
MLA_Q_LORA = 512
MLA_KV_LORA = 256
MLA_D_NOPE = 64
MLA_D_ROPE = 32
MLA_D_V = 64
ROPE_AXIS_HALF = MLA_D_ROPE // 4
ROPE_BASE = 10000.0
Q_BLOCK = 128
N_BRANCHES = 3
DEEPNORM_ALPHA = (2 * DEPTH) ** 0.25
DEEPNORM_BETA = (8 * DEPTH) ** -0.25
LN_EPS = 1e-5
RMS_EPS = 1e-6
MIX_WIDTHS = (CONV_DIM, CONV_DIM, RET_HEADS * RET_DK, RET_HEADS * RET_DK, RET_HEADS * RET_DV,
              RET_HEADS * RET_DV, MLA_Q_LORA, MLA_KV_LORA, MLA_D_ROPE, N_BRANCHES * D_MODEL)
MIX_IN = sum(MIX_WIDTHS)
MIX_SPLITS = tuple(int(s) for s in np.cumsum(MIX_WIDTHS)[:-1])

kernel_name = 'hybrid_flow_prefix_trunk_step'


def layer_norm(x, g, b):
    xf = x.astype(jnp.float32)
    mu = jnp.mean(xf, -1, keepdims=True)
    var = jnp.mean(jnp.square(xf - mu), -1, keepdims=True)
    return ((xf - mu) * lax.rsqrt(var + LN_EPS)).astype(x.dtype) * g + b


def head_norm(x):
    xf = x.astype(jnp.float32)
    mu = jnp.mean(xf, -1, keepdims=True)
    var = jnp.mean(jnp.square(xf - mu), -1, keepdims=True)
    return ((xf - mu) * lax.rsqrt(var + LN_EPS)).astype(x.dtype)


def rms_norm(x, g):
    xf = x.astype(jnp.float32)
    return (xf * lax.rsqrt(jnp.mean(jnp.square(xf), -1, keepdims=True) + RMS_EPS)).astype(x.dtype) * g


def modulate(x, shift, scale):
    return x * (1 + scale) + shift


def swiglu(x, w_in, w_out):
    gate, up = jnp.split(x @ w_in, 2, axis=-1)
    return (jax.nn.silu(gate) * up) @ w_out


def axial_rope_tables(n_tokens, dtype):
    rows = n_tokens // GRID_W
    row_id = jnp.repeat(jnp.arange(rows, dtype=jnp.float32), GRID_W)
    col_id = jnp.tile(jnp.arange(GRID_W, dtype=jnp.float32), rows)
    inv_freq = ROPE_BASE ** (-jnp.arange(ROPE_AXIS_HALF, dtype=jnp.float32) / ROPE_AXIS_HALF)
    ang = jnp.stack([row_id[:, None] * inv_freq, col_id[:, None] * inv_freq], axis=1)
    return jnp.cos(ang).astype(dtype), jnp.sin(ang).astype(dtype)


def apply_axial_rope(x, cos, sin):
    xs = x.reshape(x.shape[:-1] + (2, 2, ROPE_AXIS_HALF))
    x1, x2 = xs[..., 0, :], xs[..., 1, :]
    out = jnp.stack([x1 * cos - x2 * sin, x1 * sin + x2 * cos], axis=-2)
    return out.reshape(x.shape)


def conformer_conv(a, g, w_dw, b_dw, ln_g, ln_b, w_out):
    h = a * jax.nn.sigmoid(g)
    h = lax.conv_general_dilated(h, w_dw[:, None, :], (1,), ((CONV_WIDTH // 2, CONV_WIDTH // 2),),
                                 dimension_numbers=('NWC', 'WIO', 'NWC'),
                                 feature_group_count=CONV_DIM) + b_dw
    h = jax.nn.silu(layer_norm(h, ln_g, ln_b))
    return h @ w_out


def retention_chunkwise(q, k, v, log_g, s0, strict):
    f32 = jnp.float32
    B, T, H, dk = q.shape
    dv = v.shape[-1]
    n = T // RET_CHUNK
    qc = q.astype(f32).reshape(B, n, RET_CHUNK, H, dk)
    kc = k.astype(f32).reshape(B, n, RET_CHUNK, H, dk)
    vc = v.astype(f32).reshape(B, n, RET_CHUNK, H, dv)
    idx = jnp.arange(RET_CHUNK, dtype=f32)
    diff = idx[:, None] - idx[None, :]
    keep = diff > 0 if strict else diff >= 0
    dmask = jnp.where(keep, jnp.exp(jnp.where(keep, diff, 0.0)[None] * log_g[:, None, None]), 0.0)
    scores = jnp.einsum('bnihd,bnjhd->bnhij', qc, kc) * dmask
    inner = jnp.einsum('bnhij,bnjhe->bnihe', scores, vc)
    zeta = jnp.exp((RET_CHUNK - 1.0 - idx)[None, :] * log_g[:, None])
    kv = jnp.einsum('bnjhd,hj,bnjhe->bnhde', kc, zeta, vc)
    chunk_decay = jnp.exp(RET_CHUNK * log_g)[None, :, None, None]

    def step(s, kv_i):
        return chunk_decay * s + kv_i, s

    s_final, s_prev = lax.scan(step, s0.astype(f32), jnp.moveaxis(kv, 1, 0))
    xi = jnp.exp((idx + 1.0)[:, None] * log_g[None, :])
    cross = jnp.einsum('bnihd,nbhde->bnihe', qc, s_prev) * xi[None, None, :, :, None]
    out = (inner + cross).reshape(B, T, H, dv).astype(q.dtype)
    return out, s_final.astype(q.dtype)


def mla_attend(q_nope, q_rope, k_nope, k_rope, v):
    B, Tq, H, _ = q_nope.shape
    nb = Tq // Q_BLOCK
    scale = (MLA_D_NOPE + MLA_D_ROPE) ** -0.5

    def block(qs):
        qn, qr = qs
        s = jnp.einsum('bqhd,bkhd->bhqk', qn, k_nope) + jnp.einsum('bqhd,bkd->bhqk', qr, k_rope)
        p = jax.nn.softmax(s.astype(jnp.float32) * scale, axis=-1).astype(v.dtype)
        return jnp.einsum('bhqk,bkhd->bqhd', p, v)

    def to_blocks(t):
        return jnp.swapaxes(t.reshape(B, nb, Q_BLOCK, H, t.shape[-1]), 0, 1)

    out = lax.map(block, (to_blocks(q_nope), to_blocks(q_rope)))
    return jnp.swapaxes(out, 0, 1).reshape(B, Tq, H, v.shape[-1])


def token_mixer(u, lp, ctx, rope):
    B, T, _ = u.shape
    (glu_a, glu_g, r_q, r_k, r_v, r_g, m_q, m_kv, m_kr, br_g) = jnp.split(u @ lp['mix_w_in'], MIX_SPLITS, axis=-1)
    conv_out = conformer_conv(glu_a, glu_g, lp['conv_w_dw'], lp['conv_b_dw'], lp['conv_ln_g'],
                              lp['conv_ln_b'], lp['conv_w_out'])
    q = r_q.reshape(B, T, RET_HEADS, RET_DK)
    k = r_k.reshape(B, T, RET_HEADS, RET_DK) * (RET_DK ** -0.5)
    v = r_v.reshape(B, T, RET_HEADS, RET_DV)
    log_gf = jax.nn.log_sigmoid(lp['ret_decay_fwd'].astype(jnp.float32))
    log_gb = jax.nn.log_sigmoid(lp['ret_decay_bwd'].astype(jnp.float32))
    if ctx is None:
        s0f = jnp.zeros((B, RET_HEADS, RET_DK, RET_DV), jnp.float32)
        s0b = jnp.zeros((B, RET_HEADS, RET_DK, RET_DV), jnp.float32)
    else:
        s0f, s0b = ctx[2], ctx[3]
    o_f, s_f = retention_chunkwise(q, k, v, log_gf, s0f, False)
    o_b, s_b = retention_chunkwise(q[:, ::-1], k[:, ::-1], v[:, ::-1], log_gb, s0b, True)
    o = head_norm(o_f + o_b[:, ::-1]).reshape(B, T, RET_HEADS * RET_DV)
    ret_out = (jax.nn.silu(r_g) * o) @ lp['ret_w_out']
    q_m = (rms_norm(m_q, lp['mla_q_norm']) @ lp['mla_w_uq']).reshape(B, T, MLA_HEADS, MLA_D_NOPE + MLA_D_ROPE)
    q_nope, q_rope = q_m[..., :MLA_D_NOPE], q_m[..., MLA_D_NOPE:]
    c_kv = rms_norm(m_kv, lp['mla_kv_norm'])
    k_rope = m_kr
    if rope is not None:
        cos, sin = rope
        q_rope = apply_axial_rope(q_rope, cos[:, None], sin[:, None])
        k_rope = apply_axial_rope(k_rope, cos, sin)
    if ctx is None:
        ckv_all, kr_all = c_kv, k_rope
    else:
        ckv_all = jnp.concatenate([ctx[0], c_kv], axis=1)
        kr_all = jnp.concatenate([ctx[1], k_rope], axis=1)
    kv = (ckv_all @ lp['mla_w_ukv']).reshape(B, ckv_all.shape[1], MLA_HEADS, MLA_D_NOPE + MLA_D_V)
    attn = mla_attend(q_nope, q_rope, kv[..., :MLA_D_NOPE], kr_all, kv[..., MLA_D_NOPE:])
    mla_out = attn.reshape(B, T, MLA_HEADS * MLA_D_V) @ lp['mla_w_out']
    g = jax.nn.sigmoid(br_g).reshape(B, T, N_BRANCHES, D_MODEL)
    merged = g[..., 0, :] * conv_out + g[..., 1, :] * ret_out + g[..., 2, :] * mla_out
    out = merged @ lp['mix_w_o']
    new_ctx = (c_kv, k_rope, s_f, s_b) if ctx is None else None
    return out, new_ctx


def trunk_layer(x, cvec, lp, ctx, rope):
    mod = (jax.nn.silu(cvec) @ lp['ada_w'] + lp['ada_b']).reshape(cvec.shape[0], 1, N_MODS, D_MODEL)
    h = modulate(x, mod[:, :, 0], mod[:, :, 1])
    x = layer_norm(DEEPNORM_ALPHA * x + 0.5 * mod[:, :, 2] * swiglu(h, lp['ffn1_w_in'], lp['ffn1_w_out']),
                   lp['post_ln_g'][0], lp['post_ln_b'][0])
    m, new_ctx = token_mixer(modulate(x, mod[:, :, 3], mod[:, :, 4]), lp, ctx, rope)
    x = layer_norm(DEEPNORM_ALPHA * x + mod[:, :, 5] * m, lp['post_ln_g'][1], lp['post_ln_b'][1])
    h = modulate(x, mod[:, :, 6], mod[:, :, 7])
    x = layer_norm(DEEPNORM_ALPHA * x + 0.5 * mod[:, :, 8] * swiglu(h, lp['ffn2_w_in'], lp['ffn2_w_out']),
                   lp['post_ln_g'][2], lp['post_ln_b'][2])
    return x, new_ctx


def setup_inputs(seed: int = 0) -> dict:
    key = jax.random.key(seed)
    ks = iter(jax.random.split(key, 48))
    f32 = jnp.float32

    def nrm(shape, scale):
        return scale * jax.random.normal(next(ks), shape, f32)

    gammas = 1.0 - 2.0 ** (-5.0 - np.arange(RET_HEADS))
    decay_logit = jnp.asarray(np.log(gammas / (1.0 - gammas)), f32)
    gate_offset = jnp.tile(jnp.concatenate([jnp.zeros((2 * D_MODEL,), f32), jnp.ones((D_MODEL,), f32)]), 3)
    beta = DEEPNORM_BETA
    return {
        'x_prompt': nrm((BATCH, SEQ, D_MODEL), 1.0),
        'x_sample': nrm((DEC_BATCH, DEC_SEQ, D_MODEL), 1.0),
        'cache_mla_ckv': nrm((DEC_BATCH, DEPTH, PAST_LEN, MLA_KV_LORA), 1.0),
        'cache_mla_krope': nrm((DEC_BATCH, DEPTH, PAST_LEN, MLA_D_ROPE), 1.0),
        'state_ret_fwd': nrm((DEC_BATCH, DEPTH, RET_HEADS, RET_DK, RET_DV), 0.5),
        'state_ret_bwd': nrm((DEC_BATCH, DEPTH, RET_HEADS, RET_DK, RET_DV), 0.5),
        'c': nrm((DEC_BATCH, D_MODEL), 1.0),
        'c_ctx': nrm((D_MODEL,), 1.0),
        'ada_w': nrm((DEPTH, D_MODEL, N_MODS * D_MODEL), 0.5 * D_MODEL ** -0.5),
        'ada_b': gate_offset + nrm((DEPTH, N_MODS * D_MODEL), 0.02),
        'ffn1_w_in': nrm((DEPTH, D_MODEL, 2 * D_FF), D_MODEL ** -0.5),
        'ffn1_w_out': nrm((DEPTH, D_FF, D_MODEL), beta * D_FF ** -0.5),
        'ffn2_w_in': nrm((DEPTH, D_MODEL, 2 * D_FF), D_MODEL ** -0.5),
        'ffn2_w_out': nrm((DEPTH, D_FF, D_MODEL), beta * D_FF ** -0.5),
        'post_ln_g': 1.0 + nrm((DEPTH, 3, D_MODEL), 0.02),
        'post_ln_b': nrm((DEPTH, 3, D_MODEL), 0.02),
        'mix_w_in': nrm((DEPTH, D_MODEL, MIX_IN), D_MODEL ** -0.5),
        'conv_w_dw': nrm((DEPTH, CONV_WIDTH, CONV_DIM), CONV_WIDTH ** -0.5),
        'conv_b_dw': nrm((DEPTH, CONV_DIM), 0.02),
        'conv_ln_g': 1.0 + nrm((DEPTH, CONV_DIM), 0.02),
        'conv_ln_b': nrm((DEPTH, CONV_DIM), 0.02),
        'conv_w_out': nrm((DEPTH, CONV_DIM, D_MODEL), beta * CONV_DIM ** -0.5),
        'ret_decay_fwd': decay_logit[None] + nrm((DEPTH, RET_HEADS), 0.1),
        'ret_decay_bwd': decay_logit[None] + nrm((DEPTH, RET_HEADS), 0.1),
        'ret_w_out': nrm((DEPTH, RET_HEADS * RET_DV, D_MODEL), beta * (RET_HEADS * RET_DV) ** -0.5),
        'mla_q_norm': 1.0 + nrm((DEPTH, MLA_Q_LORA), 0.02),
        'mla_w_uq': nrm((DEPTH, MLA_Q_LORA, MLA_HEADS * (MLA_D_NOPE + MLA_D_ROPE)), MLA_Q_LORA ** -0.5),
        'mla_kv_norm': 1.0 + nrm((DEPTH, MLA_KV_LORA), 0.02),
        'mla_w_ukv': nrm((DEPTH, MLA_KV_LORA, MLA_HEADS * (MLA_D_NOPE + MLA_D_V)), MLA_KV_LORA ** -0.5),
        'mla_w_out': nrm((DEPTH, MLA_HEADS * MLA_D_V, D_MODEL), beta * (MLA_HEADS * MLA_D_V) ** -0.5),
        'mix_w_o': nrm((DEPTH, D_MODEL, D_MODEL), beta * D_MODEL ** -0.5),
    }


def reference(x_prompt, x_sample, cache_mla_ckv, cache_mla_krope, state_ret_fwd, state_ret_bwd, c, c_ctx,
              ada_w, ada_b, ffn1_w_in, ffn1_w_out, ffn2_w_in, ffn2_w_out, post_ln_g, post_ln_b,
              mix_w_in, conv_w_dw, conv_b_dw, conv_ln_g, conv_ln_b, conv_w_out,
              ret_decay_fwd, ret_decay_bwd, ret_w_out,
              mla_q_norm, mla_w_uq, mla_kv_norm, mla_w_ukv, mla_w_out, mix_w_o):
    rope = axial_rope_tables(x_sample.shape[1], x_sample.dtype)
    h_ctx = x_prompt
    h_lat = x_sample
    c_ctx_row = c_ctx[None, :]
    ckv_l, kr_l, sf_l, sb_l = [], [], [], []
    for l in range(DEPTH):
        lp = {
            'ada_w': ada_w[l], 'ada_b': ada_b[l],
            'ffn1_w_in': ffn1_w_in[l], 'ffn1_w_out': ffn1_w_out[l],
            'ffn2_w_in': ffn2_w_in[l], 'ffn2_w_out': ffn2_w_out[l],
            'post_ln_g': post_ln_g[l], 'post_ln_b': post_ln_b[l],
            'mix_w_in': mix_w_in[l], 'conv_w_dw': conv_w_dw[l], 'conv_b_dw': conv_b_dw[l],
            'conv_ln_g': conv_ln_g[l], 'conv_ln_b': conv_ln_b[l], 'conv_w_out': conv_w_out[l],
            'ret_decay_fwd': ret_decay_fwd[l], 'ret_decay_bwd': ret_decay_bwd[l], 'ret_w_out': ret_w_out[l],
            'mla_q_norm': mla_q_norm[l], 'mla_w_uq': mla_w_uq[l], 'mla_kv_norm': mla_kv_norm[l],
            'mla_w_ukv': mla_w_ukv[l], 'mla_w_out': mla_w_out[l], 'mix_w_o': mix_w_o[l],
        }
        h_ctx, (ckv, kr, sf, sb) = trunk_layer(h_ctx, c_ctx_row, lp, None, None)
        ckv_l.append(ckv)
        kr_l.append(kr)
        sf_l.append(sf)
        sb_l.append(sb)
        ctx = (cache_mla_ckv[:, l], cache_mla_krope[:, l], state_ret_fwd[:, l], state_ret_bwd[:, l])
        h_lat, _ = trunk_layer(h_lat, c, lp, ctx, rope)
    return (h_ctx, h_lat, jnp.stack(ckv_l, axis=1), jnp.stack(kr_l, axis=1),
            jnp.stack(sf_l, axis=1), jnp.stack(sb_l, axis=1))
```

```python
import functools

import jax
import jax.numpy as jnp
import numpy as np
from jax import lax
from jax.experimental import pallas as pl
from jax.experimental.pallas import tpu as pltpu

F32 = jnp.float32
BF16 = jnp.bfloat16

D_MODEL = 1024
BATCH = 16
SEQ = 256
DEPTH = 2
DEC_BATCH = 4
DEC_SEQ = 1024
PAST_LEN = 256
GRID_W = 64
D_FF = 2816
N_MODS = 9
CONV_DIM = 512
CONV_WIDTH = 31
RET_HEADS = 4
RET_DK = 128
RET_DV = 256
MLA_HEADS = 8
MLA_Q_LORA = 512
MLA_KV_LORA = 256
MLA_D_NOPE = 64
MLA_D_ROPE = 32
MLA_D_V = 64
ROPE_AXIS_HALF = MLA_D_ROPE // 4
ROPE_BASE = 10000.0
DEEPNORM_ALPHA = (2 * DEPTH) ** 0.25
LN_EPS = 1e-5
RMS_EPS = 1e-6

N_CTX = BATCH * SEQ
N_LAT = DEC_BATCH * DEC_SEQ
N_TOK = N_CTX + N_LAT
N_MOD_ROWS = 8
HEAD_PAD = 128
ROPE_LANE0 = MLA_D_NOPE
MLA_W = MLA_HEADS * HEAD_PAD
MAIN_W = 2 * CONV_DIM + 2 * RET_HEADS * RET_DK + 2 * RET_HEADS * RET_DV + MLA_Q_LORA + MLA_KV_LORA
GATE_W = 3 * D_MODEL
CONV_HALO = 16
CONV_BLOCK = 256
VMEM_LIMIT = 56 * 1024 * 1024


def _dot(a, b):
    return jnp.dot(a, b, preferred_element_type=F32)


def _dot_nt(a, b):
    return lax.dot_general(a, b, (((1,), (1,)), ((), ())), preferred_element_type=F32)


def _dot_tn(a, b):
    return lax.dot_general(a, b, (((0,), (0,)), ((), ())), preferred_element_type=F32)


def _sigmoid(x):
    return 1.0 / (1.0 + jnp.exp(-x))


def _norm_rows(z):
    mu = jnp.mean(z, axis=-1, keepdims=True)
    zc = z - mu
    var = jnp.mean(zc * zc, axis=-1, keepdims=True)
    return zc * lax.rsqrt(var + LN_EPS)


def _rms_rows(z):
    return z * lax.rsqrt(jnp.mean(z * z, axis=-1, keepdims=True) + RMS_EPS)


def _resident(shape):
    zeros = (0,) * len(shape)
    return pl.BlockSpec(shape, lambda *_: zeros, pipeline_mode=pl.Buffered(1))


def _mod_spec(tm):
    n_ctx_tiles = N_CTX // tm
    tiles_per_seq = DEC_SEQ // tm

    def index(i):
        return (jnp.where(i < n_ctx_tiles, 0, 1 + (i - n_ctx_tiles) // tiles_per_seq), 0, 0)

    return pl.BlockSpec((None, N_MODS, D_MODEL), index)


def _params(semantics):
    return pltpu.CompilerParams(dimension_semantics=semantics, vmem_limit_bytes=VMEM_LIMIT)


def _ada_kernel(c_ref, w_ref, b_ref, o_ref):
    c = c_ref[...]
    h = (c * _sigmoid(c)).astype(BF16)
    o_ref[...] = _dot(h, w_ref[...].astype(BF16)) + b_ref[...]


def _ada_mods(cvec, ada_w, ada_b):
    tn = D_MODEL
    n_out = N_MODS * D_MODEL
    return pl.pallas_call(
        _ada_kernel,
        grid=(DEPTH, n_out // tn),
        in_specs=[
            pl.BlockSpec((N_MOD_ROWS, D_MODEL), lambda l, j: (0, 0)),
            pl.BlockSpec((None, D_MODEL, tn), lambda l, j: (l, 0, j)),
            pl.BlockSpec((None, 1, tn), lambda l, j: (l, 0, j)),
        ],
        out_specs=pl.BlockSpec((None, N_MOD_ROWS, tn), lambda l, j: (l, 0, j)),
        out_shape=jax.ShapeDtypeStruct((DEPTH, N_MOD_ROWS, n_out), F32),
        compiler_params=_params(("arbitrary", "arbitrary")),
        name="ada_mods",
    )(cvec, ada_w, ada_b.reshape(DEPTH, 1, n_out))


def _ffn_kernel(x_ref, mod_ref, win_ref, wout_ref, g_ref, b_ref, o_ref, *, base, n_chunks):
    x = x_ref[...]
    shift = mod_ref[base:base + 1, :]
    scale = mod_ref[base + 1:base + 2, :]
    gate = mod_ref[base + 2:base + 3, :]
    h = (x * (1.0 + scale) + shift).astype(BF16)
    tf = D_FF // n_chunks
    y = None
    for c in range(n_chunks):
        g = _dot(h, win_ref[:, c * tf:(c + 1) * tf])
        u = _dot(h, win_ref[:, D_FF + c * tf:D_FF + (c + 1) * tf])
        a = (g * _sigmoid(g) * u).astype(BF16)
        yc = _dot(a, wout_ref[c * tf:(c + 1) * tf, :])
        y = yc if y is None else y + yc
    z = DEEPNORM_ALPHA * x + 0.5 * gate * y
    o_ref[...] = _norm_rows(z) * g_ref[...] + b_ref[...]


def _ffn(x, mods, w_in, w_out, ln_g, ln_b, *, base, tm=512, n_chunks=2):
    row = pl.BlockSpec((tm, D_MODEL), lambda i: (i, 0))
    return pl.pallas_call(
        functools.partial(_ffn_kernel, base=base, n_chunks=n_chunks),
        grid=(N_TOK // tm,),
        in_specs=[row, _mod_spec(tm), _resident((D_MODEL, 2 * D_FF)), _resident((D_FF, D_MODEL)),
                  _resident((1, D_MODEL)), _resident((1, D_MODEL))],
        out_specs=row,
        out_shape=jax.ShapeDtypeStruct((N_TOK, D_MODEL), F32),
        compiler_params=_params(("arbitrary",)),
        name="ffn",
    )(x, mods, w_in, w_out, ln_g, ln_b)


def _proj_kernel(x_ref, mod_ref, cos_ref, sin_ref, wm_ref, wkr_ref, wg_ref, wuq_ref, wuqs_ref, wk_ref,
                 wv_ref, gq_ref, gkv_ref,
                 glu_ref, rq_ref, rk_ref, rv_ref, rg_ref, q_ref, ckv_ref, kr_ref, kcat_ref, vpad_ref,
                 sig_ref):
    x = x_ref[...]
    u = (x * (1.0 + mod_ref[4:5, :]) + mod_ref[3:4, :]).astype(BF16)
    o = 0
    a = _dot(u, wm_ref[:, o:o + CONV_DIM]); o += CONV_DIM
    g = _dot(u, wm_ref[:, o:o + CONV_DIM]); o += CONV_DIM
    glu_ref[...] = a * _sigmoid(g)
    w = RET_HEADS * RET_DK
    rq_ref[...] = _dot(u, wm_ref[:, o:o + w]).astype(BF16); o += w
    rk_ref[...] = (_dot(u, wm_ref[:, o:o + w]) * (RET_DK ** -0.5)).astype(BF16); o += w
    w = RET_HEADS * RET_DV
    rv_ref[...] = _dot(u, wm_ref[:, o:o + w]).astype(BF16); o += w
    rg = _dot(u, wm_ref[:, o:o + w]); o += w
    rg_ref[...] = rg * _sigmoid(rg)
    mq = _dot(u, wm_ref[:, o:o + MLA_Q_LORA]); o += MLA_Q_LORA
    mkv = _dot(u, wm_ref[:, o:o + MLA_KV_LORA]); o += MLA_KV_LORA
    sig_ref[...] = _sigmoid(_dot(u, wg_ref[...]))

    cos = cos_ref[...]
    sin = sin_ref[...]
    qn = (_rms_rows(mq) * gq_ref[...]).astype(BF16)
    qm = _dot(qn, wuq_ref[...])
    qs = _dot(qn, wuqs_ref[...])
    ckv = _rms_rows(mkv) * gkv_ref[...]
    ckv_ref[...] = ckv
    ckvb = ckv.astype(BF16)
    kn = _dot(ckvb, wk_ref[...])
    vpad_ref[...] = _dot(ckvb, wv_ref[...]).astype(BF16)
    kr2 = _dot(u, wkr_ref[...])
    kr = kr2[:, :HEAD_PAD]
    kr_ref[...] = kr
    kr_rot = kr * cos + kr2[:, HEAD_PAD:] * sin
    for h in range(MLA_HEADS):
        sl = slice(h * HEAD_PAD, (h + 1) * HEAD_PAD)
        q_ref[:, sl] = (qm[:, sl] * cos + qs[:, sl] * sin).astype(BF16)
        kcat_ref[:, sl] = (kn[:, sl] + kr_rot).astype(BF16)


def _proj(x, mods, cos_t, sin_t, wm, wkr, wg, wuq, wuqs, wk, wv, gq, gkv, *, tm=256):
    n_ctx_tiles = N_CTX // tm
    tiles_per_seq = DEC_SEQ // tm

    def rope_index(i):
        return (jnp.where(i < n_ctx_tiles, 0, tiles_per_seq + (i - n_ctx_tiles) % tiles_per_seq), 0)

    def row(w):
        return pl.BlockSpec((tm, w), lambda i: (i, 0))

    def out(w, dt):
        return jax.ShapeDtypeStruct((N_TOK, w), dt)

    rope = pl.BlockSpec((tm, HEAD_PAD), rope_index)
    return pl.pallas_call(
        _proj_kernel,
        grid=(N_TOK // tm,),
        in_specs=[row(D_MODEL), _mod_spec(tm), rope, rope,
                  _resident(wm.shape), _resident(wkr.shape), _resident(wg.shape), _resident(wuq.shape),
                  _resident(wuqs.shape), _resident(wk.shape), _resident(wv.shape), _resident(gq.shape),
                  _resident(gkv.shape)],
        out_specs=[row(CONV_DIM), row(RET_HEADS * RET_DK), row(RET_HEADS * RET_DK), row(RET_HEADS * RET_DV),
                   row(RET_HEADS * RET_DV), row(MLA_W), row(MLA_KV_LORA), row(HEAD_PAD), row(MLA_W),
                   row(MLA_W), row(GATE_W)],
        out_shape=[out(CONV_DIM, F32), out(RET_HEADS * RET_DK, BF16), out(RET_HEADS * RET_DK, BF16),
                   out(RET_HEADS * RET_DV, BF16), out(RET_HEADS * RET_DV, F32), out(MLA_W, BF16),
                   out(MLA_KV_LORA, F32), out(HEAD_PAD, F32), out(MLA_W, BF16), out(MLA_W, BF16),
                   out(GATE_W, F32)],
        compiler_params=_params(("arbitrary",)),
        name="mix_proj",
    )(x, mods, cos_t, sin_t, wm, wkr, wg, wuq, wuqs, wk, wv, gq, gkv)


def _cache_kv_kernel(ckv_ref, kr_ref, wk_ref, wv_ref, kcat_ref, vpad_ref):
    ckvb = ckv_ref[...].astype(BF16)
    kn = _dot(ckvb, wk_ref[...])
    vpad_ref[...] = _dot(ckvb, wv_ref[...]).astype(BF16)
    kr = kr_ref[...]
    for h in range(MLA_HEADS):
        sl = slice(h * HEAD_PAD, (h + 1) * HEAD_PAD)
        kcat_ref[:, sl] = (kn[:, sl] + kr).astype(BF16)


def _cache_kv(cache_ckv, cache_kr_pad, wk, wv):
    out = jax.ShapeDtypeStruct((DEPTH, DEC_BATCH * PAST_LEN, MLA_W), BF16)
    w_spec = pl.BlockSpec((None, MLA_KV_LORA, MLA_W), lambda l, b: (l, 0, 0))
    o_spec = pl.BlockSpec((None, PAST_LEN, MLA_W), lambda l, b: (l, b, 0))
    return pl.pallas_call(
        _cache_kv_kernel,
        grid=(DEPTH, DEC_BATCH),
        in_specs=[pl.BlockSpec((None, None, PAST_LEN, MLA_KV_LORA), lambda l, b: (b, l, 0, 0)),
                  pl.BlockSpec((None, None, PAST_LEN, HEAD_PAD), lambda l, b: (b, l, 0, 0)),
                  w_spec, w_spec],
        out_specs=[o_spec, o_spec],
        out_shape=[out, out],
        compiler_params=_params(("arbitrary", "arbitrary")),
        name="cache_kv",
    )(cache_ckv, cache_kr_pad, wk, wv)


def _conv_kernel(prev_ref, cur_ref, next_ref, w_ref, b_ref, g_ref, beta_ref, o_ref, pad_ref, acc_ref):
    i = pl.program_id(0)
    n_ctx_blocks = N_CTX // CONV_BLOCK
    blocks_per_seq = DEC_SEQ // CONV_BLOCK
    pos = (i - n_ctx_blocks) % blocks_per_seq
    latent = i >= n_ctx_blocks
    has_prev = jnp.logical_and(latent, pos != 0)
    has_next = jnp.logical_and(latent, pos != blocks_per_seq - 1)
    zero = jnp.zeros((CONV_HALO, CONV_DIM), F32)
    pad_ref[0:CONV_HALO, :] = jnp.where(has_prev, prev_ref[...], zero)
    pad_ref[CONV_HALO:CONV_HALO + CONV_BLOCK, :] = cur_ref[...]
    pad_ref[CONV_HALO + CONV_BLOCK:, :] = jnp.where(has_next, next_ref[...], zero)
    rows = 128
    lanes = 128
    first = CONV_HALO - CONV_WIDTH // 2
    for c in range(CONV_DIM // lanes):
        cs = slice(c * lanes, (c + 1) * lanes)
        for r in range(CONV_BLOCK // rows):
            acc = jnp.broadcast_to(b_ref[:, cs], (rows, lanes))
            for j in range(CONV_WIDTH):
                start = r * rows + first + j
                acc = acc + w_ref[j:j + 1, cs] * pad_ref[start:start + rows, cs]
            acc_ref[r * rows:(r + 1) * rows, cs] = acc
    y = _norm_rows(acc_ref[...]) * g_ref[...] + beta_ref[...]
    o_ref[...] = (y * _sigmoid(y)).astype(BF16)


def _conv_branch(glu, w_dw, b_dw, ln_g, ln_b):
    per = CONV_BLOCK // CONV_HALO
    n_halo = N_TOK // CONV_HALO
    return pl.pallas_call(
        _conv_kernel,
        grid=(N_TOK // CONV_BLOCK,),
        in_specs=[pl.BlockSpec((CONV_HALO, CONV_DIM), lambda i: (jnp.maximum(i * per - 1, 0), 0)),
                  pl.BlockSpec((CONV_BLOCK, CONV_DIM), lambda i: (i, 0)),
                  pl.BlockSpec((CONV_HALO, CONV_DIM), lambda i: (jnp.minimum((i + 1) * per, n_halo - 1), 0)),
                  _resident((CONV_WIDTH, CONV_DIM)), _resident((1, CONV_DIM)), _resident((1, CONV_DIM)),
                  _resident((1, CONV_DIM))],
        out_specs=pl.BlockSpec((CONV_BLOCK, CONV_DIM), lambda i: (i, 0)),
        out_shape=jax.ShapeDtypeStruct((N_TOK, CONV_DIM), BF16),
        scratch_shapes=[pltpu.VMEM((CONV_BLOCK + 2 * CONV_HALO, CONV_DIM), F32),
                        pltpu.VMEM((CONV_BLOCK, CONV_DIM), F32)],
        compiler_params=_params(("arbitrary",)),
        name="conv_branch",
    )(glu, glu, glu, w_dw, b_dw, ln_g, ln_b)


def _ret_kernel(*refs, t, hp, latent):
    if latent:
        lg_ref, q_ref, k_ref, v_ref, g_ref, s0f_ref, s0b_ref, o_ref, d_ref = refs
    else:
        lg_ref, q_ref, k_ref, v_ref, g_ref, o_ref, sf_ref, sb_ref, d_ref = refs
    hblk = pl.program_id(0)

    @pl.when(pl.program_id(1) == 0)
    def _():
        diff = (lax.broadcasted_iota(jnp.int32, (t, t), 0) - lax.broadcasted_iota(jnp.int32, (t, t), 1)).astype(F32)
        for hh in range(hp):
            lgf = lg_ref[0, hblk * hp + hh]
            lgb = lg_ref[1, hblk * hp + hh]
            d_ref[hh] = jnp.exp(jnp.where(diff >= 0, diff * lgf, -diff * lgb))

    pos = lax.broadcasted_iota(jnp.int32, (t, 1), 0).astype(F32)
    for hh in range(hp):
        lgf = lg_ref[0, hblk * hp + hh]
        lgb = lg_ref[1, hblk * hp + hh]
        q = q_ref[:, hh * RET_DK:(hh + 1) * RET_DK]
        k = k_ref[:, hh * RET_DK:(hh + 1) * RET_DK]
        v = v_ref[:, hh * RET_DV:(hh + 1) * RET_DV]
        p = (_dot_nt(q, k) * d_ref[hh]).astype(BF16)
        o = _dot(p, v)
        if latent:
            o = o + jnp.exp((pos + 1.0) * lgf) * _dot(q, s0f_ref[hh].astype(BF16))
            o = o + jnp.exp((t - pos) * lgb) * _dot(q, s0b_ref[hh].astype(BF16))
        else:
            kf = k.astype(F32)
            sf_ref[hh] = _dot_tn((kf * jnp.exp((t - 1.0 - pos) * lgf)).astype(BF16), v)
            sb_ref[hh] = _dot_tn((kf * jnp.exp(pos * lgb)).astype(BF16), v)
        o_ref[:, hh * RET_DV:(hh + 1) * RET_DV] = (g_ref[:, hh * RET_DV:(hh + 1) * RET_DV] * _norm_rows(o)).astype(BF16)


def _retention(log_g, rq, rk, rv, rg, prev_out, s0f, s0b, *, latent, layer=0):
    t = DEC_SEQ if latent else SEQ
    hp = 1 if latent else RET_HEADS
    n_seq = DEC_BATCH if latent else BATCH
    row0 = (N_CTX // t) if latent else 0

    def row(w):
        return pl.BlockSpec((t, hp * w), lambda h, s: (row0 + s, h))

    smem = pl.BlockSpec(memory_space=pltpu.SMEM)
    out_shape = [jax.ShapeDtypeStruct((N_TOK, RET_HEADS * RET_DV), BF16)]
    out_specs = [row(RET_DV)]
    in_specs = [smem, row(RET_DK), row(RET_DK), row(RET_DV), row(RET_DV)]
    args = [log_g, rq, rk, rv, rg]
    aliases = {}
    if latent:
        st = pl.BlockSpec((None, None, hp, RET_DK, RET_DV), lambda h, s: (s, layer, h, 0, 0))
        in_specs += [st, st, pl.BlockSpec(memory_space=pl.ANY)]
        args += [s0f, s0b, prev_out]
        aliases = {7: 0}
    else:
        st = pl.BlockSpec((None, hp, RET_DK, RET_DV), lambda h, s: (s, h, 0, 0))
        out_specs += [st, st]
        out_shape += [jax.ShapeDtypeStruct((BATCH, RET_HEADS, RET_DK, RET_DV), F32)] * 2

    def body(*refs):
        if latent:
            refs = refs[:7] + refs[8:]
        _ret_kernel(*refs, t=t, hp=hp, latent=latent)

    return pl.pallas_call(
        body,
        grid=(RET_HEADS // hp, n_seq),
        in_specs=in_specs,
        out_specs=out_specs,
        out_shape=out_shape,
        scratch_shapes=[pltpu.VMEM((hp, t, t), F32)],
        input_output_aliases=aliases,
        compiler_params=_params(("arbitrary", "arbitrary")),
        name="retention_lat" if latent else "retention_ctx",
    )(*args)


def _attn_kernel(*refs, hp, latent):
    if latent:
        q_ref, k_ref, v_ref, kc_ref, vc_ref, o_ref = refs
    else:
        q_ref, k_ref, v_ref, o_ref = refs
    scale = (MLA_D_NOPE + MLA_D_ROPE) ** -0.5
    for hh in range(hp):
        sl = slice(hh * HEAD_PAD, (hh + 1) * HEAD_PAD)
        q = q_ref[:, sl]
        s = _dot_nt(q, k_ref[:, sl]) * scale
        m = jnp.max(s, axis=-1, keepdims=True)
        if latent:
            sc = _dot_nt(q, kc_ref[:, sl]) * scale
            m = jnp.maximum(m, jnp.max(sc, axis=-1, keepdims=True))
        e = jnp.exp(s - m)
        den = jnp.sum(e, axis=-1, keepdims=True)
        o = _dot(e.astype(BF16), v_ref[:, sl])
        if latent:
            ec = jnp.exp(sc - m)
            den = den + jnp.sum(ec, axis=-1, keepdims=True)
            o = o + _dot(ec.astype(BF16), vc_ref[:, sl])
        o_ref[:, sl] = (o / den).astype(BF16)


def _attention(q, kcat, vpad, prev_out, kcat_c, vpad_c, *, latent):
    t = DEC_SEQ if latent else SEQ
    hp = 1 if latent else MLA_HEADS
    n_seq = DEC_BATCH if latent else BATCH
    row0 = (N_CTX // t) if latent else 0
    row = pl.BlockSpec((t, hp * HEAD_PAD), lambda s, h: (row0 + s, h))
    in_specs = [row, row, row]
    args = [q, kcat, vpad]
    aliases = {}
    if latent:
        cache = pl.BlockSpec((PAST_LEN, hp * HEAD_PAD), lambda s, h: (s, h))
        in_specs += [cache, cache, pl.BlockSpec(memory_space=pl.ANY)]
        args += [kcat_c, vpad_c, prev_out]
        aliases = {5: 0}

    def body(*refs):
        if latent:
            refs = refs[:5] + refs[6:]
        _attn_kernel(*refs, hp=hp, latent=latent)

    return pl.pallas_call(
        body,
        grid=(n_seq, MLA_HEADS // hp),
        in_specs=in_specs,
        out_specs=row,
        out_shape=jax.ShapeDtypeStruct((N_TOK, MLA_W), BF16),
        input_output_aliases=aliases,
        compiler_params=_params(("arbitrary", "arbitrary")),
        name="attention_lat" if latent else "attention_ctx",
    )(*args)


def _merge_kernel(x_ref, mod_ref, a_ref, r_ref, m_ref, sig_ref, wc_ref, wr_ref, wm_ref, wo_ref, g_ref, b_ref,
                  o_ref):
    x = x_ref[...]
    merged = sig_ref[:, 0:D_MODEL] * _dot(a_ref[...], wc_ref[...])
    merged = merged + sig_ref[:, D_MODEL:2 * D_MODEL] * _dot(r_ref[...], wr_ref[...])
    merged = merged + sig_ref[:, 2 * D_MODEL:] * _dot(m_ref[...], wm_ref[...])
    y = _dot(merged.astype(BF16), wo_ref[...])
    z = DEEPNORM_ALPHA * x + mod_ref[5:6, :] * y
    o_ref[...] = _norm_rows(z) * g_ref[...] + b_ref[...]


def _merge(x, mods, a, r, m, sig, wc, wr, wm, wo, ln_g, ln_b, *, tm=512):
    def row(w):
        return pl.BlockSpec((tm, w), lambda i: (i, 0))

    return pl.pallas_call(
        _merge_kernel,
        grid=(N_TOK // tm,),
        in_specs=[row(D_MODEL), _mod_spec(tm), row(CONV_DIM), row(RET_HEADS * RET_DV), row(MLA_W), row(GATE_W),
                  _resident(wc.shape), _resident(wr.shape), _resident(wm.shape), _resident(wo.shape),
                  _resident((1, D_MODEL)), _resident((1, D_MODEL))],
        out_specs=row(D_MODEL),
        out_shape=jax.ShapeDtypeStruct((N_TOK, D_MODEL), F32),
        compiler_params=_params(("arbitrary",)),
        name="merge",
    )(x, mods, a, r, m, sig, wc, wr, wm, wo, ln_g, ln_b)


def _rope_tables():
    rows = DEC_SEQ // GRID_W
    row_id = jnp.repeat(jnp.arange(rows, dtype=F32), GRID_W)
    col_id = jnp.tile(jnp.arange(GRID_W, dtype=F32), rows)
    inv_freq = ROPE_BASE ** (-jnp.arange(ROPE_AXIS_HALF, dtype=F32) / ROPE_AXIS_HALF)
    ang = jnp.stack([row_id[:, None] * inv_freq, col_id[:, None] * inv_freq], axis=1)
    cos = jnp.cos(ang)
    sin = jnp.sin(ang)
    cos32 = jnp.stack([cos, cos], axis=2).reshape(DEC_SEQ, MLA_D_ROPE)
    sin32 = jnp.stack([-sin, sin], axis=2).reshape(DEC_SEQ, MLA_D_ROPE)
    tail = HEAD_PAD - ROPE_LANE0 - MLA_D_ROPE
    cos_t = jnp.concatenate([jnp.ones((DEC_SEQ, ROPE_LANE0), F32), cos32, jnp.ones((DEC_SEQ, tail), F32)], axis=1)
    sin_t = jnp.pad(sin32, ((0, 0), (ROPE_LANE0, tail)))
    cos_t = jnp.concatenate([jnp.ones((DEC_SEQ, HEAD_PAD), F32), cos_t], axis=0)
    sin_t = jnp.concatenate([jnp.zeros((DEC_SEQ, HEAD_PAD), F32), sin_t], axis=0)
    return cos_t, sin_t


_ROPE_SWAP = np.arange(MLA_D_ROPE) ^ ROPE_AXIS_HALF


def _head_pad_cols(w, width):
    k = w.shape[0]
    w = w.reshape(k, MLA_HEADS, width)
    return jnp.pad(w, ((0, 0), (0, 0), (0, HEAD_PAD - width))).reshape(k, MLA_W)


def kernel(x_prompt, x_sample, cache_mla_ckv, cache_mla_krope, state_ret_fwd, state_ret_bwd, c, c_ctx, ada_w, ada_b, ffn1_w_in, ffn1_w_out, ffn2_w_in, ffn2_w_out, post_ln_g, post_ln_b, mix_w_in, conv_w_dw, conv_b_dw, conv_ln_g, conv_ln_b, conv_w_out, ret_decay_fwd, ret_decay_bwd, ret_w_out, mla_q_norm, mla_w_uq, mla_kv_norm, mla_w_ukv, mla_w_out, mix_w_o):
    x = jnp.concatenate([x_prompt.reshape(N_CTX, D_MODEL), x_sample.reshape(N_LAT, D_MODEL)], axis=0)
    cvec = jnp.concatenate([c_ctx[None, :], c, jnp.zeros((N_MOD_ROWS - 1 - DEC_BATCH, D_MODEL), F32)], axis=0)
    mods_all = _ada_mods(cvec, ada_w, ada_b).reshape(DEPTH, N_MOD_ROWS, N_MODS, D_MODEL)
    cos_t, sin_t = _rope_tables()

    w_ukv = mla_w_ukv.reshape(DEPTH, MLA_KV_LORA, MLA_HEADS, MLA_D_NOPE + MLA_D_V)
    wk_all = _head_pad_cols(w_ukv[..., :MLA_D_NOPE].reshape(DEPTH * MLA_KV_LORA, -1), MLA_D_NOPE)
    wv_all = _head_pad_cols(w_ukv[..., MLA_D_NOPE:].reshape(DEPTH * MLA_KV_LORA, -1), MLA_D_V)
    wk_all = wk_all.reshape(DEPTH, MLA_KV_LORA, MLA_W).astype(BF16)
    wv_all = wv_all.reshape(DEPTH, MLA_KV_LORA, MLA_W).astype(BF16)
    kr_tail = HEAD_PAD - ROPE_LANE0 - MLA_D_ROPE
    cache_kr_pad = jnp.pad(cache_mla_krope, ((0, 0), (0, 0), (0, 0), (ROPE_LANE0, kr_tail)))
    kcat_c, vpad_c = _cache_kv(cache_mla_ckv, cache_kr_pad, wk_all, wv_all)
    log_g = jnp.stack([jax.nn.log_sigmoid(ret_decay_fwd), jax.nn.log_sigmoid(ret_decay_bwd)], axis=1)

    ckv_l, kr_l, sf_l, sb_l = [], [], [], []
    for l in range(DEPTH):
        mods = mods_all[l]
        ln_g = post_ln_g[l].reshape(3, 1, D_MODEL)
        ln_b = post_ln_b[l].reshape(3, 1, D_MODEL)
        x = _ffn(x, mods, ffn1_w_in[l].astype(BF16), ffn1_w_out[l].astype(BF16), ln_g[0], ln_b[0], base=0)

        w_in = mix_w_in[l]
        wm = w_in[:, :MAIN_W].astype(BF16)
        w_kr = w_in[:, MAIN_W:MAIN_W + MLA_D_ROPE]
        wkr = jnp.concatenate([jnp.pad(w_kr, ((0, 0), (ROPE_LANE0, kr_tail))),
                               jnp.pad(w_kr[:, _ROPE_SWAP], ((0, 0), (ROPE_LANE0, kr_tail)))], axis=1).astype(BF16)
        wg = w_in[:, MAIN_W + MLA_D_ROPE:].astype(BF16)
        w_uq = mla_w_uq[l].reshape(MLA_Q_LORA, MLA_HEADS, MLA_D_NOPE + MLA_D_ROPE)
        wuq = _head_pad_cols(w_uq.reshape(MLA_Q_LORA, -1), MLA_D_NOPE + MLA_D_ROPE).astype(BF16)
        w_uq_swap = jnp.concatenate([jnp.zeros_like(w_uq[..., :MLA_D_NOPE]), w_uq[..., MLA_D_NOPE:][..., _ROPE_SWAP]], axis=-1)
        wuqs = _head_pad_cols(w_uq_swap.reshape(MLA_Q_LORA, -1), MLA_D_NOPE + MLA_D_ROPE).astype(BF16)
        (glu, rq, rk, rv, rg, q, ckv, kr, kcat, vpad, sig) = _proj(
            x, mods, cos_t, sin_t, wm, wkr, wg, wuq, wuqs, wk_all[l], wv_all[l],
            mla_q_norm[l].reshape(1, -1), mla_kv_norm[l].reshape(1, -1))

        a = _conv_branch(glu, conv_w_dw[l], conv_b_dw[l].reshape(1, -1), conv_ln_g[l].reshape(1, -1),
                         conv_ln_b[l].reshape(1, -1))
        r, sf, sb = _retention(log_g[l], rq, rk, rv, rg, None, None, None, latent=False)
        r = _retention(log_g[l], rq, rk, rv, rg, r, state_ret_fwd, state_ret_bwd, latent=True, layer=l)[0]
        m = _attention(q, kcat, vpad, None, None, None, latent=False)
        m = _attention(q, kcat, vpad, m, kcat_c[l], vpad_c[l], latent=True)

        w_mo = jnp.pad(mla_w_out[l].reshape(MLA_HEADS, MLA_D_V, D_MODEL), ((0, 0), (0, HEAD_PAD - MLA_D_V), (0, 0)))
        x = _merge(x, mods, a, r, m, sig, conv_w_out[l].astype(BF16), ret_w_out[l].astype(BF16),
                   w_mo.reshape(MLA_W, D_MODEL).astype(BF16), mix_w_o[l].astype(BF16), ln_g[1], ln_b[1])
        x = _ffn(x, mods, ffn2_w_in[l].astype(BF16), ffn2_w_out[l].astype(BF16), ln_g[2], ln_b[2], base=6)

        ckv_l.append(ckv[:N_CTX].reshape(BATCH, SEQ, MLA_KV_LORA))
        kr_l.append(kr[:N_CTX, ROPE_LANE0:ROPE_LANE0 + MLA_D_ROPE].reshape(BATCH, SEQ, MLA_D_ROPE))
        sf_l.append(sf)
        sb_l.append(sb)

    return (x[:N_CTX].reshape(BATCH, SEQ, D_MODEL), x[N_CTX:].reshape(DEC_BATCH, DEC_SEQ, D_MODEL),
            jnp.stack(ckv_l, axis=1), jnp.stack(kr_l, axis=1), jnp.stack(sf_l, axis=1), jnp.stack(sb_l, axis=1))
```

```python
import functools

import jax
import jax.numpy as jnp
import numpy as np
from jax import lax
from jax.experimental import pallas as pl
from jax.experimental.pallas import tpu as pltpu

F32 = jnp.float32
BF16 = jnp.bfloat16

D_MODEL = 1024
BATCH = 16
SEQ = 256
DEPTH = 2
DEC_BATCH = 4
DEC_SEQ = 1024
PAST_LEN = 256
GRID_W = 64
D_FF = 2816
N_MODS = 9
CONV_DIM = 512
CONV_WIDTH = 31
RET_HEADS = 4
RET_DK = 128
RET_DV = 256
MLA_HEADS = 8
MLA_Q_LORA = 512
MLA_KV_LORA = 256
MLA_D_NOPE = 64
MLA_D_ROPE = 32
MLA_D_V = 64
ROPE_AXIS_HALF = MLA_D_ROPE // 4
ROPE_BASE = 10000.0
DEEPNORM_ALPHA = (2 * DEPTH) ** 0.25
LN_EPS = 1e-5
RMS_EPS = 1e-6

N_CTX = BATCH * SEQ
N_LAT = DEC_BATCH * DEC_SEQ
N_TOK = N_CTX + N_LAT
N_MOD_ROWS = 8
HEAD_PAD = 128
ROPE_LANE0 = MLA_D_NOPE
MLA_W = MLA_HEADS * HEAD_PAD
MAIN_W = 2 * CONV_DIM + 2 * RET_HEADS * RET_DK + 2 * RET_HEADS * RET_DV + MLA_Q_LORA + MLA_KV_LORA
GATE_W = 3 * D_MODEL
CONV_HALO = 16
CONV_BLOCK = 256
VMEM_LIMIT = 56 * 1024 * 1024
N_WSTEPS = 8
MXU_COLS = 256
FFN_CHUNKS = ((0, 6 * MXU_COLS), (6 * MXU_COLS, D_FF))


def _dot(a, b):
    return jnp.dot(a, b, preferred_element_type=F32)


def _dot_nt(a, b):
    return lax.dot_general(a, b, (((1,), (1,)), ((), ())), preferred_element_type=F32)


def _dot_tn(a, b):
    return lax.dot_general(a, b, (((0,), (0,)), ((), ())), preferred_element_type=F32)


def _sigmoid(x):
    return 1.0 / (1.0 + jnp.exp(-x))


def _norm_rows(z):
    mu = jnp.mean(z, axis=-1, keepdims=True)
    zc = z - mu
    var = jnp.mean(zc * zc, axis=-1, keepdims=True)
    return zc * lax.rsqrt(var + LN_EPS)


def _rms_rows(z):
    return z * lax.rsqrt(jnp.mean(z * z, axis=-1, keepdims=True) + RMS_EPS)


def _resident(shape):
    zeros = (0,) * len(shape)
    return pl.BlockSpec(shape, lambda *_: zeros, pipeline_mode=pl.Buffered(1))


def _tile(i):
    return jnp.maximum(i - N_WSTEPS, 0)


def _row_spec(tm, w):
    return pl.BlockSpec((tm, w), lambda i: (_tile(i), 0))


def _mod_spec(tm, layer):
    n_ctx_tiles = N_CTX // tm
    tiles_per_seq = DEC_SEQ // tm

    def index(i):
        j = _tile(i)
        return (layer, jnp.where(j < n_ctx_tiles, 0, 1 + (j - n_ctx_tiles) // tiles_per_seq), 0, 0)

    return pl.BlockSpec((None, None, N_MODS, D_MODEL), index)


def _wchunk_spec(w, layer):
    _, rows, cols = w.shape
    return pl.BlockSpec((None, rows // N_WSTEPS, cols), lambda i: (layer, jnp.minimum(i, N_WSTEPS - 1), 0))


def _layer_row_spec(w, index):
    return pl.BlockSpec((None, 1, w.shape[-1]), lambda i: (index, 0, 0), pipeline_mode=pl.Buffered(1))


def _stage_chunk(i, src_ref, dst_ref):
    rows = src_ref.shape[0]
    dst_ref[pl.ds(pl.multiple_of(i * rows, rows), rows), :] = src_ref[...].astype(BF16)


def _params(semantics):
    return pltpu.CompilerParams(dimension_semantics=semantics, vmem_limit_bytes=VMEM_LIMIT)


def _ada_kernel(c_ref, w_ref, b_ref, o_ref):
    c = c_ref[...]
    h = (c * _sigmoid(c)).astype(BF16)
    o_ref[...] = _dot(h, w_ref[...].astype(BF16)) + b_ref[...]


def _ada_mods(cvec, ada_w, ada_b):
    tn = D_MODEL
    n_out = N_MODS * D_MODEL
    return pl.pallas_call(
        _ada_kernel,
        grid=(DEPTH, n_out // tn),
        in_specs=[
            pl.BlockSpec((N_MOD_ROWS, D_MODEL), lambda l, j: (0, 0)),
            pl.BlockSpec((None, D_MODEL, tn), lambda l, j: (l, 0, j)),
            pl.BlockSpec((None, 1, tn), lambda l, j: (l, 0, j)),
        ],
        out_specs=pl.BlockSpec((None, N_MOD_ROWS, tn), lambda l, j: (l, 0, j)),
        out_shape=jax.ShapeDtypeStruct((DEPTH, N_MOD_ROWS, n_out), F32),
        compiler_params=_params(("arbitrary", "arbitrary")),
        name="ada_mods",
    )(cvec, ada_w, ada_b.reshape(DEPTH, 1, n_out))


def _ffn_kernel(x_ref, mod_ref, win_ref, wout_ref, g_ref, b_ref, o_ref, win_bf, wout_bf, *, base):
    i = pl.program_id(0)

    @pl.when(i < N_WSTEPS)
    def _():
        _stage_chunk(i, win_ref, win_bf)
        _stage_chunk(i, wout_ref, wout_bf)

    @pl.when(i >= N_WSTEPS)
    def _():
        x = x_ref[...]
        shift = mod_ref[base:base + 1, :]
        scale = mod_ref[base + 1:base + 2, :]
        gate = mod_ref[base + 2:base + 3, :]
        h = (x * (1.0 + scale) + shift).astype(BF16)
        y = None
        for lo, hi in FFN_CHUNKS:
            g = _dot(h, win_bf[:, lo:hi])
            u = _dot(h, win_bf[:, D_FF + lo:D_FF + hi])
            a = (g * _sigmoid(g) * u).astype(BF16)
            yc = _dot(a, wout_bf[lo:hi, :])
            y = yc if y is None else y + yc
        z = DEEPNORM_ALPHA * x + 0.5 * gate * y
        o_ref[...] = _norm_rows(z) * g_ref[...] + b_ref[...]


def _ffn(x, mods_all, w_in, w_out, ln_g, ln_b, *, layer, which, tm=512):
    row = _row_spec(tm, D_MODEL)
    ln_index = layer * 3 + which
    return pl.pallas_call(
        functools.partial(_ffn_kernel, base=3 * which),
        grid=(N_WSTEPS + N_TOK // tm,),
        in_specs=[row, _mod_spec(tm, layer), _wchunk_spec(w_in, layer), _wchunk_spec(w_out, layer),
                  _layer_row_spec(ln_g, ln_index), _layer_row_spec(ln_b, ln_index)],
        out_specs=row,
        out_shape=jax.ShapeDtypeStruct((N_TOK, D_MODEL), F32),
        scratch_shapes=[pltpu.VMEM((D_MODEL, 2 * D_FF), BF16), pltpu.VMEM((D_FF, D_MODEL), BF16)],
        compiler_params=_params(("arbitrary",)),
        name="ffn",
    )(x, mods_all, w_in, w_out, ln_g, ln_b)


def _proj_kernel(x_ref, mod_ref, cos_ref, sin_ref, win_ref, wkr_ref, wuq_ref, wuqs_ref, wk_ref,
                 wv_ref, gq_ref, gkv_ref,
                 glu_ref, rq_ref, rk_ref, rv_ref, rg_ref, q_ref, ckv_ref, kr_ref, kcat_ref, vpad_ref,
                 sig_ref, wm_ref, wg_ref):
    i = pl.program_id(0)

    @pl.when(i < N_WSTEPS)
    def _():
        rows = win_ref.shape[0]
        dst = pl.ds(pl.multiple_of(i * rows, rows), rows)
        wm_ref[dst, :] = win_ref[:, :MAIN_W].astype(BF16)
        tail = win_ref[:, MAIN_W:]
        wg_ref[dst, :] = tail[:, MLA_D_ROPE:].astype(BF16)

    @pl.when(i >= N_WSTEPS)
    def _():
        _proj_tile(x_ref, mod_ref, cos_ref, sin_ref, wm_ref, wkr_ref, wg_ref, wuq_ref, wuqs_ref, wk_ref,
                   wv_ref, gq_ref, gkv_ref,
                   glu_ref, rq_ref, rk_ref, rv_ref, rg_ref, q_ref, ckv_ref, kr_ref, kcat_ref, vpad_ref,
                   sig_ref)


def _proj_tile(x_ref, mod_ref, cos_ref, sin_ref, wm_ref, wkr_ref, wg_ref, wuq_ref, wuqs_ref, wk_ref,
               wv_ref, gq_ref, gkv_ref,
               glu_ref, rq_ref, rk_ref, rv_ref, rg_ref, q_ref, ckv_ref, kr_ref, kcat_ref, vpad_ref,
               sig_ref):
    x = x_ref[...]
    u = (x * (1.0 + mod_ref[4:5, :]) + mod_ref[3:4, :]).astype(BF16)
    o = 0
    a = _dot(u, wm_ref[:, o:o + CONV_DIM]); o += CONV_DIM
    g = _dot(u, wm_ref[:, o:o + CONV_DIM]); o += CONV_DIM
    glu_ref[...] = a * _sigmoid(g)
    w = RET_HEADS * RET_DK
    rq_ref[...] = _dot(u, wm_ref[:, o:o + w]).astype(BF16); o += w
    rk_ref[...] = (_dot(u, wm_ref[:, o:o + w]) * (RET_DK ** -0.5)).astype(BF16); o += w
    w = RET_HEADS * RET_DV
    rv_ref[...] = _dot(u, wm_ref[:, o:o + w]).astype(BF16); o += w
    rg = _dot(u, wm_ref[:, o:o + w]); o += w
    rg_ref[...] = rg * _sigmoid(rg)
    mq = _dot(u, wm_ref[:, o:o + MLA_Q_LORA]); o += MLA_Q_LORA
    mkv = _dot(u, wm_ref[:, o:o + MLA_KV_LORA]); o += MLA_KV_LORA
    sig_ref[...] = _sigmoid(_dot(u, wg_ref[...]))

    cos = cos_ref[...]
    sin = sin_ref[...]
    qn = (_rms_rows(mq) * gq_ref[...]).astype(BF16)
    qm = _dot(qn, wuq_ref[...])
    qs = _dot(qn, wuqs_ref[...])
    ckv = _rms_rows(mkv) * gkv_ref[...]
    ckv_ref[...] = ckv
    ckvb = ckv.astype(BF16)
    kn = _dot(ckvb, wk_ref[...])
    vpad_ref[...] = _dot(ckvb, wv_ref[...]).astype(BF16)
    kr2 = _dot(u, wkr_ref[...])
    kr = kr2[:, :HEAD_PAD]
    kr_ref[...] = kr
    kr_rot = kr * cos + kr2[:, HEAD_PAD:] * sin
    for h in range(MLA_HEADS):
        sl = slice(h * HEAD_PAD, (h + 1) * HEAD_PAD)
        q_ref[:, sl] = (qm[:, sl] * cos + qs[:, sl] * sin).astype(BF16)
        kcat_ref[:, sl] = (kn[:, sl] + kr_rot).astype(BF16)


def _proj(x, mods_all, cos_t, sin_t, mix_w_in, wkr, wuq, wuqs, wk, wv, gq, gkv, *, layer, tm=256):
    n_ctx_tiles = N_CTX // tm
    tiles_per_seq = DEC_SEQ // tm

    def rope_index(i):
        j = _tile(i)
        return (jnp.where(j < n_ctx_tiles, 0, tiles_per_seq + (j - n_ctx_tiles) % tiles_per_seq), 0)

    def row(w):
        return _row_spec(tm, w)

    def out(w, dt):
        return jax.ShapeDtypeStruct((N_TOK, w), dt)

    rope = pl.BlockSpec((tm, HEAD_PAD), rope_index)
    return pl.pallas_call(
        _proj_kernel,
        grid=(N_WSTEPS + N_TOK // tm,),
        in_specs=[row(D_MODEL), _mod_spec(tm, layer), rope, rope,
                  _wchunk_spec(mix_w_in, layer), _resident(wkr.shape), _resident(wuq.shape),
                  _resident(wuqs.shape), _resident(wk.shape), _resident(wv.shape), _resident(gq.shape),
                  _resident(gkv.shape)],
        out_specs=[row(CONV_DIM), row(RET_HEADS * RET_DK), row(RET_HEADS * RET_DK), row(RET_HEADS * RET_DV),
                   row(RET_HEADS * RET_DV), row(MLA_W), row(MLA_KV_LORA), row(HEAD_PAD), row(MLA_W),
                   row(MLA_W), row(GATE_W)],
        out_shape=[out(CONV_DIM, F32), out(RET_HEADS * RET_DK, BF16), out(RET_HEADS * RET_DK, BF16),
                   out(RET_HEADS * RET_DV, BF16), out(RET_HEADS * RET_DV, F32), out(MLA_W, BF16),
                   out(MLA_KV_LORA, F32), out(HEAD_PAD, F32), out(MLA_W, BF16), out(MLA_W, BF16),
                   out(GATE_W, F32)],
        scratch_shapes=[pltpu.VMEM((D_MODEL, MAIN_W), BF16), pltpu.VMEM((D_MODEL, GATE_W), BF16)],
        compiler_params=_params(("arbitrary",)),
        name="mix_proj",
    )(x, mods_all, cos_t, sin_t, mix_w_in, wkr, wuq, wuqs, wk, wv, gq, gkv)


def _cache_kv_kernel(ckv_ref, kr_ref, wk_ref, wv_ref, kcat_ref, vpad_ref):
    ckvb = ckv_ref[...].astype(BF16)
    kn = _dot(ckvb, wk_ref[...])
    vpad_ref[...] = _dot(ckvb, wv_ref[...]).astype(BF16)
    kr = kr_ref[...]
    for h in range(MLA_HEADS):
        sl = slice(h * HEAD_PAD, (h + 1) * HEAD_PAD)
        kcat_ref[:, sl] = (kn[:, sl] + kr).astype(BF16)


def _cache_kv(cache_ckv, cache_kr_pad, wk, wv):
    out = jax.ShapeDtypeStruct((DEPTH, DEC_BATCH * PAST_LEN, MLA_W), BF16)
    w_spec = pl.BlockSpec((None, MLA_KV_LORA, MLA_W), lambda l, b: (l, 0, 0))
    o_spec = pl.BlockSpec((None, PAST_LEN, MLA_W), lambda l, b: (l, b, 0))
    return pl.pallas_call(
        _cache_kv_kernel,
        grid=(DEPTH, DEC_BATCH),
        in_specs=[pl.BlockSpec((None, None, PAST_LEN, MLA_KV_LORA), lambda l, b: (b, l, 0, 0)),
                  pl.BlockSpec((None, None, PAST_LEN, HEAD_PAD), lambda l, b: (b, l, 0, 0)),
                  w_spec, w_spec],
        out_specs=[o_spec, o_spec],
        out_shape=[out, out],
        compiler_params=_params(("arbitrary", "arbitrary")),
        name="cache_kv",
    )(cache_ckv, cache_kr_pad, wk, wv)


def _conv_kernel(prev_ref, cur_ref, next_ref, w_ref, b_ref, g_ref, beta_ref, o_ref, pad_ref, acc_ref):
    i = pl.program_id(0)
    n_ctx_blocks = N_CTX // CONV_BLOCK
    blocks_per_seq = DEC_SEQ // CONV_BLOCK
    pos = (i - n_ctx_blocks) % blocks_per_seq
    latent = i >= n_ctx_blocks
    has_prev = jnp.logical_and(latent, pos != 0)
    has_next = jnp.logical_and(latent, pos != blocks_per_seq - 1)
    zero = jnp.zeros((CONV_HALO, CONV_DIM), F32)
    pad_ref[0:CONV_HALO, :] = jnp.where(has_prev, prev_ref[...], zero)
    pad_ref[CONV_HALO:CONV_HALO + CONV_BLOCK, :] = cur_ref[...]
    pad_ref[CONV_HALO + CONV_BLOCK:, :] = jnp.where(has_next, next_ref[...], zero)
    rows = 128
    lanes = 128
    first = CONV_HALO - CONV_WIDTH // 2
    for c in range(CONV_DIM // lanes):
        cs = slice(c * lanes, (c + 1) * lanes)
        for r in range(CONV_BLOCK // rows):
            acc = jnp.broadcast_to(b_ref[:, cs], (rows, lanes))
            for j in range(CONV_WIDTH):
                start = r * rows + first + j
                acc = acc + w_ref[j:j + 1, cs] * pad_ref[start:start + rows, cs]
            acc_ref[r * rows:(r + 1) * rows, cs] = acc
    y = _norm_rows(acc_ref[...]) * g_ref[...] + beta_ref[...]
    o_ref[...] = (y * _sigmoid(y)).astype(BF16)


def _conv_branch(glu, w_dw, b_dw, ln_g, ln_b):
    per = CONV_BLOCK // CONV_HALO
    n_halo = N_TOK // CONV_HALO
    return pl.pallas_call(
        _conv_kernel,
        grid=(N_TOK // CONV_BLOCK,),
        in_specs=[pl.BlockSpec((CONV_HALO, CONV_DIM), lambda i: (jnp.maximum(i * per - 1, 0), 0)),
                  pl.BlockSpec((CONV_BLOCK, CONV_DIM), lambda i: (i, 0)),
                  pl.BlockSpec((CONV_HALO, CONV_DIM), lambda i: (jnp.minimum((i + 1) * per, n_halo - 1), 0)),
                  _resident((CONV_WIDTH, CONV_DIM)), _resident((1, CONV_DIM)), _resident((1, CONV_DIM)),
                  _resident((1, CONV_DIM))],
        out_specs=pl.BlockSpec((CONV_BLOCK, CONV_DIM), lambda i: (i, 0)),
        out_shape=jax.ShapeDtypeStruct((N_TOK, CONV_DIM), BF16),
        scratch_shapes=[pltpu.VMEM((CONV_BLOCK + 2 * CONV_HALO, CONV_DIM), F32),
                        pltpu.VMEM((CONV_BLOCK, CONV_DIM), F32)],
        compiler_params=_params(("arbitrary",)),
        name="conv_branch",
    )(glu, glu, glu, w_dw, b_dw, ln_g, ln_b)


def _ret_kernel(*refs, t, hp, latent):
    if latent:
        lg_ref, q_ref, k_ref, v_ref, g_ref, s0f_ref, s0b_ref, o_ref, d_ref = refs
    else:
        lg_ref, q_ref, k_ref, v_ref, g_ref, o_ref, sf_ref, sb_ref, d_ref = refs
    hblk = pl.program_id(0)

    @pl.when(pl.program_id(1) == 0)
    def _():
        diff = (lax.broadcasted_iota(jnp.int32, (t, t), 0) - lax.broadcasted_iota(jnp.int32, (t, t), 1)).astype(F32)
        for hh in range(hp):
            lgf = lg_ref[0, hblk * hp + hh]
            lgb = lg_ref[1, hblk * hp + hh]
            d_ref[hh] = jnp.exp(jnp.where(diff >= 0, diff * lgf, -diff * lgb))

    pos = lax.broadcasted_iota(jnp.int32, (t, 1), 0).astype(F32)
    for hh in range(hp):
        lgf = lg_ref[0, hblk * hp + hh]
        lgb = lg_ref[1, hblk * hp + hh]
        q = q_ref[:, hh * RET_DK:(hh + 1) * RET_DK]
        k = k_ref[:, hh * RET_DK:(hh + 1) * RET_DK]
        v = v_ref[:, hh * RET_DV:(hh + 1) * RET_DV]
        p = (_dot_nt(q, k) * d_ref[hh]).astype(BF16)
        o = _dot(p, v)
        if latent:
            o = o + jnp.exp((pos + 1.0) * lgf) * _dot(q, s0f_ref[hh].astype(BF16))
            o = o + jnp.exp((t - pos) * lgb) * _dot(q, s0b_ref[hh].astype(BF16))
        else:
            kf = k.astype(F32)
            sf_ref[hh] = _dot_tn((kf * jnp.exp((t - 1.0 - pos) * lgf)).astype(BF16), v)
            sb_ref[hh] = _dot_tn((kf * jnp.exp(pos * lgb)).astype(BF16), v)
        o_ref[:, hh * RET_DV:(hh + 1) * RET_DV] = (g_ref[:, hh * RET_DV:(hh + 1) * RET_DV] * _norm_rows(o)).astype(BF16)


def _retention(log_g, rq, rk, rv, rg, prev_out, s0f, s0b, *, latent, layer=0):
    t = DEC_SEQ if latent else SEQ
    hp = 1 if latent else RET_HEADS
    n_seq = DEC_BATCH if latent else BATCH
    row0 = (N_CTX // t) if latent else 0

    def row(w):
        return pl.BlockSpec((t, hp * w), lambda h, s: (row0 + s, h))

    smem = pl.BlockSpec(memory_space=pltpu.SMEM)
    out_shape = [jax.ShapeDtypeStruct((N_TOK, RET_HEADS * RET_DV), BF16)]
    out_specs = [row(RET_DV)]
    in_specs = [smem, row(RET_DK), row(RET_DK), row(RET_DV), row(RET_DV)]
    args = [log_g, rq, rk, rv, rg]
    aliases = {}
    if latent:
        st = pl.BlockSpec((None, None, hp, RET_DK, RET_DV), lambda h, s: (s, layer, h, 0, 0))
        in_specs += [st, st, pl.BlockSpec(memory_space=pl.ANY)]
        args += [s0f, s0b, prev_out]
        aliases = {7: 0}
    else:
        st = pl.BlockSpec((None, hp, RET_DK, RET_DV), lambda h, s: (s, h, 0, 0))
        out_specs += [st, st]
        out_shape += [jax.ShapeDtypeStruct((BATCH, RET_HEADS, RET_DK, RET_DV), F32)] * 2

    def body(*refs):
        if latent:
            refs = refs[:7] + refs[8:]
        _ret_kernel(*refs, t=t, hp=hp, latent=latent)

    return pl.pallas_call(
        body,
        grid=(RET_HEADS // hp, n_seq),
        in_specs=in_specs,
        out_specs=out_specs,
        out_shape=out_shape,
        scratch_shapes=[pltpu.VMEM((hp, t, t), F32)],
        input_output_aliases=aliases,
        compiler_params=_params(("arbitrary", "arbitrary")),
        name="retention_lat" if latent else "retention_ctx",
    )(*args)


def _attn_kernel(*refs, hp, latent):
    if latent:
        q_ref, k_ref, v_ref, kc_ref, vc_ref, o_ref = refs
    else:
        q_ref, k_ref, v_ref, o_ref = refs
    scale = (MLA_D_NOPE + MLA_D_ROPE) ** -0.5
    for hh in range(hp):
        sl = slice(hh * HEAD_PAD, (hh + 1) * HEAD_PAD)
        q = q_ref[:, sl]
        s = _dot_nt(q, k_ref[:, sl]) * scale
        m = jnp.max(s, axis=-1, keepdims=True)
        if latent:
            sc = _dot_nt(q, kc_ref[:, sl]) * scale
            m = jnp.maximum(m, jnp.max(sc, axis=-1, keepdims=True))
        e = jnp.exp(s - m)
        den = jnp.sum(e, axis=-1, keepdims=True)
        o = _dot(e.astype(BF16), v_ref[:, sl])
        if latent:
            ec = jnp.exp(sc - m)
            den = den + jnp.sum(ec, axis=-1, keepdims=True)
            o = o + _dot(ec.astype(BF16), vc_ref[:, sl])
        o_ref[:, sl] = (o / den).astype(BF16)


def _attention(q, kcat, vpad, prev_out, kcat_c, vpad_c, *, latent):
    t = DEC_SEQ if latent else SEQ
    hp = 1 if latent else MLA_HEADS
    n_seq = DEC_BATCH if latent else BATCH
    row0 = (N_CTX // t) if latent else 0
    row = pl.BlockSpec((t, hp * HEAD_PAD), lambda s, h: (row0 + s, h))
    in_specs = [row, row, row]
    args = [q, kcat, vpad]
    aliases = {}
    if latent:
        cache = pl.BlockSpec((PAST_LEN, hp * HEAD_PAD), lambda s, h: (s, h))
        in_specs += [cache, cache, pl.BlockSpec(memory_space=pl.ANY)]
        args += [kcat_c, vpad_c, prev_out]
        aliases = {5: 0}

    def body(*refs):
        if latent:
            refs = refs[:5] + refs[6:]
        _attn_kernel(*refs, hp=hp, latent=latent)

    return pl.pallas_call(
        body,
        grid=(n_seq, MLA_HEADS // hp),
        in_specs=in_specs,
        out_specs=row,
        out_shape=jax.ShapeDtypeStruct((N_TOK, MLA_W), BF16),
        input_output_aliases=aliases,
        compiler_params=_params(("arbitrary", "arbitrary")),
        name="attention_lat" if latent else "attention_ctx",
    )(*args)


def _merge_kernel(x_ref, mod_ref, a_ref, r_ref, m_ref, sig_ref, wc_ref, wr_ref, wm_ref, wo_ref, g_ref, b_ref,
                  o_ref, wc_bf, wr_bf, wm_bf, wo_bf):
    i = pl.program_id(0)

    @pl.when(i < N_WSTEPS)
    def _():
        _stage_chunk(i, wc_ref, wc_bf)
        _stage_chunk(i, wr_ref, wr_bf)
        _stage_chunk(i, wo_ref, wo_bf)
        head = pl.ds(pl.multiple_of(i * HEAD_PAD, HEAD_PAD), MLA_D_V)
        rest = pl.ds(pl.multiple_of(i * HEAD_PAD + MLA_D_V, MLA_D_V), HEAD_PAD - MLA_D_V)
        wm_bf[head, :] = wm_ref[...].astype(BF16)
        wm_bf[rest, :] = jnp.zeros((HEAD_PAD - MLA_D_V, D_MODEL), BF16)

    @pl.when(i >= N_WSTEPS)
    def _():
        x = x_ref[...]
        merged = sig_ref[:, 0:D_MODEL] * _dot(a_ref[...], wc_bf[...])
        merged = merged + sig_ref[:, D_MODEL:2 * D_MODEL] * _dot(r_ref[...], wr_bf[...])
        merged = merged + sig_ref[:, 2 * D_MODEL:] * _dot(m_ref[...], wm_bf[...])
        y = _dot(merged.astype(BF16), wo_bf[...])
        z = DEEPNORM_ALPHA * x + mod_ref[5:6, :] * y
        o_ref[...] = _norm_rows(z) * g_ref[...] + b_ref[...]


def _merge(x, mods_all, a, r, m, sig, wc, wr, wm, wo, ln_g, ln_b, *, layer, tm=512):
    assert wm.shape[1] // N_WSTEPS == MLA_D_V

    def row(w):
        return _row_spec(tm, w)

    ln_index = layer * 3 + 1
    return pl.pallas_call(
        _merge_kernel,
        grid=(N_WSTEPS + N_TOK // tm,),
        in_specs=[row(D_MODEL), _mod_spec(tm, layer), row(CONV_DIM), row(RET_HEADS * RET_DV), row(MLA_W),
                  row(GATE_W), _wchunk_spec(wc, layer), _wchunk_spec(wr, layer), _wchunk_spec(wm, layer),
                  _wchunk_spec(wo, layer), _layer_row_spec(ln_g, ln_index), _layer_row_spec(ln_b, ln_index)],
        out_specs=row(D_MODEL),
        out_shape=jax.ShapeDtypeStruct((N_TOK, D_MODEL), F32),
        scratch_shapes=[pltpu.VMEM((CONV_DIM, D_MODEL), BF16), pltpu.VMEM((RET_HEADS * RET_DV, D_MODEL), BF16),
                        pltpu.VMEM((MLA_W, D_MODEL), BF16), pltpu.VMEM((D_MODEL, D_MODEL), BF16)],
        compiler_params=_params(("arbitrary",)),
        name="merge",
    )(x, mods_all, a, r, m, sig, wc, wr, wm, wo, ln_g, ln_b)


def _rope_tables():
    rows = DEC_SEQ // GRID_W
    row_id = jnp.repeat(jnp.arange(rows, dtype=F32), GRID_W)
    col_id = jnp.tile(jnp.arange(GRID_W, dtype=F32), rows)
    inv_freq = ROPE_BASE ** (-jnp.arange(ROPE_AXIS_HALF, dtype=F32) / ROPE_AXIS_HALF)
    ang = jnp.stack([row_id[:, None] * inv_freq, col_id[:, None] * inv_freq], axis=1)
    cos = jnp.cos(ang)
    sin = jnp.sin(ang)
    cos32 = jnp.stack([cos, cos], axis=2).reshape(DEC_SEQ, MLA_D_ROPE)
    sin32 = jnp.stack([-sin, sin], axis=2).reshape(DEC_SEQ, MLA_D_ROPE)
    tail = HEAD_PAD - ROPE_LANE0 - MLA_D_ROPE
    cos_t = jnp.concatenate([jnp.ones((DEC_SEQ, ROPE_LANE0), F32), cos32, jnp.ones((DEC_SEQ, tail), F32)], axis=1)
    sin_t = jnp.pad(sin32, ((0, 0), (ROPE_LANE0, tail)))
    cos_t = jnp.concatenate([jnp.ones((DEC_SEQ, HEAD_PAD), F32), cos_t], axis=0)
    sin_t = jnp.concatenate([jnp.zeros((DEC_SEQ, HEAD_PAD), F32), sin_t], axis=0)
    return cos_t, sin_t


_ROPE_SWAP = np.arange(MLA_D_ROPE) ^ ROPE_AXIS_HALF


def _head_pad_cols(w, width):
    k = w.shape[0]
    w = w.reshape(k, MLA_HEADS, width)
    return jnp.pad(w, ((0, 0), (0, 0), (0, HEAD_PAD - width))).reshape(k, MLA_W)


def kernel(x_prompt, x_sample, cache_mla_ckv, cache_mla_krope, state_ret_fwd, state_ret_bwd, c, c_ctx, ada_w, ada_b, ffn1_w_in, ffn1_w_out, ffn2_w_in, ffn2_w_out, post_ln_g, post_ln_b, mix_w_in, conv_w_dw, conv_b_dw, conv_ln_g, conv_ln_b, conv_w_out, ret_decay_fwd, ret_decay_bwd, ret_w_out, mla_q_norm, mla_w_uq, mla_kv_norm, mla_w_ukv, mla_w_out, mix_w_o):
    x = jnp.concatenate([x_prompt.reshape(N_CTX, D_MODEL), x_sample.reshape(N_LAT, D_MODEL)], axis=0)
    cvec = jnp.concatenate([c_ctx[None, :], c, jnp.zeros((N_MOD_ROWS - 1 - DEC_BATCH, D_MODEL), F32)], axis=0)
    mods_all = _ada_mods(cvec, ada_w, ada_b).reshape(DEPTH, N_MOD_ROWS, N_MODS, D_MODEL)
    cos_t, sin_t = _rope_tables()

    w_ukv = mla_w_ukv.reshape(DEPTH, MLA_KV_LORA, MLA_HEADS, MLA_D_NOPE + MLA_D_V)
    wk_all = _head_pad_cols(w_ukv[..., :MLA_D_NOPE].reshape(DEPTH * MLA_KV_LORA, -1), MLA_D_NOPE)
    wv_all = _head_pad_cols(w_ukv[..., MLA_D_NOPE:].reshape(DEPTH * MLA_KV_LORA, -1), MLA_D_V)
    wk_all = wk_all.reshape(DEPTH, MLA_KV_LORA, MLA_W).astype(BF16)
    wv_all = wv_all.reshape(DEPTH, MLA_KV_LORA, MLA_W).astype(BF16)
    kr_tail = HEAD_PAD - ROPE_LANE0 - MLA_D_ROPE
    cache_kr_pad = jnp.pad(cache_mla_krope, ((0, 0), (0, 0), (0, 0), (ROPE_LANE0, kr_tail)))
    kcat_c, vpad_c = _cache_kv(cache_mla_ckv, cache_kr_pad, wk_all, wv_all)
    log_g = jnp.stack([jax.nn.log_sigmoid(ret_decay_fwd), jax.nn.log_sigmoid(ret_decay_bwd)], axis=1)

    ln_g = post_ln_g.reshape(DEPTH * 3, 1, D_MODEL)
    ln_b = post_ln_b.reshape(DEPTH * 3, 1, D_MODEL)
    ckv_l, kr_l, sf_l, sb_l = [], [], [], []
    for l in range(DEPTH):
        x = _ffn(x, mods_all, ffn1_w_in, ffn1_w_out, ln_g, ln_b, layer=l, which=0)

        w_kr = mix_w_in[l, :, MAIN_W:MAIN_W + MLA_D_ROPE]
        wkr = jnp.concatenate([jnp.pad(w_kr, ((0, 0), (ROPE_LANE0, kr_tail))),
                               jnp.pad(w_kr[:, _ROPE_SWAP], ((0, 0), (ROPE_LANE0, kr_tail)))], axis=1).astype(BF16)
        w_uq = mla_w_uq[l].reshape(MLA_Q_LORA, MLA_HEADS, MLA_D_NOPE + MLA_D_ROPE)
        wuq = _head_pad_cols(w_uq.reshape(MLA_Q_LORA, -1), MLA_D_NOPE + MLA_D_ROPE).astype(BF16)
        w_uq_swap = jnp.concatenate([jnp.zeros_like(w_uq[..., :MLA_D_NOPE]), w_uq[..., MLA_D_NOPE:][..., _ROPE_SWAP]], axis=-1)
        wuqs = _head_pad_cols(w_uq_swap.reshape(MLA_Q_LORA, -1), MLA_D_NOPE + MLA_D_ROPE).astype(BF16)
        (glu, rq, rk, rv, rg, q, ckv, kr, kcat, vpad, sig) = _proj(
            x, mods_all, cos_t, sin_t, mix_w_in, wkr, wuq, wuqs, wk_all[l], wv_all[l],
            mla_q_norm[l].reshape(1, -1), mla_kv_norm[l].reshape(1, -1), layer=l)

        a = _conv_branch(glu, conv_w_dw[l], conv_b_dw[l].reshape(1, -1), conv_ln_g[l].reshape(1, -1),
                         conv_ln_b[l].reshape(1, -1))
        r, sf, sb = _retention(log_g[l], rq, rk, rv, rg, None, None, None, latent=False)
        r = _retention(log_g[l], rq, rk, rv, rg, r, state_ret_fwd, state_ret_bwd, latent=True, layer=l)[0]
        m = _attention(q, kcat, vpad, None, None, None, latent=False)
        m = _attention(q, kcat, vpad, m, kcat_c[l], vpad_c[l], latent=True)

        x = _merge(x, mods_all, a, r, m, sig, conv_w_out, ret_w_out, mla_w_out, mix_w_o, ln_g, ln_b, layer=l)
        x = _ffn(x, mods_all, ffn2_w_in, ffn2_w_out, ln_g, ln_b, layer=l, which=2)

        ckv_l.append(ckv[:N_CTX].reshape(BATCH, SEQ, MLA_KV_LORA))
        kr_l.append(kr[:N_CTX, ROPE_LANE0:ROPE_LANE0 + MLA_D_ROPE].reshape(BATCH, SEQ, MLA_D_ROPE))
        sf_l.append(sf)
        sb_l.append(sb)

    return (x[:N_CTX].reshape(BATCH, SEQ, D_MODEL), x[N_CTX:].reshape(DEC_BATCH, DEC_SEQ, D_MODEL),
            jnp.stack(ckv_l, axis=1), jnp.stack(kr_l, axis=1), jnp.stack(sf_l, axis=1), jnp.stack(sb_l, axis=1))
```

```python
import functools
import math

import jax
import jax.numpy as jnp
import numpy as np
from jax import lax
from jax.experimental import pallas as pl
from jax.experimental.pallas import tpu as pltpu

F32 = jnp.float32
BF16 = jnp.bfloat16

D_MODEL = 1024
BATCH = 16
SEQ = 256
DEPTH = 2
DEC_BATCH = 4
DEC_SEQ = 1024
PAST_LEN = 256
GRID_W = 64
D_FF = 2816
N_MODS = 9
CONV_DIM = 512
CONV_WIDTH = 31
RET_HEADS = 4
RET_DK = 128
RET_DV = 256
MLA_HEADS = 8
MLA_Q_LORA = 512
MLA_KV_LORA = 256
MLA_D_NOPE = 64
MLA_D_ROPE = 32
MLA_D_V = 64
ROPE_AXIS_HALF = MLA_D_ROPE // 4
ROPE_BASE = 10000.0
DEEPNORM_ALPHA = (2 * DEPTH) ** 0.25
LN_EPS = 1e-5
RMS_EPS = 1e-6

N_CTX = BATCH * SEQ
N_LAT = DEC_BATCH * DEC_SEQ
N_TOK = N_CTX + N_LAT
N_MOD_ROWS = 8
HEAD_PAD = 128
ROPE_LANE0 = MLA_D_NOPE
MLA_W = MLA_HEADS * HEAD_PAD
MLA_V_W = MLA_HEADS * MLA_D_V
MAIN_W = 2 * CONV_DIM + 2 * RET_HEADS * RET_DK + 2 * RET_HEADS * RET_DV + MLA_Q_LORA + MLA_KV_LORA
GATE_W = 3 * D_MODEL
SUBLANES = 8
CONV_HALO = 16
CONV_BLOCK = 256
VMEM_LIMIT = 56 * 1024 * 1024
N_WSTEPS = 8
MXU_COLS = 256
FFN_CHUNKS = ((0, 6 * MXU_COLS), (6 * MXU_COLS, D_FF))
ATTN_QBLOCK = 256
ATTN_QSCALE = (MLA_D_NOPE + MLA_D_ROPE) ** -0.5 * math.log2(math.e)


def _dot(a, b):
    return jnp.dot(a, b, preferred_element_type=F32)


def _dot_nt(a, b):
    return lax.dot_general(a, b, (((1,), (1,)), ((), ())), preferred_element_type=F32)


def _dot_tn(a, b):
    return lax.dot_general(a, b, (((0,), (0,)), ((), ())), preferred_element_type=F32)


def _sigmoid(x):
    return 1.0 / (1.0 + jnp.exp(-x))


def _norm_rows(z):
    mu = jnp.mean(z, axis=-1, keepdims=True)
    zc = z - mu
    var = jnp.mean(zc * zc, axis=-1, keepdims=True)
    return zc * lax.rsqrt(var + LN_EPS)


def _rms_rows(z):
    return z * lax.rsqrt(jnp.mean(z * z, axis=-1, keepdims=True) + RMS_EPS)


def _resident(shape):
    zeros = (0,) * len(shape)
    return pl.BlockSpec(shape, lambda *_: zeros, pipeline_mode=pl.Buffered(1))


def _layer_resident(w, layer):
    zeros = (0,) * (w.ndim - 1)
    return pl.BlockSpec((None,) + w.shape[1:], lambda *_: (layer,) + zeros, pipeline_mode=pl.Buffered(1))


def _tile(i, n_w):
    return jnp.maximum(i - n_w, 0)


def _row_spec(tm, w, n_w):
    return pl.BlockSpec((tm, w), lambda i: (_tile(i, n_w), 0))


def _ctx_row_spec(tm, w, n_w):
    last = N_CTX // tm - 1
    return pl.BlockSpec((tm, w), lambda i: (jnp.minimum(_tile(i, n_w), last), 0))


def _lat_row_spec(tm, w, n_w):
    first = N_CTX // tm
    return pl.BlockSpec((tm, w), lambda i: (jnp.maximum(_tile(i, n_w) - first, 0), 0))


def _mod_spec(tm, layer, n_w):
    n_ctx_tiles = N_CTX // tm
    tiles_per_seq = DEC_SEQ // tm

    def index(i):
        j = _tile(i, n_w)
        return (layer, jnp.where(j < n_ctx_tiles, 0, 1 + (j - n_ctx_tiles) // tiles_per_seq), 0, 0)

    return pl.BlockSpec((None, None, N_MODS, D_MODEL), index)


def _wchunk_spec(w, layer):
    _, rows, cols = w.shape
    return pl.BlockSpec((None, rows // N_WSTEPS, cols), lambda i: (layer, jnp.minimum(i, N_WSTEPS - 1), 0))


def _layer_row_spec(w, index):
    return pl.BlockSpec((None, 1, w.shape[-1]), lambda i: (index, 0, 0), pipeline_mode=pl.Buffered(1))


def _stage_chunk(i, src_ref, dst_ref):
    rows = src_ref.shape[0]
    dst_ref[pl.ds(pl.multiple_of(i * rows, rows), rows), :] = src_ref[...].astype(BF16)


def _params(semantics):
    return pltpu.CompilerParams(dimension_semantics=semantics, vmem_limit_bytes=VMEM_LIMIT)


def _ada_kernel(c_ref, w_ref, b_ref, o_ref):
    c = c_ref[...]
    h = (c * _sigmoid(c)).astype(BF16)
    o_ref[...] = _dot(h, w_ref[...].astype(BF16)) + b_ref[...]


def _ada_mods(cvec, ada_w, ada_b):
    tn = D_MODEL
    n_out = N_MODS * D_MODEL
    return pl.pallas_call(
        _ada_kernel,
        grid=(DEPTH, n_out // tn),
        in_specs=[
            pl.BlockSpec((N_MOD_ROWS, D_MODEL), lambda l, j: (0, 0)),
            pl.BlockSpec((None, D_MODEL, tn), lambda l, j: (l, 0, j)),
            pl.BlockSpec((None, 1, tn), lambda l, j: (l, 0, j)),
        ],
        out_specs=pl.BlockSpec((None, N_MOD_ROWS, tn), lambda l, j: (l, 0, j)),
        out_shape=jax.ShapeDtypeStruct((DEPTH, N_MOD_ROWS, n_out), F32),
        compiler_params=_params(("arbitrary", "arbitrary")),
        name="ada_mods",
    )(cvec, ada_w, ada_b.reshape(DEPTH, 1, n_out))


def _ffn_kernel(*refs, base, tm, n_x, n_out):
    x_refs = refs[:n_x]
    mod_ref, win_ref, wout_ref, g_ref, b_ref = refs[n_x:n_x + 5]
    o_refs = refs[n_x + 5:n_x + 5 + n_out]
    win_bf, wout_bf = refs[n_x + 5 + n_out:]
    i = pl.program_id(0)
    is_ctx = i < N_WSTEPS + N_CTX // tm

    @pl.when(i < N_WSTEPS)
    def _():
        _stage_chunk(i, win_ref, win_bf)
        _stage_chunk(i, wout_ref, wout_bf)

    @pl.when(i >= N_WSTEPS)
    def _():
        if n_x == 2:
            x = jnp.where(is_ctx, x_refs[0][...], x_refs[1][...])
        else:
            x = x_refs[0][...]
        shift = mod_ref[base:base + 1, :]
        scale = mod_ref[base + 1:base + 2, :]
        gate = mod_ref[base + 2:base + 3, :]
        h = (x * (1.0 + scale) + shift).astype(BF16)
        y = None
        for lo, hi in FFN_CHUNKS:
            g = _dot(h, win_bf[:, lo:hi])
            u = _dot(h, win_bf[:, D_FF + lo:D_FF + hi])
            a = (g * _sigmoid(g) * u).astype(BF16)
            yc = _dot(a, wout_bf[lo:hi, :])
            y = yc if y is None else y + yc
        z = DEEPNORM_ALPHA * x + 0.5 * gate * y
        res = _norm_rows(z) * g_ref[...] + b_ref[...]
        if n_out == 2:
            @pl.when(is_ctx)
            def _():
                o_refs[0][...] = res

            @pl.when(jnp.logical_not(is_ctx))
            def _():
                o_refs[1][...] = res
        else:
            o_refs[0][...] = res


def _ffn(xs, mods_all, w_in, w_out, ln_g, ln_b, *, layer, which, split_out=False, tm=512):
    row = _row_spec(tm, D_MODEL, N_WSTEPS)
    pair = [_ctx_row_spec(tm, D_MODEL, N_WSTEPS), _lat_row_spec(tm, D_MODEL, N_WSTEPS)]
    ln_index = layer * 3 + which
    if split_out:
        out_specs = pair
        out_shape = [jax.ShapeDtypeStruct((N_CTX, D_MODEL), F32), jax.ShapeDtypeStruct((N_LAT, D_MODEL), F32)]
    else:
        out_specs = [row]
        out_shape = [jax.ShapeDtypeStruct((N_TOK, D_MODEL), F32)]
    return pl.pallas_call(
        functools.partial(_ffn_kernel, base=3 * which, tm=tm, n_x=len(xs), n_out=len(out_specs)),
        grid=(N_WSTEPS + N_TOK // tm,),
        in_specs=(pair if len(xs) == 2 else [row]) + [
            _mod_spec(tm, layer, N_WSTEPS), _wchunk_spec(w_in, layer), _wchunk_spec(w_out, layer),
            _layer_row_spec(ln_g, ln_index), _layer_row_spec(ln_b, ln_index)],
        out_specs=out_specs,
        out_shape=out_shape,
        scratch_shapes=[pltpu.VMEM((D_MODEL, 2 * D_FF), BF16), pltpu.VMEM((D_FF, D_MODEL), BF16)],
        compiler_params=_params(("arbitrary",)),
        name="ffn",
    )(*xs, mods_all, w_in, w_out, ln_g, ln_b)


def _proj_kernel(*refs, tm, carry):
    (x_ref, mod_ref, cos_ref, sin_ref, wm_ref, wkr_ref, wg_ref, wuq_ref, wuqs_ref, wk_ref, wvt_ref, gq_ref,
     gkv_ref) = refs[:13]
    refs = refs[13:]
    if carry:
        ckv_prev_ref, kr_prev_ref = refs[:2]
        refs = refs[2:]
    (glu_ref, rq_ref, rk_ref, rv_ref, rg_ref, q_ref, kcat_ref, vt_ref, sig_ref, ckv_ref, kr_ref) = refs
    is_ctx = pl.program_id(0) < N_CTX // tm

    x = x_ref[...]
    u = (x * (1.0 + mod_ref[4:5, :]) + mod_ref[3:4, :]).astype(BF16)
    o = 0
    a = _dot(u, wm_ref[:, o:o + CONV_DIM]); o += CONV_DIM
    g = _dot(u, wm_ref[:, o:o + CONV_DIM]); o += CONV_DIM
    glu_ref[...] = a * _sigmoid(g)
    w = RET_HEADS * RET_DK
    rq_ref[...] = _dot(u, wm_ref[:, o:o + w]).astype(BF16); o += w
    rk_ref[...] = (_dot(u, wm_ref[:, o:o + w]) * (RET_DK ** -0.5)).astype(BF16); o += w
    w = RET_HEADS * RET_DV
    rv_ref[...] = _dot(u, wm_ref[:, o:o + w]).astype(BF16); o += w
    rg = _dot(u, wm_ref[:, o:o + w]); o += w
    rg_ref[...] = (rg * _sigmoid(rg)).astype(BF16)
    mq = _dot(u, wm_ref[:, o:o + MLA_Q_LORA]); o += MLA_Q_LORA
    mkv = _dot(u, wm_ref[:, o:o + MLA_KV_LORA]); o += MLA_KV_LORA
    sig_ref[...] = _sigmoid(_dot(u, wg_ref[...])).astype(BF16)

    cos = cos_ref[...]
    sin = sin_ref[...]
    qn = (_rms_rows(mq) * gq_ref[...]).astype(BF16)
    qm = _dot(qn, wuq_ref[...])
    qs = _dot(qn, wuqs_ref[...])
    ckv = _rms_rows(mkv) * gkv_ref[...]
    ckvb = ckv.astype(BF16)
    kn = _dot(ckvb, wk_ref[...])
    vt_ref[...] = _dot_nt(wvt_ref[...], ckvb).astype(BF16)
    kr2 = _dot(u, wkr_ref[...])
    kr = kr2[:, :HEAD_PAD]
    kr_rot = kr * cos + kr2[:, HEAD_PAD:] * sin
    for h in range(MLA_HEADS):
        sl = slice(h * HEAD_PAD, (h + 1) * HEAD_PAD)
        q_ref[:, sl] = ((qm[:, sl] * cos + qs[:, sl] * sin) * ATTN_QSCALE).astype(BF16)
        kcat_ref[:, sl] = (kn[:, sl] + kr_rot).astype(BF16)

    @pl.when(is_ctx)
    def _():
        seqs = tm // SEQ
        ckv3 = ckv.reshape(seqs, SEQ, MLA_KV_LORA)
        kr3 = kr[:, ROPE_LANE0:ROPE_LANE0 + MLA_D_ROPE].reshape(seqs, SEQ, MLA_D_ROPE)
        if carry:
            ckv_ref[:, 0] = ckv_prev_ref[...]
            kr_ref[:, 0] = kr_prev_ref[...]
            ckv_ref[:, 1] = ckv3
            kr_ref[:, 1] = kr3
        else:
            ckv_ref[...] = ckv3
            kr_ref[...] = kr3


def _proj(x, mods_all, cos_t, sin_t, wm_all, wkr, wg_all, wuq, wuqs, wk, wvt, gq, gkv, carry, *, layer, tm=256):
    n_ctx_tiles = N_CTX // tm
    tiles_per_seq = DEC_SEQ // tm
    seqs = tm // SEQ

    def rope_index(i):
        return (jnp.where(i < n_ctx_tiles, 0, tiles_per_seq + (i - n_ctx_tiles) % tiles_per_seq), 0)

    def row(w):
        return _row_spec(tm, w, 0)

    def out(w, dt):
        return jax.ShapeDtypeStruct((N_TOK, w), dt)

    def ctx_seq_spec(*tail):
        zeros = (0,) * len(tail)
        return pl.BlockSpec((seqs,) + tail, lambda i: (jnp.minimum(i, n_ctx_tiles - 1),) + zeros)

    rope = pl.BlockSpec((tm, HEAD_PAD), rope_index)
    in_specs = [row(D_MODEL), _mod_spec(tm, layer, 0), rope, rope,
                _layer_resident(wm_all, layer), _resident(wkr.shape), _layer_resident(wg_all, layer),
                _resident(wuq.shape), _resident(wuqs.shape), _resident(wk.shape), _resident(wvt.shape),
                _resident(gq.shape), _resident(gkv.shape)]
    args = [x, mods_all, cos_t, sin_t, wm_all, wkr, wg_all, wuq, wuqs, wk, wvt, gq, gkv]
    if carry is None:
        ctx_specs = [ctx_seq_spec(SEQ, MLA_KV_LORA), ctx_seq_spec(SEQ, MLA_D_ROPE)]
        ctx_shapes = [jax.ShapeDtypeStruct((BATCH, SEQ, MLA_KV_LORA), F32),
                      jax.ShapeDtypeStruct((BATCH, SEQ, MLA_D_ROPE), F32)]
    else:
        in_specs += [ctx_seq_spec(SEQ, MLA_KV_LORA), ctx_seq_spec(SEQ, MLA_D_ROPE)]
        args += list(carry)
        ctx_specs = [ctx_seq_spec(DEPTH, SEQ, MLA_KV_LORA), ctx_seq_spec(DEPTH, SEQ, MLA_D_ROPE)]
        ctx_shapes = [jax.ShapeDtypeStruct((BATCH, DEPTH, SEQ, MLA_KV_LORA), F32),
                      jax.ShapeDtypeStruct((BATCH, DEPTH, SEQ, MLA_D_ROPE), F32)]
    return pl.pallas_call(
        functools.partial(_proj_kernel, tm=tm, carry=carry is not None),
        grid=(N_TOK // tm,),
        in_specs=in_specs,
        out_specs=[row(CONV_DIM), row(RET_HEADS * RET_DK), row(RET_HEADS * RET_DK), row(RET_HEADS * RET_DV),
                   row(RET_HEADS * RET_DV), row(MLA_W), row(MLA_W),
                   pl.BlockSpec((MLA_V_W, tm), lambda i: (0, i)), row(GATE_W)] + ctx_specs,
        out_shape=[out(CONV_DIM, F32), out(RET_HEADS * RET_DK, BF16), out(RET_HEADS * RET_DK, BF16),
                   out(RET_HEADS * RET_DV, BF16), out(RET_HEADS * RET_DV, BF16), out(MLA_W, BF16),
                   out(MLA_W, BF16), jax.ShapeDtypeStruct((MLA_V_W, N_TOK), BF16), out(GATE_W, BF16)] + ctx_shapes,
        compiler_params=_params(("arbitrary",)),
        name="mix_proj",
    )(*args)


def _cache_kv_kernel(ckv_ref, kr_ref, wk_ref, wvt_ref, kcat_ref, vt_ref):
    ckvb = ckv_ref[...].astype(BF16)
    kn = _dot(ckvb, wk_ref[...])
    vt_ref[...] = _dot_nt(wvt_ref[...], ckvb).astype(BF16)
    kr = kr_ref[...]
    for h in range(MLA_HEADS):
        sl = slice(h * HEAD_PAD, (h + 1) * HEAD_PAD)
        kcat_ref[:, sl] = (kn[:, sl] + kr).astype(BF16)


def _cache_kv(cache_ckv, cache_kr_pad, wk, wvt):
    n = DEC_BATCH * PAST_LEN
    return pl.pallas_call(
        _cache_kv_kernel,
        grid=(DEPTH, DEC_BATCH),
        in_specs=[pl.BlockSpec((None, None, PAST_LEN, MLA_KV_LORA), lambda l, b: (b, l, 0, 0)),
                  pl.BlockSpec((None, None, PAST_LEN, HEAD_PAD), lambda l, b: (b, l, 0, 0)),
                  pl.BlockSpec((None, MLA_KV_LORA, MLA_W), lambda l, b: (l, 0, 0)),
                  pl.BlockSpec((None, MLA_V_W, MLA_KV_LORA), lambda l, b: (l, 0, 0))],
        out_specs=[pl.BlockSpec((None, PAST_LEN, MLA_W), lambda l, b: (l, b, 0)),
                   pl.BlockSpec((None, MLA_V_W, PAST_LEN), lambda l, b: (l, 0, b))],
        out_shape=[jax.ShapeDtypeStruct((DEPTH, n, MLA_W), BF16), jax.ShapeDtypeStruct((DEPTH, MLA_V_W, n), BF16)],
        compiler_params=_params(("arbitrary", "arbitrary")),
        name="cache_kv",
    )(cache_ckv, cache_kr_pad, wk, wvt)


def _conv_kernel(prev_ref, cur_ref, next_ref, w_ref, b_ref, g_ref, beta_ref, o_ref, pad_ref, shift_ref, acc_ref):
    i = pl.program_id(0)
    n_ctx_blocks = N_CTX // CONV_BLOCK
    blocks_per_seq = DEC_SEQ // CONV_BLOCK
    pos = (i - n_ctx_blocks) % blocks_per_seq
    latent = i >= n_ctx_blocks
    has_prev = jnp.logical_and(latent, pos != 0)
    has_next = jnp.logical_and(latent, pos != blocks_per_seq - 1)
    zero = jnp.zeros((CONV_HALO, CONV_DIM), F32)
    pad_ref[0:CONV_HALO, :] = jnp.where(has_prev, prev_ref[...], zero)
    pad_ref[CONV_HALO:CONV_HALO + CONV_BLOCK, :] = cur_ref[...]
    pad_ref[CONV_HALO + CONV_BLOCK:, :] = jnp.where(has_next, next_ref[...], zero)
    span = shift_ref.shape[1]
    for ph in range(SUBLANES):
        shift_ref[ph] = pad_ref[ph:ph + span, :]
    rows = 64
    lanes = 128
    first = CONV_HALO - CONV_WIDTH // 2
    for c in range(CONV_DIM // lanes):
        cs = slice(c * lanes, (c + 1) * lanes)
        for r in range(CONV_BLOCK // rows):
            acc = jnp.broadcast_to(b_ref[:, cs], (rows, lanes))
            for j in range(CONV_WIDTH):
                ph = (first + j) % SUBLANES
                start = r * rows + (first + j) - ph
                acc = acc + w_ref[j:j + 1, cs] * shift_ref[ph, start:start + rows, cs]
            acc_ref[r * rows:(r + 1) * rows, cs] = acc
    y = _norm_rows(acc_ref[...]) * g_ref[...] + beta_ref[...]
    o_ref[...] = (y * _sigmoid(y)).astype(BF16)


def _conv_branch(glu, w_dw, b_dw, ln_g, ln_b):
    per = CONV_BLOCK // CONV_HALO
    n_halo = N_TOK // CONV_HALO
    span = CONV_BLOCK + 2 * CONV_HALO - SUBLANES
    return pl.pallas_call(
        _conv_kernel,
        grid=(N_TOK // CONV_BLOCK,),
        in_specs=[pl.BlockSpec((CONV_HALO, CONV_DIM), lambda i: (jnp.maximum(i * per - 1, 0), 0)),
                  pl.BlockSpec((CONV_BLOCK, CONV_DIM), lambda i: (i, 0)),
                  pl.BlockSpec((CONV_HALO, CONV_DIM), lambda i: (jnp.minimum((i + 1) * per, n_halo - 1), 0)),
                  _resident((CONV_WIDTH, CONV_DIM)), _resident((1, CONV_DIM)), _resident((1, CONV_DIM)),
                  _resident((1, CONV_DIM))],
        out_specs=pl.BlockSpec((CONV_BLOCK, CONV_DIM), lambda i: (i, 0)),
        out_shape=jax.ShapeDtypeStruct((N_TOK, CONV_DIM), BF16),
        scratch_shapes=[pltpu.VMEM((CONV_BLOCK + 2 * CONV_HALO, CONV_DIM), F32),
                        pltpu.VMEM((SUBLANES, span, CONV_DIM), F32),
                        pltpu.VMEM((CONV_BLOCK, CONV_DIM), F32)],
        compiler_params=_params(("arbitrary",)),
        name="conv_branch",
    )(glu, glu, glu, w_dw, b_dw, ln_g, ln_b)


def _ret_kernel(*refs, t, hp, latent, carry):
    if latent:
        lg_ref, q_ref, k_ref, v_ref, g_ref, s0f_ref, s0b_ref, o_ref, d_ref = refs
    elif carry:
        lg_ref, q_ref, k_ref, v_ref, g_ref, sf_prev_ref, sb_prev_ref, o_ref, sf_ref, sb_ref, d_ref = refs
    else:
        lg_ref, q_ref, k_ref, v_ref, g_ref, o_ref, sf_ref, sb_ref, d_ref = refs
    hblk = pl.program_id(0)

    @pl.when(pl.program_id(1) == 0)
    def _():
        diff = (lax.broadcasted_iota(jnp.int32, (t, t), 0) - lax.broadcasted_iota(jnp.int32, (t, t), 1)).astype(F32)
        for hh in range(hp):
            lgf = lg_ref[0, hblk * hp + hh]
            lgb = lg_ref[1, hblk * hp + hh]
            d_ref[hh] = jnp.exp(jnp.where(diff >= 0, diff * lgf, -diff * lgb))

    if carry:
        sf_ref[0] = sf_prev_ref[...]
        sb_ref[0] = sb_prev_ref[...]
    pos = lax.broadcasted_iota(jnp.int32, (t, 1), 0).astype(F32)
    for hh in range(hp):
        lgf = lg_ref[0, hblk * hp + hh]
        lgb = lg_ref[1, hblk * hp + hh]
        q = q_ref[:, hh * RET_DK:(hh + 1) * RET_DK]
        k = k_ref[:, hh * RET_DK:(hh + 1) * RET_DK]
        v = v_ref[:, hh * RET_DV:(hh + 1) * RET_DV]
        p = (_dot_nt(q, k) * d_ref[hh]).astype(BF16)
        o = _dot(p, v)
        if latent:
            o = o + jnp.exp((pos + 1.0) * lgf) * _dot(q, s0f_ref[hh].astype(BF16))
            o = o + jnp.exp((t - pos) * lgb) * _dot(q, s0b_ref[hh].astype(BF16))
        else:
            kf = k.astype(F32)
            sf = _dot_tn((kf * jnp.exp((t - 1.0 - pos) * lgf)).astype(BF16), v)
            sb = _dot_tn((kf * jnp.exp(pos * lgb)).astype(BF16), v)
            if carry:
                sf_ref[1, hh] = sf
                sb_ref[1, hh] = sb
            else:
                sf_ref[hh] = sf
                sb_ref[hh] = sb
        o_ref[:, hh * RET_DV:(hh + 1) * RET_DV] = (g_ref[:, hh * RET_DV:(hh + 1) * RET_DV] * _norm_rows(o)).astype(BF16)


def _retention(log_g, rq, rk, rv, rg, states, *, latent, layer):
    t = DEC_SEQ if latent else SEQ
    hp = 1 if latent else RET_HEADS
    n_seq = DEC_BATCH if latent else BATCH
    row0 = (N_CTX // t) if latent else 0

    def row(w):
        return pl.BlockSpec((t, hp * w), lambda h, s: (row0 + s, h))

    smem = pl.BlockSpec(memory_space=pltpu.SMEM)
    out_shape = [jax.ShapeDtypeStruct((n_seq * t, RET_HEADS * RET_DV), BF16)]
    out_specs = [pl.BlockSpec((t, hp * RET_DV), lambda h, s: (s, h))]
    in_specs = [smem, row(RET_DK), row(RET_DK), row(RET_DV), row(RET_DV)]
    args = [log_g, rq, rk, rv, rg]
    carry = False
    if latent:
        st = pl.BlockSpec((None, None, hp, RET_DK, RET_DV), lambda h, s: (s, layer, h, 0, 0))
        in_specs += [st, st]
        args += list(states)
    else:
        st = pl.BlockSpec((None, hp, RET_DK, RET_DV), lambda h, s: (s, h, 0, 0))
        if states is None:
            out_specs += [st, st]
            out_shape += [jax.ShapeDtypeStruct((BATCH, RET_HEADS, RET_DK, RET_DV), F32)] * 2
        else:
            carry = True
            in_specs += [st, st]
            args += list(states)
            st2 = pl.BlockSpec((None, DEPTH, hp, RET_DK, RET_DV), lambda h, s: (s, 0, h, 0, 0))
            out_specs += [st2, st2]
            out_shape += [jax.ShapeDtypeStruct((BATCH, DEPTH, RET_HEADS, RET_DK, RET_DV), F32)] * 2

    return pl.pallas_call(
        functools.partial(_ret_kernel, t=t, hp=hp, latent=latent, carry=carry),
        grid=(RET_HEADS // hp, n_seq),
        in_specs=in_specs,
        out_specs=out_specs,
        out_shape=out_shape,
        scratch_shapes=[pltpu.VMEM((hp, t, t), F32)],
        compiler_params=_params(("arbitrary", "arbitrary")),
        name="retention_lat" if latent else "retention_ctx",
    )(*args)


def _attn_kernel(*refs, t, hp, latent):
    if latent:
        q_ref, k_ref, vt_ref, kc_ref, vtc_ref, o_ref = refs
    else:
        q_ref, k_ref, vt_ref, o_ref = refs
    qb = min(ATTN_QBLOCK, t)
    units = [(slice(hh * HEAD_PAD, (hh + 1) * HEAD_PAD), slice(hh * MLA_D_V, (hh + 1) * MLA_D_V),
              slice(b * qb, (b + 1) * qb)) for b in range(t // qb) for hh in range(hp)]
    qs = [q_ref[rows, sl] for sl, _, rows in units]
    ss = [_dot_nt(k_ref[:, sl], q) for (sl, _, _), q in zip(units, qs)]
    ms = [jnp.max(s, axis=0, keepdims=True) for s in ss]
    if latent:
        scs = [_dot_nt(kc_ref[:, sl], q) for (sl, _, _), q in zip(units, qs)]
        ms = [jnp.maximum(m, jnp.max(sc, axis=0, keepdims=True)) for m, sc in zip(ms, scs)]
    es = [jnp.exp2(s - m) for s, m in zip(ss, ms)]
    dens = [jnp.sum(e, axis=0, keepdims=True) for e in es]
    outs = [_dot(vt_ref[vs, :], e.astype(BF16)) for (_, vs, _), e in zip(units, es)]
    if latent:
        ecs = [jnp.exp2(sc - m) for sc, m in zip(scs, ms)]
        dens = [den + jnp.sum(ec, axis=0, keepdims=True) for den, ec in zip(dens, ecs)]
        outs = [o + _dot(vtc_ref[vs, :], ec.astype(BF16)) for (_, vs, _), o, ec in zip(units, outs, ecs)]
    outs = [o / den for o, den in zip(outs, dens)]
    for u in range(0, len(units), 2):
        (_, vs0, rows), (_, vs1, _) = units[u], units[u + 1]
        pair = jnp.concatenate([outs[u], outs[u + 1]], axis=0)
        o_ref[rows, vs0.start:vs1.stop] = pair.T.astype(BF16)


def _attention(q, kcat, vt, cache, *, latent):
    t = DEC_SEQ if latent else SEQ
    hp = 2 if latent else MLA_HEADS
    n_seq = DEC_BATCH if latent else BATCH
    row0 = (N_CTX // t) if latent else 0
    row = pl.BlockSpec((t, hp * HEAD_PAD), lambda s, h: (row0 + s, h))
    col = pl.BlockSpec((hp * MLA_D_V, t), lambda s, h: (h, row0 + s))
    in_specs = [row, row, col]
    args = [q, kcat, vt]
    if latent:
        in_specs += [pl.BlockSpec((PAST_LEN, hp * HEAD_PAD), lambda s, h: (s, h)),
                     pl.BlockSpec((hp * MLA_D_V, PAST_LEN), lambda s, h: (h, s))]
        args += list(cache)
    return pl.pallas_call(
        functools.partial(_attn_kernel, t=t, hp=hp, latent=latent),
        grid=(n_seq, MLA_HEADS // hp),
        in_specs=in_specs,
        out_specs=pl.BlockSpec((t, hp * MLA_D_V), lambda s, h: (s, h)),
        out_shape=jax.ShapeDtypeStruct((n_seq * t, MLA_V_W), BF16),
        compiler_params=_params(("arbitrary", "arbitrary")),
        name="attention_lat" if latent else "attention_ctx",
    )(*args)


def _merge_kernel(x_ref, mod_ref, a_ref, rc_ref, rl_ref, mc_ref, ml_ref, sig_ref, wc_ref, wr_ref, wm_ref, wo_ref,
                  g_ref, b_ref, o_ref, wc_bf, wr_bf, wm_bf, wo_bf, *, tm):
    i = pl.program_id(0)
    is_ctx = i < N_WSTEPS + N_CTX // tm

    @pl.when(i < N_WSTEPS)
    def _():
        _stage_chunk(i, wc_ref, wc_bf)
        _stage_chunk(i, wr_ref, wr_bf)
        _stage_chunk(i, wo_ref, wo_bf)
        _stage_chunk(i, wm_ref, wm_bf)

    @pl.when(i >= N_WSTEPS)
    def _():
        x = x_ref[...]
        r = jnp.where(is_ctx, rc_ref[...], rl_ref[...])
        m = jnp.where(is_ctx, mc_ref[...], ml_ref[...])
        merged = sig_ref[:, 0:D_MODEL] * _dot(a_ref[...], wc_bf[...])
        merged = merged + sig_ref[:, D_MODEL:2 * D_MODEL] * _dot(r, wr_bf[...])
        merged = merged + sig_ref[:, 2 * D_MODEL:] * _dot(m, wm_bf[...])
        y = _dot(merged.astype(BF16), wo_bf[...])
        z = DEEPNORM_ALPHA * x + mod_ref[5:6, :] * y
        o_ref[...] = _norm_rows(z) * g_ref[...] + b_ref[...]


def _merge(x, mods_all, a, r_pair, m_pair, sig, wc, wr, wm, wo, ln_g, ln_b, *, layer, tm=512):
    def row(w):
        return _row_spec(tm, w, N_WSTEPS)

    def pair(w):
        return [_ctx_row_spec(tm, w, N_WSTEPS), _lat_row_spec(tm, w, N_WSTEPS)]

    ln_index = layer * 3 + 1
    return pl.pallas_call(
        functools.partial(_merge_kernel, tm=tm),
        grid=(N_WSTEPS + N_TOK // tm,),
        in_specs=[row(D_MODEL), _mod_spec(tm, layer, N_WSTEPS), row(CONV_DIM)] + pair(RET_HEADS * RET_DV)
        + pair(MLA_V_W) + [row(GATE_W), _wchunk_spec(wc, layer), _wchunk_spec(wr, layer), _wchunk_spec(wm, layer),
                         _wchunk_spec(wo, layer), _layer_row_spec(ln_g, ln_index), _layer_row_spec(ln_b, ln_index)],
        out_specs=row(D_MODEL),
        out_shape=jax.ShapeDtypeStruct((N_TOK, D_MODEL), F32),
        scratch_shapes=[pltpu.VMEM((CONV_DIM, D_MODEL), BF16), pltpu.VMEM((RET_HEADS * RET_DV, D_MODEL), BF16),
                        pltpu.VMEM((MLA_V_W, D_MODEL), BF16), pltpu.VMEM((D_MODEL, D_MODEL), BF16)],
        compiler_params=_params(("arbitrary",)),
        name="merge",
    )(x, mods_all, a, *r_pair, *m_pair, sig, wc, wr, wm, wo, ln_g, ln_b)


def _rope_tables():
    rows = DEC_SEQ // GRID_W
    row_id = jnp.repeat(jnp.arange(rows, dtype=F32), GRID_W)
    col_id = jnp.tile(jnp.arange(GRID_W, dtype=F32), rows)
    inv_freq = ROPE_BASE ** (-jnp.arange(ROPE_AXIS_HALF, dtype=F32) / ROPE_AXIS_HALF)
    ang = jnp.stack([row_id[:, None] * inv_freq, col_id[:, None] * inv_freq], axis=1)
    cos = jnp.cos(ang)
    sin = jnp.sin(ang)
    cos32 = jnp.stack([cos, cos], axis=2).reshape(DEC_SEQ, MLA_D_ROPE)
    sin32 = jnp.stack([-sin, sin], axis=2).reshape(DEC_SEQ, MLA_D_ROPE)
    tail = HEAD_PAD - ROPE_LANE0 - MLA_D_ROPE
    cos_t = jnp.concatenate([jnp.ones((DEC_SEQ, ROPE_LANE0), F32), cos32, jnp.ones((DEC_SEQ, tail), F32)], axis=1)
    sin_t = jnp.pad(sin32, ((0, 0), (ROPE_LANE0, tail)))
    cos_t = jnp.concatenate([jnp.ones((DEC_SEQ, HEAD_PAD), F32), cos_t], axis=0)
    sin_t = jnp.concatenate([jnp.zeros((DEC_SEQ, HEAD_PAD), F32), sin_t], axis=0)
    return cos_t, sin_t


_ROPE_SWAP = np.arange(MLA_D_ROPE) ^ ROPE_AXIS_HALF


def _head_pad_cols(w, width):
    k = w.shape[0]
    w = w.reshape(k, MLA_HEADS, width)
    return jnp.pad(w, ((0, 0), (0, 0), (0, HEAD_PAD - width))).reshape(k, MLA_W)


def kernel(x_prompt, x_sample, cache_mla_ckv, cache_mla_krope, state_ret_fwd, state_ret_bwd, c, c_ctx, ada_w, ada_b, ffn1_w_in, ffn1_w_out, ffn2_w_in, ffn2_w_out, post_ln_g, post_ln_b, mix_w_in, conv_w_dw, conv_b_dw, conv_ln_g, conv_ln_b, conv_w_out, ret_decay_fwd, ret_decay_bwd, ret_w_out, mla_q_norm, mla_w_uq, mla_kv_norm, mla_w_ukv, mla_w_out, mix_w_o):
    assert DEPTH == 2
    cvec = jnp.concatenate([c_ctx[None, :], c, jnp.zeros((N_MOD_ROWS - 1 - DEC_BATCH, D_MODEL), F32)], axis=0)
    mods_all = _ada_mods(cvec, ada_w, ada_b).reshape(DEPTH, N_MOD_ROWS, N_MODS, D_MODEL)
    cos_t, sin_t = _rope_tables()

    w_ukv = mla_w_ukv.reshape(DEPTH, MLA_KV_LORA, MLA_HEADS, MLA_D_NOPE + MLA_D_V)
    wk_all = _head_pad_cols(w_ukv[..., :MLA_D_NOPE].reshape(DEPTH * MLA_KV_LORA, -1), MLA_D_NOPE)
    wk_all = wk_all.reshape(DEPTH, MLA_KV_LORA, MLA_W).astype(BF16)
    wvt_all = jnp.swapaxes(w_ukv[..., MLA_D_NOPE:].reshape(DEPTH, MLA_KV_LORA, MLA_V_W), 1, 2).astype(BF16)
    kr_tail = HEAD_PAD - ROPE_LANE0 - MLA_D_ROPE
    cache_kr_pad = jnp.pad(cache_mla_krope, ((0, 0), (0, 0), (0, 0), (ROPE_LANE0, kr_tail)))
    kcat_c, vt_c = _cache_kv(cache_mla_ckv, cache_kr_pad, wk_all, wvt_all)
    log_g = jnp.stack([jax.nn.log_sigmoid(ret_decay_fwd), jax.nn.log_sigmoid(ret_decay_bwd)], axis=1)
    wm_all = mix_w_in[:, :, :MAIN_W].astype(BF16)
    wg_all = mix_w_in[:, :, MAIN_W + MLA_D_ROPE:].astype(BF16)

    ln_g = post_ln_g.reshape(DEPTH * 3, 1, D_MODEL)
    ln_b = post_ln_b.reshape(DEPTH * 3, 1, D_MODEL)
    xs = (x_prompt.reshape(N_CTX, D_MODEL), x_sample.reshape(N_LAT, D_MODEL))
    ctx_carry = None
    state_carry = None
    for l in range(DEPTH):
        last = l == DEPTH - 1
        x = _ffn(xs, mods_all, ffn1_w_in, ffn1_w_out, ln_g, ln_b, layer=l, which=0)[0]

        w_kr = mix_w_in[l, :, MAIN_W:MAIN_W + MLA_D_ROPE]
        wkr = jnp.concatenate([jnp.pad(w_kr, ((0, 0), (ROPE_LANE0, kr_tail))),
                               jnp.pad(w_kr[:, _ROPE_SWAP], ((0, 0), (ROPE_LANE0, kr_tail)))], axis=1).astype(BF16)
        w_uq = mla_w_uq[l].reshape(MLA_Q_LORA, MLA_HEADS, MLA_D_NOPE + MLA_D_ROPE)
        wuq = _head_pad_cols(w_uq.reshape(MLA_Q_LORA, -1), MLA_D_NOPE + MLA_D_ROPE).astype(BF16)
        w_uq_swap = jnp.concatenate([jnp.zeros_like(w_uq[..., :MLA_D_NOPE]), w_uq[..., MLA_D_NOPE:][..., _ROPE_SWAP]], axis=-1)
        wuqs = _head_pad_cols(w_uq_swap.reshape(MLA_Q_LORA, -1), MLA_D_NOPE + MLA_D_ROPE).astype(BF16)
        (glu, rq, rk, rv, rg, q, kcat, vt, sig, ckv, kr) = _proj(
            x, mods_all, cos_t, sin_t, wm_all, wkr, wg_all, wuq, wuqs, wk_all[l], wvt_all[l],
            mla_q_norm[l].reshape(1, -1), mla_kv_norm[l].reshape(1, -1), ctx_carry, layer=l)
        ctx_carry = (ckv, kr)

        a = _conv_branch(glu, conv_w_dw[l], conv_b_dw[l].reshape(1, -1), conv_ln_g[l].reshape(1, -1),
                         conv_ln_b[l].reshape(1, -1))
        r_ctx, sf, sb = _retention(log_g[l], rq, rk, rv, rg, state_carry, latent=False, layer=l)
        state_carry = (sf, sb)
        r_lat = _retention(log_g[l], rq, rk, rv, rg, (state_ret_fwd, state_ret_bwd), latent=True, layer=l)[0]
        m_ctx = _attention(q, kcat, vt, None, latent=False)
        m_lat = _attention(q, kcat, vt, (kcat_c[l], vt_c[l]), latent=True)

        x = _merge(x, mods_all, a, (r_ctx, r_lat), (m_ctx, m_lat), sig, conv_w_out, ret_w_out, mla_w_out, mix_w_o,
                   ln_g, ln_b, layer=l)
        xs = _ffn((x,), mods_all, ffn2_w_in, ffn2_w_out, ln_g, ln_b, layer=l, which=2, split_out=last)

    y_ctx, y_lat = xs
    return (y_ctx.reshape(BATCH, SEQ, D_MODEL), y_lat.reshape(DEC_BATCH, DEC_SEQ, D_MODEL),
            ctx_carry[0], ctx_carry[1], state_carry[0], state_carry[1])
```

```python
import functools
import math

import jax
import jax.numpy as jnp
from jax import lax
from jax.experimental import pallas as pl
from jax.experimental.pallas import tpu as pltpu

F32 = jnp.float32
BF16 = jnp.bfloat16

D_MODEL = 1024
BATCH = 16
SEQ = 256
DEPTH = 2
DEC_BATCH = 4
DEC_SEQ = 1024
PAST_LEN = 256
GRID_W = 64
D_FF = 2816
N_MODS = 9
CONV_DIM = 512
CONV_WIDTH = 31
RET_HEADS = 4
RET_DK = 128
RET_DV = 256
MLA_HEADS = 8
MLA_Q_LORA = 512
MLA_KV_LORA = 256
MLA_D_NOPE = 64
MLA_D_ROPE = 32
MLA_D_V = 64
ROPE_AXIS_HALF = MLA_D_ROPE // 4
ROPE_BASE = 10000.0
DEEPNORM_ALPHA = (2 * DEPTH) ** 0.25
LN_EPS = 1e-5
RMS_EPS = 1e-6

N_CTX = BATCH * SEQ
N_LAT = DEC_BATCH * DEC_SEQ
N_TOK = N_CTX + N_LAT
N_MOD_ROWS = 8
HEAD_PAD = 128
ROPE_LANE0 = MLA_D_NOPE
MLA_W = MLA_HEADS * HEAD_PAD
MLA_V_W = MLA_HEADS * MLA_D_V
MAIN_W = 2 * CONV_DIM + 2 * RET_HEADS * RET_DK + 2 * RET_HEADS * RET_DV + MLA_Q_LORA + MLA_KV_LORA
GATE_W = 3 * D_MODEL
SUBLANES = 8
CONV_HALO = 16
CONV_BLOCK = 256
VMEM_LIMIT = 56 * 1024 * 1024
N_WSTEPS = 8
MXU_COLS = 256
FFN_CHUNKS = ((0, 6 * MXU_COLS), (6 * MXU_COLS, D_FF))
ATTN_QBLOCK = 256
ATTN_QSCALE = (MLA_D_NOPE + MLA_D_ROPE) ** -0.5 * math.log2(math.e)


def _dot(a, b):
    return jnp.dot(a, b, preferred_element_type=F32)


def _dot_nt(a, b):
    return lax.dot_general(a, b, (((1,), (1,)), ((), ())), preferred_element_type=F32)


def _dot_tn(a, b):
    return lax.dot_general(a, b, (((0,), (0,)), ((), ())), preferred_element_type=F32)


def _sigmoid(x):
    return 1.0 / (1.0 + jnp.exp(-x))


def _norm_rows(z):
    mu = jnp.mean(z, axis=-1, keepdims=True)
    zc = z - mu
    var = jnp.mean(zc * zc, axis=-1, keepdims=True)
    return zc * lax.rsqrt(var + LN_EPS)


def _rms_rows(z):
    return z * lax.rsqrt(jnp.mean(z * z, axis=-1, keepdims=True) + RMS_EPS)


def _resident(shape):
    zeros = (0,) * len(shape)
    return pl.BlockSpec(shape, lambda *_: zeros, pipeline_mode=pl.Buffered(1))


def _layer_resident(w, layer):
    zeros = (0,) * (w.ndim - 1)
    return pl.BlockSpec((None,) + w.shape[1:], lambda *_: (layer,) + zeros, pipeline_mode=pl.Buffered(1))


def _tile(i, n_w):
    return jnp.maximum(i - n_w, 0)


def _row_spec(tm, w, n_w):
    return pl.BlockSpec((tm, w), lambda i: (_tile(i, n_w), 0))


def _ctx_row_spec(tm, w, n_w):
    last = N_CTX // tm - 1
    return pl.BlockSpec((tm, w), lambda i: (jnp.minimum(_tile(i, n_w), last), 0))


def _lat_row_spec(tm, w, n_w):
    first = N_CTX // tm
    return pl.BlockSpec((tm, w), lambda i: (jnp.maximum(_tile(i, n_w) - first, 0), 0))


def _mod_spec(tm, layer, n_w):
    n_ctx_tiles = N_CTX // tm
    tiles_per_seq = DEC_SEQ // tm

    def index(i):
        j = _tile(i, n_w)
        return (layer, jnp.where(j < n_ctx_tiles, 0, 1 + (j - n_ctx_tiles) // tiles_per_seq), 0, 0)

    return pl.BlockSpec((None, None, N_MODS, D_MODEL), index)


def _wchunk_spec(w, layer):
    _, rows, cols = w.shape
    return pl.BlockSpec((None, rows // N_WSTEPS, cols), lambda i: (layer, jnp.minimum(i, N_WSTEPS - 1), 0))


def _layer_row_spec(w, index):
    return pl.BlockSpec((None, 1, w.shape[-1]), lambda i: (index, 0, 0), pipeline_mode=pl.Buffered(1))


def _stage_chunk(i, src_ref, dst_ref):
    rows = src_ref.shape[0]
    dst_ref[pl.ds(pl.multiple_of(i * rows, rows), rows), :] = src_ref[...].astype(BF16)


def _params(semantics):
    return pltpu.CompilerParams(dimension_semantics=semantics, vmem_limit_bytes=VMEM_LIMIT)


def _ada_kernel(c_ref, w_ref, b_ref, o_ref):
    c = c_ref[...]
    h = (c * _sigmoid(c)).astype(BF16)
    o_ref[...] = _dot(h, w_ref[...].astype(BF16)) + b_ref[...]


def _ada_mods(cvec, ada_w, ada_b):
    tn = D_MODEL
    n_out = N_MODS * D_MODEL
    return pl.pallas_call(
        _ada_kernel,
        grid=(DEPTH, n_out // tn),
        in_specs=[
            pl.BlockSpec((N_MOD_ROWS, D_MODEL), lambda l, j: (0, 0)),
            pl.BlockSpec((None, D_MODEL, tn), lambda l, j: (l, 0, j)),
            pl.BlockSpec((None, 1, tn), lambda l, j: (l, 0, j)),
        ],
        out_specs=pl.BlockSpec((None, N_MOD_ROWS, tn), lambda l, j: (l, 0, j)),
        out_shape=jax.ShapeDtypeStruct((DEPTH, N_MOD_ROWS, n_out), F32),
        compiler_params=_params(("arbitrary", "arbitrary")),
        name="ada_mods",
    )(cvec, ada_w, ada_b.reshape(DEPTH, 1, n_out))


def _ffn_kernel(*refs, base, tm, n_x, n_out):
    x_refs = refs[:n_x]
    mod_ref, win_ref, wout_ref, g_ref, b_ref = refs[n_x:n_x + 5]
    o_refs = refs[n_x + 5:n_x + 5 + n_out]
    win_bf, wout_bf = refs[n_x + 5 + n_out:]
    i = pl.program_id(0)
    is_ctx = i < N_WSTEPS + N_CTX // tm

    @pl.when(i < N_WSTEPS)
    def _():
        _stage_chunk(i, win_ref, win_bf)
        _stage_chunk(i, wout_ref, wout_bf)

    @pl.when(i >= N_WSTEPS)
    def _():
        if n_x == 2:
            x = jnp.where(is_ctx, x_refs[0][...], x_refs[1][...])
        else:
            x = x_refs[0][...]
        shift = mod_ref[base:base + 1, :]
        scale = mod_ref[base + 1:base + 2, :]
        gate = mod_ref[base + 2:base + 3, :]
        h = (x * (1.0 + scale) + shift).astype(BF16)
        y = None
        for lo, hi in FFN_CHUNKS:
            g = _dot(h, win_bf[:, lo:hi])
            u = _dot(h, win_bf[:, D_FF + lo:D_FF + hi])
            a = (g * _sigmoid(g) * u).astype(BF16)
            yc = _dot(a, wout_bf[lo:hi, :])
            y = yc if y is None else y + yc
        z = DEEPNORM_ALPHA * x + 0.5 * gate * y
        res = _norm_rows(z) * g_ref[...] + b_ref[...]
        if n_out == 2:
            @pl.when(is_ctx)
            def _():
                o_refs[0][...] = res

            @pl.when(jnp.logical_not(is_ctx))
            def _():
                o_refs[1][...] = res
        else:
            o_refs[0][...] = res


def _ffn(xs, mods_all, w_in, w_out, ln_g, ln_b, *, layer, which, split_out=False, tm=512):
    row = _row_spec(tm, D_MODEL, N_WSTEPS)
    pair = [_ctx_row_spec(tm, D_MODEL, N_WSTEPS), _lat_row_spec(tm, D_MODEL, N_WSTEPS)]
    ln_index = layer * 3 + which
    if split_out:
        out_specs = pair
        out_shape = [jax.ShapeDtypeStruct((N_CTX, D_MODEL), F32), jax.ShapeDtypeStruct((N_LAT, D_MODEL), F32)]
    else:
        out_specs = [row]
        out_shape = [jax.ShapeDtypeStruct((N_TOK, D_MODEL), F32)]
    return pl.pallas_call(
        functools.partial(_ffn_kernel, base=3 * which, tm=tm, n_x=len(xs), n_out=len(out_specs)),
        grid=(N_WSTEPS + N_TOK // tm,),
        in_specs=(pair if len(xs) == 2 else [row]) + [
            _mod_spec(tm, layer, N_WSTEPS), _wchunk_spec(w_in, layer), _wchunk_spec(w_out, layer),
            _layer_row_spec(ln_g, ln_index), _layer_row_spec(ln_b, ln_index)],
        out_specs=out_specs,
        out_shape=out_shape,
        scratch_shapes=[pltpu.VMEM((D_MODEL, 2 * D_FF), BF16), pltpu.VMEM((D_FF, D_MODEL), BF16)],
        compiler_params=_params(("arbitrary",)),
        name="ffn",
    )(*xs, mods_all, w_in, w_out, ln_g, ln_b)


def _proj_kernel(*refs, tm, carry):
    (x_ref, mod_ref, cos_ref, sin_lo_ref, sin_hi_ref, wm_ref, wkr_ref, wg_ref, wuq_ref, wk_ref, wvt_ref, gq_ref,
     gkv_ref) = refs[:13]
    refs = refs[13:]
    if carry:
        ckv_prev_ref, kr_prev_ref = refs[:2]
        refs = refs[2:]
    (glu_ref, rq_ref, rk_ref, rv_ref, rg_ref, q_ref, kcat_ref, vt_ref, sig_ref, ckv_ref, kr_ref) = refs
    is_ctx = pl.program_id(0) < N_CTX // tm

    x = x_ref[...]
    u = (x * (1.0 + mod_ref[4:5, :]) + mod_ref[3:4, :]).astype(BF16)
    widths = (CONV_DIM, CONV_DIM, RET_HEADS * RET_DK, RET_HEADS * RET_DK, RET_HEADS * RET_DV, RET_HEADS * RET_DV,
              MLA_Q_LORA, MLA_KV_LORA)
    starts = [sum(widths[:n]) for n in range(len(widths))]

    def proj(n):
        return _dot(u, wm_ref[:, starts[n]:starts[n] + widths[n]])

    mq = proj(6)
    mkv = proj(7)
    cos = cos_ref[...]
    sin_lo = sin_lo_ref[...]
    sin_hi = sin_hi_ref[...]

    def rotary(v):
        up = pltpu.roll(v, HEAD_PAD - ROPE_AXIS_HALF, 1)
        down = pltpu.roll(v, ROPE_AXIS_HALF, 1)
        return v * cos + up * sin_lo + down * sin_hi

    qn = (_rms_rows(mq) * gq_ref[...]).astype(BF16)
    qm = _dot(qn, wuq_ref[...])
    ckv = _rms_rows(mkv) * gkv_ref[...]
    ckvb = ckv.astype(BF16)
    kn = _dot(ckvb, wk_ref[...])
    vt_ref[...] = _dot_nt(wvt_ref[...], ckvb).astype(BF16)
    kr = _dot(u, wkr_ref[...])
    kr_rot = rotary(kr)
    for h in range(MLA_HEADS):
        sl = slice(h * HEAD_PAD, (h + 1) * HEAD_PAD)
        q_ref[:, sl] = (rotary(qm[:, sl]) * ATTN_QSCALE).astype(BF16)
        kcat_ref[:, sl] = (kn[:, sl] + kr_rot).astype(BF16)

    glu_ref[...] = proj(0) * _sigmoid(proj(1))
    sig_ref[...] = _sigmoid(_dot(u, wg_ref[...])).astype(BF16)
    rg = proj(5)
    rg_ref[...] = (rg * _sigmoid(rg)).astype(BF16)
    rk_ref[...] = (proj(3) * (RET_DK ** -0.5)).astype(BF16)
    rv_ref[...] = proj(4).astype(BF16)
    rq_ref[...] = proj(2).astype(BF16)

    @pl.when(is_ctx)
    def _():
        seqs = tm // SEQ
        ckv3 = ckv.reshape(seqs, SEQ, MLA_KV_LORA)
        kr3 = kr[:, ROPE_LANE0:ROPE_LANE0 + MLA_D_ROPE].reshape(seqs, SEQ, MLA_D_ROPE)
        if carry:
            ckv_ref[:, 0] = ckv_prev_ref[...]
            kr_ref[:, 0] = kr_prev_ref[...]
            ckv_ref[:, 1] = ckv3
            kr_ref[:, 1] = kr3
        else:
            ckv_ref[...] = ckv3
            kr_ref[...] = kr3


def _proj(x, mods_all, rope_tabs, wm_all, wkr, wg_all, wuq, wk, wvt, gq, gkv, carry, *, layer, tm=256):
    n_ctx_tiles = N_CTX // tm
    tiles_per_seq = DEC_SEQ // tm
    seqs = tm // SEQ

    def rope_index(i):
        return (jnp.where(i < n_ctx_tiles, 0, tiles_per_seq + (i - n_ctx_tiles) % tiles_per_seq), 0)

    def row(w):
        return _row_spec(tm, w, 0)

    def out(w, dt):
        return jax.ShapeDtypeStruct((N_TOK, w), dt)

    def ctx_seq_spec(*tail):
        zeros = (0,) * len(tail)
        return pl.BlockSpec((seqs,) + tail, lambda i: (jnp.minimum(i, n_ctx_tiles - 1),) + zeros)

    rope = pl.BlockSpec((tm, HEAD_PAD), rope_index)
    in_specs = [row(D_MODEL), _mod_spec(tm, layer, 0), rope, rope, rope,
                _layer_resident(wm_all, layer), _resident(wkr.shape), _layer_resident(wg_all, layer),
                _resident(wuq.shape), _resident(wk.shape), _resident(wvt.shape),
                _resident(gq.shape), _resident(gkv.shape)]
    args = [x, mods_all, *rope_tabs, wm_all, wkr, wg_all, wuq, wk, wvt, gq, gkv]
    if carry is None:
        ctx_specs = [ctx_seq_spec(SEQ, MLA_KV_LORA), ctx_seq_spec(SEQ, MLA_D_ROPE)]
        ctx_shapes = [jax.ShapeDtypeStruct((BATCH, SEQ, MLA_KV_LORA), F32),
                      jax.ShapeDtypeStruct((BATCH, SEQ, MLA_D_ROPE), F32)]
    else:
        in_specs += [ctx_seq_spec(SEQ, MLA_KV_LORA), ctx_seq_spec(SEQ, MLA_D_ROPE)]
        args += list(carry)
        ctx_specs = [ctx_seq_spec(DEPTH, SEQ, MLA_KV_LORA), ctx_seq_spec(DEPTH, SEQ, MLA_D_ROPE)]
        ctx_shapes = [jax.ShapeDtypeStruct((BATCH, DEPTH, SEQ, MLA_KV_LORA), F32),
                      jax.ShapeDtypeStruct((BATCH, DEPTH, SEQ, MLA_D_ROPE), F32)]
    return pl.pallas_call(
        functools.partial(_proj_kernel, tm=tm, carry=carry is not None),
        grid=(N_TOK // tm,),
        in_specs=in_specs,
        out_specs=[row(CONV_DIM), row(RET_HEADS * RET_DK), row(RET_HEADS * RET_DK), row(RET_HEADS * RET_DV),
                   row(RET_HEADS * RET_DV), row(MLA_W), row(MLA_W),
                   pl.BlockSpec((MLA_V_W, tm), lambda i: (0, i)), row(GATE_W)] + ctx_specs,
        out_shape=[out(CONV_DIM, F32), out(RET_HEADS * RET_DK, BF16), out(RET_HEADS * RET_DK, BF16),
                   out(RET_HEADS * RET_DV, BF16), out(RET_HEADS * RET_DV, BF16), out(MLA_W, BF16),
                   out(MLA_W, BF16), jax.ShapeDtypeStruct((MLA_V_W, N_TOK), BF16), out(GATE_W, BF16)] + ctx_shapes,
        compiler_params=_params(("arbitrary",)),
        name="mix_proj",
    )(*args)


def _cache_kv_kernel(ckv_ref, kr_ref, wk_ref, wvt_ref, kcat_ref, vt_ref):
    ckvb = ckv_ref[...].astype(BF16)
    kn = _dot(ckvb, wk_ref[...])
    vt_ref[...] = _dot_nt(wvt_ref[...], ckvb).astype(BF16)
    kr = kr_ref[...]
    for h in range(MLA_HEADS):
        sl = slice(h * HEAD_PAD, (h + 1) * HEAD_PAD)
        kcat_ref[:, sl] = (kn[:, sl] + kr).astype(BF16)


def _cache_kv(cache_ckv, cache_kr_pad, wk, wvt):
    n = DEC_BATCH * PAST_LEN
    return pl.pallas_call(
        _cache_kv_kernel,
        grid=(DEPTH, DEC_BATCH),
        in_specs=[pl.BlockSpec((None, None, PAST_LEN, MLA_KV_LORA), lambda l, b: (b, l, 0, 0)),
                  pl.BlockSpec((None, None, PAST_LEN, HEAD_PAD), lambda l, b: (b, l, 0, 0)),
                  pl.BlockSpec((None, MLA_KV_LORA, MLA_W), lambda l, b: (l, 0, 0)),
                  pl.BlockSpec((None, MLA_V_W, MLA_KV_LORA), lambda l, b: (l, 0, 0))],
        out_specs=[pl.BlockSpec((None, PAST_LEN, MLA_W), lambda l, b: (l, b, 0)),
                   pl.BlockSpec((None, MLA_V_W, PAST_LEN), lambda l, b: (l, 0, b))],
        out_shape=[jax.ShapeDtypeStruct((DEPTH, n, MLA_W), BF16), jax.ShapeDtypeStruct((DEPTH, MLA_V_W, n), BF16)],
        compiler_params=_params(("arbitrary", "arbitrary")),
        name="cache_kv",
    )(cache_ckv, cache_kr_pad, wk, wvt)


def _conv_kernel(prev_ref, cur_ref, next_ref, w_ref, b_ref, g_ref, beta_ref, o_ref, pad_ref, shift_ref, acc_ref):
    i = pl.program_id(0)
    n_ctx_blocks = N_CTX // CONV_BLOCK
    blocks_per_seq = DEC_SEQ // CONV_BLOCK
    pos = (i - n_ctx_blocks) % blocks_per_seq
    latent = i >= n_ctx_blocks
    has_prev = jnp.logical_and(latent, pos != 0)
    has_next = jnp.logical_and(latent, pos != blocks_per_seq - 1)
    zero = jnp.zeros((CONV_HALO, CONV_DIM), F32)
    pad_ref[0:CONV_HALO, :] = jnp.where(has_prev, prev_ref[...], zero)
    pad_ref[CONV_HALO:CONV_HALO + CONV_BLOCK, :] = cur_ref[...]
    pad_ref[CONV_HALO + CONV_BLOCK:, :] = jnp.where(has_next, next_ref[...], zero)
    span = shift_ref.shape[1]
    for ph in range(SUBLANES):
        shift_ref[ph] = pad_ref[ph:ph + span, :]
    rows = 64
    lanes = 128
    first = CONV_HALO - CONV_WIDTH // 2
    for c in range(CONV_DIM // lanes):
        cs = slice(c * lanes, (c + 1) * lanes)
        for r in range(CONV_BLOCK // rows):
            acc = jnp.broadcast_to(b_ref[:, cs], (rows, lanes))
            for j in range(CONV_WIDTH):
                ph = (first + j) % SUBLANES
                start = r * rows + (first + j) - ph
                acc = acc + w_ref[j:j + 1, cs] * shift_ref[ph, start:start + rows, cs]
            acc_ref[r * rows:(r + 1) * rows, cs] = acc
    y = _norm_rows(acc_ref[...]) * g_ref[...] + beta_ref[...]
    o_ref[...] = (y * _sigmoid(y)).astype(BF16)


def _conv_branch(glu, w_dw, b_dw, ln_g, ln_b):
    per = CONV_BLOCK // CONV_HALO
    n_halo = N_TOK // CONV_HALO
    span = CONV_BLOCK + 2 * CONV_HALO - SUBLANES
    return pl.pallas_call(
        _conv_kernel,
        grid=(N_TOK // CONV_BLOCK,),
        in_specs=[pl.BlockSpec((CONV_HALO, CONV_DIM), lambda i: (jnp.maximum(i * per - 1, 0), 0)),
                  pl.BlockSpec((CONV_BLOCK, CONV_DIM), lambda i: (i, 0)),
                  pl.BlockSpec((CONV_HALO, CONV_DIM), lambda i: (jnp.minimum((i + 1) * per, n_halo - 1), 0)),
                  _resident((CONV_WIDTH, CONV_DIM)), _resident((1, CONV_DIM)), _resident((1, CONV_DIM)),
                  _resident((1, CONV_DIM))],
        out_specs=pl.BlockSpec((CONV_BLOCK, CONV_DIM), lambda i: (i, 0)),
        out_shape=jax.ShapeDtypeStruct((N_TOK, CONV_DIM), BF16),
        scratch_shapes=[pltpu.VMEM((CONV_BLOCK + 2 * CONV_HALO, CONV_DIM), F32),
                        pltpu.VMEM((SUBLANES, span, CONV_DIM), F32),
                        pltpu.VMEM((CONV_BLOCK, CONV_DIM), F32)],
        compiler_params=_params(("arbitrary",)),
        name="conv_branch",
    )(glu, glu, glu, w_dw, b_dw, ln_g, ln_b)


def _ret_kernel(*refs, t, hp, latent, carry):
    if latent:
        lg_ref, q_ref, k_ref, v_ref, g_ref, s0f_ref, s0b_ref, o_ref, d_ref = refs
    elif carry:
        lg_ref, q_ref, k_ref, v_ref, g_ref, sf_prev_ref, sb_prev_ref, o_ref, sf_ref, sb_ref, d_ref = refs
    else:
        lg_ref, q_ref, k_ref, v_ref, g_ref, o_ref, sf_ref, sb_ref, d_ref = refs
    hblk = pl.program_id(0)

    @pl.when(pl.program_id(1) == 0)
    def _():
        diff = (lax.broadcasted_iota(jnp.int32, (t, t), 0) - lax.broadcasted_iota(jnp.int32, (t, t), 1)).astype(F32)
        for hh in range(hp):
            lgf = lg_ref[0, hblk * hp + hh]
            lgb = lg_ref[1, hblk * hp + hh]
            d_ref[hh] = jnp.exp(jnp.where(diff >= 0, diff * lgf, -diff * lgb))

    if carry:
        sf_ref[0] = sf_prev_ref[...]
        sb_ref[0] = sb_prev_ref[...]
    pos = lax.broadcasted_iota(jnp.int32, (t, 1), 0).astype(F32)
    for hh in range(hp):
        lgf = lg_ref[0, hblk * hp + hh]
        lgb = lg_ref[1, hblk * hp + hh]
        q = q_ref[:, hh * RET_DK:(hh + 1) * RET_DK]
        k = k_ref[:, hh * RET_DK:(hh + 1) * RET_DK]
        v = v_ref[:, hh * RET_DV:(hh + 1) * RET_DV]
        p = (_dot_nt(q, k) * d_ref[hh]).astype(BF16)
        o = _dot(p, v)
        if latent:
            o = o + jnp.exp((pos + 1.0) * lgf) * _dot(q, s0f_ref[hh].astype(BF16))
            o = o + jnp.exp((t - pos) * lgb) * _dot(q, s0b_ref[hh].astype(BF16))
        else:
            kf = k.astype(F32)
            sf = _dot_tn((kf * jnp.exp((t - 1.0 - pos) * lgf)).astype(BF16), v)
            sb = _dot_tn((kf * jnp.exp(pos * lgb)).astype(BF16), v)
            if carry:
                sf_ref[1, hh] = sf
                sb_ref[1, hh] = sb
            else:
                sf_ref[hh] = sf
                sb_ref[hh] = sb
        o_ref[:, hh * RET_DV:(hh + 1) * RET_DV] = (g_ref[:, hh * RET_DV:(hh + 1) * RET_DV] * _norm_rows(o)).astype(BF16)


def _retention(log_g, rq, rk, rv, rg, states, *, latent, layer):
    t = DEC_SEQ if latent else SEQ
    hp = 1 if latent else RET_HEADS
    n_seq = DEC_BATCH if latent else BATCH
    row0 = (N_CTX // t) if latent else 0

    def row(w):
        return pl.BlockSpec((t, hp * w), lambda h, s: (row0 + s, h))

    smem = pl.BlockSpec(memory_space=pltpu.SMEM)
    out_shape = [jax.ShapeDtypeStruct((n_seq * t, RET_HEADS * RET_DV), BF16)]
    out_specs = [pl.BlockSpec((t, hp * RET_DV), lambda h, s: (s, h))]
    in_specs = [smem, row(RET_DK), row(RET_DK), row(RET_DV), row(RET_DV)]
    args = [log_g, rq, rk, rv, rg]
    carry = False
    if latent:
        st = pl.BlockSpec((None, None, hp, RET_DK, RET_DV), lambda h, s: (s, layer, h, 0, 0))
        in_specs += [st, st]
        args += list(states)
    else:
        st = pl.BlockSpec((None, hp, RET_DK, RET_DV), lambda h, s: (s, h, 0, 0))
        if states is None:
            out_specs += [st, st]
            out_shape += [jax.ShapeDtypeStruct((BATCH, RET_HEADS, RET_DK, RET_DV), F32)] * 2
        else:
            carry = True
            in_specs += [st, st]
            args += list(states)
            st2 = pl.BlockSpec((None, DEPTH, hp, RET_DK, RET_DV), lambda h, s: (s, 0, h, 0, 0))
            out_specs += [st2, st2]
            out_shape += [jax.ShapeDtypeStruct((BATCH, DEPTH, RET_HEADS, RET_DK, RET_DV), F32)] * 2

    return pl.pallas_call(
        functools.partial(_ret_kernel, t=t, hp=hp, latent=latent, carry=carry),
        grid=(RET_HEADS // hp, n_seq),
        in_specs=in_specs,
        out_specs=out_specs,
        out_shape=out_shape,
        scratch_shapes=[pltpu.VMEM((hp, t, t), F32)],
        compiler_params=_params(("arbitrary", "arbitrary")),
        name="retention_lat" if latent else "retention_ctx",
    )(*args)


def _attn_kernel(*refs, t, hp, latent):
    if latent:
        q_ref, k_ref, vt_ref, kc_ref, vtc_ref, o_ref = refs
    else:
        q_ref, k_ref, vt_ref, o_ref = refs
    qb = min(ATTN_QBLOCK, t)
    units = [(slice(hh * HEAD_PAD, (hh + 1) * HEAD_PAD), slice(hh * MLA_D_V, (hh + 1) * MLA_D_V),
              slice(b * qb, (b + 1) * qb)) for b in range(t // qb) for hh in range(hp)]
    def scores(unit):
        sl, _, rows = unit
        q = q_ref[rows, sl]
        s = [_dot_nt(k_ref[:, sl], q)]
        if latent:
            s.append(_dot_nt(kc_ref[:, sl], q))
        return s

    def softmax(s):
        m = functools.reduce(jnp.maximum, [jnp.max(x, axis=0, keepdims=True) for x in s])
        e = [jnp.exp2(x - m) for x in s]
        den = functools.reduce(jnp.add, [jnp.sum(x, axis=0, keepdims=True) for x in e])
        return [x.astype(BF16) for x in e], den

    def values(unit, e, den):
        _, vs, _ = unit
        o = _dot(vt_ref[vs, :], e[0])
        if latent:
            o = o + _dot(vtc_ref[vs, :], e[1])
        return o / den

    pairs = [units[u:u + 2] for u in range(0, len(units), 2)]
    s_next = [scores(u) for u in pairs[0]]
    sm_prev = None
    for g in range(len(pairs) + 1):
        s_cur = s_next
        if g + 1 < len(pairs):
            s_next = [scores(u) for u in pairs[g + 1]]
        sm_cur = [softmax(s) for s in s_cur] if g < len(pairs) else None
        if sm_prev is not None:
            (_, vs0, rows), (_, vs1, _) = pairs[g - 1]
            outs = [values(u, e, den) for u, (e, den) in zip(pairs[g - 1], sm_prev)]
            o_ref[rows, vs0.start:vs1.stop] = jnp.concatenate(outs, axis=0).T.astype(BF16)
        sm_prev = sm_cur


def _attention(q, kcat, vt, cache, *, latent):
    t = DEC_SEQ if latent else SEQ
    hp = 2 if latent else MLA_HEADS
    n_seq = DEC_BATCH if latent else BATCH
    row0 = (N_CTX // t) if latent else 0
    row = pl.BlockSpec((t, hp * HEAD_PAD), lambda s, h: (row0 + s, h))
    col = pl.BlockSpec((hp * MLA_D_V, t), lambda s, h: (h, row0 + s))
    in_specs = [row, row, col]
    args = [q, kcat, vt]
    if latent:
        in_specs += [pl.BlockSpec((PAST_LEN, hp * HEAD_PAD), lambda s, h: (s, h)),
                     pl.BlockSpec((hp * MLA_D_V, PAST_LEN), lambda s, h: (h, s))]
        args += list(cache)
    return pl.pallas_call(
        functools.partial(_attn_kernel, t=t, hp=hp, latent=latent),
        grid=(n_seq, MLA_HEADS // hp),
        in_specs=in_specs,
        out_specs=pl.BlockSpec((t, hp * MLA_D_V), lambda s, h: (s, h)),
        out_shape=jax.ShapeDtypeStruct((n_seq * t, MLA_V_W), BF16),
        compiler_params=_params(("arbitrary", "arbitrary")),
        name="attention_lat" if latent else "attention_ctx",
    )(*args)


def _merge_kernel(x_ref, mod_ref, a_ref, rc_ref, rl_ref, mc_ref, ml_ref, sig_ref, wc_ref, wr_ref, wm_ref, wo_ref,
                  g_ref, b_ref, o_ref, wc_bf, wr_bf, wm_bf, wo_bf, *, tm):
    i = pl.program_id(0)
    is_ctx = i < N_WSTEPS + N_CTX // tm

    @pl.when(i < N_WSTEPS)
    def _():
        _stage_chunk(i, wc_ref, wc_bf)
        _stage_chunk(i, wr_ref, wr_bf)
        _stage_chunk(i, wo_ref, wo_bf)
        _stage_chunk(i, wm_ref, wm_bf)

    @pl.when(i >= N_WSTEPS)
    def _():
        x = x_ref[...]
        r = jnp.where(is_ctx, rc_ref[...], rl_ref[...])
        m = jnp.where(is_ctx, mc_ref[...], ml_ref[...])
        merged = sig_ref[:, 0:D_MODEL] * _dot(a_ref[...], wc_bf[...])
        merged = merged + sig_ref[:, D_MODEL:2 * D_MODEL] * _dot(r, wr_bf[...])
        merged = merged + sig_ref[:, 2 * D_MODEL:] * _dot(m, wm_bf[...])
        y = _dot(merged.astype(BF16), wo_bf[...])
        z = DEEPNORM_ALPHA * x + mod_ref[5:6, :] * y
        o_ref[...] = _norm_rows(z) * g_ref[...] + b_ref[...]


def _merge(x, mods_all, a, r_pair, m_pair, sig, wc, wr, wm, wo, ln_g, ln_b, *, layer, tm=512):
    def row(w):
        return _row_spec(tm, w, N_WSTEPS)

    def pair(w):
        return [_ctx_row_spec(tm, w, N_WSTEPS), _lat_row_spec(tm, w, N_WSTEPS)]

    ln_index = layer * 3 + 1
    return pl.pallas_call(
        functools.partial(_merge_kernel, tm=tm),
        grid=(N_WSTEPS + N_TOK // tm,),
        in_specs=[row(D_MODEL), _mod_spec(tm, layer, N_WSTEPS), row(CONV_DIM)] + pair(RET_HEADS * RET_DV)
        + pair(MLA_V_W) + [row(GATE_W), _wchunk_spec(wc, layer), _wchunk_spec(wr, layer), _wchunk_spec(wm, layer),
                         _wchunk_spec(wo, layer), _layer_row_spec(ln_g, ln_index), _layer_row_spec(ln_b, ln_index)],
        out_specs=row(D_MODEL),
        out_shape=jax.ShapeDtypeStruct((N_TOK, D_MODEL), F32),
        scratch_shapes=[pltpu.VMEM((CONV_DIM, D_MODEL), BF16), pltpu.VMEM((RET_HEADS * RET_DV, D_MODEL), BF16),
                        pltpu.VMEM((MLA_V_W, D_MODEL), BF16), pltpu.VMEM((D_MODEL, D_MODEL), BF16)],
        compiler_params=_params(("arbitrary",)),
        name="merge",
    )(x, mods_all, a, *r_pair, *m_pair, sig, wc, wr, wm, wo, ln_g, ln_b)


def _rope_tables():
    rows = DEC_SEQ // GRID_W
    row_id = jnp.repeat(jnp.arange(rows, dtype=F32), GRID_W)
    col_id = jnp.tile(jnp.arange(GRID_W, dtype=F32), rows)
    inv_freq = ROPE_BASE ** (-jnp.arange(ROPE_AXIS_HALF, dtype=F32) / ROPE_AXIS_HALF)
    ang = jnp.stack([row_id[:, None] * inv_freq, col_id[:, None] * inv_freq], axis=1)
    cos = jnp.cos(ang)
    sin = jnp.sin(ang)
    cos32 = jnp.stack([cos, cos], axis=2).reshape(DEC_SEQ, MLA_D_ROPE)
    zero = jnp.zeros_like(sin)
    sin_lo32 = jnp.stack([-sin, zero], axis=2).reshape(DEC_SEQ, MLA_D_ROPE)
    sin_hi32 = jnp.stack([zero, sin], axis=2).reshape(DEC_SEQ, MLA_D_ROPE)
    tail = HEAD_PAD - ROPE_LANE0 - MLA_D_ROPE
    cos_t = jnp.concatenate([jnp.ones((DEC_SEQ, ROPE_LANE0), F32), cos32, jnp.ones((DEC_SEQ, tail), F32)], axis=1)
    cos_t = jnp.concatenate([jnp.ones((DEC_SEQ, HEAD_PAD), F32), cos_t], axis=0)

    def sin_table(s32):
        return jnp.pad(s32, ((DEC_SEQ, 0), (ROPE_LANE0, tail)))

    return cos_t, sin_table(sin_lo32), sin_table(sin_hi32)


def _head_pad_cols(w, width):
    k = w.shape[0]
    w = w.reshape(k, MLA_HEADS, width)
    return jnp.pad(w, ((0, 0), (0, 0), (0, HEAD_PAD - width))).reshape(k, MLA_W)


def kernel(x_prompt, x_sample, cache_mla_ckv, cache_mla_krope, state_ret_fwd, state_ret_bwd, c, c_ctx, ada_w, ada_b, ffn1_w_in, ffn1_w_out, ffn2_w_in, ffn2_w_out, post_ln_g, post_ln_b, mix_w_in, conv_w_dw, conv_b_dw, conv_ln_g, conv_ln_b, conv_w_out, ret_decay_fwd, ret_decay_bwd, ret_w_out, mla_q_norm, mla_w_uq, mla_kv_norm, mla_w_ukv, mla_w_out, mix_w_o):
    assert DEPTH == 2
    cvec = jnp.concatenate([c_ctx[None, :], c, jnp.zeros((N_MOD_ROWS - 1 - DEC_BATCH, D_MODEL), F32)], axis=0)
    mods_all = _ada_mods(cvec, ada_w, ada_b).reshape(DEPTH, N_MOD_ROWS, N_MODS, D_MODEL)
    rope_tabs = _rope_tables()

    w_ukv = mla_w_ukv.reshape(DEPTH, MLA_KV_LORA, MLA_HEADS, MLA_D_NOPE + MLA_D_V)
    wk_all = _head_pad_cols(w_ukv[..., :MLA_D_NOPE].reshape(DEPTH * MLA_KV_LORA, -1), MLA_D_NOPE)
    wk_all = wk_all.reshape(DEPTH, MLA_KV_LORA, MLA_W).astype(BF16)
    wvt_all = jnp.swapaxes(w_ukv[..., MLA_D_NOPE:].reshape(DEPTH, MLA_KV_LORA, MLA_V_W), 1, 2).astype(BF16)
    kr_tail = HEAD_PAD - ROPE_LANE0 - MLA_D_ROPE
    cache_kr_pad = jnp.pad(cache_mla_krope, ((0, 0), (0, 0), (0, 0), (ROPE_LANE0, kr_tail)))
    kcat_c, vt_c = _cache_kv(cache_mla_ckv, cache_kr_pad, wk_all, wvt_all)
    log_g = jnp.stack([jax.nn.log_sigmoid(ret_decay_fwd), jax.nn.log_sigmoid(ret_decay_bwd)], axis=1)
    w_in_t = jnp.swapaxes(mix_w_in, 1, 2)
    wm_all = jnp.swapaxes(w_in_t[:, :MAIN_W].astype(BF16), 1, 2)
    wg_all = jnp.swapaxes(w_in_t[:, MAIN_W + MLA_D_ROPE:].astype(BF16), 1, 2)

    ln_g = post_ln_g.reshape(DEPTH * 3, 1, D_MODEL)
    ln_b = post_ln_b.reshape(DEPTH * 3, 1, D_MODEL)
    xs = (x_prompt.reshape(N_CTX, D_MODEL), x_sample.reshape(N_LAT, D_MODEL))
    ctx_carry = None
    state_carry = None
    for l in range(DEPTH):
        last = l == DEPTH - 1
        x = _ffn(xs, mods_all, ffn1_w_in, ffn1_w_out, ln_g, ln_b, layer=l, which=0)[0]

        w_kr = mix_w_in[l, :, MAIN_W:MAIN_W + MLA_D_ROPE]
        wkr = jnp.pad(w_kr, ((0, 0), (ROPE_LANE0, kr_tail))).astype(BF16)
        wuq = _head_pad_cols(mla_w_uq[l], MLA_D_NOPE + MLA_D_ROPE).astype(BF16)
        (glu, rq, rk, rv, rg, q, kcat, vt, sig, ckv, kr) = _proj(
            x, mods_all, rope_tabs, wm_all, wkr, wg_all, wuq, wk_all[l], wvt_all[l],
            mla_q_norm[l].reshape(1, -1), mla_kv_norm[l].reshape(1, -1), ctx_carry, layer=l)
        ctx_carry = (ckv, kr)

        a = _conv_branch(glu, conv_w_dw[l], conv_b_dw[l].reshape(1, -1), conv_ln_g[l].reshape(1, -1),
                         conv_ln_b[l].reshape(1, -1))
        r_ctx, sf, sb = _retention(log_g[l], rq, rk, rv, rg, state_carry, latent=False, layer=l)
        state_carry = (sf, sb)
        r_lat = _retention(log_g[l], rq, rk, rv, rg, (state_ret_fwd, state_ret_bwd), latent=True, layer=l)[0]
        m_ctx = _attention(q, kcat, vt, None, latent=False)
        m_lat = _attention(q, kcat, vt, (kcat_c[l], vt_c[l]), latent=True)

        x = _merge(x, mods_all, a, (r_ctx, r_lat), (m_ctx, m_lat), sig, conv_w_out, ret_w_out, mla_w_out, mix_w_o,
                   ln_g, ln_b, layer=l)
        xs = _ffn((x,), mods_all, ffn2_w_in, ffn2_w_out, ln_g, ln_b, layer=l, which=2, split_out=last)

    y_ctx, y_lat = xs
    return (y_ctx.reshape(BATCH, SEQ, D_MODEL), y_lat.reshape(DEC_BATCH, DEC_SEQ, D_MODEL),
            ctx_carry[0], ctx_carry[1], state_carry[0], state_carry[1])
```

```python
import functools
import math

import jax
import jax.numpy as jnp
from jax import lax
from jax.experimental import pallas as pl
from jax.experimental.pallas import tpu as pltpu

F32 = jnp.float32
BF16 = jnp.bfloat16

D_MODEL = 1024
BATCH = 16
SEQ = 256
DEPTH = 2
DEC_BATCH = 4
DEC_SEQ = 1024
PAST_LEN = 256
GRID_W = 64
D_FF = 2816
N_MODS = 9
CONV_DIM = 512
CONV_WIDTH = 31
RET_HEADS = 4
RET_DK = 128
RET_DV = 256
MLA_HEADS = 8
MLA_Q_LORA = 512
MLA_KV_LORA = 256
MLA_D_NOPE = 64
MLA_D_ROPE = 32
MLA_D_V = 64
ROPE_AXIS_HALF = MLA_D_ROPE // 4
ROPE_BASE = 10000.0
DEEPNORM_ALPHA = (2 * DEPTH) ** 0.25
LN_EPS = 1e-5
RMS_EPS = 1e-6

N_CTX = BATCH * SEQ
N_LAT = DEC_BATCH * DEC_SEQ
N_TOK = N_CTX + N_LAT
N_MOD_ROWS = 8
HEAD_PAD = 128
ROPE_LANE0 = MLA_D_NOPE
MLA_W = MLA_HEADS * HEAD_PAD
MLA_V_W = MLA_HEADS * MLA_D_V
MAIN_W = 2 * CONV_DIM + 2 * RET_HEADS * RET_DK + 2 * RET_HEADS * RET_DV + MLA_Q_LORA + MLA_KV_LORA
GATE_W = 3 * D_MODEL
SUBLANES = 8
CONV_HALO = 16
CONV_BLOCK = 256
VMEM_LIMIT = 56 * 1024 * 1024
N_WSTEPS = 8
MXU_COLS = 256
FFN_CHUNKS = ((0, 6 * MXU_COLS), (6 * MXU_COLS, D_FF))
ATTN_QBLOCK = 256
ATTN_QSCALE = (MLA_D_NOPE + MLA_D_ROPE) ** -0.5 * math.log2(math.e)


def _dot(a, b):
    return jnp.dot(a, b, preferred_element_type=F32)


def _dot_nt(a, b):
    return lax.dot_general(a, b, (((1,), (1,)), ((), ())), preferred_element_type=F32)


def _dot_tn(a, b):
    return lax.dot_general(a, b, (((0,), (0,)), ((), ())), preferred_element_type=F32)


def _sigmoid(x):
    return 1.0 / (1.0 + jnp.exp(-x))


def _norm_rows(z):
    mu = jnp.mean(z, axis=-1, keepdims=True)
    zc = z - mu
    var = jnp.mean(zc * zc, axis=-1, keepdims=True)
    return zc * lax.rsqrt(var + LN_EPS)


def _rms_rows(z):
    return z * lax.rsqrt(jnp.mean(z * z, axis=-1, keepdims=True) + RMS_EPS)


def _resident(shape):
    zeros = (0,) * len(shape)
    return pl.BlockSpec(shape, lambda *_: zeros, pipeline_mode=pl.Buffered(1))


def _layer_resident(w, layer):
    zeros = (0,) * (w.ndim - 1)
    return pl.BlockSpec((None,) + w.shape[1:], lambda *_: (layer,) + zeros, pipeline_mode=pl.Buffered(1))


def _tile(i, n_w):
    return jnp.maximum(i - n_w, 0)


def _row_spec(tm, w, n_w):
    return pl.BlockSpec((tm, w), lambda i: (_tile(i, n_w), 0))


def _ctx_row_spec(tm, w, n_w):
    last = N_CTX // tm - 1
    return pl.BlockSpec((tm, w), lambda i: (jnp.minimum(_tile(i, n_w), last), 0))


def _lat_row_spec(tm, w, n_w):
    first = N_CTX // tm
    return pl.BlockSpec((tm, w), lambda i: (jnp.maximum(_tile(i, n_w) - first, 0), 0))


def _mod_spec(tm, layer, n_w):
    n_ctx_tiles = N_CTX // tm
    tiles_per_seq = DEC_SEQ // tm

    def index(i):
        j = _tile(i, n_w)
        return (layer, jnp.where(j < n_ctx_tiles, 0, 1 + (j - n_ctx_tiles) // tiles_per_seq), 0, 0)

    return pl.BlockSpec((None, None, N_MODS, D_MODEL), index)


def _wchunk_spec(w, layer):
    _, rows, cols = w.shape
    return pl.BlockSpec((None, rows // N_WSTEPS, cols), lambda i: (layer, jnp.minimum(i, N_WSTEPS - 1), 0))


def _layer_row_spec(w, index):
    return pl.BlockSpec((None, 1, w.shape[-1]), lambda i: (index, 0, 0), pipeline_mode=pl.Buffered(1))


def _stage_chunk(i, src_ref, dst_ref):
    rows = src_ref.shape[0]
    dst_ref[pl.ds(pl.multiple_of(i * rows, rows), rows), :] = src_ref[...].astype(BF16)


def _params(semantics):
    return pltpu.CompilerParams(dimension_semantics=semantics, vmem_limit_bytes=VMEM_LIMIT)


def _ada_kernel(c_ref, w_ref, b_ref, o_ref):
    c = c_ref[...]
    h = (c * _sigmoid(c)).astype(BF16)
    o_ref[...] = _dot(h, w_ref[...].astype(BF16)) + b_ref[...]


def _ada_mods(cvec, ada_w, ada_b):
    tn = D_MODEL
    n_out = N_MODS * D_MODEL
    return pl.pallas_call(
        _ada_kernel,
        grid=(DEPTH, n_out // tn),
        in_specs=[
            pl.BlockSpec((N_MOD_ROWS, D_MODEL), lambda l, j: (0, 0)),
            pl.BlockSpec((None, D_MODEL, tn), lambda l, j: (l, 0, j)),
            pl.BlockSpec((None, 1, tn), lambda l, j: (l, 0, j)),
        ],
        out_specs=pl.BlockSpec((None, N_MOD_ROWS, tn), lambda l, j: (l, 0, j)),
        out_shape=jax.ShapeDtypeStruct((DEPTH, N_MOD_ROWS, n_out), F32),
        compiler_params=_params(("arbitrary", "arbitrary")),
        name="ada_mods",
    )(cvec, ada_w, ada_b.reshape(DEPTH, 1, n_out))


def _ffn_kernel(*refs, base, tm, n_x, n_out):
    x_refs = refs[:n_x]
    mod_ref, win_ref, wout_ref, g_ref, b_ref = refs[n_x:n_x + 5]
    o_refs = refs[n_x + 5:n_x + 5 + n_out]
    win_bf, wout_bf = refs[n_x + 5 + n_out:]
    i = pl.program_id(0)
    is_ctx = i < N_WSTEPS + N_CTX // tm

    @pl.when(i < N_WSTEPS)
    def _():
        _stage_chunk(i, win_ref, win_bf)
        _stage_chunk(i, wout_ref, wout_bf)

    @pl.when(i >= N_WSTEPS)
    def _():
        if n_x == 2:
            x = jnp.where(is_ctx, x_refs[0][...], x_refs[1][...])
        else:
            x = x_refs[0][...]
        shift = mod_ref[base:base + 1, :]
        scale = mod_ref[base + 1:base + 2, :]
        gate = mod_ref[base + 2:base + 3, :]
        h = (x * (1.0 + scale) + shift).astype(BF16)
        y = None
        for lo, hi in FFN_CHUNKS:
            g = _dot(h, win_bf[:, lo:hi])
            u = _dot(h, win_bf[:, D_FF + lo:D_FF + hi])
            a = (g * _sigmoid(g) * u).astype(BF16)
            yc = _dot(a, wout_bf[lo:hi, :])
            y = yc if y is None else y + yc
        z = DEEPNORM_ALPHA * x + 0.5 * gate * y
        res = _norm_rows(z) * g_ref[...] + b_ref[...]
        if n_out == 2:
            @pl.when(is_ctx)
            def _():
                o_refs[0][...] = res

            @pl.when(jnp.logical_not(is_ctx))
            def _():
                o_refs[1][...] = res
        else:
            o_refs[0][...] = res


def _ffn(xs, mods_all, w_in, w_out, ln_g, ln_b, *, layer, which, split_out=False, tm=512):
    row = _row_spec(tm, D_MODEL, N_WSTEPS)
    pair = [_ctx_row_spec(tm, D_MODEL, N_WSTEPS), _lat_row_spec(tm, D_MODEL, N_WSTEPS)]
    ln_index = layer * 3 + which
    if split_out:
        out_specs = pair
        out_shape = [jax.ShapeDtypeStruct((N_CTX, D_MODEL), F32), jax.ShapeDtypeStruct((N_LAT, D_MODEL), F32)]
    else:
        out_specs = [row]
        out_shape = [jax.ShapeDtypeStruct((N_TOK, D_MODEL), F32)]
    return pl.pallas_call(
        functools.partial(_ffn_kernel, base=3 * which, tm=tm, n_x=len(xs), n_out=len(out_specs)),
        grid=(N_WSTEPS + N_TOK // tm,),
        in_specs=(pair if len(xs) == 2 else [row]) + [
            _mod_spec(tm, layer, N_WSTEPS), _wchunk_spec(w_in, layer), _wchunk_spec(w_out, layer),
            _layer_row_spec(ln_g, ln_index), _layer_row_spec(ln_b, ln_index)],
        out_specs=out_specs,
        out_shape=out_shape,
        scratch_shapes=[pltpu.VMEM((D_MODEL, 2 * D_FF), BF16), pltpu.VMEM((D_FF, D_MODEL), BF16)],
        compiler_params=_params(("arbitrary",)),
        name="ffn",
    )(*xs, mods_all, w_in, w_out, ln_g, ln_b)


def _proj_kernel(*refs, tm, carry):
    (x_ref, mod_ref, cos_ref, sin_lo_ref, sin_hi_ref, wt_ref, wuq_ref, wk_ref, wvt_ref, gq_ref,
     gkv_ref) = refs[:11]
    refs = refs[11:]
    if carry:
        ckv_prev_ref, kr_prev_ref = refs[:2]
        refs = refs[2:]
    (glu_ref, rq_ref, rk_ref, rv_ref, rg_ref, q_ref, kcat_ref, vt_ref, sig_ref, ckv_ref, kr_ref) = refs
    is_ctx = pl.program_id(0) < N_CTX // tm

    x = x_ref[...]
    u = (x * (1.0 + mod_ref[4:5, :]) + mod_ref[3:4, :]).astype(BF16)
    widths = (CONV_DIM, CONV_DIM, RET_HEADS * RET_DK, RET_HEADS * RET_DK, RET_HEADS * RET_DV, RET_HEADS * RET_DV,
              MLA_Q_LORA, MLA_KV_LORA)
    starts = [sum(widths[:n]) for n in range(len(widths))]

    def proj(n):
        return _dot_nt(u, wt_ref[starts[n]:starts[n] + widths[n], :])

    mq = proj(6)
    mkv = proj(7)
    cos = cos_ref[...]
    sin_lo = sin_lo_ref[...]
    sin_hi = sin_hi_ref[...]

    def rotary(v):
        up = pltpu.roll(v, HEAD_PAD - ROPE_AXIS_HALF, 1)
        down = pltpu.roll(v, ROPE_AXIS_HALF, 1)
        return v * cos + up * sin_lo + down * sin_hi

    qn = (_rms_rows(mq) * gq_ref[...]).astype(BF16)
    qm = _dot(qn, wuq_ref[...])
    ckv = _rms_rows(mkv) * gkv_ref[...]
    ckvb = ckv.astype(BF16)
    kn = _dot(ckvb, wk_ref[...])
    vt_ref[...] = _dot_nt(wvt_ref[...], ckvb).astype(BF16)
    kr_grp = _dot_nt(u, wt_ref[MAIN_W:MAIN_W + HEAD_PAD, :])
    lane = lax.broadcasted_iota(jnp.int32, kr_grp.shape, 1)
    in_rope = jnp.logical_and(lane >= ROPE_LANE0, lane < ROPE_LANE0 + MLA_D_ROPE)
    kr = jnp.where(in_rope, pltpu.roll(kr_grp, ROPE_LANE0, 1), 0.0)
    kr_rot = rotary(kr)
    for h in range(MLA_HEADS):
        sl = slice(h * HEAD_PAD, (h + 1) * HEAD_PAD)
        q_ref[:, sl] = (rotary(qm[:, sl]) * ATTN_QSCALE).astype(BF16)
        kcat_ref[:, sl] = (kn[:, sl] + kr_rot).astype(BF16)

    glu_ref[...] = proj(0) * _sigmoid(proj(1))
    sig_ref[...] = _sigmoid(_dot_nt(u, wt_ref[MAIN_W + MLA_D_ROPE:, :])).astype(BF16)
    rg = proj(5)
    rg_ref[...] = (rg * _sigmoid(rg)).astype(BF16)
    rk_ref[...] = (proj(3) * (RET_DK ** -0.5)).astype(BF16)
    rv_ref[...] = proj(4).astype(BF16)
    rq_ref[...] = proj(2).astype(BF16)

    @pl.when(is_ctx)
    def _():
        seqs = tm // SEQ
        ckv3 = ckv.reshape(seqs, SEQ, MLA_KV_LORA)
        kr3 = kr_grp[:, :MLA_D_ROPE].reshape(seqs, SEQ, MLA_D_ROPE)
        if carry:
            ckv_ref[:, 0] = ckv_prev_ref[...]
            kr_ref[:, 0] = kr_prev_ref[...]
            ckv_ref[:, 1] = ckv3
            kr_ref[:, 1] = kr3
        else:
            ckv_ref[...] = ckv3
            kr_ref[...] = kr3


def _proj(x, mods_all, rope_tabs, wt_all, wuq, wk, wvt, gq, gkv, carry, *, layer, tm=256):
    n_ctx_tiles = N_CTX // tm
    tiles_per_seq = DEC_SEQ // tm
    seqs = tm // SEQ

    def rope_index(i):
        return (jnp.where(i < n_ctx_tiles, 0, tiles_per_seq + (i - n_ctx_tiles) % tiles_per_seq), 0)

    def row(w):
        return _row_spec(tm, w, 0)

    def out(w, dt):
        return jax.ShapeDtypeStruct((N_TOK, w), dt)

    def ctx_seq_spec(*tail):
        zeros = (0,) * len(tail)
        return pl.BlockSpec((seqs,) + tail, lambda i: (jnp.minimum(i, n_ctx_tiles - 1),) + zeros)

    rope = pl.BlockSpec((tm, HEAD_PAD), rope_index)
    in_specs = [row(D_MODEL), _mod_spec(tm, layer, 0), rope, rope, rope,
                _layer_resident(wt_all, layer), _resident(wuq.shape), _resident(wk.shape), _resident(wvt.shape),
                _resident(gq.shape), _resident(gkv.shape)]
    args = [x, mods_all, *rope_tabs, wt_all, wuq, wk, wvt, gq, gkv]
    if carry is None:
        ctx_specs = [ctx_seq_spec(SEQ, MLA_KV_LORA), ctx_seq_spec(SEQ, MLA_D_ROPE)]
        ctx_shapes = [jax.ShapeDtypeStruct((BATCH, SEQ, MLA_KV_LORA), F32),
                      jax.ShapeDtypeStruct((BATCH, SEQ, MLA_D_ROPE), F32)]
    else:
        in_specs += [ctx_seq_spec(SEQ, MLA_KV_LORA), ctx_seq_spec(SEQ, MLA_D_ROPE)]
        args += list(carry)
        ctx_specs = [ctx_seq_spec(DEPTH, SEQ, MLA_KV_LORA), ctx_seq_spec(DEPTH, SEQ, MLA_D_ROPE)]
        ctx_shapes = [jax.ShapeDtypeStruct((BATCH, DEPTH, SEQ, MLA_KV_LORA), F32),
                      jax.ShapeDtypeStruct((BATCH, DEPTH, SEQ, MLA_D_ROPE), F32)]
    return pl.pallas_call(
        functools.partial(_proj_kernel, tm=tm, carry=carry is not None),
        grid=(N_TOK // tm,),
        in_specs=in_specs,
        out_specs=[row(CONV_DIM), row(RET_HEADS * RET_DK), row(RET_HEADS * RET_DK), row(RET_HEADS * RET_DV),
                   row(RET_HEADS * RET_DV), row(MLA_W), row(MLA_W),
                   pl.BlockSpec((MLA_V_W, tm), lambda i: (0, i)), row(GATE_W)] + ctx_specs,
        out_shape=[out(CONV_DIM, F32), out(RET_HEADS * RET_DK, BF16), out(RET_HEADS * RET_DK, BF16),
                   out(RET_HEADS * RET_DV, BF16), out(RET_HEADS * RET_DV, BF16), out(MLA_W, BF16),
                   out(MLA_W, BF16), jax.ShapeDtypeStruct((MLA_V_W, N_TOK), BF16), out(GATE_W, BF16)] + ctx_shapes,
        compiler_params=_params(("arbitrary",)),
        name="mix_proj",
    )(*args)


def _cache_kv_kernel(ckv_ref, kr_ref, wk_ref, wvt_ref, kcat_ref, vt_ref):
    ckvb = ckv_ref[...].astype(BF16)
    kn = _dot(ckvb, wk_ref[...])
    vt_ref[...] = _dot_nt(wvt_ref[...], ckvb).astype(BF16)
    kr = kr_ref[...]
    for h in range(MLA_HEADS):
        sl = slice(h * HEAD_PAD, (h + 1) * HEAD_PAD)
        kcat_ref[:, sl] = (kn[:, sl] + kr).astype(BF16)


def _cache_kv(cache_ckv, cache_kr_pad, wk, wvt):
    n = DEC_BATCH * PAST_LEN
    return pl.pallas_call(
        _cache_kv_kernel,
        grid=(DEPTH, DEC_BATCH),
        in_specs=[pl.BlockSpec((None, None, PAST_LEN, MLA_KV_LORA), lambda l, b: (b, l, 0, 0)),
                  pl.BlockSpec((None, None, PAST_LEN, HEAD_PAD), lambda l, b: (b, l, 0, 0)),
                  pl.BlockSpec((None, MLA_KV_LORA, MLA_W), lambda l, b: (l, 0, 0)),
                  pl.BlockSpec((None, MLA_V_W, MLA_KV_LORA), lambda l, b: (l, 0, 0))],
        out_specs=[pl.BlockSpec((None, PAST_LEN, MLA_W), lambda l, b: (l, b, 0)),
                   pl.BlockSpec((None, MLA_V_W, PAST_LEN), lambda l, b: (l, 0, b))],
        out_shape=[jax.ShapeDtypeStruct((DEPTH, n, MLA_W), BF16), jax.ShapeDtypeStruct((DEPTH, MLA_V_W, n), BF16)],
        compiler_params=_params(("arbitrary", "arbitrary")),
        name="cache_kv",
    )(cache_ckv, cache_kr_pad, wk, wvt)


def _conv_kernel(prev_ref, cur_ref, next_ref, w_ref, b_ref, g_ref, beta_ref, o_ref, pad_ref, shift_ref, acc_ref):
    i = pl.program_id(0)
    n_ctx_blocks = N_CTX // CONV_BLOCK
    blocks_per_seq = DEC_SEQ // CONV_BLOCK
    pos = (i - n_ctx_blocks) % blocks_per_seq
    latent = i >= n_ctx_blocks
    has_prev = jnp.logical_and(latent, pos != 0)
    has_next = jnp.logical_and(latent, pos != blocks_per_seq - 1)
    zero = jnp.zeros((CONV_HALO, CONV_DIM), F32)
    pad_ref[0:CONV_HALO, :] = jnp.where(has_prev, prev_ref[...], zero)
    pad_ref[CONV_HALO:CONV_HALO + CONV_BLOCK, :] = cur_ref[...]
    pad_ref[CONV_HALO + CONV_BLOCK:, :] = jnp.where(has_next, next_ref[...], zero)
    span = shift_ref.shape[1]
    for ph in range(SUBLANES):
        shift_ref[ph] = pad_ref[ph:ph + span, :]
    rows = 64
    lanes = 128
    first = CONV_HALO - CONV_WIDTH // 2
    for c in range(CONV_DIM // lanes):
        cs = slice(c * lanes, (c + 1) * lanes)
        for r in range(CONV_BLOCK // rows):
            acc = jnp.broadcast_to(b_ref[:, cs], (rows, lanes))
            for j in range(CONV_WIDTH):
                ph = (first + j) % SUBLANES
                start = r * rows + (first + j) - ph
                acc = acc + w_ref[j:j + 1, cs] * shift_ref[ph, start:start + rows, cs]
            acc_ref[r * rows:(r + 1) * rows, cs] = acc
    y = _norm_rows(acc_ref[...]) * g_ref[...] + beta_ref[...]
    o_ref[...] = (y * _sigmoid(y)).astype(BF16)


def _conv_branch(glu, w_dw, b_dw, ln_g, ln_b):
    per = CONV_BLOCK // CONV_HALO
    n_halo = N_TOK // CONV_HALO
    span = CONV_BLOCK + 2 * CONV_HALO - SUBLANES
    return pl.pallas_call(
        _conv_kernel,
        grid=(N_TOK // CONV_BLOCK,),
        in_specs=[pl.BlockSpec((CONV_HALO, CONV_DIM), lambda i: (jnp.maximum(i * per - 1, 0), 0)),
                  pl.BlockSpec((CONV_BLOCK, CONV_DIM), lambda i: (i, 0)),
                  pl.BlockSpec((CONV_HALO, CONV_DIM), lambda i: (jnp.minimum((i + 1) * per, n_halo - 1), 0)),
                  _resident((CONV_WIDTH, CONV_DIM)), _resident((1, CONV_DIM)), _resident((1, CONV_DIM)),
                  _resident((1, CONV_DIM))],
        out_specs=pl.BlockSpec((CONV_BLOCK, CONV_DIM), lambda i: (i, 0)),
        out_shape=jax.ShapeDtypeStruct((N_TOK, CONV_DIM), BF16),
        scratch_shapes=[pltpu.VMEM((CONV_BLOCK + 2 * CONV_HALO, CONV_DIM), F32),
                        pltpu.VMEM((SUBLANES, span, CONV_DIM), F32),
                        pltpu.VMEM((CONV_BLOCK, CONV_DIM), F32)],
        compiler_params=_params(("arbitrary",)),
        name="conv_branch",
    )(glu, glu, glu, w_dw, b_dw, ln_g, ln_b)


def _ret_kernel(*refs, t, hp, latent, carry):
    if latent:
        lg_ref, q_ref, k_ref, v_ref, g_ref, s0f_ref, s0b_ref, o_ref, d_ref = refs
    elif carry:
        lg_ref, q_ref, k_ref, v_ref, g_ref, sf_prev_ref, sb_prev_ref, o_ref, sf_ref, sb_ref, d_ref = refs
    else:
        lg_ref, q_ref, k_ref, v_ref, g_ref, o_ref, sf_ref, sb_ref, d_ref = refs
    hblk = pl.program_id(0)

    @pl.when(pl.program_id(1) == 0)
    def _():
        diff = (lax.broadcasted_iota(jnp.int32, (t, t), 0) - lax.broadcasted_iota(jnp.int32, (t, t), 1)).astype(F32)
        for hh in range(hp):
            lgf = lg_ref[0, hblk * hp + hh]
            lgb = lg_ref[1, hblk * hp + hh]
            d_ref[hh] = jnp.exp(jnp.where(diff >= 0, diff * lgf, -diff * lgb))

    if carry:
        sf_ref[0] = sf_prev_ref[...]
        sb_ref[0] = sb_prev_ref[...]
    pos = lax.broadcasted_iota(jnp.int32, (t, 1), 0).astype(F32)
    for hh in range(hp):
        lgf = lg_ref[0, hblk * hp + hh]
        lgb = lg_ref[1, hblk * hp + hh]
        q = q_ref[:, hh * RET_DK:(hh + 1) * RET_DK]
        k = k_ref[:, hh * RET_DK:(hh + 1) * RET_DK]
        v = v_ref[:, hh * RET_DV:(hh + 1) * RET_DV]
        p = (_dot_nt(q, k) * d_ref[hh]).astype(BF16)
        o = _dot(p, v)
        if latent:
            o = o + jnp.exp((pos + 1.0) * lgf) * _dot(q, s0f_ref[hh].astype(BF16))
            o = o + jnp.exp((t - pos) * lgb) * _dot(q, s0b_ref[hh].astype(BF16))
        else:
            kf = k.astype(F32)
            sf = _dot_tn((kf * jnp.exp((t - 1.0 - pos) * lgf)).astype(BF16), v)
            sb = _dot_tn((kf * jnp.exp(pos * lgb)).astype(BF16), v)
            if carry:
                sf_ref[1, hh] = sf
                sb_ref[1, hh] = sb
            else:
                sf_ref[hh] = sf
                sb_ref[hh] = sb
        o_ref[:, hh * RET_DV:(hh + 1) * RET_DV] = (g_ref[:, hh * RET_DV:(hh + 1) * RET_DV] * _norm_rows(o)).astype(BF16)


def _retention(log_g, rq, rk, rv, rg, states, *, latent, layer):
    t = DEC_SEQ if latent else SEQ
    hp = 1 if latent else RET_HEADS
    n_seq = DEC_BATCH if latent else BATCH
    row0 = (N_CTX // t) if latent else 0

    def row(w):
        return pl.BlockSpec((t, hp * w), lambda h, s: (row0 + s, h))

    smem = pl.BlockSpec(memory_space=pltpu.SMEM)
    out_shape = [jax.ShapeDtypeStruct((n_seq * t, RET_HEADS * RET_DV), BF16)]
    out_specs = [pl.BlockSpec((t, hp * RET_DV), lambda h, s: (s, h))]
    in_specs = [smem, row(RET_DK), row(RET_DK), row(RET_DV), row(RET_DV)]
    args = [log_g, rq, rk, rv, rg]
    carry = False
    if latent:
        st = pl.BlockSpec((None, None, hp, RET_DK, RET_DV), lambda h, s: (s, layer, h, 0, 0))
        in_specs += [st, st]
        args += list(states)
    else:
        st = pl.BlockSpec((None, hp, RET_DK, RET_DV), lambda h, s: (s, h, 0, 0))
        if states is None:
            out_specs += [st, st]
            out_shape += [jax.ShapeDtypeStruct((BATCH, RET_HEADS, RET_DK, RET_DV), F32)] * 2
        else:
            carry = True
            in_specs += [st, st]
            args += list(states)
            st2 = pl.BlockSpec((None, DEPTH, hp, RET_DK, RET_DV), lambda h, s: (s, 0, h, 0, 0))
            out_specs += [st2, st2]
            out_shape += [jax.ShapeDtypeStruct((BATCH, DEPTH, RET_HEADS, RET_DK, RET_DV), F32)] * 2

    return pl.pallas_call(
        functools.partial(_ret_kernel, t=t, hp=hp, latent=latent, carry=carry),
        grid=(RET_HEADS // hp, n_seq),
        in_specs=in_specs,
        out_specs=out_specs,
        out_shape=out_shape,
        scratch_shapes=[pltpu.VMEM((hp, t, t), F32)],
        compiler_params=_params(("arbitrary", "arbitrary")),
        name="retention_lat" if latent else "retention_ctx",
    )(*args)


def _attn_kernel(*refs, t, hp, latent):
    if latent:
        q_ref, k_ref, vt_ref, kc_ref, vtc_ref, o_ref = refs
    else:
        q_ref, k_ref, vt_ref, o_ref = refs
    qb = min(ATTN_QBLOCK, t)
    units = [(slice(hh * HEAD_PAD, (hh + 1) * HEAD_PAD), slice(hh * MLA_D_V, (hh + 1) * MLA_D_V),
              slice(b * qb, (b + 1) * qb)) for b in range(t // qb) for hh in range(hp)]
    def scores(unit):
        sl, _, rows = unit
        q = q_ref[rows, sl]
        s = [_dot_nt(k_ref[:, sl], q)]
        if latent:
            s.append(_dot_nt(kc_ref[:, sl], q))
        return s

    def softmax(s):
        m = functools.reduce(jnp.maximum, [jnp.max(x, axis=0, keepdims=True) for x in s])
        e = [jnp.exp2(x - m) for x in s]
        den = functools.reduce(jnp.add, [jnp.sum(x, axis=0, keepdims=True) for x in e])
        return [x.astype(BF16) for x in e], den

    def values(unit, e, den):
        _, vs, _ = unit
        o = _dot(vt_ref[vs, :], e[0])
        if latent:
            o = o + _dot(vtc_ref[vs, :], e[1])
        return o / den

    pairs = [units[u:u + 2] for u in range(0, len(units), 2)]
    s_next = [scores(u) for u in pairs[0]]
    sm_prev = None
    for g in range(len(pairs) + 1):
        s_cur = s_next
        if g + 1 < len(pairs):
            s_next = [scores(u) for u in pairs[g + 1]]
        sm_cur = [softmax(s) for s in s_cur] if g < len(pairs) else None
        if sm_prev is not None:
            (_, vs0, rows), (_, vs1, _) = pairs[g - 1]
            outs = [values(u, e, den) for u, (e, den) in zip(pairs[g - 1], sm_prev)]
            o_ref[rows, vs0.start:vs1.stop] = jnp.concatenate(outs, axis=0).T.astype(BF16)
        sm_prev = sm_cur


def _attention(q, kcat, vt, cache, *, latent):
    t = DEC_SEQ if latent else SEQ
    hp = 2 if latent else MLA_HEADS
    n_seq = DEC_BATCH if latent else BATCH
    row0 = (N_CTX // t) if latent else 0
    row = pl.BlockSpec((t, hp * HEAD_PAD), lambda s, h: (row0 + s, h))
    col = pl.BlockSpec((hp * MLA_D_V, t), lambda s, h: (h, row0 + s))
    in_specs = [row, row, col]
    args = [q, kcat, vt]
    if latent:
        in_specs += [pl.BlockSpec((PAST_LEN, hp * HEAD_PAD), lambda s, h: (s, h)),
                     pl.BlockSpec((hp * MLA_D_V, PAST_LEN), lambda s, h: (h, s))]
        args += list(cache)
    return pl.pallas_call(
        functools.partial(_attn_kernel, t=t, hp=hp, latent=latent),
        grid=(n_seq, MLA_HEADS // hp),
        in_specs=in_specs,
        out_specs=pl.BlockSpec((t, hp * MLA_D_V), lambda s, h: (s, h)),
        out_shape=jax.ShapeDtypeStruct((n_seq * t, MLA_V_W), BF16),
        compiler_params=_params(("arbitrary", "arbitrary")),
        name="attention_lat" if latent else "attention_ctx",
    )(*args)


def _merge_kernel(x_ref, mod_ref, a_ref, rc_ref, rl_ref, mc_ref, ml_ref, sig_ref, wc_ref, wr_ref, wm_ref, wo_ref,
                  g_ref, b_ref, o_ref, wc_bf, wr_bf, wm_bf, wo_bf, *, tm):
    i = pl.program_id(0)
    is_ctx = i < N_WSTEPS + N_CTX // tm

    @pl.when(i < N_WSTEPS)
    def _():
        _stage_chunk(i, wc_ref, wc_bf)
        _stage_chunk(i, wr_ref, wr_bf)
        _stage_chunk(i, wo_ref, wo_bf)
        _stage_chunk(i, wm_ref, wm_bf)

    @pl.when(i >= N_WSTEPS)
    def _():
        x = x_ref[...]
        r = jnp.where(is_ctx, rc_ref[...], rl_ref[...])
        m = jnp.where(is_ctx, mc_ref[...], ml_ref[...])
        merged = sig_ref[:, 0:D_MODEL] * _dot(a_ref[...], wc_bf[...])
        merged = merged + sig_ref[:, D_MODEL:2 * D_MODEL] * _dot(r, wr_bf[...])
        merged = merged + sig_ref[:, 2 * D_MODEL:] * _dot(m, wm_bf[...])
        y = _dot(merged.astype(BF16), wo_bf[...])
        z = DEEPNORM_ALPHA * x + mod_ref[5:6, :] * y
        o_ref[...] = _norm_rows(z) * g_ref[...] + b_ref[...]


def _merge(x, mods_all, a, r_pair, m_pair, sig, wc, wr, wm, wo, ln_g, ln_b, *, layer, tm=512):
    def row(w):
        return _row_spec(tm, w, N_WSTEPS)

    def pair(w):
        return [_ctx_row_spec(tm, w, N_WSTEPS), _lat_row_spec(tm, w, N_WSTEPS)]

    ln_index = layer * 3 + 1
    return pl.pallas_call(
        functools.partial(_merge_kernel, tm=tm),
        grid=(N_WSTEPS + N_TOK // tm,),
        in_specs=[row(D_MODEL), _mod_spec(tm, layer, N_WSTEPS), row(CONV_DIM)] + pair(RET_HEADS * RET_DV)
        + pair(MLA_V_W) + [row(GATE_W), _wchunk_spec(wc, layer), _wchunk_spec(wr, layer), _wchunk_spec(wm, layer),
                         _wchunk_spec(wo, layer), _layer_row_spec(ln_g, ln_index), _layer_row_spec(ln_b, ln_index)],
        out_specs=row(D_MODEL),
        out_shape=jax.ShapeDtypeStruct((N_TOK, D_MODEL), F32),
        scratch_shapes=[pltpu.VMEM((CONV_DIM, D_MODEL), BF16), pltpu.VMEM((RET_HEADS * RET_DV, D_MODEL), BF16),
                        pltpu.VMEM((MLA_V_W, D_MODEL), BF16), pltpu.VMEM((D_MODEL, D_MODEL), BF16)],
        compiler_params=_params(("arbitrary",)),
        name="merge",
    )(x, mods_all, a, *r_pair, *m_pair, sig, wc, wr, wm, wo, ln_g, ln_b)


def _rope_tables():
    rows = DEC_SEQ // GRID_W
    row_id = jnp.repeat(jnp.arange(rows, dtype=F32), GRID_W)
    col_id = jnp.tile(jnp.arange(GRID_W, dtype=F32), rows)
    inv_freq = ROPE_BASE ** (-jnp.arange(ROPE_AXIS_HALF, dtype=F32) / ROPE_AXIS_HALF)
    ang = jnp.stack([row_id[:, None] * inv_freq, col_id[:, None] * inv_freq], axis=1)
    cos = jnp.cos(ang)
    sin = jnp.sin(ang)
    cos32 = jnp.stack([cos, cos], axis=2).reshape(DEC_SEQ, MLA_D_ROPE)
    zero = jnp.zeros_like(sin)
    sin_lo32 = jnp.stack([-sin, zero], axis=2).reshape(DEC_SEQ, MLA_D_ROPE)
    sin_hi32 = jnp.stack([zero, sin], axis=2).reshape(DEC_SEQ, MLA_D_ROPE)
    tail = HEAD_PAD - ROPE_LANE0 - MLA_D_ROPE
    cos_t = jnp.concatenate([jnp.ones((DEC_SEQ, ROPE_LANE0), F32), cos32, jnp.ones((DEC_SEQ, tail), F32)], axis=1)
    cos_t = jnp.concatenate([jnp.ones((DEC_SEQ, HEAD_PAD), F32), cos_t], axis=0)

    def sin_table(s32):
        return jnp.pad(s32, ((DEC_SEQ, 0), (ROPE_LANE0, tail)))

    return cos_t, sin_table(sin_lo32), sin_table(sin_hi32)


def _head_pad_cols(w, width):
    k = w.shape[0]
    w = w.reshape(k, MLA_HEADS, width)
    return jnp.pad(w, ((0, 0), (0, 0), (0, HEAD_PAD - width))).reshape(k, MLA_W)


def kernel(x_prompt, x_sample, cache_mla_ckv, cache_mla_krope, state_ret_fwd, state_ret_bwd, c, c_ctx, ada_w, ada_b, ffn1_w_in, ffn1_w_out, ffn2_w_in, ffn2_w_out, post_ln_g, post_ln_b, mix_w_in, conv_w_dw, conv_b_dw, conv_ln_g, conv_ln_b, conv_w_out, ret_decay_fwd, ret_decay_bwd, ret_w_out, mla_q_norm, mla_w_uq, mla_kv_norm, mla_w_ukv, mla_w_out, mix_w_o):
    assert DEPTH == 2
    cvec = jnp.concatenate([c_ctx[None, :], c, jnp.zeros((N_MOD_ROWS - 1 - DEC_BATCH, D_MODEL), F32)], axis=0)
    mods_all = _ada_mods(cvec, ada_w, ada_b).reshape(DEPTH, N_MOD_ROWS, N_MODS, D_MODEL)
    rope_tabs = _rope_tables()

    w_ukv = mla_w_ukv.reshape(DEPTH, MLA_KV_LORA, MLA_HEADS, MLA_D_NOPE + MLA_D_V)
    wk_all = _head_pad_cols(w_ukv[..., :MLA_D_NOPE].reshape(DEPTH * MLA_KV_LORA, -1), MLA_D_NOPE)
    wk_all = wk_all.reshape(DEPTH, MLA_KV_LORA, MLA_W).astype(BF16)
    wvt_all = jnp.swapaxes(w_ukv[..., MLA_D_NOPE:].reshape(DEPTH, MLA_KV_LORA, MLA_V_W), 1, 2).astype(BF16)
    kr_tail = HEAD_PAD - ROPE_LANE0 - MLA_D_ROPE
    cache_kr_pad = jnp.pad(cache_mla_krope, ((0, 0), (0, 0), (0, 0), (ROPE_LANE0, kr_tail)))
    kcat_c, vt_c = _cache_kv(cache_mla_ckv, cache_kr_pad, wk_all, wvt_all)
    log_g = jnp.stack([jax.nn.log_sigmoid(ret_decay_fwd), jax.nn.log_sigmoid(ret_decay_bwd)], axis=1)
    wt_all = jnp.swapaxes(mix_w_in, 1, 2).astype(BF16)

    ln_g = post_ln_g.reshape(DEPTH * 3, 1, D_MODEL)
    ln_b = post_ln_b.reshape(DEPTH * 3, 1, D_MODEL)
    xs = (x_prompt.reshape(N_CTX, D_MODEL), x_sample.reshape(N_LAT, D_MODEL))
    ctx_carry = None
    state_carry = None
    for l in range(DEPTH):
        last = l == DEPTH - 1
        x = _ffn(xs, mods_all, ffn1_w_in, ffn1_w_out, ln_g, ln_b, layer=l, which=0)[0]

        wuq = _head_pad_cols(mla_w_uq[l], MLA_D_NOPE + MLA_D_ROPE).astype(BF16)
        (glu, rq, rk, rv, rg, q, kcat, vt, sig, ckv, kr) = _proj(
            x, mods_all, rope_tabs, wt_all, wuq, wk_all[l], wvt_all[l],
            mla_q_norm[l].reshape(1, -1), mla_kv_norm[l].reshape(1, -1), ctx_carry, layer=l)
        ctx_carry = (ckv, kr)

        a = _conv_branch(glu, conv_w_dw[l], conv_b_dw[l].reshape(1, -1), conv_ln_g[l].reshape(1, -1),
                         conv_ln_b[l].reshape(1, -1))
        r_ctx, sf, sb = _retention(log_g[l], rq, rk, rv, rg, state_carry, latent=False, layer=l)
        state_carry = (sf, sb)
        r_lat = _retention(log_g[l], rq, rk, rv, rg, (state_ret_fwd, state_ret_bwd), latent=True, layer=l)[0]
        m_ctx = _attention(q, kcat, vt, None, latent=False)
        m_lat = _attention(q, kcat, vt, (kcat_c[l], vt_c[l]), latent=True)

        x = _merge(x, mods_all, a, (r_ctx, r_lat), (m_ctx, m_lat), sig, conv_w_out, ret_w_out, mla_w_out, mix_w_o,
                   ln_g, ln_b, layer=l)
        xs = _ffn((x,), mods_all, ffn2_w_in, ffn2_w_out, ln_g, ln_b, layer=l, which=2, split_out=last)

    y_ctx, y_lat = xs
    return (y_ctx.reshape(BATCH, SEQ, D_MODEL), y_lat.reshape(DEC_BATCH, DEC_SEQ, D_MODEL),
            ctx_carry[0], ctx_carry[1], state_carry[0], state_carry[1])
```

```python
import functools
import math

import jax
import jax.numpy as jnp
from jax import lax
from jax.experimental import pallas as pl
from jax.experimental.pallas import tpu as pltpu

F32 = jnp.float32
BF16 = jnp.bfloat16

D_MODEL = 1024
BATCH = 16
SEQ = 256
DEPTH = 2
DEC_BATCH = 4
DEC_SEQ = 1024
PAST_LEN = 256
GRID_W = 64
D_FF = 2816
N_MODS = 9
CONV_DIM = 512
CONV_WIDTH = 31
RET_HEADS = 4
RET_DK = 128
RET_DV = 256
MLA_HEADS = 8
MLA_Q_LORA = 512
MLA_KV_LORA = 256
MLA_D_NOPE = 64
MLA_D_ROPE = 32
MLA_D_V = 64
ROPE_AXIS_HALF = MLA_D_ROPE // 4
ROPE_BASE = 10000.0
DEEPNORM_ALPHA = (2 * DEPTH) ** 0.25
LN_EPS = 1e-5
RMS_EPS = 1e-6

N_CTX = BATCH * SEQ
N_LAT = DEC_BATCH * DEC_SEQ
N_TOK = N_CTX + N_LAT
N_MOD_ROWS = 8
HEAD_PAD = 128
ROPE_LANE0 = MLA_D_NOPE
MLA_W = MLA_HEADS * HEAD_PAD
MLA_V_W = MLA_HEADS * MLA_D_V
MAIN_W = 2 * CONV_DIM + 2 * RET_HEADS * RET_DK + 2 * RET_HEADS * RET_DV + MLA_Q_LORA + MLA_KV_LORA
N_BRANCHES = 3
GATE_W = N_BRANCHES * D_MODEL
SUBLANES = 8
CONV_HALO = 16
CONV_BLOCK = 256
VMEM_LIMIT = 56 * 1024 * 1024
N_WSTEPS = 8
MXU_COLS = 256
FFN_CHUNKS = ((0, 6 * MXU_COLS), (6 * MXU_COLS, D_FF))
ATTN_QBLOCK = 256
ATTN_QSCALE = (MLA_D_NOPE + MLA_D_ROPE) ** -0.5 * math.log2(math.e)


def _dot(a, b):
    return jnp.dot(a, b, preferred_element_type=F32)


def _dot_nt(a, b):
    return lax.dot_general(a, b, (((1,), (1,)), ((), ())), preferred_element_type=F32)


def _dot_tn(a, b):
    return lax.dot_general(a, b, (((0,), (0,)), ((), ())), preferred_element_type=F32)


def _sigmoid(x):
    return 1.0 / (1.0 + jnp.exp(-x))


def _norm_rows(z):
    mu = jnp.mean(z, axis=-1, keepdims=True)
    zc = z - mu
    var = jnp.mean(zc * zc, axis=-1, keepdims=True)
    return zc * lax.rsqrt(var + LN_EPS)


def _rms_rows(z):
    return z * lax.rsqrt(jnp.mean(z * z, axis=-1, keepdims=True) + RMS_EPS)


def _resident(shape):
    zeros = (0,) * len(shape)
    return pl.BlockSpec(shape, lambda *_: zeros, pipeline_mode=pl.Buffered(1))


def _layer_resident(w, layer):
    zeros = (0,) * (w.ndim - 1)
    return pl.BlockSpec((None,) + w.shape[1:], lambda *_: (layer,) + zeros, pipeline_mode=pl.Buffered(1))


def _tile(i, n_w, tm):
    return jnp.clip(i - n_w, 0, N_TOK // tm - 1)


def _row_spec(tm, w, n_w):
    return pl.BlockSpec((tm, w), lambda i: (_tile(i, n_w, tm), 0))


def _ctx_row_spec(tm, w, n_w):
    last = N_CTX // tm - 1
    return pl.BlockSpec((tm, w), lambda i: (jnp.minimum(_tile(i, n_w, tm), last), 0))


def _lat_row_spec(tm, w, n_w):
    first = N_CTX // tm
    return pl.BlockSpec((tm, w), lambda i: (jnp.maximum(_tile(i, n_w, tm) - first, 0), 0))


def _mod_spec(tm, layer, n_w):
    n_ctx_tiles = N_CTX // tm
    tiles_per_seq = DEC_SEQ // tm

    def index(i):
        j = _tile(i, n_w, tm)
        return (layer, jnp.where(j < n_ctx_tiles, 0, 1 + (j - n_ctx_tiles) // tiles_per_seq), 0, 0)

    return pl.BlockSpec((None, None, N_MODS, D_MODEL), index)


def _wchunk_spec(w, layer):
    _, rows, cols = w.shape
    return pl.BlockSpec((None, rows // N_WSTEPS, cols), lambda i: (layer, jnp.minimum(i, N_WSTEPS - 1), 0))


def _layer_row_spec(w, index):
    return pl.BlockSpec((None, 1, w.shape[-1]), lambda i: (index, 0, 0), pipeline_mode=pl.Buffered(1))


def _stage_chunk(i, src_ref, dst_ref):
    rows = src_ref.shape[0]
    dst_ref[pl.ds(pl.multiple_of(i * rows, rows), rows), :] = src_ref[...].astype(BF16)


def _params(semantics):
    return pltpu.CompilerParams(dimension_semantics=semantics, vmem_limit_bytes=VMEM_LIMIT)


def _ada_kernel(c_ref, w_ref, b_ref, o_ref):
    c = c_ref[...]
    h = (c * _sigmoid(c)).astype(BF16)
    o_ref[...] = _dot(h, w_ref[...].astype(BF16)) + b_ref[...]


def _ada_mods(cvec, ada_w, ada_b):
    tn = D_MODEL
    n_out = N_MODS * D_MODEL
    return pl.pallas_call(
        _ada_kernel,
        grid=(DEPTH, n_out // tn),
        in_specs=[
            pl.BlockSpec((N_MOD_ROWS, D_MODEL), lambda l, j: (0, 0)),
            pl.BlockSpec((None, D_MODEL, tn), lambda l, j: (l, 0, j)),
            pl.BlockSpec((None, 1, tn), lambda l, j: (l, 0, j)),
        ],
        out_specs=pl.BlockSpec((None, N_MOD_ROWS, tn), lambda l, j: (l, 0, j)),
        out_shape=jax.ShapeDtypeStruct((DEPTH, N_MOD_ROWS, n_out), F32),
        compiler_params=_params(("arbitrary", "arbitrary")),
        name="ada_mods",
    )(cvec, ada_w, ada_b.reshape(DEPTH, 1, n_out))


def _ffn_kernel(*refs, base, tm, n_x, n_out):
    x_refs = refs[:n_x]
    mod_ref, win_ref, wout_ref, g_ref, b_ref = refs[n_x:n_x + 5]
    o_refs = refs[n_x + 5:n_x + 5 + n_out]
    win_bf, wout_bf = refs[n_x + 5 + n_out:]
    i = pl.program_id(0)
    is_ctx = i < N_WSTEPS + N_CTX // tm

    @pl.when(i < N_WSTEPS)
    def _():
        _stage_chunk(i, win_ref, win_bf)
        _stage_chunk(i, wout_ref, wout_bf)

    @pl.when(i >= N_WSTEPS)
    def _():
        if n_x == 2:
            x = jnp.where(is_ctx, x_refs[0][...], x_refs[1][...])
        else:
            x = x_refs[0][...]
        shift = mod_ref[base:base + 1, :]
        scale = mod_ref[base + 1:base + 2, :]
        gate = mod_ref[base + 2:base + 3, :]
        h = (x * (1.0 + scale) + shift).astype(BF16)
        y = None
        for lo, hi in FFN_CHUNKS:
            g = _dot(h, win_bf[:, lo:hi])
            u = _dot(h, win_bf[:, D_FF + lo:D_FF + hi])
            a = (g * _sigmoid(g) * u).astype(BF16)
            yc = _dot(a, wout_bf[lo:hi, :])
            y = yc if y is None else y + yc
        z = DEEPNORM_ALPHA * x + 0.5 * gate * y
        res = _norm_rows(z) * g_ref[...] + b_ref[...]
        if n_out == 2:
            @pl.when(is_ctx)
            def _():
                o_refs[0][...] = res

            @pl.when(jnp.logical_not(is_ctx))
            def _():
                o_refs[1][...] = res
        else:
            o_refs[0][...] = res


def _ffn(xs, mods_all, w_in, w_out, ln_g, ln_b, *, layer, which, split_out=False, tm=512):
    row = _row_spec(tm, D_MODEL, N_WSTEPS)
    pair = [_ctx_row_spec(tm, D_MODEL, N_WSTEPS), _lat_row_spec(tm, D_MODEL, N_WSTEPS)]
    ln_index = layer * 3 + which
    if split_out:
        out_specs = pair
        out_shape = [jax.ShapeDtypeStruct((N_CTX, D_MODEL), F32), jax.ShapeDtypeStruct((N_LAT, D_MODEL), F32)]
    else:
        out_specs = [row]
        out_shape = [jax.ShapeDtypeStruct((N_TOK, D_MODEL), F32)]
    return pl.pallas_call(
        functools.partial(_ffn_kernel, base=3 * which, tm=tm, n_x=len(xs), n_out=len(out_specs)),
        grid=(N_WSTEPS + N_TOK // tm,),
        in_specs=(pair if len(xs) == 2 else [row]) + [
            _mod_spec(tm, layer, N_WSTEPS), _wchunk_spec(w_in, layer), _wchunk_spec(w_out, layer),
            _layer_row_spec(ln_g, ln_index), _layer_row_spec(ln_b, ln_index)],
        out_specs=out_specs,
        out_shape=out_shape,
        scratch_shapes=[pltpu.VMEM((D_MODEL, 2 * D_FF), BF16), pltpu.VMEM((D_FF, D_MODEL), BF16)],
        compiler_params=_params(("arbitrary",)),
        name="ffn",
    )(*xs, mods_all, w_in, w_out, ln_g, ln_b)


def _conv_stages(prev_halo, main, next_halo, w_ref, b_ref, g_ref, beta_ref, o_ref, pad_ref, shift_ref, acc_ref):
    rows = 32
    lanes = 128
    first = CONV_HALO - CONV_WIDTH // 2

    def setup():
        pad_ref[0:CONV_HALO, :] = prev_halo()
        pad_ref[CONV_HALO:CONV_HALO + CONV_BLOCK, :] = main()
        pad_ref[CONV_HALO + CONV_BLOCK:, :] = next_halo()
        span = shift_ref.shape[1]
        for ph in range(SUBLANES):
            shift_ref[ph] = pad_ref[ph:ph + span, :]

    def taps(c, r):
        cs = slice(c * lanes, (c + 1) * lanes)
        acc = jnp.broadcast_to(b_ref[:, cs], (rows, lanes))
        for j in range(CONV_WIDTH):
            ph = (first + j) % SUBLANES
            start = r * rows + (first + j) - ph
            acc = acc + w_ref[j:j + 1, cs] * shift_ref[ph, start:start + rows, cs]
        acc_ref[r * rows:(r + 1) * rows, cs] = acc

    def finish():
        y = _norm_rows(acc_ref[...]) * g_ref[...] + beta_ref[...]
        o_ref[...] = (y * _sigmoid(y)).astype(BF16)

    def lane_group_half(c, half):
        per_half = CONV_BLOCK // rows // 2
        for r in range(half * per_half, (half + 1) * per_half):
            taps(c, r)

    chunks = [functools.partial(lane_group_half, c, half) for c in range(CONV_DIM // lanes) for half in range(2)]
    return [setup] + chunks + [finish]


def _proj_kernel(*refs, tm, carry):
    (x_ref, mod_ref, cos_ref, sin_lo_ref, sin_hi_ref, wt_ref, wuq_ref, wk_ref, wvt_ref, gq_ref,
     gkv_ref, cw_ref, cb_ref, cg_ref, cbeta_ref) = refs[:15]
    refs = refs[15:]
    if carry:
        ckv_prev_ref, kr_prev_ref = refs[:2]
        refs = refs[2:]
    (a_ref, rq_ref, rk_ref, rv_ref, rg_ref, q_ref, kcat_ref, vt_ref, sig_ref, ckv_ref, kr_ref,
     glu_ring, pad_ref, shift_ref, acc_ref) = refs
    t = pl.program_id(0)
    n_tiles = N_TOK // tm
    n_ctx_tiles = N_CTX // tm
    tiles_per_seq = DEC_SEQ // tm
    is_ctx = t < n_ctx_tiles
    pos = (t - 1 - n_ctx_tiles) % tiles_per_seq
    conv_latent = t - 1 >= n_ctx_tiles
    has_prev = jnp.logical_and(conv_latent, pos != 0)
    has_next = jnp.logical_and(conv_latent, pos != tiles_per_seq - 1)
    slot = t % 2
    no_halo = jnp.zeros((CONV_HALO, CONV_DIM), F32)

    def conv_prev_tile(next_rows):
        return _conv_stages(
            lambda: jnp.where(has_prev, glu_ring[slot, tm - CONV_HALO:tm, :], no_halo),
            lambda: glu_ring[1 - slot],
            lambda: jnp.where(has_next, next_rows, no_halo),
            cw_ref, cb_ref, cg_ref, cbeta_ref, a_ref, pad_ref, shift_ref, acc_ref)

    @pl.when(t == 0)
    def _():
        glu_ring[...] = jnp.zeros(glu_ring.shape, F32)

    @pl.when(t == n_tiles)
    def _():
        for step in conv_prev_tile(no_halo):
            step()

    @pl.when(t < n_tiles)
    def _():
        _proj_tile(x_ref, mod_ref, cos_ref, sin_lo_ref, sin_hi_ref, wt_ref, wuq_ref, wk_ref, wvt_ref, gq_ref,
                   gkv_ref, ckv_prev_ref if carry else None, kr_prev_ref if carry else None,
                   rq_ref, rk_ref, rv_ref, rg_ref, q_ref, kcat_ref, vt_ref, sig_ref, ckv_ref, kr_ref,
                   glu_ring, conv_prev_tile, slot, is_ctx, tm)


def _proj_tile(x_ref, mod_ref, cos_ref, sin_lo_ref, sin_hi_ref, wt_ref, wuq_ref, wk_ref, wvt_ref, gq_ref,
               gkv_ref, ckv_prev_ref, kr_prev_ref,
               rq_ref, rk_ref, rv_ref, rg_ref, q_ref, kcat_ref, vt_ref, sig_ref, ckv_ref, kr_ref,
               glu_ring, conv_prev_tile, slot, is_ctx, tm):
    carry = ckv_prev_ref is not None
    x = x_ref[...]
    u = (x * (1.0 + mod_ref[4:5, :]) + mod_ref[3:4, :]).astype(BF16)
    widths = (CONV_DIM, CONV_DIM, RET_HEADS * RET_DK, RET_HEADS * RET_DK, RET_HEADS * RET_DV, RET_HEADS * RET_DV,
              MLA_Q_LORA, MLA_KV_LORA)
    starts = [sum(widths[:n]) for n in range(len(widths))]

    def proj(n):
        return _dot_nt(u, wt_ref[starts[n]:starts[n] + widths[n], :])

    glu = proj(0) * _sigmoid(proj(1))
    conv_steps = conv_prev_tile(glu[0:CONV_HALO, :])

    def conv(n):
        for _ in range(min(n, len(conv_steps))):
            conv_steps.pop(0)()

    conv(1)
    glu_ring[slot] = glu

    mq = proj(6)
    mkv = proj(7)
    conv(2)
    cos = cos_ref[...]
    sin_lo = sin_lo_ref[...]
    sin_hi = sin_hi_ref[...]

    def rotary(v):
        up = pltpu.roll(v, HEAD_PAD - ROPE_AXIS_HALF, 1)
        down = pltpu.roll(v, ROPE_AXIS_HALF, 1)
        return v * cos + up * sin_lo + down * sin_hi

    qn = (_rms_rows(mq) * gq_ref[...]).astype(BF16)
    qm = _dot(qn, wuq_ref[...])
    ckv = _rms_rows(mkv) * gkv_ref[...]
    ckvb = ckv.astype(BF16)
    kn = _dot(ckvb, wk_ref[...])
    vt_ref[...] = _dot_nt(wvt_ref[...], ckvb).astype(BF16)
    kr_grp = _dot_nt(u, wt_ref[MAIN_W:MAIN_W + HEAD_PAD, :])
    conv(2)
    lane = lax.broadcasted_iota(jnp.int32, kr_grp.shape, 1)
    in_rope = jnp.logical_and(lane >= ROPE_LANE0, lane < ROPE_LANE0 + MLA_D_ROPE)
    kr = jnp.where(in_rope, pltpu.roll(kr_grp, ROPE_LANE0, 1), 0.0)
    kr_rot = rotary(kr)
    for h in range(MLA_HEADS):
        sl = slice(h * HEAD_PAD, (h + 1) * HEAD_PAD)
        q_ref[:, sl] = (rotary(qm[:, sl]) * ATTN_QSCALE).astype(BF16)
        kcat_ref[:, sl] = (kn[:, sl] + kr_rot).astype(BF16)

    gate0 = MAIN_W + MLA_D_ROPE
    for blk in range(N_BRANCHES):
        cols = slice(blk * D_MODEL, (blk + 1) * D_MODEL)
        sig_ref[:, cols] = _sigmoid(_dot_nt(u, wt_ref[gate0 + cols.start:gate0 + cols.stop, :])).astype(BF16)
        conv(3)
    rg = proj(5)
    rg_ref[...] = (rg * _sigmoid(rg)).astype(BF16)
    conv(3)
    rv_ref[...] = proj(4).astype(BF16)
    conv(len(conv_steps))
    rk_ref[...] = (proj(3) * (RET_DK ** -0.5)).astype(BF16)
    rq_ref[...] = proj(2).astype(BF16)

    @pl.when(is_ctx)
    def _():
        seqs = tm // SEQ
        ckv3 = ckv.reshape(seqs, SEQ, MLA_KV_LORA)
        kr3 = kr_grp[:, :MLA_D_ROPE].reshape(seqs, SEQ, MLA_D_ROPE)
        if carry:
            ckv_ref[:, 0] = ckv_prev_ref[...]
            kr_ref[:, 0] = kr_prev_ref[...]
            ckv_ref[:, 1] = ckv3
            kr_ref[:, 1] = kr3
        else:
            ckv_ref[...] = ckv3
            kr_ref[...] = kr3


def _proj(x, mods_all, rope_tabs, wt_all, wuq, wk, wvt, gq, gkv, conv_w, carry, *, layer):
    tm = CONV_BLOCK
    n_tiles = N_TOK // tm
    n_ctx_tiles = N_CTX // tm
    tiles_per_seq = DEC_SEQ // tm
    seqs = tm // SEQ

    def rope_index(i):
        j = _tile(i, 0, tm)
        return (jnp.where(j < n_ctx_tiles, 0, tiles_per_seq + (j - n_ctx_tiles) % tiles_per_seq), 0)

    def row(w):
        return _row_spec(tm, w, 0)

    def out(w, dt):
        return jax.ShapeDtypeStruct((N_TOK, w), dt)

    def ctx_seq_spec(*tail):
        zeros = (0,) * len(tail)
        return pl.BlockSpec((seqs,) + tail, lambda i: (jnp.minimum(i, n_ctx_tiles - 1),) + zeros)

    rope = pl.BlockSpec((tm, HEAD_PAD), rope_index)
    in_specs = [row(D_MODEL), _mod_spec(tm, layer, 0), rope, rope, rope,
                _layer_resident(wt_all, layer), _resident(wuq.shape), _resident(wk.shape), _resident(wvt.shape),
                _resident(gq.shape), _resident(gkv.shape)] + [_resident(w.shape) for w in conv_w]
    args = [x, mods_all, *rope_tabs, wt_all, wuq, wk, wvt, gq, gkv, *conv_w]
    if carry is None:
        ctx_specs = [ctx_seq_spec(SEQ, MLA_KV_LORA), ctx_seq_spec(SEQ, MLA_D_ROPE)]
        ctx_shapes = [jax.ShapeDtypeStruct((BATCH, SEQ, MLA_KV_LORA), F32),
                      jax.ShapeDtypeStruct((BATCH, SEQ, MLA_D_ROPE), F32)]
    else:
        in_specs += [ctx_seq_spec(SEQ, MLA_KV_LORA), ctx_seq_spec(SEQ, MLA_D_ROPE)]
        args += list(carry)
        ctx_specs = [ctx_seq_spec(DEPTH, SEQ, MLA_KV_LORA), ctx_seq_spec(DEPTH, SEQ, MLA_D_ROPE)]
        ctx_shapes = [jax.ShapeDtypeStruct((BATCH, DEPTH, SEQ, MLA_KV_LORA), F32),
                      jax.ShapeDtypeStruct((BATCH, DEPTH, SEQ, MLA_D_ROPE), F32)]
    span = CONV_BLOCK + 2 * CONV_HALO - SUBLANES
    return pl.pallas_call(
        functools.partial(_proj_kernel, tm=tm, carry=carry is not None),
        grid=(n_tiles + 1,),
        in_specs=in_specs,
        out_specs=[pl.BlockSpec((tm, CONV_DIM), lambda i: (jnp.maximum(i - 1, 0), 0)),
                   row(RET_HEADS * RET_DK), row(RET_HEADS * RET_DK), row(RET_HEADS * RET_DV),
                   row(RET_HEADS * RET_DV), row(MLA_W), row(MLA_W),
                   pl.BlockSpec((MLA_V_W, tm), lambda i: (0, _tile(i, 0, tm))), row(GATE_W)] + ctx_specs,
        out_shape=[out(CONV_DIM, BF16), out(RET_HEADS * RET_DK, BF16), out(RET_HEADS * RET_DK, BF16),
                   out(RET_HEADS * RET_DV, BF16), out(RET_HEADS * RET_DV, BF16), out(MLA_W, BF16),
                   out(MLA_W, BF16), jax.ShapeDtypeStruct((MLA_V_W, N_TOK), BF16), out(GATE_W, BF16)] + ctx_shapes,
        scratch_shapes=[pltpu.VMEM((2, tm, CONV_DIM), F32),
                        pltpu.VMEM((CONV_BLOCK + 2 * CONV_HALO, CONV_DIM), F32),
                        pltpu.VMEM((SUBLANES, span, CONV_DIM), F32),
                        pltpu.VMEM((CONV_BLOCK, CONV_DIM), F32)],
        compiler_params=_params(("arbitrary",)),
        name="mix_proj",
    )(*args)


def _cache_kv_kernel(ckv_ref, kr_ref, wk_ref, wvt_ref, kcat_ref, vt_ref):
    ckvb = ckv_ref[...].astype(BF16)
    kn = _dot(ckvb, wk_ref[...])
    vt_ref[...] = _dot_nt(wvt_ref[...], ckvb).astype(BF16)
    kr = kr_ref[...]
    for h in range(MLA_HEADS):
        sl = slice(h * HEAD_PAD, (h + 1) * HEAD_PAD)
        kcat_ref[:, sl] = (kn[:, sl] + kr).astype(BF16)


def _cache_kv(cache_ckv, cache_kr_pad, wk, wvt):
    n = DEC_BATCH * PAST_LEN
    return pl.pallas_call(
        _cache_kv_kernel,
        grid=(DEPTH, DEC_BATCH),
        in_specs=[pl.BlockSpec((None, None, PAST_LEN, MLA_KV_LORA), lambda l, b: (b, l, 0, 0)),
                  pl.BlockSpec((None, None, PAST_LEN, HEAD_PAD), lambda l, b: (b, l, 0, 0)),
                  pl.BlockSpec((None, MLA_KV_LORA, MLA_W), lambda l, b: (l, 0, 0)),
                  pl.BlockSpec((None, MLA_V_W, MLA_KV_LORA), lambda l, b: (l, 0, 0))],
        out_specs=[pl.BlockSpec((None, PAST_LEN, MLA_W), lambda l, b: (l, b, 0)),
                   pl.BlockSpec((None, MLA_V_W, PAST_LEN), lambda l, b: (l, 0, b))],
        out_shape=[jax.ShapeDtypeStruct((DEPTH, n, MLA_W), BF16), jax.ShapeDtypeStruct((DEPTH, MLA_V_W, n), BF16)],
        compiler_params=_params(("arbitrary", "arbitrary")),
        name="cache_kv",
    )(cache_ckv, cache_kr_pad, wk, wvt)


def _ret_kernel(*refs, t, hp, latent, carry):
    if latent:
        lg_ref, q_ref, k_ref, v_ref, g_ref, s0f_ref, s0b_ref, o_ref, d_ref = refs
    elif carry:
        lg_ref, q_ref, k_ref, v_ref, g_ref, sf_prev_ref, sb_prev_ref, o_ref, sf_ref, sb_ref, d_ref = refs
    else:
        lg_ref, q_ref, k_ref, v_ref, g_ref, o_ref, sf_ref, sb_ref, d_ref = refs
    hblk = pl.program_id(0)

    @pl.when(pl.program_id(1) == 0)
    def _():
        diff = (lax.broadcasted_iota(jnp.int32, (t, t), 0) - lax.broadcasted_iota(jnp.int32, (t, t), 1)).astype(F32)
        for hh in range(hp):
            lgf = lg_ref[0, hblk * hp + hh]
            lgb = lg_ref[1, hblk * hp + hh]
            d_ref[hh] = jnp.exp(jnp.where(diff >= 0, diff * lgf, -diff * lgb))

    if carry:
        sf_ref[0] = sf_prev_ref[...]
        sb_ref[0] = sb_prev_ref[...]
    pos = lax.broadcasted_iota(jnp.int32, (t, 1), 0).astype(F32)
    for hh in range(hp):
        lgf = lg_ref[0, hblk * hp + hh]
        lgb = lg_ref[1, hblk * hp + hh]
        q = q_ref[:, hh * RET_DK:(hh + 1) * RET_DK]
        k = k_ref[:, hh * RET_DK:(hh + 1) * RET_DK]
        v = v_ref[:, hh * RET_DV:(hh + 1) * RET_DV]
        p = (_dot_nt(q, k) * d_ref[hh]).astype(BF16)
        o = _dot(p, v)
        if latent:
            o = o + jnp.exp((pos + 1.0) * lgf) * _dot(q, s0f_ref[hh].astype(BF16))
            o = o + jnp.exp((t - pos) * lgb) * _dot(q, s0b_ref[hh].astype(BF16))
        else:
            kf = k.astype(F32)
            sf = _dot_tn((kf * jnp.exp((t - 1.0 - pos) * lgf)).astype(BF16), v)
            sb = _dot_tn((kf * jnp.exp(pos * lgb)).astype(BF16), v)
            if carry:
                sf_ref[1, hh] = sf
                sb_ref[1, hh] = sb
            else:
                sf_ref[hh] = sf
                sb_ref[hh] = sb
        o_ref[:, hh * RET_DV:(hh + 1) * RET_DV] = (g_ref[:, hh * RET_DV:(hh + 1) * RET_DV] * _norm_rows(o)).astype(BF16)


def _retention(log_g, rq, rk, rv, rg, states, *, latent, layer):
    t = DEC_SEQ if latent else SEQ
    hp = 1 if latent else RET_HEADS
    n_seq = DEC_BATCH if latent else BATCH
    row0 = (N_CTX // t) if latent else 0

    def row(w):
        return pl.BlockSpec((t, hp * w), lambda h, s: (row0 + s, h))

    smem = pl.BlockSpec(memory_space=pltpu.SMEM)
    out_shape = [jax.ShapeDtypeStruct((n_seq * t, RET_HEADS * RET_DV), BF16)]
    out_specs = [pl.BlockSpec((t, hp * RET_DV), lambda h, s: (s, h))]
    in_specs = [smem, row(RET_DK), row(RET_DK), row(RET_DV), row(RET_DV)]
    args = [log_g, rq, rk, rv, rg]
    carry = False
    if latent:
        st = pl.BlockSpec((None, None, hp, RET_DK, RET_DV), lambda h, s: (s, layer, h, 0, 0))
        in_specs += [st, st]
        args += list(states)
    else:
        st = pl.BlockSpec((None, hp, RET_DK, RET_DV), lambda h, s: (s, h, 0, 0))
        if states is None:
            out_specs += [st, st]
            out_shape += [jax.ShapeDtypeStruct((BATCH, RET_HEADS, RET_DK, RET_DV), F32)] * 2
        else:
            carry = True
            in_specs += [st, st]
            args += list(states)
            st2 = pl.BlockSpec((None, DEPTH, hp, RET_DK, RET_DV), lambda h, s: (s, 0, h, 0, 0))
            out_specs += [st2, st2]
            out_shape += [jax.ShapeDtypeStruct((BATCH, DEPTH, RET_HEADS, RET_DK, RET_DV), F32)] * 2

    return pl.pallas_call(
        functools.partial(_ret_kernel, t=t, hp=hp, latent=latent, carry=carry),
        grid=(RET_HEADS // hp, n_seq),
        in_specs=in_specs,
        out_specs=out_specs,
        out_shape=out_shape,
        scratch_shapes=[pltpu.VMEM((hp, t, t), F32)],
        compiler_params=_params(("arbitrary", "arbitrary")),
        name="retention_lat" if latent else "retention_ctx",
    )(*args)


def _attn_kernel(*refs, t, hp, latent):
    if latent:
        q_ref, k_ref, vt_ref, kc_ref, vtc_ref, o_ref = refs
    else:
        q_ref, k_ref, vt_ref, o_ref = refs
    qb = min(ATTN_QBLOCK, t)
    units = [(slice(hh * HEAD_PAD, (hh + 1) * HEAD_PAD), slice(hh * MLA_D_V, (hh + 1) * MLA_D_V),
              slice(b * qb, (b + 1) * qb)) for b in range(t // qb) for hh in range(hp)]
    def scores(unit):
        sl, _, rows = unit
        q = q_ref[rows, sl]
        s = [_dot_nt(k_ref[:, sl], q)]
        if latent:
            s.append(_dot_nt(kc_ref[:, sl], q))
        return s

    def softmax(s):
        m = functools.reduce(jnp.maximum, [jnp.max(x, axis=0, keepdims=True) for x in s])
        e = [jnp.exp2(x - m) for x in s]
        den = functools.reduce(jnp.add, [jnp.sum(x, axis=0, keepdims=True) for x in e])
        return [x.astype(BF16) for x in e], den

    def values(unit, e, den):
        _, vs, _ = unit
        o = _dot(vt_ref[vs, :], e[0])
        if latent:
            o = o + _dot(vtc_ref[vs, :], e[1])
        return o / den

    pairs = [units[u:u + 2] for u in range(0, len(units), 2)]
    s_next = [scores(u) for u in pairs[0]]
    sm_prev = None
    for g in range(len(pairs) + 1):
        s_cur = s_next
        if g + 1 < len(pairs):
            s_next = [scores(u) for u in pairs[g + 1]]
        sm_cur = [softmax(s) for s in s_cur] if g < len(pairs) else None
        if sm_prev is not None:
            (_, vs0, rows), (_, vs1, _) = pairs[g - 1]
            outs = [values(u, e, den) for u, (e, den) in zip(pairs[g - 1], sm_prev)]
            o_ref[rows, vs0.start:vs1.stop] = jnp.concatenate(outs, axis=0).T.astype(BF16)
        sm_prev = sm_cur


def _attention(q, kcat, vt, cache, *, latent):
    t = DEC_SEQ if latent else SEQ
    hp = 2 if latent else MLA_HEADS
    n_seq = DEC_BATCH if latent else BATCH
    row0 = (N_CTX // t) if latent else 0
    row = pl.BlockSpec((t, hp * HEAD_PAD), lambda s, h: (row0 + s, h))
    col = pl.BlockSpec((hp * MLA_D_V, t), lambda s, h: (h, row0 + s))
    in_specs = [row, row, col]
    args = [q, kcat, vt]
    if latent:
        in_specs += [pl.BlockSpec((PAST_LEN, hp * HEAD_PAD), lambda s, h: (s, h)),
                     pl.BlockSpec((hp * MLA_D_V, PAST_LEN), lambda s, h: (h, s))]
        args += list(cache)
    return pl.pallas_call(
        functools.partial(_attn_kernel, t=t, hp=hp, latent=latent),
        grid=(n_seq, MLA_HEADS // hp),
        in_specs=in_specs,
        out_specs=pl.BlockSpec((t, hp * MLA_D_V), lambda s, h: (s, h)),
        out_shape=jax.ShapeDtypeStruct((n_seq * t, MLA_V_W), BF16),
        compiler_params=_params(("arbitrary", "arbitrary")),
        name="attention_lat" if latent else "attention_ctx",
    )(*args)


def _merge_kernel(x_ref, mod_ref, a_ref, rc_ref, rl_ref, mc_ref, ml_ref, sig_ref, wc_ref, wr_ref, wm_ref, wo_ref,
                  g_ref, b_ref, o_ref, wc_bf, wr_bf, wm_bf, wo_bf, *, tm):
    i = pl.program_id(0)
    is_ctx = i < N_WSTEPS + N_CTX // tm

    @pl.when(i < N_WSTEPS)
    def _():
        _stage_chunk(i, wc_ref, wc_bf)
        _stage_chunk(i, wr_ref, wr_bf)
        _stage_chunk(i, wo_ref, wo_bf)
        _stage_chunk(i, wm_ref, wm_bf)

    @pl.when(i >= N_WSTEPS)
    def _():
        x = x_ref[...]
        r = jnp.where(is_ctx, rc_ref[...], rl_ref[...])
        m = jnp.where(is_ctx, mc_ref[...], ml_ref[...])
        merged = sig_ref[:, 0:D_MODEL] * _dot(a_ref[...], wc_bf[...])
        merged = merged + sig_ref[:, D_MODEL:2 * D_MODEL] * _dot(r, wr_bf[...])
        merged = merged + sig_ref[:, 2 * D_MODEL:] * _dot(m, wm_bf[...])
        y = _dot(merged.astype(BF16), wo_bf[...])
        z = DEEPNORM_ALPHA * x + mod_ref[5:6, :] * y
        o_ref[...] = _norm_rows(z) * g_ref[...] + b_ref[...]


def _merge(x, mods_all, a, r_pair, m_pair, sig, wc, wr, wm, wo, ln_g, ln_b, *, layer, tm=512):
    def row(w):
        return _row_spec(tm, w, N_WSTEPS)

    def pair(w):
        return [_ctx_row_spec(tm, w, N_WSTEPS), _lat_row_spec(tm, w, N_WSTEPS)]

    ln_index = layer * 3 + 1
    return pl.pallas_call(
        functools.partial(_merge_kernel, tm=tm),
        grid=(N_WSTEPS + N_TOK // tm,),
        in_specs=[row(D_MODEL), _mod_spec(tm, layer, N_WSTEPS), row(CONV_DIM)] + pair(RET_HEADS * RET_DV)
        + pair(MLA_V_W) + [row(GATE_W), _wchunk_spec(wc, layer), _wchunk_spec(wr, layer), _wchunk_spec(wm, layer),
                         _wchunk_spec(wo, layer), _layer_row_spec(ln_g, ln_index), _layer_row_spec(ln_b, ln_index)],
        out_specs=row(D_MODEL),
        out_shape=jax.ShapeDtypeStruct((N_TOK, D_MODEL), F32),
        scratch_shapes=[pltpu.VMEM((CONV_DIM, D_MODEL), BF16), pltpu.VMEM((RET_HEADS * RET_DV, D_MODEL), BF16),
                        pltpu.VMEM((MLA_V_W, D_MODEL), BF16), pltpu.VMEM((D_MODEL, D_MODEL), BF16)],
        compiler_params=_params(("arbitrary",)),
        name="merge",
    )(x, mods_all, a, *r_pair, *m_pair, sig, wc, wr, wm, wo, ln_g, ln_b)


def _rope_tables():
    rows = DEC_SEQ // GRID_W
    row_id = jnp.repeat(jnp.arange(rows, dtype=F32), GRID_W)
    col_id = jnp.tile(jnp.arange(GRID_W, dtype=F32), rows)
    inv_freq = ROPE_BASE ** (-jnp.arange(ROPE_AXIS_HALF, dtype=F32) / ROPE_AXIS_HALF)
    ang = jnp.stack([row_id[:, None] * inv_freq, col_id[:, None] * inv_freq], axis=1)
    cos = jnp.cos(ang)
    sin = jnp.sin(ang)
    cos32 = jnp.stack([cos, cos], axis=2).reshape(DEC_SEQ, MLA_D_ROPE)
    zero = jnp.zeros_like(sin)
    sin_lo32 = jnp.stack([-sin, zero], axis=2).reshape(DEC_SEQ, MLA_D_ROPE)
    sin_hi32 = jnp.stack([zero, sin], axis=2).reshape(DEC_SEQ, MLA_D_ROPE)
    tail = HEAD_PAD - ROPE_LANE0 - MLA_D_ROPE
    cos_t = jnp.concatenate([jnp.ones((DEC_SEQ, ROPE_LANE0), F32), cos32, jnp.ones((DEC_SEQ, tail), F32)], axis=1)
    cos_t = jnp.concatenate([jnp.ones((DEC_SEQ, HEAD_PAD), F32), cos_t], axis=0)

    def sin_table(s32):
        return jnp.pad(s32, ((DEC_SEQ, 0), (ROPE_LANE0, tail)))

    return cos_t, sin_table(sin_lo32), sin_table(sin_hi32)


def _head_pad_cols(w, width):
    k = w.shape[0]
    w = w.reshape(k, MLA_HEADS, width)
    return jnp.pad(w, ((0, 0), (0, 0), (0, HEAD_PAD - width))).reshape(k, MLA_W)


def kernel(x_prompt, x_sample, cache_mla_ckv, cache_mla_krope, state_ret_fwd, state_ret_bwd, c, c_ctx, ada_w, ada_b, ffn1_w_in, ffn1_w_out, ffn2_w_in, ffn2_w_out, post_ln_g, post_ln_b, mix_w_in, conv_w_dw, conv_b_dw, conv_ln_g, conv_ln_b, conv_w_out, ret_decay_fwd, ret_decay_bwd, ret_w_out, mla_q_norm, mla_w_uq, mla_kv_norm, mla_w_ukv, mla_w_out, mix_w_o):
    assert DEPTH == 2
    cvec = jnp.concatenate([c_ctx[None, :], c, jnp.zeros((N_MOD_ROWS - 1 - DEC_BATCH, D_MODEL), F32)], axis=0)
    mods_all = _ada_mods(cvec, ada_w, ada_b).reshape(DEPTH, N_MOD_ROWS, N_MODS, D_MODEL)
    rope_tabs = _rope_tables()

    w_ukv = mla_w_ukv.reshape(DEPTH, MLA_KV_LORA, MLA_HEADS, MLA_D_NOPE + MLA_D_V)
    wk_all = _head_pad_cols(w_ukv[..., :MLA_D_NOPE].reshape(DEPTH * MLA_KV_LORA, -1), MLA_D_NOPE)
    wk_all = wk_all.reshape(DEPTH, MLA_KV_LORA, MLA_W).astype(BF16)
    wvt_all = jnp.swapaxes(w_ukv[..., MLA_D_NOPE:].reshape(DEPTH, MLA_KV_LORA, MLA_V_W), 1, 2).astype(BF16)
    kr_tail = HEAD_PAD - ROPE_LANE0 - MLA_D_ROPE
    cache_kr_pad = jnp.pad(cache_mla_krope, ((0, 0), (0, 0), (0, 0), (ROPE_LANE0, kr_tail)))
    kcat_c, vt_c = _cache_kv(cache_mla_ckv, cache_kr_pad, wk_all, wvt_all)
    log_g = jnp.stack([jax.nn.log_sigmoid(ret_decay_fwd), jax.nn.log_sigmoid(ret_decay_bwd)], axis=1)
    wt_all = jnp.swapaxes(mix_w_in, 1, 2).astype(BF16)

    ln_g = post_ln_g.reshape(DEPTH * 3, 1, D_MODEL)
    ln_b = post_ln_b.reshape(DEPTH * 3, 1, D_MODEL)
    xs = (x_prompt.reshape(N_CTX, D_MODEL), x_sample.reshape(N_LAT, D_MODEL))
    ctx_carry = None
    state_carry = None
    for l in range(DEPTH):
        last = l == DEPTH - 1
        x = _ffn(xs, mods_all, ffn1_w_in, ffn1_w_out, ln_g, ln_b, layer=l, which=0)[0]

        wuq = _head_pad_cols(mla_w_uq[l], MLA_D_NOPE + MLA_D_ROPE).astype(BF16)
        conv_w = (conv_w_dw[l], conv_b_dw[l].reshape(1, -1), conv_ln_g[l].reshape(1, -1), conv_ln_b[l].reshape(1, -1))
        (a, rq, rk, rv, rg, q, kcat, vt, sig, ckv, kr) = _proj(
            x, mods_all, rope_tabs, wt_all, wuq, wk_all[l], wvt_all[l],
            mla_q_norm[l].reshape(1, -1), mla_kv_norm[l].reshape(1, -1), conv_w, ctx_carry, layer=l)
        ctx_carry = (ckv, kr)

        r_ctx, sf, sb = _retention(log_g[l], rq, rk, rv, rg, state_carry, latent=False, layer=l)
        state_carry = (sf, sb)
        r_lat = _retention(log_g[l], rq, rk, rv, rg, (state_ret_fwd, state_ret_bwd), latent=True, layer=l)[0]
        m_ctx = _attention(q, kcat, vt, None, latent=False)
        m_lat = _attention(q, kcat, vt, (kcat_c[l], vt_c[l]), latent=True)

        x = _merge(x, mods_all, a, (r_ctx, r_lat), (m_ctx, m_lat), sig, conv_w_out, ret_w_out, mla_w_out, mix_w_o,
                   ln_g, ln_b, layer=l)
        xs = _ffn((x,), mods_all, ffn2_w_in, ffn2_w_out, ln_g, ln_b, layer=l, which=2, split_out=last)

    y_ctx, y_lat = xs
    return (y_ctx.reshape(BATCH, SEQ, D_MODEL), y_lat.reshape(DEC_BATCH, DEC_SEQ, D_MODEL),
            ctx_carry[0], ctx_carry[1], state_carry[0], state_carry[1])
```

```python
import functools
import math

import jax
import jax.numpy as jnp
from jax import lax
from jax.experimental import pallas as pl
from jax.experimental.pallas import tpu as pltpu

F32 = jnp.float32
BF16 = jnp.bfloat16

D_MODEL = 1024
BATCH = 16
SEQ = 256
DEPTH = 2
DEC_BATCH = 4
DEC_SEQ = 1024
PAST_LEN = 256
GRID_W = 64
D_FF = 2816
N_MODS = 9
CONV_DIM = 512
CONV_WIDTH = 31
RET_HEADS = 4
RET_DK = 128
RET_DV = 256
MLA_HEADS = 8
MLA_Q_LORA = 512
MLA_KV_LORA = 256
MLA_D_NOPE = 64
MLA_D_ROPE = 32
MLA_D_V = 64
ROPE_AXIS_HALF = MLA_D_ROPE // 4
ROPE_BASE = 10000.0
DEEPNORM_ALPHA = (2 * DEPTH) ** 0.25
LN_EPS = 1e-5
RMS_EPS = 1e-6

N_CTX = BATCH * SEQ
N_LAT = DEC_BATCH * DEC_SEQ
N_TOK = N_CTX + N_LAT
N_MOD_ROWS = 8
HEAD_PAD = 128
ROPE_LANE0 = MLA_D_NOPE
MLA_W = MLA_HEADS * HEAD_PAD
MLA_V_W = MLA_HEADS * MLA_D_V
MAIN_W = 2 * CONV_DIM + 2 * RET_HEADS * RET_DK + 2 * RET_HEADS * RET_DV + MLA_Q_LORA + MLA_KV_LORA
N_BRANCHES = 3
GATE_W = N_BRANCHES * D_MODEL
SUBLANES = 8
CONV_HALO = 16
CONV_BLOCK = 256
VMEM_LIMIT = 56 * 1024 * 1024
N_WSTEPS = 8
PROJ_WSTEPS = 6
MXU_COLS = 256
FFN_CHUNKS = ((0, 6 * MXU_COLS), (6 * MXU_COLS, D_FF))
ATTN_QBLOCK = 256
ATTN_QSCALE = (MLA_D_NOPE + MLA_D_ROPE) ** -0.5 * math.log2(math.e)


def _dot(a, b):
    return jnp.dot(a, b, preferred_element_type=F32)


def _dot_nt(a, b):
    return lax.dot_general(a, b, (((1,), (1,)), ((), ())), preferred_element_type=F32)


def _dot_tn(a, b):
    return lax.dot_general(a, b, (((0,), (0,)), ((), ())), preferred_element_type=F32)


def _sigmoid(x):
    return 1.0 / (1.0 + jnp.exp(-x))


def _norm_rows(z):
    mu = jnp.mean(z, axis=-1, keepdims=True)
    zc = z - mu
    var = jnp.mean(zc * zc, axis=-1, keepdims=True)
    return zc * lax.rsqrt(var + LN_EPS)


def _rms_rows(z):
    return z * lax.rsqrt(jnp.mean(z * z, axis=-1, keepdims=True) + RMS_EPS)


def _resident(shape):
    zeros = (0,) * len(shape)
    return pl.BlockSpec(shape, lambda *_: zeros, pipeline_mode=pl.Buffered(1))


def _layer_resident(w, layer):
    zeros = (0,) * (w.ndim - 1)
    return pl.BlockSpec((None,) + w.shape[1:], lambda *_: (layer,) + zeros, pipeline_mode=pl.Buffered(1))


def _tile(i, n_w, tm):
    return jnp.clip(i - n_w, 0, N_TOK // tm - 1)


def _row_spec(tm, w, n_w):
    return pl.BlockSpec((tm, w), lambda i: (_tile(i, n_w, tm), 0))


def _ctx_row_spec(tm, w, n_w):
    last = N_CTX // tm - 1
    return pl.BlockSpec((tm, w), lambda i: (jnp.minimum(_tile(i, n_w, tm), last), 0))


def _lat_row_spec(tm, w, n_w):
    first = N_CTX // tm
    return pl.BlockSpec((tm, w), lambda i: (jnp.maximum(_tile(i, n_w, tm) - first, 0), 0))


def _mod_spec(tm, layer, n_w):
    n_ctx_tiles = N_CTX // tm
    tiles_per_seq = DEC_SEQ // tm

    def index(i):
        j = _tile(i, n_w, tm)
        return (layer, jnp.where(j < n_ctx_tiles, 0, 1 + (j - n_ctx_tiles) // tiles_per_seq), 0, 0)

    return pl.BlockSpec((None, None, N_MODS, D_MODEL), index)


def _wchunk_spec(w, layer, n_w=N_WSTEPS):
    _, rows, cols = w.shape
    return pl.BlockSpec((None, rows // n_w, cols), lambda i: (layer, jnp.minimum(i, n_w - 1), 0))


def _layer_row_spec(w, index):
    return pl.BlockSpec((None, 1, w.shape[-1]), lambda i: (index, 0, 0), pipeline_mode=pl.Buffered(1))


def _stage_chunk(i, src_ref, dst_ref):
    rows = src_ref.shape[0]
    dst_ref[pl.ds(pl.multiple_of(i * rows, rows), rows), :] = src_ref[...].astype(BF16)


def _params(semantics):
    return pltpu.CompilerParams(dimension_semantics=semantics, vmem_limit_bytes=VMEM_LIMIT)


def _ada_kernel(c_ref, w_ref, b_ref, o_ref):
    c = c_ref[...]
    h = (c * _sigmoid(c)).astype(BF16)
    o_ref[...] = _dot(h, w_ref[...].astype(BF16)) + b_ref[...]


def _ada_mods(cvec, ada_w, ada_b):
    tn = D_MODEL
    n_out = N_MODS * D_MODEL
    return pl.pallas_call(
        _ada_kernel,
        grid=(DEPTH, n_out // tn),
        in_specs=[
            pl.BlockSpec((N_MOD_ROWS, D_MODEL), lambda l, j: (0, 0)),
            pl.BlockSpec((None, D_MODEL, tn), lambda l, j: (l, 0, j)),
            pl.BlockSpec((None, 1, tn), lambda l, j: (l, 0, j)),
        ],
        out_specs=pl.BlockSpec((None, N_MOD_ROWS, tn), lambda l, j: (l, 0, j)),
        out_shape=jax.ShapeDtypeStruct((DEPTH, N_MOD_ROWS, n_out), F32),
        compiler_params=_params(("arbitrary", "arbitrary")),
        name="ada_mods",
    )(cvec, ada_w, ada_b.reshape(DEPTH, 1, n_out))


def _ffn_kernel(*refs, base, tm, n_x, n_out):
    x_refs = refs[:n_x]
    mod_ref, win_ref, wout_ref, g_ref, b_ref = refs[n_x:n_x + 5]
    o_refs = refs[n_x + 5:n_x + 5 + n_out]
    win_bf, wout_bf = refs[n_x + 5 + n_out:]
    i = pl.program_id(0)
    is_ctx = i < N_WSTEPS + N_CTX // tm

    @pl.when(i < N_WSTEPS)
    def _():
        _stage_chunk(i, win_ref, win_bf)
        _stage_chunk(i, wout_ref, wout_bf)

    @pl.when(i >= N_WSTEPS)
    def _():
        if n_x == 2:
            x = jnp.where(is_ctx, x_refs[0][...], x_refs[1][...])
        else:
            x = x_refs[0][...]
        shift = mod_ref[base:base + 1, :]
        scale = mod_ref[base + 1:base + 2, :]
        gate = mod_ref[base + 2:base + 3, :]
        h = (x * (1.0 + scale) + shift).astype(BF16)
        y = None
        for lo, hi in FFN_CHUNKS:
            g = _dot(h, win_bf[:, lo:hi])
            u = _dot(h, win_bf[:, D_FF + lo:D_FF + hi])
            a = (g * _sigmoid(g) * u).astype(BF16)
            yc = _dot(a, wout_bf[lo:hi, :])
            y = yc if y is None else y + yc
        z = DEEPNORM_ALPHA * x + 0.5 * gate * y
        res = _norm_rows(z) * g_ref[...] + b_ref[...]
        if n_out == 2:
            @pl.when(is_ctx)
            def _():
                o_refs[0][...] = res

            @pl.when(jnp.logical_not(is_ctx))
            def _():
                o_refs[1][...] = res
        else:
            o_refs[0][...] = res


def _ffn(xs, mods_all, w_in, w_out, ln_g, ln_b, *, layer, which, split_out=False, tm=512):
    row = _row_spec(tm, D_MODEL, N_WSTEPS)
    pair = [_ctx_row_spec(tm, D_MODEL, N_WSTEPS), _lat_row_spec(tm, D_MODEL, N_WSTEPS)]
    ln_index = layer * 3 + which
    if split_out:
        out_specs = pair
        out_shape = [jax.ShapeDtypeStruct((N_CTX, D_MODEL), F32), jax.ShapeDtypeStruct((N_LAT, D_MODEL), F32)]
    else:
        out_specs = [row]
        out_shape = [jax.ShapeDtypeStruct((N_TOK, D_MODEL), F32)]
    return pl.pallas_call(
        functools.partial(_ffn_kernel, base=3 * which, tm=tm, n_x=len(xs), n_out=len(out_specs)),
        grid=(N_WSTEPS + N_TOK // tm,),
        in_specs=(pair if len(xs) == 2 else [row]) + [
            _mod_spec(tm, layer, N_WSTEPS), _wchunk_spec(w_in, layer), _wchunk_spec(w_out, layer),
            _layer_row_spec(ln_g, ln_index), _layer_row_spec(ln_b, ln_index)],
        out_specs=out_specs,
        out_shape=out_shape,
        scratch_shapes=[pltpu.VMEM((D_MODEL, 2 * D_FF), BF16), pltpu.VMEM((D_FF, D_MODEL), BF16)],
        compiler_params=_params(("arbitrary",)),
        name="ffn",
    )(*xs, mods_all, w_in, w_out, ln_g, ln_b)


def _conv_stages(prev_halo, main, next_halo, w_ref, b_ref, g_ref, beta_ref, o_ref, pad_ref, shift_ref, acc_ref):
    rows = 32
    lanes = 128
    first = CONV_HALO - CONV_WIDTH // 2

    def setup(after):
        del after
        pad_ref[0:CONV_HALO, :] = prev_halo()
        pad_ref[CONV_HALO:CONV_HALO + CONV_BLOCK, :] = main()
        pad_ref[CONV_HALO + CONV_BLOCK:, :] = next_halo()
        span = shift_ref.shape[1]
        for ph in range(SUBLANES):
            shift_ref[ph] = pad_ref[ph:ph + span, :]

    def taps(c, r, start_from):
        cs = slice(c * lanes, (c + 1) * lanes)
        acc = jnp.broadcast_to(b_ref[:, cs], (rows, lanes)) + start_from
        for j in range(CONV_WIDTH):
            ph = (first + j) % SUBLANES
            start = r * rows + (first + j) - ph
            acc = acc + w_ref[j:j + 1, cs] * shift_ref[ph, start:start + rows, cs]
        acc_ref[r * rows:(r + 1) * rows, cs] = acc

    def finish(after):
        del after
        y = _norm_rows(acc_ref[...]) * g_ref[...] + beta_ref[...]
        o_ref[...] = (y * _sigmoid(y)).astype(BF16)

    def lane_group_half(c, half, after):
        if after is None:
            start_from = jnp.zeros((rows, lanes), F32)
        else:
            bits = lax.bitcast_convert_type(after, jnp.int32)
            zero = lax.shift_right_logical(lax.shift_right_logical(bits, 16), 16).astype(F32)
            start_from = jnp.concatenate([zero] * (rows // SUBLANES), axis=0)
        per_half = CONV_BLOCK // rows // 2
        for r in range(half * per_half, (half + 1) * per_half):
            taps(c, r, start_from)

    chunks = [functools.partial(lane_group_half, c, half) for c in range(CONV_DIM // lanes) for half in range(2)]
    return [setup] + chunks + [finish]


def _proj_kernel(*refs, tm, carry):
    (x_ref, mod_ref, cos_ref, sin_lo_ref, sin_hi_ref, wt_ref, wuq_ref, wk_ref, wvt_ref, gq_ref,
     gkv_ref, cw_ref, cb_ref, cg_ref, cbeta_ref) = refs[:15]
    refs = refs[15:]
    if carry:
        ckv_prev_ref, kr_prev_ref = refs[:2]
        refs = refs[2:]
    (a_ref, rq_ref, rk_ref, rv_ref, rg_ref, q_ref, kcat_ref, vt_ref, sig_ref, ckv_ref, kr_ref,
     glu_ring, pad_ref, shift_ref, acc_ref, wt_bf) = refs
    t = pl.program_id(0) - PROJ_WSTEPS
    n_tiles = N_TOK // tm
    n_ctx_tiles = N_CTX // tm
    tiles_per_seq = DEC_SEQ // tm
    is_ctx = t < n_ctx_tiles
    pos = (t - 1 - n_ctx_tiles) % tiles_per_seq
    conv_latent = t - 1 >= n_ctx_tiles
    has_prev = jnp.logical_and(conv_latent, pos != 0)
    has_next = jnp.logical_and(conv_latent, pos != tiles_per_seq - 1)
    slot = t % 2
    no_halo = jnp.zeros((CONV_HALO, CONV_DIM), F32)

    def conv_prev_tile(next_rows):
        return _conv_stages(
            lambda: jnp.where(has_prev, glu_ring[slot, tm - CONV_HALO:tm, :], no_halo),
            lambda: glu_ring[1 - slot],
            lambda: jnp.where(has_next, next_rows, no_halo),
            cw_ref, cb_ref, cg_ref, cbeta_ref, a_ref, pad_ref, shift_ref, acc_ref)

    @pl.when(t < 0)
    def _():
        _stage_chunk(pl.program_id(0), wt_ref, wt_bf)

    @pl.when(t == 0)
    def _():
        glu_ring[...] = jnp.zeros(glu_ring.shape, F32)

    @pl.when(t == n_tiles)
    def _():
        for step in conv_prev_tile(no_halo):
            step(None)

    @pl.when(jnp.logical_and(t >= 0, t < n_tiles))
    def _():
        _proj_tile(x_ref, mod_ref, cos_ref, sin_lo_ref, sin_hi_ref, wt_bf, wuq_ref, wk_ref, wvt_ref, gq_ref,
                   gkv_ref, ckv_prev_ref if carry else None, kr_prev_ref if carry else None,
                   rq_ref, rk_ref, rv_ref, rg_ref, q_ref, kcat_ref, vt_ref, sig_ref, ckv_ref, kr_ref,
                   glu_ring, conv_prev_tile, slot, is_ctx, tm)


def _proj_tile(x_ref, mod_ref, cos_ref, sin_lo_ref, sin_hi_ref, wt_ref, wuq_ref, wk_ref, wvt_ref, gq_ref,
               gkv_ref, ckv_prev_ref, kr_prev_ref,
               rq_ref, rk_ref, rv_ref, rg_ref, q_ref, kcat_ref, vt_ref, sig_ref, ckv_ref, kr_ref,
               glu_ring, conv_prev_tile, slot, is_ctx, tm):
    carry = ckv_prev_ref is not None
    x = x_ref[...]
    u = (x * (1.0 + mod_ref[4:5, :]) + mod_ref[3:4, :]).astype(BF16)
    widths = (CONV_DIM, CONV_DIM, RET_HEADS * RET_DK, RET_HEADS * RET_DK, RET_HEADS * RET_DV, RET_HEADS * RET_DV,
              MLA_Q_LORA, MLA_KV_LORA)
    starts = [sum(widths[:n]) for n in range(len(widths))]

    def proj(n):
        return _dot_nt(u, wt_ref[starts[n]:starts[n] + widths[n], :])

    glu = proj(0) * _sigmoid(proj(1))
    conv_steps = conv_prev_tile(glu[0:CONV_HALO, :])

    def conv(n, result):
        for _ in range(n):
            conv_steps.pop(0)(result[0:SUBLANES, 0:128])

    conv(1, glu)
    glu_ring[slot] = glu

    mq = proj(6)
    mkv = proj(7)
    conv(2, mq)
    cos = cos_ref[...]
    sin_lo = sin_lo_ref[...]
    sin_hi = sin_hi_ref[...]

    def rotary(v):
        up = pltpu.roll(v, HEAD_PAD - ROPE_AXIS_HALF, 1)
        down = pltpu.roll(v, ROPE_AXIS_HALF, 1)
        return v * cos + up * sin_lo + down * sin_hi

    qn = (_rms_rows(mq) * gq_ref[...]).astype(BF16)
    qm = _dot(qn, wuq_ref[...])
    ckv = _rms_rows(mkv) * gkv_ref[...]
    ckvb = ckv.astype(BF16)
    kn = _dot(ckvb, wk_ref[...])
    vt_ref[...] = _dot_nt(wvt_ref[...], ckvb).astype(BF16)
    kr_grp = _dot_nt(u, wt_ref[MAIN_W:MAIN_W + HEAD_PAD, :])
    conv(2, kn)
    lane = lax.broadcasted_iota(jnp.int32, kr_grp.shape, 1)
    in_rope = jnp.logical_and(lane >= ROPE_LANE0, lane < ROPE_LANE0 + MLA_D_ROPE)
    kr = jnp.where(in_rope, pltpu.roll(kr_grp, ROPE_LANE0, 1), 0.0)
    kr_rot = rotary(kr)
    for h in range(MLA_HEADS):
        sl = slice(h * HEAD_PAD, (h + 1) * HEAD_PAD)
        q_ref[:, sl] = (rotary(qm[:, sl]) * ATTN_QSCALE).astype(BF16)
        kcat_ref[:, sl] = (kn[:, sl] + kr_rot).astype(BF16)

    gate0 = MAIN_W + MLA_D_ROPE

    def branch_gate(blk):
        cols = slice(blk * D_MODEL, (blk + 1) * D_MODEL)
        gate = _dot_nt(u, wt_ref[gate0 + cols.start:gate0 + cols.stop, :])
        sig_ref[:, cols] = _sigmoid(gate).astype(BF16)
        return gate

    for blk in range(N_BRANCHES):
        conv(1, branch_gate(blk))
    rg = proj(5)
    rg_ref[...] = (rg * _sigmoid(rg)).astype(BF16)
    conv(2, rg)
    rv_ref[...] = proj(4).astype(BF16)
    rk_ref[...] = (proj(3) * (RET_DK ** -0.5)).astype(BF16)
    rq_ref[...] = proj(2).astype(BF16)

    @pl.when(is_ctx)
    def _():
        seqs = tm // SEQ
        ckv3 = ckv.reshape(seqs, SEQ, MLA_KV_LORA)
        kr3 = kr_grp[:, :MLA_D_ROPE].reshape(seqs, SEQ, MLA_D_ROPE)
        if carry:
            ckv_ref[:, 0] = ckv_prev_ref[...]
            kr_ref[:, 0] = kr_prev_ref[...]
            ckv_ref[:, 1] = ckv3
            kr_ref[:, 1] = kr3
        else:
            ckv_ref[...] = ckv3
            kr_ref[...] = kr3


def _proj(x, mods_all, rope_tabs, wt_all, wuq, wk, wvt, gq, gkv, conv_w, carry, *, layer):
    tm = CONV_BLOCK
    n_w = PROJ_WSTEPS
    n_tiles = N_TOK // tm
    n_ctx_tiles = N_CTX // tm
    tiles_per_seq = DEC_SEQ // tm
    seqs = tm // SEQ

    def rope_index(i):
        j = _tile(i, n_w, tm)
        return (jnp.where(j < n_ctx_tiles, 0, tiles_per_seq + (j - n_ctx_tiles) % tiles_per_seq), 0)

    def row(w):
        return _row_spec(tm, w, n_w)

    def out(w, dt):
        return jax.ShapeDtypeStruct((N_TOK, w), dt)

    def ctx_seq_spec(*tail):
        zeros = (0,) * len(tail)
        return pl.BlockSpec((seqs,) + tail, lambda i: (jnp.minimum(_tile(i, n_w, tm), n_ctx_tiles - 1),) + zeros)

    rope = pl.BlockSpec((tm, HEAD_PAD), rope_index)
    in_specs = [row(D_MODEL), _mod_spec(tm, layer, n_w), rope, rope, rope,
                _wchunk_spec(wt_all, layer, n_w), _resident(wuq.shape), _resident(wk.shape), _resident(wvt.shape),
                _resident(gq.shape), _resident(gkv.shape)] + [_resident(w.shape) for w in conv_w]
    args = [x, mods_all, *rope_tabs, wt_all, wuq, wk, wvt, gq, gkv, *conv_w]
    if carry is None:
        ctx_specs = [ctx_seq_spec(SEQ, MLA_KV_LORA), ctx_seq_spec(SEQ, MLA_D_ROPE)]
        ctx_shapes = [jax.ShapeDtypeStruct((BATCH, SEQ, MLA_KV_LORA), F32),
                      jax.ShapeDtypeStruct((BATCH, SEQ, MLA_D_ROPE), F32)]
    else:
        in_specs += [ctx_seq_spec(SEQ, MLA_KV_LORA), ctx_seq_spec(SEQ, MLA_D_ROPE)]
        args += list(carry)
        ctx_specs = [ctx_seq_spec(DEPTH, SEQ, MLA_KV_LORA), ctx_seq_spec(DEPTH, SEQ, MLA_D_ROPE)]
        ctx_shapes = [jax.ShapeDtypeStruct((BATCH, DEPTH, SEQ, MLA_KV_LORA), F32),
                      jax.ShapeDtypeStruct((BATCH, DEPTH, SEQ, MLA_D_ROPE), F32)]
    span = CONV_BLOCK + 2 * CONV_HALO - SUBLANES
    return pl.pallas_call(
        functools.partial(_proj_kernel, tm=tm, carry=carry is not None),
        grid=(n_w + n_tiles + 1,),
        in_specs=in_specs,
        out_specs=[pl.BlockSpec((tm, CONV_DIM), lambda i: (_tile(i, n_w + 1, tm), 0)),
                   row(RET_HEADS * RET_DK), row(RET_HEADS * RET_DK), row(RET_HEADS * RET_DV),
                   row(RET_HEADS * RET_DV), row(MLA_W), row(MLA_W),
                   pl.BlockSpec((MLA_V_W, tm), lambda i: (0, _tile(i, n_w, tm))), row(GATE_W)] + ctx_specs,
        out_shape=[out(CONV_DIM, BF16), out(RET_HEADS * RET_DK, BF16), out(RET_HEADS * RET_DK, BF16),
                   out(RET_HEADS * RET_DV, BF16), out(RET_HEADS * RET_DV, BF16), out(MLA_W, BF16),
                   out(MLA_W, BF16), jax.ShapeDtypeStruct((MLA_V_W, N_TOK), BF16), out(GATE_W, BF16)] + ctx_shapes,
        scratch_shapes=[pltpu.VMEM((2, tm, CONV_DIM), F32),
                        pltpu.VMEM((CONV_BLOCK + 2 * CONV_HALO, CONV_DIM), F32),
                        pltpu.VMEM((SUBLANES, span, CONV_DIM), F32),
                        pltpu.VMEM((CONV_BLOCK, CONV_DIM), F32),
                        pltpu.VMEM(wt_all.shape[1:], BF16)],
        compiler_params=_params(("arbitrary",)),
        name="mix_proj",
    )(*args)


def _cache_kv_kernel(ckv_ref, kr_ref, wk_ref, wvt_ref, kcat_ref, vt_ref):
    ckvb = ckv_ref[...].astype(BF16)
    kn = _dot(ckvb, wk_ref[...])
    vt_ref[...] = _dot_nt(wvt_ref[...], ckvb).astype(BF16)
    kr = kr_ref[...]
    for h in range(MLA_HEADS):
        sl = slice(h * HEAD_PAD, (h + 1) * HEAD_PAD)
        kcat_ref[:, sl] = (kn[:, sl] + kr).astype(BF16)


def _cache_kv(cache_ckv, cache_kr_pad, wk, wvt):
    n = DEC_BATCH * PAST_LEN
    return pl.pallas_call(
        _cache_kv_kernel,
        grid=(DEPTH, DEC_BATCH),
        in_specs=[pl.BlockSpec((None, None, PAST_LEN, MLA_KV_LORA), lambda l, b: (b, l, 0, 0)),
                  pl.BlockSpec((None, None, PAST_LEN, HEAD_PAD), lambda l, b: (b, l, 0, 0)),
                  pl.BlockSpec((None, MLA_KV_LORA, MLA_W), lambda l, b: (l, 0, 0)),
                  pl.BlockSpec((None, MLA_V_W, MLA_KV_LORA), lambda l, b: (l, 0, 0))],
        out_specs=[pl.BlockSpec((None, PAST_LEN, MLA_W), lambda l, b: (l, b, 0)),
                   pl.BlockSpec((None, MLA_V_W, PAST_LEN), lambda l, b: (l, 0, b))],
        out_shape=[jax.ShapeDtypeStruct((DEPTH, n, MLA_W), BF16), jax.ShapeDtypeStruct((DEPTH, MLA_V_W, n), BF16)],
        compiler_params=_params(("arbitrary", "arbitrary")),
        name="cache_kv",
    )(cache_ckv, cache_kr_pad, wk, wvt)


def _ret_kernel(*refs, t, hp, latent, carry):
    if latent:
        lg_ref, q_ref, k_ref, v_ref, g_ref, s0f_ref, s0b_ref, o_ref, d_ref = refs
    elif carry:
        lg_ref, q_ref, k_ref, v_ref, g_ref, sf_prev_ref, sb_prev_ref, o_ref, sf_ref, sb_ref, d_ref = refs
    else:
        lg_ref, q_ref, k_ref, v_ref, g_ref, o_ref, sf_ref, sb_ref, d_ref = refs
    hblk = pl.program_id(0)

    @pl.when(pl.program_id(1) == 0)
    def _():
        diff = (lax.broadcasted_iota(jnp.int32, (t, t), 0) - lax.broadcasted_iota(jnp.int32, (t, t), 1)).astype(F32)
        for hh in range(hp):
            lgf = lg_ref[0, hblk * hp + hh]
            lgb = lg_ref[1, hblk * hp + hh]
            d_ref[hh] = jnp.exp(jnp.where(diff >= 0, diff * lgf, -diff * lgb))

    if carry:
        sf_ref[0] = sf_prev_ref[...]
        sb_ref[0] = sb_prev_ref[...]
    pos = lax.broadcasted_iota(jnp.int32, (t, 1), 0).astype(F32)
    for hh in range(hp):
        lgf = lg_ref[0, hblk * hp + hh]
        lgb = lg_ref[1, hblk * hp + hh]
        q = q_ref[:, hh * RET_DK:(hh + 1) * RET_DK]
        k = k_ref[:, hh * RET_DK:(hh + 1) * RET_DK]
        v = v_ref[:, hh * RET_DV:(hh + 1) * RET_DV]
        p = (_dot_nt(q, k) * d_ref[hh]).astype(BF16)
        o = _dot(p, v)
        if latent:
            o = o + jnp.exp((pos + 1.0) * lgf) * _dot(q, s0f_ref[hh].astype(BF16))
            o = o + jnp.exp((t - pos) * lgb) * _dot(q, s0b_ref[hh].astype(BF16))
        else:
            kf = k.astype(F32)
            sf = _dot_tn((kf * jnp.exp((t - 1.0 - pos) * lgf)).astype(BF16), v)
            sb = _dot_tn((kf * jnp.exp(pos * lgb)).astype(BF16), v)
            if carry:
                sf_ref[1, hh] = sf
                sb_ref[1, hh] = sb
            else:
                sf_ref[hh] = sf
                sb_ref[hh] = sb
        o_ref[:, hh * RET_DV:(hh + 1) * RET_DV] = (g_ref[:, hh * RET_DV:(hh + 1) * RET_DV] * _norm_rows(o)).astype(BF16)


def _retention(log_g, rq, rk, rv, rg, states, *, latent, layer):
    t = DEC_SEQ if latent else SEQ
    hp = 1 if latent else RET_HEADS
    n_seq = DEC_BATCH if latent else BATCH
    row0 = (N_CTX // t) if latent else 0

    def row(w):
        return pl.BlockSpec((t, hp * w), lambda h, s: (row0 + s, h))

    smem = pl.BlockSpec(memory_space=pltpu.SMEM)
    out_shape = [jax.ShapeDtypeStruct((n_seq * t, RET_HEADS * RET_DV), BF16)]
    out_specs = [pl.BlockSpec((t, hp * RET_DV), lambda h, s: (s, h))]
    in_specs = [smem, row(RET_DK), row(RET_DK), row(RET_DV), row(RET_DV)]
    args = [log_g, rq, rk, rv, rg]
    carry = False
    if latent:
        st = pl.BlockSpec((None, None, hp, RET_DK, RET_DV), lambda h, s: (s, layer, h, 0, 0))
        in_specs += [st, st]
        args += list(states)
    else:
        st = pl.BlockSpec((None, hp, RET_DK, RET_DV), lambda h, s: (s, h, 0, 0))
        if states is None:
            out_specs += [st, st]
            out_shape += [jax.ShapeDtypeStruct((BATCH, RET_HEADS, RET_DK, RET_DV), F32)] * 2
        else:
            carry = True
            in_specs += [st, st]
            args += list(states)
            st2 = pl.BlockSpec((None, DEPTH, hp, RET_DK, RET_DV), lambda h, s: (s, 0, h, 0, 0))
            out_specs += [st2, st2]
            out_shape += [jax.ShapeDtypeStruct((BATCH, DEPTH, RET_HEADS, RET_DK, RET_DV), F32)] * 2

    return pl.pallas_call(
        functools.partial(_ret_kernel, t=t, hp=hp, latent=latent, carry=carry),
        grid=(RET_HEADS // hp, n_seq),
        in_specs=in_specs,
        out_specs=out_specs,
        out_shape=out_shape,
        scratch_shapes=[pltpu.VMEM((hp, t, t), F32)],
        compiler_params=_params(("arbitrary", "arbitrary")),
        name="retention_lat" if latent else "retention_ctx",
    )(*args)


def _attn_kernel(*refs, t, hp, latent):
    if latent:
        q_ref, k_ref, vt_ref, kc_ref, vtc_ref, o_ref = refs
    else:
        q_ref, k_ref, vt_ref, o_ref = refs
    qb = min(ATTN_QBLOCK, t)
    units = [(slice(hh * HEAD_PAD, (hh + 1) * HEAD_PAD), slice(hh * MLA_D_V, (hh + 1) * MLA_D_V),
              slice(b * qb, (b + 1) * qb)) for b in range(t // qb) for hh in range(hp)]
    def scores(unit):
        sl, _, rows = unit
        q = q_ref[rows, sl]
        s = [_dot_nt(k_ref[:, sl], q)]
        if latent:
            s.append(_dot_nt(kc_ref[:, sl], q))
        return s

    def softmax(s):
        m = functools.reduce(jnp.maximum, [jnp.max(x, axis=0, keepdims=True) for x in s])
        e = [jnp.exp2(x - m) for x in s]
        den = functools.reduce(jnp.add, [jnp.sum(x, axis=0, keepdims=True) for x in e])
        return [x.astype(BF16) for x in e], den

    def values(unit, e, den):
        _, vs, _ = unit
        o = _dot(vt_ref[vs, :], e[0])
        if latent:
            o = o + _dot(vtc_ref[vs, :], e[1])
        return o / den

    pairs = [units[u:u + 2] for u in range(0, len(units), 2)]
    s_next = [scores(u) for u in pairs[0]]
    sm_prev = None
    for g in range(len(pairs) + 1):
        s_cur = s_next
        if g + 1 < len(pairs):
            s_next = [scores(u) for u in pairs[g + 1]]
        sm_cur = [softmax(s) for s in s_cur] if g < len(pairs) else None
        if sm_prev is not None:
            (_, vs0, rows), (_, vs1, _) = pairs[g - 1]
            outs = [values(u, e, den) for u, (e, den) in zip(pairs[g - 1], sm_prev)]
            o_ref[rows, vs0.start:vs1.stop] = jnp.concatenate(outs, axis=0).T.astype(BF16)
        sm_prev = sm_cur


def _attention(q, kcat, vt, cache, *, latent):
    t = DEC_SEQ if latent else SEQ
    hp = 2 if latent else MLA_HEADS
    n_seq = DEC_BATCH if latent else BATCH
    row0 = (N_CTX // t) if latent else 0
    row = pl.BlockSpec((t, hp * HEAD_PAD), lambda s, h: (row0 + s, h))
    col = pl.BlockSpec((hp * MLA_D_V, t), lambda s, h: (h, row0 + s))
    in_specs = [row, row, col]
    args = [q, kcat, vt]
    if latent:
        in_specs += [pl.BlockSpec((PAST_LEN, hp * HEAD_PAD), lambda s, h: (s, h)),
                     pl.BlockSpec((hp * MLA_D_V, PAST_LEN), lambda s, h: (h, s))]
        args += list(cache)
    return pl.pallas_call(
        functools.partial(_attn_kernel, t=t, hp=hp, latent=latent),
        grid=(n_seq, MLA_HEADS // hp),
        in_specs=in_specs,
        out_specs=pl.BlockSpec((t, hp * MLA_D_V), lambda s, h: (s, h)),
        out_shape=jax.ShapeDtypeStruct((n_seq * t, MLA_V_W), BF16),
        compiler_params=_params(("arbitrary", "arbitrary")),
        name="attention_lat" if latent else "attention_ctx",
    )(*args)


def _merge_kernel(x_ref, mod_ref, a_ref, rc_ref, rl_ref, mc_ref, ml_ref, sig_ref, wc_ref, wr_ref, wm_ref, wo_ref,
                  g_ref, b_ref, o_ref, wc_bf, wr_bf, wm_bf, wo_bf, *, tm):
    i = pl.program_id(0)
    is_ctx = i < N_WSTEPS + N_CTX // tm

    @pl.when(i < N_WSTEPS)
    def _():
        _stage_chunk(i, wc_ref, wc_bf)
        _stage_chunk(i, wr_ref, wr_bf)
        _stage_chunk(i, wo_ref, wo_bf)
        _stage_chunk(i, wm_ref, wm_bf)

    @pl.when(i >= N_WSTEPS)
    def _():
        x = x_ref[...]
        r = jnp.where(is_ctx, rc_ref[...], rl_ref[...])
        m = jnp.where(is_ctx, mc_ref[...], ml_ref[...])
        merged = sig_ref[:, 0:D_MODEL] * _dot(a_ref[...], wc_bf[...])
        merged = merged + sig_ref[:, D_MODEL:2 * D_MODEL] * _dot(r, wr_bf[...])
        merged = merged + sig_ref[:, 2 * D_MODEL:] * _dot(m, wm_bf[...])
        y = _dot(merged.astype(BF16), wo_bf[...])
        z = DEEPNORM_ALPHA * x + mod_ref[5:6, :] * y
        o_ref[...] = _norm_rows(z) * g_ref[...] + b_ref[...]


def _merge(x, mods_all, a, r_pair, m_pair, sig, wc, wr, wm, wo, ln_g, ln_b, *, layer, tm=512):
    def row(w):
        return _row_spec(tm, w, N_WSTEPS)

    def pair(w):
        return [_ctx_row_spec(tm, w, N_WSTEPS), _lat_row_spec(tm, w, N_WSTEPS)]

    ln_index = layer * 3 + 1
    return pl.pallas_call(
        functools.partial(_merge_kernel, tm=tm),
        grid=(N_WSTEPS + N_TOK // tm,),
        in_specs=[row(D_MODEL), _mod_spec(tm, layer, N_WSTEPS), row(CONV_DIM)] + pair(RET_HEADS * RET_DV)
        + pair(MLA_V_W) + [row(GATE_W), _wchunk_spec(wc, layer), _wchunk_spec(wr, layer), _wchunk_spec(wm, layer),
                         _wchunk_spec(wo, layer), _layer_row_spec(ln_g, ln_index), _layer_row_spec(ln_b, ln_index)],
        out_specs=row(D_MODEL),
        out_shape=jax.ShapeDtypeStruct((N_TOK, D_MODEL), F32),
        scratch_shapes=[pltpu.VMEM((CONV_DIM, D_MODEL), BF16), pltpu.VMEM((RET_HEADS * RET_DV, D_MODEL), BF16),
                        pltpu.VMEM((MLA_V_W, D_MODEL), BF16), pltpu.VMEM((D_MODEL, D_MODEL), BF16)],
        compiler_params=_params(("arbitrary",)),
        name="merge",
    )(x, mods_all, a, *r_pair, *m_pair, sig, wc, wr, wm, wo, ln_g, ln_b)


def _rope_tables():
    rows = DEC_SEQ // GRID_W
    row_id = jnp.repeat(jnp.arange(rows, dtype=F32), GRID_W)
    col_id = jnp.tile(jnp.arange(GRID_W, dtype=F32), rows)
    inv_freq = ROPE_BASE ** (-jnp.arange(ROPE_AXIS_HALF, dtype=F32) / ROPE_AXIS_HALF)
    ang = jnp.stack([row_id[:, None] * inv_freq, col_id[:, None] * inv_freq], axis=1)
    cos = jnp.cos(ang)
    sin = jnp.sin(ang)
    cos32 = jnp.stack([cos, cos], axis=2).reshape(DEC_SEQ, MLA_D_ROPE)
    zero = jnp.zeros_like(sin)
    sin_lo32 = jnp.stack([-sin, zero], axis=2).reshape(DEC_SEQ, MLA_D_ROPE)
    sin_hi32 = jnp.stack([zero, sin], axis=2).reshape(DEC_SEQ, MLA_D_ROPE)
    tail = HEAD_PAD - ROPE_LANE0 - MLA_D_ROPE
    cos_t = jnp.concatenate([jnp.ones((DEC_SEQ, ROPE_LANE0), F32), cos32, jnp.ones((DEC_SEQ, tail), F32)], axis=1)
    cos_t = jnp.concatenate([jnp.ones((DEC_SEQ, HEAD_PAD), F32), cos_t], axis=0)

    def sin_table(s32):
        return jnp.pad(s32, ((DEC_SEQ, 0), (ROPE_LANE0, tail)))

    return cos_t, sin_table(sin_lo32), sin_table(sin_hi32)


def _head_pad_cols(w, width):
    k = w.shape[0]
    w = w.reshape(k, MLA_HEADS, width)
    return jnp.pad(w, ((0, 0), (0, 0), (0, HEAD_PAD - width))).reshape(k, MLA_W)


def kernel(x_prompt, x_sample, cache_mla_ckv, cache_mla_krope, state_ret_fwd, state_ret_bwd, c, c_ctx, ada_w, ada_b, ffn1_w_in, ffn1_w_out, ffn2_w_in, ffn2_w_out, post_ln_g, post_ln_b, mix_w_in, conv_w_dw, conv_b_dw, conv_ln_g, conv_ln_b, conv_w_out, ret_decay_fwd, ret_decay_bwd, ret_w_out, mla_q_norm, mla_w_uq, mla_kv_norm, mla_w_ukv, mla_w_out, mix_w_o):
    assert DEPTH == 2
    cvec = jnp.concatenate([c_ctx[None, :], c, jnp.zeros((N_MOD_ROWS - 1 - DEC_BATCH, D_MODEL), F32)], axis=0)
    mods_all = _ada_mods(cvec, ada_w, ada_b).reshape(DEPTH, N_MOD_ROWS, N_MODS, D_MODEL)
    rope_tabs = _rope_tables()

    w_ukv = mla_w_ukv.reshape(DEPTH, MLA_KV_LORA, MLA_HEADS, MLA_D_NOPE + MLA_D_V)
    wk_all = _head_pad_cols(w_ukv[..., :MLA_D_NOPE].reshape(DEPTH * MLA_KV_LORA, -1), MLA_D_NOPE)
    wk_all = wk_all.reshape(DEPTH, MLA_KV_LORA, MLA_W).astype(BF16)
    wvt_all = jnp.swapaxes(w_ukv[..., MLA_D_NOPE:].reshape(DEPTH, MLA_KV_LORA, MLA_V_W), 1, 2).astype(BF16)
    kr_tail = HEAD_PAD - ROPE_LANE0 - MLA_D_ROPE
    cache_kr_pad = jnp.pad(cache_mla_krope, ((0, 0), (0, 0), (0, 0), (ROPE_LANE0, kr_tail)))
    kcat_c, vt_c = _cache_kv(cache_mla_ckv, cache_kr_pad, wk_all, wvt_all)
    log_g = jnp.stack([jax.nn.log_sigmoid(ret_decay_fwd), jax.nn.log_sigmoid(ret_decay_bwd)], axis=1)
    wt_all = jnp.swapaxes(mix_w_in, 1, 2)

    ln_g = post_ln_g.reshape(DEPTH * 3, 1, D_MODEL)
    ln_b = post_ln_b.reshape(DEPTH * 3, 1, D_MODEL)
    xs = (x_prompt.reshape(N_CTX, D_MODEL), x_sample.reshape(N_LAT, D_MODEL))
    ctx_carry = None
    state_carry = None
    for l in range(DEPTH):
        last = l == DEPTH - 1
        x = _ffn(xs, mods_all, ffn1_w_in, ffn1_w_out, ln_g, ln_b, layer=l, which=0)[0]

        wuq = _head_pad_cols(mla_w_uq[l], MLA_D_NOPE + MLA_D_ROPE).astype(BF16)
        conv_w = (conv_w_dw[l], conv_b_dw[l].reshape(1, -1), conv_ln_g[l].reshape(1, -1), conv_ln_b[l].reshape(1, -1))
        (a, rq, rk, rv, rg, q, kcat, vt, sig, ckv, kr) = _proj(
            x, mods_all, rope_tabs, wt_all, wuq, wk_all[l], wvt_all[l],
            mla_q_norm[l].reshape(1, -1), mla_kv_norm[l].reshape(1, -1), conv_w, ctx_carry, layer=l)
        ctx_carry = (ckv, kr)

        r_ctx, sf, sb = _retention(log_g[l], rq, rk, rv, rg, state_carry, latent=False, layer=l)
        state_carry = (sf, sb)
        r_lat = _retention(log_g[l], rq, rk, rv, rg, (state_ret_fwd, state_ret_bwd), latent=True, layer=l)[0]
        m_ctx = _attention(q, kcat, vt, None, latent=False)
        m_lat = _attention(q, kcat, vt, (kcat_c[l], vt_c[l]), latent=True)

        x = _merge(x, mods_all, a, (r_ctx, r_lat), (m_ctx, m_lat), sig, conv_w_out, ret_w_out, mla_w_out, mix_w_o,
                   ln_g, ln_b, layer=l)
        xs = _ffn((x,), mods_all, ffn2_w_in, ffn2_w_out, ln_g, ln_b, layer=l, which=2, split_out=last)

    y_ctx, y_lat = xs
    return (y_ctx.reshape(BATCH, SEQ, D_MODEL), y_lat.reshape(DEC_BATCH, DEC_SEQ, D_MODEL),
            ctx_carry[0], ctx_carry[1], state_carry[0], state_carry[1])
```

```python
import functools
import math

import jax
import jax.numpy as jnp
from jax import lax
from jax.experimental import pallas as pl
from jax.experimental.pallas import tpu as pltpu

F32 = jnp.float32
BF16 = jnp.bfloat16

D_MODEL = 1024
BATCH = 16
SEQ = 256
DEPTH = 2
DEC_BATCH = 4
DEC_SEQ = 1024
PAST_LEN = 256
GRID_W = 64
D_FF = 2816
N_MODS = 9
CONV_DIM = 512
CONV_WIDTH = 31
RET_HEADS = 4
RET_DK = 128
RET_DV = 256
MLA_HEADS = 8
MLA_Q_LORA = 512
MLA_KV_LORA = 256
MLA_D_NOPE = 64
MLA_D_ROPE = 32
MLA_D_V = 64
ROPE_AXIS_HALF = MLA_D_ROPE // 4
ROPE_BASE = 10000.0
DEEPNORM_ALPHA = (2 * DEPTH) ** 0.25
LN_EPS = 1e-5
RMS_EPS = 1e-6

N_CTX = BATCH * SEQ
N_LAT = DEC_BATCH * DEC_SEQ
N_TOK = N_CTX + N_LAT
N_MOD_ROWS = 8
HEAD_PAD = 128
ROPE_LANE0 = MLA_D_NOPE
MLA_W = MLA_HEADS * HEAD_PAD
MLA_V_W = MLA_HEADS * MLA_D_V
MAIN_W = 2 * CONV_DIM + 2 * RET_HEADS * RET_DK + 2 * RET_HEADS * RET_DV + MLA_Q_LORA + MLA_KV_LORA
N_BRANCHES = 3
GATE_W = N_BRANCHES * D_MODEL
SLAB_COLS = {}
for _name, _w in (("sig", GATE_W), ("rq", RET_HEADS * RET_DK), ("rk", RET_HEADS * RET_DK), ("rv", RET_HEADS * RET_DV),
                  ("rg", RET_HEADS * RET_DV), ("q", MLA_W), ("kcat", MLA_W)):
    SLAB_COLS[_name] = (sum(w for _, w in SLAB_COLS.values()), _w)
SLAB_W = sum(w for _, w in SLAB_COLS.values())
SUBLANES = 8
CONV_HALO = 16
CONV_BLOCK = 256
VMEM_LIMIT = 56 * 1024 * 1024
N_WSTEPS = 8
PROJ_WSTEPS = 6
MXU_COLS = 256
FFN_CHUNKS = ((0, 6 * MXU_COLS), (6 * MXU_COLS, D_FF))
ATTN_QBLOCK = 256
ATTN_QSCALE = (MLA_D_NOPE + MLA_D_ROPE) ** -0.5 * math.log2(math.e)


def _dot(a, b):
    return jnp.dot(a, b, preferred_element_type=F32)


def _dot_nt(a, b):
    return lax.dot_general(a, b, (((1,), (1,)), ((), ())), preferred_element_type=F32)


def _dot_tn(a, b):
    return lax.dot_general(a, b, (((0,), (0,)), ((), ())), preferred_element_type=F32)


def _sigmoid(x):
    return 1.0 / (1.0 + jnp.exp(-x))


def _norm_rows(z):
    mu = jnp.mean(z, axis=-1, keepdims=True)
    zc = z - mu
    var = jnp.mean(zc * zc, axis=-1, keepdims=True)
    return zc * lax.rsqrt(var + LN_EPS)


def _rms_rows(z):
    return z * lax.rsqrt(jnp.mean(z * z, axis=-1, keepdims=True) + RMS_EPS)


def _resident(shape):
    zeros = (0,) * len(shape)
    return pl.BlockSpec(shape, lambda *_: zeros, pipeline_mode=pl.Buffered(1))


def _layer_resident(w, layer):
    zeros = (0,) * (w.ndim - 1)
    return pl.BlockSpec((None,) + w.shape[1:], lambda *_: (layer,) + zeros, pipeline_mode=pl.Buffered(1))


def _tile(i, n_w, tm):
    return jnp.clip(i - n_w, 0, N_TOK // tm - 1)


def _row_spec(tm, w, n_w):
    return pl.BlockSpec((tm, w), lambda i: (_tile(i, n_w, tm), 0))


def _ctx_row_spec(tm, w, n_w):
    last = N_CTX // tm - 1
    return pl.BlockSpec((tm, w), lambda i: (jnp.minimum(_tile(i, n_w, tm), last), 0))


def _lat_row_spec(tm, w, n_w):
    first = N_CTX // tm
    return pl.BlockSpec((tm, w), lambda i: (jnp.maximum(_tile(i, n_w, tm) - first, 0), 0))


def _mod_spec(tm, layer, n_w):
    n_ctx_tiles = N_CTX // tm
    tiles_per_seq = DEC_SEQ // tm

    def index(i):
        j = _tile(i, n_w, tm)
        return (layer, jnp.where(j < n_ctx_tiles, 0, 1 + (j - n_ctx_tiles) // tiles_per_seq), 0, 0)

    return pl.BlockSpec((None, None, N_MODS, D_MODEL), index)


def _wchunk_spec(w, layer, n_w=N_WSTEPS):
    _, rows, cols = w.shape
    return pl.BlockSpec((None, rows // n_w, cols), lambda i: (layer, jnp.minimum(i, n_w - 1), 0))


def _layer_row_spec(w, index):
    return pl.BlockSpec((None, 1, w.shape[-1]), lambda i: (index, 0, 0), pipeline_mode=pl.Buffered(1))


def _stage_chunk(i, src_ref, dst_ref):
    rows = src_ref.shape[0]
    dst_ref[pl.ds(pl.multiple_of(i * rows, rows), rows), :] = src_ref[...].astype(BF16)


def _params(semantics):
    return pltpu.CompilerParams(dimension_semantics=semantics, vmem_limit_bytes=VMEM_LIMIT)


def _ada_kernel(c_ref, w_ref, b_ref, o_ref):
    c = c_ref[...]
    h = (c * _sigmoid(c)).astype(BF16)
    o_ref[...] = _dot(h, w_ref[...].astype(BF16)) + b_ref[...]


def _ada_mods(cvec, ada_w, ada_b):
    tn = D_MODEL
    n_out = N_MODS * D_MODEL
    return pl.pallas_call(
        _ada_kernel,
        grid=(DEPTH, n_out // tn),
        in_specs=[
            pl.BlockSpec((N_MOD_ROWS, D_MODEL), lambda l, j: (0, 0)),
            pl.BlockSpec((None, D_MODEL, tn), lambda l, j: (l, 0, j)),
            pl.BlockSpec((None, 1, tn), lambda l, j: (l, 0, j)),
        ],
        out_specs=pl.BlockSpec((None, N_MOD_ROWS, tn), lambda l, j: (l, 0, j)),
        out_shape=jax.ShapeDtypeStruct((DEPTH, N_MOD_ROWS, n_out), F32),
        compiler_params=_params(("arbitrary", "arbitrary")),
        name="ada_mods",
    )(cvec, ada_w, ada_b.reshape(DEPTH, 1, n_out))


def _ffn_kernel(*refs, base, tm, n_x, n_out):
    x_refs = refs[:n_x]
    mod_ref, win_ref, wout_ref, g_ref, b_ref = refs[n_x:n_x + 5]
    o_refs = refs[n_x + 5:n_x + 5 + n_out]
    win_bf, wout_bf = refs[n_x + 5 + n_out:]
    i = pl.program_id(0)
    is_ctx = i < N_WSTEPS + N_CTX // tm

    @pl.when(i < N_WSTEPS)
    def _():
        _stage_chunk(i, win_ref, win_bf)
        _stage_chunk(i, wout_ref, wout_bf)

    @pl.when(i >= N_WSTEPS)
    def _():
        if n_x == 2:
            x = jnp.where(is_ctx, x_refs[0][...], x_refs[1][...])
        else:
            x = x_refs[0][...]
        shift = mod_ref[base:base + 1, :]
        scale = mod_ref[base + 1:base + 2, :]
        gate = mod_ref[base + 2:base + 3, :]
        h = (x * (1.0 + scale) + shift).astype(BF16)
        y = None
        for lo, hi in FFN_CHUNKS:
            g = _dot(h, win_bf[:, lo:hi])
            u = _dot(h, win_bf[:, D_FF + lo:D_FF + hi])
            a = (g * _sigmoid(g) * u).astype(BF16)
            yc = _dot(a, wout_bf[lo:hi, :])
            y = yc if y is None else y + yc
        z = DEEPNORM_ALPHA * x + 0.5 * gate * y
        res = _norm_rows(z) * g_ref[...] + b_ref[...]
        if n_out == 2:
            @pl.when(is_ctx)
            def _():
                o_refs[0][...] = res

            @pl.when(jnp.logical_not(is_ctx))
            def _():
                o_refs[1][...] = res
        else:
            o_refs[0][...] = res


def _ffn(xs, mods_all, w_in, w_out, ln_g, ln_b, *, layer, which, split_out=False, tm=512):
    row = _row_spec(tm, D_MODEL, N_WSTEPS)
    pair = [_ctx_row_spec(tm, D_MODEL, N_WSTEPS), _lat_row_spec(tm, D_MODEL, N_WSTEPS)]
    ln_index = layer * 3 + which
    if split_out:
        out_specs = pair
        out_shape = [jax.ShapeDtypeStruct((N_CTX, D_MODEL), F32), jax.ShapeDtypeStruct((N_LAT, D_MODEL), F32)]
    else:
        out_specs = [row]
        out_shape = [jax.ShapeDtypeStruct((N_TOK, D_MODEL), F32)]
    return pl.pallas_call(
        functools.partial(_ffn_kernel, base=3 * which, tm=tm, n_x=len(xs), n_out=len(out_specs)),
        grid=(N_WSTEPS + N_TOK // tm,),
        in_specs=(pair if len(xs) == 2 else [row]) + [
            _mod_spec(tm, layer, N_WSTEPS), _wchunk_spec(w_in, layer), _wchunk_spec(w_out, layer),
            _layer_row_spec(ln_g, ln_index), _layer_row_spec(ln_b, ln_index)],
        out_specs=out_specs,
        out_shape=out_shape,
        scratch_shapes=[pltpu.VMEM((D_MODEL, 2 * D_FF), BF16), pltpu.VMEM((D_FF, D_MODEL), BF16)],
        compiler_params=_params(("arbitrary",)),
        name="ffn",
    )(*xs, mods_all, w_in, w_out, ln_g, ln_b)


def _conv_stages(prev_halo, main, next_halo, w_ref, b_ref, g_ref, beta_ref, o_ref, pad_ref, shift_ref, acc_ref):
    rows = 32
    lanes = 128
    first = CONV_HALO - CONV_WIDTH // 2

    def setup(after):
        del after
        pad_ref[0:CONV_HALO, :] = prev_halo()
        pad_ref[CONV_HALO:CONV_HALO + CONV_BLOCK, :] = main()
        pad_ref[CONV_HALO + CONV_BLOCK:, :] = next_halo()
        span = shift_ref.shape[1]
        for ph in range(SUBLANES):
            shift_ref[ph] = pad_ref[ph:ph + span, :]

    def taps(c, r, start_from):
        cs = slice(c * lanes, (c + 1) * lanes)
        acc = jnp.broadcast_to(b_ref[:, cs], (rows, lanes)) + start_from
        for j in range(CONV_WIDTH):
            ph = (first + j) % SUBLANES
            start = r * rows + (first + j) - ph
            acc = acc + w_ref[j:j + 1, cs] * shift_ref[ph, start:start + rows, cs]
        acc_ref[r * rows:(r + 1) * rows, cs] = acc

    def finish(after):
        del after
        y = _norm_rows(acc_ref[...]) * g_ref[...] + beta_ref[...]
        o_ref[...] = (y * _sigmoid(y)).astype(BF16)

    def lane_group_half(c, half, after):
        if after is None:
            start_from = jnp.zeros((rows, lanes), F32)
        else:
            bits = lax.bitcast_convert_type(after, jnp.int32)
            zero = lax.shift_right_logical(lax.shift_right_logical(bits, 16), 16).astype(F32)
            start_from = jnp.concatenate([zero] * (rows // SUBLANES), axis=0)
        per_half = CONV_BLOCK // rows // 2
        for r in range(half * per_half, (half + 1) * per_half):
            taps(c, r, start_from)

    chunks = [functools.partial(lane_group_half, c, half) for c in range(CONV_DIM // lanes) for half in range(2)]
    return [setup] + chunks + [finish]


def _proj_kernel(*refs, tm, carry):
    (x_ref, mod_ref, cos_ref, sin_lo_ref, sin_hi_ref, wt_ref, wuq_ref, wk_ref, wvt_ref, gq_ref,
     gkv_ref, cw_ref, cb_ref, cg_ref, cbeta_ref) = refs[:15]
    refs = refs[15:]
    if carry:
        ckv_prev_ref, kr_prev_ref = refs[:2]
        refs = refs[2:]
    (a_ref, slab_ref, vt_ref, ckv_ref, kr_ref, glu_ring, pad_ref, shift_ref, acc_ref, wt_bf) = refs
    sig_ref, rq_ref, rk_ref, rv_ref, rg_ref, q_ref, kcat_ref = (
        slab_ref.at[:, lo:lo + w] for lo, w in (SLAB_COLS[n] for n in ("sig", "rq", "rk", "rv", "rg", "q", "kcat")))
    t = pl.program_id(0) - PROJ_WSTEPS
    n_tiles = N_TOK // tm
    n_ctx_tiles = N_CTX // tm
    tiles_per_seq = DEC_SEQ // tm
    is_ctx = t < n_ctx_tiles
    pos = (t - 1 - n_ctx_tiles) % tiles_per_seq
    conv_latent = t - 1 >= n_ctx_tiles
    has_prev = jnp.logical_and(conv_latent, pos != 0)
    has_next = jnp.logical_and(conv_latent, pos != tiles_per_seq - 1)
    slot = t % 2
    no_halo = jnp.zeros((CONV_HALO, CONV_DIM), F32)

    def conv_prev_tile(next_rows):
        return _conv_stages(
            lambda: jnp.where(has_prev, glu_ring[slot, tm - CONV_HALO:tm, :], no_halo),
            lambda: glu_ring[1 - slot],
            lambda: jnp.where(has_next, next_rows, no_halo),
            cw_ref, cb_ref, cg_ref, cbeta_ref, a_ref, pad_ref, shift_ref, acc_ref)

    @pl.when(t < 0)
    def _():
        _stage_chunk(pl.program_id(0), wt_ref, wt_bf)

    @pl.when(t == 0)
    def _():
        glu_ring[...] = jnp.zeros(glu_ring.shape, F32)

    @pl.when(t == n_tiles)
    def _():
        for step in conv_prev_tile(no_halo):
            step(None)

    @pl.when(jnp.logical_and(t >= 0, t < n_tiles))
    def _():
        _proj_tile(x_ref, mod_ref, cos_ref, sin_lo_ref, sin_hi_ref, wt_bf, wuq_ref, wk_ref, wvt_ref, gq_ref,
                   gkv_ref, ckv_prev_ref if carry else None, kr_prev_ref if carry else None,
                   rq_ref, rk_ref, rv_ref, rg_ref, q_ref, kcat_ref, vt_ref, sig_ref, ckv_ref, kr_ref,
                   glu_ring, conv_prev_tile, slot, is_ctx, tm)


def _proj_tile(x_ref, mod_ref, cos_ref, sin_lo_ref, sin_hi_ref, wt_ref, wuq_ref, wk_ref, wvt_ref, gq_ref,
               gkv_ref, ckv_prev_ref, kr_prev_ref,
               rq_ref, rk_ref, rv_ref, rg_ref, q_ref, kcat_ref, vt_ref, sig_ref, ckv_ref, kr_ref,
               glu_ring, conv_prev_tile, slot, is_ctx, tm):
    carry = ckv_prev_ref is not None
    x = x_ref[...]
    u = (x * (1.0 + mod_ref[4:5, :]) + mod_ref[3:4, :]).astype(BF16)
    widths = (CONV_DIM, CONV_DIM, RET_HEADS * RET_DK, RET_HEADS * RET_DK, RET_HEADS * RET_DV, RET_HEADS * RET_DV,
              MLA_Q_LORA, MLA_KV_LORA)
    starts = [sum(widths[:n]) for n in range(len(widths))]

    def proj(n):
        return _dot_nt(u, wt_ref[starts[n]:starts[n] + widths[n], :])

    glu = proj(0) * _sigmoid(proj(1))
    conv_steps = conv_prev_tile(glu[0:CONV_HALO, :])

    def conv(n, result):
        for _ in range(n):
            conv_steps.pop(0)(result[0:SUBLANES, 0:128])

    conv(1, glu)
    glu_ring[slot] = glu

    mq = proj(6)
    mkv = proj(7)
    conv(2, mq)
    cos = cos_ref[...]
    sin_lo = sin_lo_ref[...]
    sin_hi = sin_hi_ref[...]

    def rotary(v):
        up = pltpu.roll(v, HEAD_PAD - ROPE_AXIS_HALF, 1)
        down = pltpu.roll(v, ROPE_AXIS_HALF, 1)
        return v * cos + up * sin_lo + down * sin_hi

    qn = (_rms_rows(mq) * gq_ref[...]).astype(BF16)
    qm = _dot(qn, wuq_ref[...])
    ckv = _rms_rows(mkv) * gkv_ref[...]
    ckvb = ckv.astype(BF16)
    kn = _dot(ckvb, wk_ref[...])
    vt_ref[...] = _dot_nt(wvt_ref[...], ckvb).astype(BF16)
    kr_grp = _dot_nt(u, wt_ref[MAIN_W:MAIN_W + HEAD_PAD, :])
    conv(2, kn)
    lane = lax.broadcasted_iota(jnp.int32, kr_grp.shape, 1)
    in_rope = jnp.logical_and(lane >= ROPE_LANE0, lane < ROPE_LANE0 + MLA_D_ROPE)
    kr = jnp.where(in_rope, pltpu.roll(kr_grp, ROPE_LANE0, 1), 0.0)
    kr_rot = rotary(kr)
    for h in range(MLA_HEADS):
        sl = slice(h * HEAD_PAD, (h + 1) * HEAD_PAD)
        q_ref[:, sl] = (rotary(qm[:, sl]) * ATTN_QSCALE).astype(BF16)
        kcat_ref[:, sl] = (kn[:, sl] + kr_rot).astype(BF16)

    gate0 = MAIN_W + MLA_D_ROPE

    def branch_gate(blk):
        cols = slice(blk * D_MODEL, (blk + 1) * D_MODEL)
        gate = _dot_nt(u, wt_ref[gate0 + cols.start:gate0 + cols.stop, :])
        sig_ref[:, cols] = _sigmoid(gate).astype(BF16)
        return gate

    for blk in range(N_BRANCHES):
        conv(1, branch_gate(blk))
    rg = proj(5)
    rg_ref[...] = (rg * _sigmoid(rg)).astype(BF16)
    conv(2, rg)
    rv_ref[...] = proj(4).astype(BF16)
    rk_ref[...] = (proj(3) * (RET_DK ** -0.5)).astype(BF16)
    rq_ref[...] = proj(2).astype(BF16)

    @pl.when(is_ctx)
    def _():
        seqs = tm // SEQ
        ckv3 = ckv.reshape(seqs, SEQ, MLA_KV_LORA)
        kr3 = kr_grp[:, :MLA_D_ROPE].reshape(seqs, SEQ, MLA_D_ROPE)
        if carry:
            ckv_ref[:, 0] = ckv_prev_ref[...]
            kr_ref[:, 0] = kr_prev_ref[...]
            ckv_ref[:, 1] = ckv3
            kr_ref[:, 1] = kr3
        else:
            ckv_ref[...] = ckv3
            kr_ref[...] = kr3


def _proj(x, mods_all, rope_tabs, wt_all, wuq, wk, wvt, gq, gkv, conv_w, carry, *, layer):
    tm = CONV_BLOCK
    n_w = PROJ_WSTEPS
    n_tiles = N_TOK // tm
    n_ctx_tiles = N_CTX // tm
    tiles_per_seq = DEC_SEQ // tm
    seqs = tm // SEQ

    def rope_index(i):
        j = _tile(i, n_w, tm)
        return (jnp.where(j < n_ctx_tiles, 0, tiles_per_seq + (j - n_ctx_tiles) % tiles_per_seq), 0)

    def row(w):
        return _row_spec(tm, w, n_w)

    def out(w, dt):
        return jax.ShapeDtypeStruct((N_TOK, w), dt)

    def ctx_seq_spec(*tail):
        zeros = (0,) * len(tail)
        return pl.BlockSpec((seqs,) + tail, lambda i: (jnp.minimum(_tile(i, n_w, tm), n_ctx_tiles - 1),) + zeros)

    rope = pl.BlockSpec((tm, HEAD_PAD), rope_index)
    in_specs = [row(D_MODEL), _mod_spec(tm, layer, n_w), rope, rope, rope,
                _wchunk_spec(wt_all, layer, n_w)] + [_layer_resident(w, layer) for w in (wuq, wk, wvt, gq, gkv, *conv_w)]
    args = [x, mods_all, *rope_tabs, wt_all, wuq, wk, wvt, gq, gkv, *conv_w]
    if carry is None:
        ctx_specs = [ctx_seq_spec(SEQ, MLA_KV_LORA), ctx_seq_spec(SEQ, MLA_D_ROPE)]
        ctx_shapes = [jax.ShapeDtypeStruct((BATCH, SEQ, MLA_KV_LORA), F32),
                      jax.ShapeDtypeStruct((BATCH, SEQ, MLA_D_ROPE), F32)]
    else:
        in_specs += [ctx_seq_spec(SEQ, MLA_KV_LORA), ctx_seq_spec(SEQ, MLA_D_ROPE)]
        args += list(carry)
        ctx_specs = [ctx_seq_spec(DEPTH, SEQ, MLA_KV_LORA), ctx_seq_spec(DEPTH, SEQ, MLA_D_ROPE)]
        ctx_shapes = [jax.ShapeDtypeStruct((BATCH, DEPTH, SEQ, MLA_KV_LORA), F32),
                      jax.ShapeDtypeStruct((BATCH, DEPTH, SEQ, MLA_D_ROPE), F32)]
    span = CONV_BLOCK + 2 * CONV_HALO - SUBLANES
    return pl.pallas_call(
        functools.partial(_proj_kernel, tm=tm, carry=carry is not None),
        grid=(n_w + n_tiles + 1,),
        in_specs=in_specs,
        out_specs=[pl.BlockSpec((tm, CONV_DIM), lambda i: (_tile(i, n_w + 1, tm), 0)), row(SLAB_W),
                   pl.BlockSpec((MLA_V_W, tm), lambda i: (0, _tile(i, n_w, tm)))] + ctx_specs,
        out_shape=[out(CONV_DIM, BF16), out(SLAB_W, BF16),
                   jax.ShapeDtypeStruct((MLA_V_W, N_TOK), BF16)] + ctx_shapes,
        scratch_shapes=[pltpu.VMEM((2, tm, CONV_DIM), F32),
                        pltpu.VMEM((CONV_BLOCK + 2 * CONV_HALO, CONV_DIM), F32),
                        pltpu.VMEM((SUBLANES, span, CONV_DIM), F32),
                        pltpu.VMEM((CONV_BLOCK, CONV_DIM), F32),
                        pltpu.VMEM(wt_all.shape[1:], BF16)],
        compiler_params=_params(("arbitrary",)),
        name="mix_proj",
    )(*args)


def _cache_kv_kernel(ckv_ref, kr_ref, wk_ref, wvt_ref, kcat_ref, vt_ref):
    ckvb = ckv_ref[...].astype(BF16)
    kn = _dot(ckvb, wk_ref[...])
    vt_ref[...] = _dot_nt(wvt_ref[...], ckvb).astype(BF16)
    kr = kr_ref[...]
    for h in range(MLA_HEADS):
        sl = slice(h * HEAD_PAD, (h + 1) * HEAD_PAD)
        kcat_ref[:, sl] = (kn[:, sl] + kr).astype(BF16)


def _cache_kv(cache_ckv, cache_kr_pad, wk, wvt):
    n = DEC_BATCH * PAST_LEN
    return pl.pallas_call(
        _cache_kv_kernel,
        grid=(DEPTH, DEC_BATCH),
        in_specs=[pl.BlockSpec((None, None, PAST_LEN, MLA_KV_LORA), lambda l, b: (b, l, 0, 0)),
                  pl.BlockSpec((None, None, PAST_LEN, HEAD_PAD), lambda l, b: (b, l, 0, 0)),
                  pl.BlockSpec((None, MLA_KV_LORA, MLA_W), lambda l, b: (l, 0, 0)),
                  pl.BlockSpec((None, MLA_V_W, MLA_KV_LORA), lambda l, b: (l, 0, 0))],
        out_specs=[pl.BlockSpec((None, PAST_LEN, MLA_W), lambda l, b: (l, b, 0)),
                   pl.BlockSpec((None, MLA_V_W, PAST_LEN), lambda l, b: (l, 0, b))],
        out_shape=[jax.ShapeDtypeStruct((DEPTH, n, MLA_W), BF16), jax.ShapeDtypeStruct((DEPTH, MLA_V_W, n), BF16)],
        compiler_params=_params(("arbitrary", "arbitrary")),
        name="cache_kv",
    )(cache_ckv, cache_kr_pad, wk, wvt)


def _ret_kernel(*refs, t, hp, latent, carry, layer):
    if latent:
        lg_ref, q_ref, k_ref, v_ref, g_ref, s0f_ref, s0b_ref, o_ref, d_ref = refs
    elif carry:
        lg_ref, q_ref, k_ref, v_ref, g_ref, sf_prev_ref, sb_prev_ref, o_ref, sf_ref, sb_ref, d_ref = refs
    else:
        lg_ref, q_ref, k_ref, v_ref, g_ref, o_ref, sf_ref, sb_ref, d_ref = refs
    hblk = pl.program_id(0)

    @pl.when(pl.program_id(1) == 0)
    def _():
        diff = (lax.broadcasted_iota(jnp.int32, (t, t), 0) - lax.broadcasted_iota(jnp.int32, (t, t), 1)).astype(F32)
        for hh in range(hp):
            lgf = lg_ref[layer, 0, hblk * hp + hh]
            lgb = lg_ref[layer, 1, hblk * hp + hh]
            d_ref[hh] = jnp.exp(jnp.where(diff >= 0, diff * lgf, -diff * lgb))

    if carry:
        sf_ref[0] = sf_prev_ref[...]
        sb_ref[0] = sb_prev_ref[...]
    pos = lax.broadcasted_iota(jnp.int32, (t, 1), 0).astype(F32)
    for hh in range(hp):
        lgf = lg_ref[layer, 0, hblk * hp + hh]
        lgb = lg_ref[layer, 1, hblk * hp + hh]
        q = q_ref[:, hh * RET_DK:(hh + 1) * RET_DK]
        k = k_ref[:, hh * RET_DK:(hh + 1) * RET_DK]
        v = v_ref[:, hh * RET_DV:(hh + 1) * RET_DV]
        p = (_dot_nt(q, k) * d_ref[hh]).astype(BF16)
        o = _dot(p, v)
        if latent:
            o = o + jnp.exp((pos + 1.0) * lgf) * _dot(q, s0f_ref[hh].astype(BF16))
            o = o + jnp.exp((t - pos) * lgb) * _dot(q, s0b_ref[hh].astype(BF16))
        else:
            kf = k.astype(F32)
            sf = _dot_tn((kf * jnp.exp((t - 1.0 - pos) * lgf)).astype(BF16), v)
            sb = _dot_tn((kf * jnp.exp(pos * lgb)).astype(BF16), v)
            if carry:
                sf_ref[1, hh] = sf
                sb_ref[1, hh] = sb
            else:
                sf_ref[hh] = sf
                sb_ref[hh] = sb
        o_ref[:, hh * RET_DV:(hh + 1) * RET_DV] = (g_ref[:, hh * RET_DV:(hh + 1) * RET_DV] * _norm_rows(o)).astype(BF16)


def _retention(log_g, slab, states, *, latent, layer):
    t = DEC_SEQ if latent else SEQ
    hp = 1 if latent else RET_HEADS
    n_seq = DEC_BATCH if latent else BATCH
    row0 = (N_CTX // t) if latent else 0

    def row(name, w):
        assert SLAB_COLS[name][0] % (hp * w) == 0
        first = SLAB_COLS[name][0] // (hp * w)
        return pl.BlockSpec((t, hp * w), lambda h, s: (row0 + s, first + h))

    smem = pl.BlockSpec(memory_space=pltpu.SMEM)
    out_shape = [jax.ShapeDtypeStruct((n_seq * t, RET_HEADS * RET_DV), BF16)]
    out_specs = [pl.BlockSpec((t, hp * RET_DV), lambda h, s: (s, h))]
    in_specs = [smem, row("rq", RET_DK), row("rk", RET_DK), row("rv", RET_DV), row("rg", RET_DV)]
    args = [log_g, slab, slab, slab, slab]
    carry = False
    if latent:
        st = pl.BlockSpec((None, None, hp, RET_DK, RET_DV), lambda h, s: (s, layer, h, 0, 0))
        in_specs += [st, st]
        args += list(states)
    else:
        st = pl.BlockSpec((None, hp, RET_DK, RET_DV), lambda h, s: (s, h, 0, 0))
        if states is None:
            out_specs += [st, st]
            out_shape += [jax.ShapeDtypeStruct((BATCH, RET_HEADS, RET_DK, RET_DV), F32)] * 2
        else:
            carry = True
            in_specs += [st, st]
            args += list(states)
            st2 = pl.BlockSpec((None, DEPTH, hp, RET_DK, RET_DV), lambda h, s: (s, 0, h, 0, 0))
            out_specs += [st2, st2]
            out_shape += [jax.ShapeDtypeStruct((BATCH, DEPTH, RET_HEADS, RET_DK, RET_DV), F32)] * 2

    return pl.pallas_call(
        functools.partial(_ret_kernel, t=t, hp=hp, latent=latent, carry=carry, layer=layer),
        grid=(RET_HEADS // hp, n_seq),
        in_specs=in_specs,
        out_specs=out_specs,
        out_shape=out_shape,
        scratch_shapes=[pltpu.VMEM((hp, t, t), F32)],
        compiler_params=_params(("arbitrary", "arbitrary")),
        name="retention_lat" if latent else "retention_ctx",
    )(*args)


def _attn_kernel(*refs, t, hp, latent):
    if latent:
        q_ref, k_ref, vt_ref, kc_ref, vtc_ref, o_ref = refs
    else:
        q_ref, k_ref, vt_ref, o_ref = refs
    qb = min(ATTN_QBLOCK, t)
    units = [(slice(hh * HEAD_PAD, (hh + 1) * HEAD_PAD), slice(hh * MLA_D_V, (hh + 1) * MLA_D_V),
              slice(b * qb, (b + 1) * qb)) for b in range(t // qb) for hh in range(hp)]
    def scores(unit):
        sl, _, rows = unit
        q = q_ref[rows, sl]
        s = [_dot_nt(k_ref[:, sl], q)]
        if latent:
            s.append(_dot_nt(kc_ref[:, sl], q))
        return s

    def softmax(s):
        m = functools.reduce(jnp.maximum, [jnp.max(x, axis=0, keepdims=True) for x in s])
        e = [jnp.exp2(x - m) for x in s]
        den = functools.reduce(jnp.add, [jnp.sum(x, axis=0, keepdims=True) for x in e])
        return [x.astype(BF16) for x in e], den

    def values(unit, e, den):
        _, vs, _ = unit
        o = _dot(vt_ref[vs, :], e[0])
        if latent:
            o = o + _dot(vtc_ref[vs, :], e[1])
        return o / den

    pairs = [units[u:u + 2] for u in range(0, len(units), 2)]
    s_next = [scores(u) for u in pairs[0]]
    sm_prev = None
    for g in range(len(pairs) + 1):
        s_cur = s_next
        if g + 1 < len(pairs):
            s_next = [scores(u) for u in pairs[g + 1]]
        sm_cur = [softmax(s) for s in s_cur] if g < len(pairs) else None
        if sm_prev is not None:
            (_, vs0, rows), (_, vs1, _) = pairs[g - 1]
            outs = [values(u, e, den) for u, (e, den) in zip(pairs[g - 1], sm_prev)]
            o_ref[rows, vs0.start:vs1.stop] = jnp.concatenate(outs, axis=0).T.astype(BF16)
        sm_prev = sm_cur


def _attention(slab, vt, cache, *, latent, layer):
    t = DEC_SEQ if latent else SEQ
    hp = 2 if latent else MLA_HEADS
    n_seq = DEC_BATCH if latent else BATCH
    row0 = (N_CTX // t) if latent else 0

    def row(name):
        assert SLAB_COLS[name][0] % (hp * HEAD_PAD) == 0
        first = SLAB_COLS[name][0] // (hp * HEAD_PAD)
        return pl.BlockSpec((t, hp * HEAD_PAD), lambda s, h: (row0 + s, first + h))

    col = pl.BlockSpec((hp * MLA_D_V, t), lambda s, h: (h, row0 + s))
    in_specs = [row("q"), row("kcat"), col]
    args = [slab, slab, vt]
    if latent:
        in_specs += [pl.BlockSpec((None, PAST_LEN, hp * HEAD_PAD), lambda s, h: (layer, s, h)),
                     pl.BlockSpec((None, hp * MLA_D_V, PAST_LEN), lambda s, h: (layer, h, s))]
        args += list(cache)
    return pl.pallas_call(
        functools.partial(_attn_kernel, t=t, hp=hp, latent=latent),
        grid=(n_seq, MLA_HEADS // hp),
        in_specs=in_specs,
        out_specs=pl.BlockSpec((t, hp * MLA_D_V), lambda s, h: (s, h)),
        out_shape=jax.ShapeDtypeStruct((n_seq * t, MLA_V_W), BF16),
        compiler_params=_params(("arbitrary", "arbitrary")),
        name="attention_lat" if latent else "attention_ctx",
    )(*args)


def _merge_kernel(x_ref, mod_ref, a_ref, rc_ref, rl_ref, mc_ref, ml_ref, sig_ref, wc_ref, wr_ref, wm_ref, wo_ref,
                  g_ref, b_ref, o_ref, wc_bf, wr_bf, wm_bf, wo_bf, *, tm):
    i = pl.program_id(0)
    is_ctx = i < N_WSTEPS + N_CTX // tm

    @pl.when(i < N_WSTEPS)
    def _():
        _stage_chunk(i, wc_ref, wc_bf)
        _stage_chunk(i, wr_ref, wr_bf)
        _stage_chunk(i, wo_ref, wo_bf)
        _stage_chunk(i, wm_ref, wm_bf)

    @pl.when(i >= N_WSTEPS)
    def _():
        x = x_ref[...]
        r = jnp.where(is_ctx, rc_ref[...], rl_ref[...])
        m = jnp.where(is_ctx, mc_ref[...], ml_ref[...])
        merged = sig_ref[:, 0:D_MODEL] * _dot(a_ref[...], wc_bf[...])
        merged = merged + sig_ref[:, D_MODEL:2 * D_MODEL] * _dot(r, wr_bf[...])
        merged = merged + sig_ref[:, 2 * D_MODEL:] * _dot(m, wm_bf[...])
        y = _dot(merged.astype(BF16), wo_bf[...])
        z = DEEPNORM_ALPHA * x + mod_ref[5:6, :] * y
        o_ref[...] = _norm_rows(z) * g_ref[...] + b_ref[...]


def _merge(x, mods_all, a, r_pair, m_pair, sig, wc, wr, wm, wo, ln_g, ln_b, *, layer, tm=512):
    def row(w):
        return _row_spec(tm, w, N_WSTEPS)

    def pair(w):
        return [_ctx_row_spec(tm, w, N_WSTEPS), _lat_row_spec(tm, w, N_WSTEPS)]

    ln_index = layer * 3 + 1
    return pl.pallas_call(
        functools.partial(_merge_kernel, tm=tm),
        grid=(N_WSTEPS + N_TOK // tm,),
        in_specs=[row(D_MODEL), _mod_spec(tm, layer, N_WSTEPS), row(CONV_DIM)] + pair(RET_HEADS * RET_DV)
        + pair(MLA_V_W) + [row(GATE_W), _wchunk_spec(wc, layer), _wchunk_spec(wr, layer), _wchunk_spec(wm, layer),
                         _wchunk_spec(wo, layer), _layer_row_spec(ln_g, ln_index), _layer_row_spec(ln_b, ln_index)],
        out_specs=row(D_MODEL),
        out_shape=jax.ShapeDtypeStruct((N_TOK, D_MODEL), F32),
        scratch_shapes=[pltpu.VMEM((CONV_DIM, D_MODEL), BF16), pltpu.VMEM((RET_HEADS * RET_DV, D_MODEL), BF16),
                        pltpu.VMEM((MLA_V_W, D_MODEL), BF16), pltpu.VMEM((D_MODEL, D_MODEL), BF16)],
        compiler_params=_params(("arbitrary",)),
        name="merge",
    )(x, mods_all, a, *r_pair, *m_pair, sig, wc, wr, wm, wo, ln_g, ln_b)


def _rope_tables():
    rows = DEC_SEQ // GRID_W
    row_id = jnp.repeat(jnp.arange(rows, dtype=F32), GRID_W)
    col_id = jnp.tile(jnp.arange(GRID_W, dtype=F32), rows)
    inv_freq = ROPE_BASE ** (-jnp.arange(ROPE_AXIS_HALF, dtype=F32) / ROPE_AXIS_HALF)
    ang = jnp.stack([row_id[:, None] * inv_freq, col_id[:, None] * inv_freq], axis=1)
    cos = jnp.cos(ang)
    sin = jnp.sin(ang)
    cos32 = jnp.stack([cos, cos], axis=2).reshape(DEC_SEQ, MLA_D_ROPE)
    zero = jnp.zeros_like(sin)
    sin_lo32 = jnp.stack([-sin, zero], axis=2).reshape(DEC_SEQ, MLA_D_ROPE)
    sin_hi32 = jnp.stack([zero, sin], axis=2).reshape(DEC_SEQ, MLA_D_ROPE)
    tail = HEAD_PAD - ROPE_LANE0 - MLA_D_ROPE
    cos_t = jnp.concatenate([jnp.ones((DEC_SEQ, ROPE_LANE0), F32), cos32, jnp.ones((DEC_SEQ, tail), F32)], axis=1)
    cos_t = jnp.concatenate([jnp.ones((DEC_SEQ, HEAD_PAD), F32), cos_t], axis=0)

    def sin_table(s32):
        return jnp.pad(s32, ((DEC_SEQ, 0), (ROPE_LANE0, tail)))

    return cos_t, sin_table(sin_lo32), sin_table(sin_hi32)


def _head_pad_cols(w, width):
    k = w.shape[0]
    w = w.reshape(k, MLA_HEADS, width)
    return jnp.pad(w, ((0, 0), (0, 0), (0, HEAD_PAD - width))).reshape(k, MLA_W)


def kernel(x_prompt, x_sample, cache_mla_ckv, cache_mla_krope, state_ret_fwd, state_ret_bwd, c, c_ctx, ada_w, ada_b, ffn1_w_in, ffn1_w_out, ffn2_w_in, ffn2_w_out, post_ln_g, post_ln_b, mix_w_in, conv_w_dw, conv_b_dw, conv_ln_g, conv_ln_b, conv_w_out, ret_decay_fwd, ret_decay_bwd, ret_w_out, mla_q_norm, mla_w_uq, mla_kv_norm, mla_w_ukv, mla_w_out, mix_w_o):
    assert DEPTH == 2
    cvec = jnp.concatenate([c_ctx[None, :], c, jnp.zeros((N_MOD_ROWS - 1 - DEC_BATCH, D_MODEL), F32)], axis=0)
    mods_all = _ada_mods(cvec, ada_w, ada_b).reshape(DEPTH, N_MOD_ROWS, N_MODS, D_MODEL)
    rope_tabs = _rope_tables()

    w_ukv = mla_w_ukv.reshape(DEPTH, MLA_KV_LORA, MLA_HEADS, MLA_D_NOPE + MLA_D_V)
    wk_all = _head_pad_cols(w_ukv[..., :MLA_D_NOPE].reshape(DEPTH * MLA_KV_LORA, -1), MLA_D_NOPE)
    wk_all = wk_all.reshape(DEPTH, MLA_KV_LORA, MLA_W).astype(BF16)
    wvt_all = jnp.swapaxes(w_ukv[..., MLA_D_NOPE:].reshape(DEPTH, MLA_KV_LORA, MLA_V_W), 1, 2).astype(BF16)
    kr_tail = HEAD_PAD - ROPE_LANE0 - MLA_D_ROPE
    cache_kr_pad = jnp.pad(cache_mla_krope, ((0, 0), (0, 0), (0, 0), (ROPE_LANE0, kr_tail)))
    kcat_c, vt_c = _cache_kv(cache_mla_ckv, cache_kr_pad, wk_all, wvt_all)
    log_g = jnp.stack([jax.nn.log_sigmoid(ret_decay_fwd), jax.nn.log_sigmoid(ret_decay_bwd)], axis=1)
    wt_all = jnp.swapaxes(mix_w_in, 1, 2)
    wuq_all = _head_pad_cols(mla_w_uq.reshape(DEPTH * MLA_Q_LORA, -1), MLA_D_NOPE + MLA_D_ROPE)
    wuq_all = wuq_all.reshape(DEPTH, MLA_Q_LORA, MLA_W).astype(BF16)
    gq_all = mla_q_norm.reshape(DEPTH, 1, MLA_Q_LORA)
    gkv_all = mla_kv_norm.reshape(DEPTH, 1, MLA_KV_LORA)
    conv_w = (conv_w_dw, conv_b_dw.reshape(DEPTH, 1, CONV_DIM), conv_ln_g.reshape(DEPTH, 1, CONV_DIM),
              conv_ln_b.reshape(DEPTH, 1, CONV_DIM))

    ln_g = post_ln_g.reshape(DEPTH * 3, 1, D_MODEL)
    ln_b = post_ln_b.reshape(DEPTH * 3, 1, D_MODEL)
    xs = (x_prompt.reshape(N_CTX, D_MODEL), x_sample.reshape(N_LAT, D_MODEL))
    ctx_carry = None
    state_carry = None
    for l in range(DEPTH):
        last = l == DEPTH - 1
        x = _ffn(xs, mods_all, ffn1_w_in, ffn1_w_out, ln_g, ln_b, layer=l, which=0)[0]

        a, slab, vt, ckv, kr = _proj(x, mods_all, rope_tabs, wt_all, wuq_all, wk_all, wvt_all, gq_all, gkv_all, conv_w,
                                     ctx_carry, layer=l)
        ctx_carry = (ckv, kr)

        r_ctx, sf, sb = _retention(log_g, slab, state_carry, latent=False, layer=l)
        state_carry = (sf, sb)
        r_lat = _retention(log_g, slab, (state_ret_fwd, state_ret_bwd), latent=True, layer=l)[0]
        m_ctx = _attention(slab, vt, None, latent=False, layer=l)
        m_lat = _attention(slab, vt, (kcat_c, vt_c), latent=True, layer=l)

        x = _merge(x, mods_all, a, (r_ctx, r_lat), (m_ctx, m_lat), slab, conv_w_out, ret_w_out, mla_w_out, mix_w_o,
                   ln_g, ln_b, layer=l)
        xs = _ffn((x,), mods_all, ffn2_w_in, ffn2_w_out, ln_g, ln_b, layer=l, which=2, split_out=last)

    y_ctx, y_lat = xs
    return (y_ctx.reshape(BATCH, SEQ, D_MODEL), y_lat.reshape(DEC_BATCH, DEC_SEQ, D_MODEL),
            ctx_carry[0], ctx_carry[1], state_carry[0], state_carry[1])
```

```python
import functools
import math

import jax
import jax.numpy as jnp
from jax import lax
from jax.experimental import pallas as pl
from jax.experimental.pallas import tpu as pltpu

F32 = jnp.float32
BF16 = jnp.bfloat16

D_MODEL = 1024
BATCH = 16
SEQ = 256
DEPTH = 2
DEC_BATCH = 4
DEC_SEQ = 1024
PAST_LEN = 256
GRID_W = 64
D_FF = 2816
N_MODS = 9
CONV_DIM = 512
CONV_WIDTH = 31
RET_HEADS = 4
RET_DK = 128
RET_DV = 256
MLA_HEADS = 8
MLA_Q_LORA = 512
MLA_KV_LORA = 256
MLA_D_NOPE = 64
MLA_D_ROPE = 32
MLA_D_V = 64
ROPE_AXIS_HALF = MLA_D_ROPE // 4
ROPE_BASE = 10000.0
DEEPNORM_ALPHA = (2 * DEPTH) ** 0.25
LN_EPS = 1e-5
RMS_EPS = 1e-6

N_CTX = BATCH * SEQ
N_LAT = DEC_BATCH * DEC_SEQ
N_TOK = N_CTX + N_LAT
N_MOD_ROWS = 8
HEAD_PAD = 128
ROPE_LANE0 = MLA_D_NOPE
MLA_W = MLA_HEADS * HEAD_PAD
MLA_V_W = MLA_HEADS * MLA_D_V
MAIN_W = 2 * CONV_DIM + 2 * RET_HEADS * RET_DK + 2 * RET_HEADS * RET_DV + MLA_Q_LORA + MLA_KV_LORA
N_BRANCHES = 3
GATE_W = N_BRANCHES * D_MODEL
SLAB_COLS = {}
for _name, _w in (("sig", GATE_W), ("rq", RET_HEADS * RET_DK), ("rk", RET_HEADS * RET_DK), ("rv", RET_HEADS * RET_DV),
                  ("rg", RET_HEADS * RET_DV), ("q", MLA_W), ("kcat", MLA_W)):
    SLAB_COLS[_name] = (sum(w for _, w in SLAB_COLS.values()), _w)
SLAB_W = sum(w for _, w in SLAB_COLS.values())
SUBLANES = 8
CONV_HALO = 16
CONV_BLOCK = 256
VMEM_LIMIT = 56 * 1024 * 1024
N_WSTEPS = 8
PROJ_WSTEPS = 6
MXU_COLS = 256
FFN_CHUNKS = ((0, 6 * MXU_COLS), (6 * MXU_COLS, D_FF))
ROW_HALVES = 2
ATTN_QBLOCK = 256
ATTN_QSCALE = (MLA_D_NOPE + MLA_D_ROPE) ** -0.5 * math.log2(math.e)


def _dot(a, b):
    return jnp.dot(a, b, preferred_element_type=F32)


def _dot_nt(a, b):
    return lax.dot_general(a, b, (((1,), (1,)), ((), ())), preferred_element_type=F32)


def _dot_tn(a, b):
    return lax.dot_general(a, b, (((0,), (0,)), ((), ())), preferred_element_type=F32)


def _sigmoid(x):
    return 1.0 / (1.0 + jnp.exp(-x))


def _norm_rows(z):
    mu = jnp.mean(z, axis=-1, keepdims=True)
    zc = z - mu
    var = jnp.mean(zc * zc, axis=-1, keepdims=True)
    return zc * lax.rsqrt(var + LN_EPS)


def _rms_rows(z):
    return z * lax.rsqrt(jnp.mean(z * z, axis=-1, keepdims=True) + RMS_EPS)


def _resident(shape):
    zeros = (0,) * len(shape)
    return pl.BlockSpec(shape, lambda *_: zeros, pipeline_mode=pl.Buffered(1))


def _layer_resident(w, layer):
    zeros = (0,) * (w.ndim - 1)
    return pl.BlockSpec((None,) + w.shape[1:], lambda *_: (layer,) + zeros, pipeline_mode=pl.Buffered(1))


def _tile(i, n_w, tm):
    return jnp.clip(i - n_w, 0, N_TOK // tm - 1)


def _row_spec(tm, w, n_w):
    return pl.BlockSpec((tm, w), lambda i: (_tile(i, n_w, tm), 0))


def _ctx_row_spec(tm, w, n_w):
    last = N_CTX // tm - 1
    return pl.BlockSpec((tm, w), lambda i: (jnp.minimum(_tile(i, n_w, tm), last), 0))


def _lat_row_spec(tm, w, n_w):
    first = N_CTX // tm
    return pl.BlockSpec((tm, w), lambda i: (jnp.maximum(_tile(i, n_w, tm) - first, 0), 0))


def _mod_spec(tm, layer, n_w):
    n_ctx_tiles = N_CTX // tm
    tiles_per_seq = DEC_SEQ // tm

    def index(i):
        j = _tile(i, n_w, tm)
        return (layer, jnp.where(j < n_ctx_tiles, 0, 1 + (j - n_ctx_tiles) // tiles_per_seq), 0, 0)

    return pl.BlockSpec((None, None, N_MODS, D_MODEL), index)


def _wchunk_spec(w, layer, n_w=N_WSTEPS):
    _, rows, cols = w.shape
    return pl.BlockSpec((None, rows // n_w, cols), lambda i: (layer, jnp.minimum(i, n_w - 1), 0))


def _layer_row_spec(w, index):
    return pl.BlockSpec((None, 1, w.shape[-1]), lambda i: (index, 0, 0), pipeline_mode=pl.Buffered(1))


def _stage_chunk(i, src_ref, dst_ref):
    rows = src_ref.shape[0]
    dst_ref[pl.ds(pl.multiple_of(i * rows, rows), rows), :] = src_ref[...].astype(BF16)


def _params(semantics):
    return pltpu.CompilerParams(dimension_semantics=semantics, vmem_limit_bytes=VMEM_LIMIT)


def _ada_kernel(c_ref, w_ref, b_ref, o_ref):
    c = c_ref[...]
    h = (c * _sigmoid(c)).astype(BF16)
    o_ref[...] = _dot(h, w_ref[...].astype(BF16)) + b_ref[...]


def _ada_mods(cvec, ada_w, ada_b):
    tn = D_MODEL
    n_out = N_MODS * D_MODEL
    return pl.pallas_call(
        _ada_kernel,
        grid=(DEPTH, n_out // tn),
        in_specs=[
            pl.BlockSpec((N_MOD_ROWS, D_MODEL), lambda l, j: (0, 0)),
            pl.BlockSpec((None, D_MODEL, tn), lambda l, j: (l, 0, j)),
            pl.BlockSpec((None, 1, tn), lambda l, j: (l, 0, j)),
        ],
        out_specs=pl.BlockSpec((None, N_MOD_ROWS, tn), lambda l, j: (l, 0, j)),
        out_shape=jax.ShapeDtypeStruct((DEPTH, N_MOD_ROWS, n_out), F32),
        compiler_params=_params(("arbitrary", "arbitrary")),
        name="ada_mods",
    )(cvec, ada_w, ada_b.reshape(DEPTH, 1, n_out))


def _ffn_kernel(*refs, base, tm, n_x, n_out):
    x_refs = refs[:n_x]
    mod_ref, win_ref, wout_ref, g_ref, b_ref = refs[n_x:n_x + 5]
    o_refs = refs[n_x + 5:n_x + 5 + n_out]
    win_bf, wout_bf = refs[n_x + 5 + n_out:]
    i = pl.program_id(0)
    is_ctx = i < N_WSTEPS + N_CTX // tm

    @pl.when(i < N_WSTEPS)
    def _():
        _stage_chunk(i, win_ref, win_bf)
        _stage_chunk(i, wout_ref, wout_bf)

    @pl.when(i >= N_WSTEPS)
    def _():
        shift = mod_ref[base:base + 1, :]
        scale = mod_ref[base + 1:base + 2, :]
        gate = mod_ref[base + 2:base + 3, :]
        halves = [slice(k * tm // ROW_HALVES, (k + 1) * tm // ROW_HALVES) for k in range(ROW_HALVES)]
        results = []
        for rows in halves:
            if n_x == 2:
                x = jnp.where(is_ctx, x_refs[0][rows, :], x_refs[1][rows, :])
            else:
                x = x_refs[0][rows, :]
            h = (x * (1.0 + scale) + shift).astype(BF16)
            y = None
            for lo, hi in FFN_CHUNKS:
                g = _dot(h, win_bf[:, lo:hi])
                u = _dot(h, win_bf[:, D_FF + lo:D_FF + hi])
                a = (g * _sigmoid(g) * u).astype(BF16)
                yc = _dot(a, wout_bf[lo:hi, :])
                y = yc if y is None else y + yc
            z = DEEPNORM_ALPHA * x + 0.5 * gate * y
            res = _norm_rows(z) * g_ref[...] + b_ref[...]
            if n_out == 1:
                o_refs[0][rows, :] = res
            results.append(res)
        if n_out == 2:
            @pl.when(is_ctx)
            def _():
                for rows, res in zip(halves, results):
                    o_refs[0][rows, :] = res

            @pl.when(jnp.logical_not(is_ctx))
            def _():
                for rows, res in zip(halves, results):
                    o_refs[1][rows, :] = res


def _ffn(xs, mods_all, w_in, w_out, ln_g, ln_b, *, layer, which, split_out=False, tm=512):
    row = _row_spec(tm, D_MODEL, N_WSTEPS)
    pair = [_ctx_row_spec(tm, D_MODEL, N_WSTEPS), _lat_row_spec(tm, D_MODEL, N_WSTEPS)]
    ln_index = layer * 3 + which
    if split_out:
        out_specs = pair
        out_shape = [jax.ShapeDtypeStruct((N_CTX, D_MODEL), F32), jax.ShapeDtypeStruct((N_LAT, D_MODEL), F32)]
    else:
        out_specs = [row]
        out_shape = [jax.ShapeDtypeStruct((N_TOK, D_MODEL), F32)]
    return pl.pallas_call(
        functools.partial(_ffn_kernel, base=3 * which, tm=tm, n_x=len(xs), n_out=len(out_specs)),
        grid=(N_WSTEPS + N_TOK // tm,),
        in_specs=(pair if len(xs) == 2 else [row]) + [
            _mod_spec(tm, layer, N_WSTEPS), _wchunk_spec(w_in, layer), _wchunk_spec(w_out, layer),
            _layer_row_spec(ln_g, ln_index), _layer_row_spec(ln_b, ln_index)],
        out_specs=out_specs,
        out_shape=out_shape,
        scratch_shapes=[pltpu.VMEM((D_MODEL, 2 * D_FF), BF16), pltpu.VMEM((D_FF, D_MODEL), BF16)],
        compiler_params=_params(("arbitrary",)),
        name="ffn",
    )(*xs, mods_all, w_in, w_out, ln_g, ln_b)


def _conv_stages(prev_halo, main, next_halo, w_ref, b_ref, g_ref, beta_ref, o_ref, pad_ref, shift_ref, acc_ref):
    rows = 32
    lanes = 128
    first = CONV_HALO - CONV_WIDTH // 2

    def setup(after):
        del after
        pad_ref[0:CONV_HALO, :] = prev_halo()
        pad_ref[CONV_HALO:CONV_HALO + CONV_BLOCK, :] = main()
        pad_ref[CONV_HALO + CONV_BLOCK:, :] = next_halo()
        span = shift_ref.shape[1]
        for ph in range(SUBLANES):
            shift_ref[ph] = pad_ref[ph:ph + span, :]

    def taps(c, r, start_from):
        cs = slice(c * lanes, (c + 1) * lanes)
        acc = jnp.broadcast_to(b_ref[:, cs], (rows, lanes)) + start_from
        for j in range(CONV_WIDTH):
            ph = (first + j) % SUBLANES
            start = r * rows + (first + j) - ph
            acc = acc + w_ref[j:j + 1, cs] * shift_ref[ph, start:start + rows, cs]
        acc_ref[r * rows:(r + 1) * rows, cs] = acc

    def finish(after):
        del after
        y = _norm_rows(acc_ref[...]) * g_ref[...] + beta_ref[...]
        o_ref[...] = (y * _sigmoid(y)).astype(BF16)

    def lane_group_half(c, half, after):
        if after is None:
            start_from = jnp.zeros((rows, lanes), F32)
        else:
            bits = lax.bitcast_convert_type(after, jnp.int32)
            zero = lax.shift_right_logical(lax.shift_right_logical(bits, 16), 16).astype(F32)
            start_from = jnp.concatenate([zero] * (rows // SUBLANES), axis=0)
        per_half = CONV_BLOCK // rows // 2
        for r in range(half * per_half, (half + 1) * per_half):
            taps(c, r, start_from)

    chunks = [functools.partial(lane_group_half, c, half) for c in range(CONV_DIM // lanes) for half in range(2)]
    return [setup] + chunks + [finish]


def _proj_kernel(*refs, tm, carry):
    (x_ref, mod_ref, cos_ref, sin_lo_ref, sin_hi_ref, wt_ref, wuq_ref, wk_ref, wvt_ref, gq_ref,
     gkv_ref, cw_ref, cb_ref, cg_ref, cbeta_ref) = refs[:15]
    refs = refs[15:]
    if carry:
        ckv_prev_ref, kr_prev_ref = refs[:2]
        refs = refs[2:]
    (a_ref, slab_ref, vt_ref, ckv_ref, kr_ref, glu_ring, pad_ref, shift_ref, acc_ref, wt_bf) = refs
    sig_ref, rq_ref, rk_ref, rv_ref, rg_ref, q_ref, kcat_ref = (
        slab_ref.at[:, lo:lo + w] for lo, w in (SLAB_COLS[n] for n in ("sig", "rq", "rk", "rv", "rg", "q", "kcat")))
    t = pl.program_id(0) - PROJ_WSTEPS
    n_tiles = N_TOK // tm
    n_ctx_tiles = N_CTX // tm
    tiles_per_seq = DEC_SEQ // tm
    is_ctx = t < n_ctx_tiles
    pos = (t - 1 - n_ctx_tiles) % tiles_per_seq
    conv_latent = t - 1 >= n_ctx_tiles
    has_prev = jnp.logical_and(conv_latent, pos != 0)
    has_next = jnp.logical_and(conv_latent, pos != tiles_per_seq - 1)
    slot = t % 2
    no_halo = jnp.zeros((CONV_HALO, CONV_DIM), F32)

    def conv_prev_tile(next_rows):
        return _conv_stages(
            lambda: jnp.where(has_prev, glu_ring[slot, tm - CONV_HALO:tm, :], no_halo),
            lambda: glu_ring[1 - slot],
            lambda: jnp.where(has_next, next_rows, no_halo),
            cw_ref, cb_ref, cg_ref, cbeta_ref, a_ref, pad_ref, shift_ref, acc_ref)

    @pl.when(t < 0)
    def _():
        _stage_chunk(pl.program_id(0), wt_ref, wt_bf)

    @pl.when(t == 0)
    def _():
        glu_ring[...] = jnp.zeros(glu_ring.shape, F32)

    @pl.when(t == n_tiles)
    def _():
        for step in conv_prev_tile(no_halo):
            step(None)

    @pl.when(jnp.logical_and(t >= 0, t < n_tiles))
    def _():
        _proj_tile(x_ref, mod_ref, cos_ref, sin_lo_ref, sin_hi_ref, wt_bf, wuq_ref, wk_ref, wvt_ref, gq_ref,
                   gkv_ref, ckv_prev_ref if carry else None, kr_prev_ref if carry else None,
                   rq_ref, rk_ref, rv_ref, rg_ref, q_ref, kcat_ref, vt_ref, sig_ref, ckv_ref, kr_ref,
                   glu_ring, conv_prev_tile, slot, is_ctx, tm)


def _proj_tile(x_ref, mod_ref, cos_ref, sin_lo_ref, sin_hi_ref, wt_ref, wuq_ref, wk_ref, wvt_ref, gq_ref,
               gkv_ref, ckv_prev_ref, kr_prev_ref,
               rq_ref, rk_ref, rv_ref, rg_ref, q_ref, kcat_ref, vt_ref, sig_ref, ckv_ref, kr_ref,
               glu_ring, conv_prev_tile, slot, is_ctx, tm):
    carry = ckv_prev_ref is not None
    x = x_ref[...]
    u = (x * (1.0 + mod_ref[4:5, :]) + mod_ref[3:4, :]).astype(BF16)
    widths = (CONV_DIM, CONV_DIM, RET_HEADS * RET_DK, RET_HEADS * RET_DK, RET_HEADS * RET_DV, RET_HEADS * RET_DV,
              MLA_Q_LORA, MLA_KV_LORA)
    starts = [sum(widths[:n]) for n in range(len(widths))]

    def proj(n):
        return _dot_nt(u, wt_ref[starts[n]:starts[n] + widths[n], :])

    glu = proj(0) * _sigmoid(proj(1))
    conv_steps = conv_prev_tile(glu[0:CONV_HALO, :])

    def conv(n, result):
        for _ in range(n):
            conv_steps.pop(0)(result[0:SUBLANES, 0:128])

    conv(1, glu)
    glu_ring[slot] = glu

    mq = proj(6)
    mkv = proj(7)
    conv(2, mq)
    cos = cos_ref[...]
    sin_lo = sin_lo_ref[...]
    sin_hi = sin_hi_ref[...]

    def rotary(v):
        up = pltpu.roll(v, HEAD_PAD - ROPE_AXIS_HALF, 1)
        down = pltpu.roll(v, ROPE_AXIS_HALF, 1)
        return v * cos + up * sin_lo + down * sin_hi

    qn = (_rms_rows(mq) * gq_ref[...]).astype(BF16)
    qm = _dot(qn, wuq_ref[...])
    ckv = _rms_rows(mkv) * gkv_ref[...]
    ckvb = ckv.astype(BF16)
    kn = _dot(ckvb, wk_ref[...])
    vt_ref[...] = _dot_nt(wvt_ref[...], ckvb).astype(BF16)
    kr_grp = _dot_nt(u, wt_ref[MAIN_W:MAIN_W + HEAD_PAD, :])
    conv(2, kn)
    lane = lax.broadcasted_iota(jnp.int32, kr_grp.shape, 1)
    in_rope = jnp.logical_and(lane >= ROPE_LANE0, lane < ROPE_LANE0 + MLA_D_ROPE)
    kr = jnp.where(in_rope, pltpu.roll(kr_grp, ROPE_LANE0, 1), 0.0)
    kr_rot = rotary(kr)
    for h in range(MLA_HEADS):
        sl = slice(h * HEAD_PAD, (h + 1) * HEAD_PAD)
        q_ref[:, sl] = (rotary(qm[:, sl]) * ATTN_QSCALE).astype(BF16)
        kcat_ref[:, sl] = (kn[:, sl] + kr_rot).astype(BF16)

    gate0 = MAIN_W + MLA_D_ROPE

    def branch_gate(blk):
        cols = slice(blk * D_MODEL, (blk + 1) * D_MODEL)
        gate = _dot_nt(u, wt_ref[gate0 + cols.start:gate0 + cols.stop, :])
        sig_ref[:, cols] = _sigmoid(gate).astype(BF16)
        return gate

    for blk in range(N_BRANCHES):
        conv(1, branch_gate(blk))
    rg = proj(5)
    rg_ref[...] = (rg * _sigmoid(rg)).astype(BF16)
    conv(2, rg)
    rv_ref[...] = proj(4).astype(BF16)
    rk_ref[...] = (proj(3) * (RET_DK ** -0.5)).astype(BF16)
    rq_ref[...] = proj(2).astype(BF16)

    @pl.when(is_ctx)
    def _():
        seqs = tm // SEQ
        ckv3 = ckv.reshape(seqs, SEQ, MLA_KV_LORA)
        kr3 = kr_grp[:, :MLA_D_ROPE].reshape(seqs, SEQ, MLA_D_ROPE)
        if carry:
            ckv_ref[:, 0] = ckv_prev_ref[...]
            kr_ref[:, 0] = kr_prev_ref[...]
            ckv_ref[:, 1] = ckv3
            kr_ref[:, 1] = kr3
        else:
            ckv_ref[...] = ckv3
            kr_ref[...] = kr3


def _proj(x, mods_all, rope_tabs, wt_all, wuq, wk, wvt, gq, gkv, conv_w, carry, *, layer):
    tm = CONV_BLOCK
    n_w = PROJ_WSTEPS
    n_tiles = N_TOK // tm
    n_ctx_tiles = N_CTX // tm
    tiles_per_seq = DEC_SEQ // tm
    seqs = tm // SEQ

    def rope_index(i):
        j = _tile(i, n_w, tm)
        return (jnp.where(j < n_ctx_tiles, 0, tiles_per_seq + (j - n_ctx_tiles) % tiles_per_seq), 0)

    def row(w):
        return _row_spec(tm, w, n_w)

    def out(w, dt):
        return jax.ShapeDtypeStruct((N_TOK, w), dt)

    def ctx_seq_spec(*tail):
        zeros = (0,) * len(tail)
        return pl.BlockSpec((seqs,) + tail, lambda i: (jnp.minimum(_tile(i, n_w, tm), n_ctx_tiles - 1),) + zeros)

    rope = pl.BlockSpec((tm, HEAD_PAD), rope_index)
    in_specs = [row(D_MODEL), _mod_spec(tm, layer, n_w), rope, rope, rope,
                _wchunk_spec(wt_all, layer, n_w)] + [_layer_resident(w, layer) for w in (wuq, wk, wvt, gq, gkv, *conv_w)]
    args = [x, mods_all, *rope_tabs, wt_all, wuq, wk, wvt, gq, gkv, *conv_w]
    if carry is None:
        ctx_specs = [ctx_seq_spec(SEQ, MLA_KV_LORA), ctx_seq_spec(SEQ, MLA_D_ROPE)]
        ctx_shapes = [jax.ShapeDtypeStruct((BATCH, SEQ, MLA_KV_LORA), F32),
                      jax.ShapeDtypeStruct((BATCH, SEQ, MLA_D_ROPE), F32)]
    else:
        in_specs += [ctx_seq_spec(SEQ, MLA_KV_LORA), ctx_seq_spec(SEQ, MLA_D_ROPE)]
        args += list(carry)
        ctx_specs = [ctx_seq_spec(DEPTH, SEQ, MLA_KV_LORA), ctx_seq_spec(DEPTH, SEQ, MLA_D_ROPE)]
        ctx_shapes = [jax.ShapeDtypeStruct((BATCH, DEPTH, SEQ, MLA_KV_LORA), F32),
                      jax.ShapeDtypeStruct((BATCH, DEPTH, SEQ, MLA_D_ROPE), F32)]
    span = CONV_BLOCK + 2 * CONV_HALO - SUBLANES
    return pl.pallas_call(
        functools.partial(_proj_kernel, tm=tm, carry=carry is not None),
        grid=(n_w + n_tiles + 1,),
        in_specs=in_specs,
        out_specs=[pl.BlockSpec((tm, CONV_DIM), lambda i: (_tile(i, n_w + 1, tm), 0)), row(SLAB_W),
                   pl.BlockSpec((MLA_V_W, tm), lambda i: (0, _tile(i, n_w, tm)))] + ctx_specs,
        out_shape=[out(CONV_DIM, BF16), out(SLAB_W, BF16),
                   jax.ShapeDtypeStruct((MLA_V_W, N_TOK), BF16)] + ctx_shapes,
        scratch_shapes=[pltpu.VMEM((2, tm, CONV_DIM), F32),
                        pltpu.VMEM((CONV_BLOCK + 2 * CONV_HALO, CONV_DIM), F32),
                        pltpu.VMEM((SUBLANES, span, CONV_DIM), F32),
                        pltpu.VMEM((CONV_BLOCK, CONV_DIM), F32),
                        pltpu.VMEM(wt_all.shape[1:], BF16)],
        compiler_params=_params(("arbitrary",)),
        name="mix_proj",
    )(*args)


def _cache_kv_kernel(ckv_ref, kr_ref, wk_ref, wvt_ref, kcat_ref, vt_ref):
    ckvb = ckv_ref[...].astype(BF16)
    kn = _dot(ckvb, wk_ref[...])
    vt_ref[...] = _dot_nt(wvt_ref[...], ckvb).astype(BF16)
    kr = kr_ref[...]
    for h in range(MLA_HEADS):
        sl = slice(h * HEAD_PAD, (h + 1) * HEAD_PAD)
        kcat_ref[:, sl] = (kn[:, sl] + kr).astype(BF16)


def _cache_kv(cache_ckv, cache_kr_pad, wk, wvt):
    n = DEC_BATCH * PAST_LEN
    return pl.pallas_call(
        _cache_kv_kernel,
        grid=(DEPTH, DEC_BATCH),
        in_specs=[pl.BlockSpec((None, None, PAST_LEN, MLA_KV_LORA), lambda l, b: (b, l, 0, 0)),
                  pl.BlockSpec((None, None, PAST_LEN, HEAD_PAD), lambda l, b: (b, l, 0, 0)),
                  pl.BlockSpec((None, MLA_KV_LORA, MLA_W), lambda l, b: (l, 0, 0)),
                  pl.BlockSpec((None, MLA_V_W, MLA_KV_LORA), lambda l, b: (l, 0, 0))],
        out_specs=[pl.BlockSpec((None, PAST_LEN, MLA_W), lambda l, b: (l, b, 0)),
                   pl.BlockSpec((None, MLA_V_W, PAST_LEN), lambda l, b: (l, 0, b))],
        out_shape=[jax.ShapeDtypeStruct((DEPTH, n, MLA_W), BF16), jax.ShapeDtypeStruct((DEPTH, MLA_V_W, n), BF16)],
        compiler_params=_params(("arbitrary", "arbitrary")),
        name="cache_kv",
    )(cache_ckv, cache_kr_pad, wk, wvt)


def _ret_kernel(*refs, t, hp, latent, carry, layer):
    if latent:
        lg_ref, q_ref, k_ref, v_ref, g_ref, s0f_ref, s0b_ref, o_ref, d_ref = refs
    elif carry:
        lg_ref, q_ref, k_ref, v_ref, g_ref, sf_prev_ref, sb_prev_ref, o_ref, sf_ref, sb_ref, d_ref = refs
    else:
        lg_ref, q_ref, k_ref, v_ref, g_ref, o_ref, sf_ref, sb_ref, d_ref = refs
    hblk = pl.program_id(0)

    @pl.when(pl.program_id(1) == 0)
    def _():
        diff = (lax.broadcasted_iota(jnp.int32, (t, t), 0) - lax.broadcasted_iota(jnp.int32, (t, t), 1)).astype(F32)
        for hh in range(hp):
            lgf = lg_ref[layer, 0, hblk * hp + hh]
            lgb = lg_ref[layer, 1, hblk * hp + hh]
            d_ref[hh] = jnp.exp(jnp.where(diff >= 0, diff * lgf, -diff * lgb))

    if carry:
        sf_ref[0] = sf_prev_ref[...]
        sb_ref[0] = sb_prev_ref[...]
    pos = lax.broadcasted_iota(jnp.int32, (t, 1), 0).astype(F32)
    for hh in range(hp):
        lgf = lg_ref[layer, 0, hblk * hp + hh]
        lgb = lg_ref[layer, 1, hblk * hp + hh]
        q = q_ref[:, hh * RET_DK:(hh + 1) * RET_DK]
        k = k_ref[:, hh * RET_DK:(hh + 1) * RET_DK]
        v = v_ref[:, hh * RET_DV:(hh + 1) * RET_DV]
        p = (_dot_nt(q, k) * d_ref[hh]).astype(BF16)
        o = _dot(p, v)
        if latent:
            o = o + jnp.exp((pos + 1.0) * lgf) * _dot(q, s0f_ref[hh].astype(BF16))
            o = o + jnp.exp((t - pos) * lgb) * _dot(q, s0b_ref[hh].astype(BF16))
        else:
            kf = k.astype(F32)
            sf = _dot_tn((kf * jnp.exp((t - 1.0 - pos) * lgf)).astype(BF16), v)
            sb = _dot_tn((kf * jnp.exp(pos * lgb)).astype(BF16), v)
            if carry:
                sf_ref[1, hh] = sf
                sb_ref[1, hh] = sb
            else:
                sf_ref[hh] = sf
                sb_ref[hh] = sb
        o_ref[:, hh * RET_DV:(hh + 1) * RET_DV] = (g_ref[:, hh * RET_DV:(hh + 1) * RET_DV] * _norm_rows(o)).astype(BF16)


def _retention(log_g, slab, states, *, latent, layer):
    t = DEC_SEQ if latent else SEQ
    hp = 2 if latent else RET_HEADS
    n_seq = DEC_BATCH if latent else BATCH
    row0 = (N_CTX // t) if latent else 0

    def row(name, w):
        assert SLAB_COLS[name][0] % (hp * w) == 0
        first = SLAB_COLS[name][0] // (hp * w)
        return pl.BlockSpec((t, hp * w), lambda h, s: (row0 + s, first + h))

    smem = pl.BlockSpec(memory_space=pltpu.SMEM)
    out_shape = [jax.ShapeDtypeStruct((n_seq * t, RET_HEADS * RET_DV), BF16)]
    out_specs = [pl.BlockSpec((t, hp * RET_DV), lambda h, s: (s, h))]
    in_specs = [smem, row("rq", RET_DK), row("rk", RET_DK), row("rv", RET_DV), row("rg", RET_DV)]
    args = [log_g, slab, slab, slab, slab]
    carry = False
    if latent:
        st = pl.BlockSpec((None, None, hp, RET_DK, RET_DV), lambda h, s: (s, layer, h, 0, 0))
        in_specs += [st, st]
        args += list(states)
    else:
        st = pl.BlockSpec((None, hp, RET_DK, RET_DV), lambda h, s: (s, h, 0, 0))
        if states is None:
            out_specs += [st, st]
            out_shape += [jax.ShapeDtypeStruct((BATCH, RET_HEADS, RET_DK, RET_DV), F32)] * 2
        else:
            carry = True
            in_specs += [st, st]
            args += list(states)
            st2 = pl.BlockSpec((None, DEPTH, hp, RET_DK, RET_DV), lambda h, s: (s, 0, h, 0, 0))
            out_specs += [st2, st2]
            out_shape += [jax.ShapeDtypeStruct((BATCH, DEPTH, RET_HEADS, RET_DK, RET_DV), F32)] * 2

    return pl.pallas_call(
        functools.partial(_ret_kernel, t=t, hp=hp, latent=latent, carry=carry, layer=layer),
        grid=(RET_HEADS // hp, n_seq),
        in_specs=in_specs,
        out_specs=out_specs,
        out_shape=out_shape,
        scratch_shapes=[pltpu.VMEM((hp, t, t), F32)],
        compiler_params=_params(("arbitrary", "arbitrary")),
        name="retention_lat" if latent else "retention_ctx",
    )(*args)


def _attn_kernel(*refs, t, hp, latent):
    if latent:
        q_ref, k_ref, vt_ref, kc_ref, vtc_ref, o_ref = refs
    else:
        q_ref, k_ref, vt_ref, o_ref = refs
    qb = min(ATTN_QBLOCK, t)
    units = [(slice(hh * HEAD_PAD, (hh + 1) * HEAD_PAD), slice(hh * MLA_D_V, (hh + 1) * MLA_D_V),
              slice(b * qb, (b + 1) * qb)) for b in range(t // qb) for hh in range(hp)]
    def scores(unit):
        sl, _, rows = unit
        q = q_ref[rows, sl]
        s = [_dot_nt(k_ref[:, sl], q)]
        if latent:
            s.append(_dot_nt(kc_ref[:, sl], q))
        return s

    def softmax(s):
        m = functools.reduce(jnp.maximum, [jnp.max(x, axis=0, keepdims=True) for x in s])
        e = [jnp.exp2(x - m) for x in s]
        den = functools.reduce(jnp.add, [jnp.sum(x, axis=0, keepdims=True) for x in e])
        return [x.astype(BF16) for x in e], den

    def values(unit, e, den):
        _, vs, _ = unit
        o = _dot(vt_ref[vs, :], e[0])
        if latent:
            o = o + _dot(vtc_ref[vs, :], e[1])
        return o / den

    pairs = [units[u:u + 2] for u in range(0, len(units), 2)]
    s_next = [scores(u) for u in pairs[0]]
    sm_prev = None
    for g in range(len(pairs) + 1):
        s_cur = s_next
        if g + 1 < len(pairs):
            s_next = [scores(u) for u in pairs[g + 1]]
        sm_cur = [softmax(s) for s in s_cur] if g < len(pairs) else None
        if sm_prev is not None:
            (_, vs0, rows), (_, vs1, _) = pairs[g - 1]
            outs = [values(u, e, den) for u, (e, den) in zip(pairs[g - 1], sm_prev)]
            o_ref[rows, vs0.start:vs1.stop] = jnp.concatenate(outs, axis=0).T.astype(BF16)
        sm_prev = sm_cur


def _attention(slab, vt, cache, *, latent, layer):
    t = DEC_SEQ if latent else SEQ
    hp = 4 if latent else MLA_HEADS
    n_seq = DEC_BATCH if latent else BATCH
    row0 = (N_CTX // t) if latent else 0

    def row(name):
        assert SLAB_COLS[name][0] % (hp * HEAD_PAD) == 0
        first = SLAB_COLS[name][0] // (hp * HEAD_PAD)
        return pl.BlockSpec((t, hp * HEAD_PAD), lambda s, h: (row0 + s, first + h))

    col = pl.BlockSpec((hp * MLA_D_V, t), lambda s, h: (h, row0 + s))
    in_specs = [row("q"), row("kcat"), col]
    args = [slab, slab, vt]
    if latent:
        in_specs += [pl.BlockSpec((None, PAST_LEN, hp * HEAD_PAD), lambda s, h: (layer, s, h)),
                     pl.BlockSpec((None, hp * MLA_D_V, PAST_LEN), lambda s, h: (layer, h, s))]
        args += list(cache)
    return pl.pallas_call(
        functools.partial(_attn_kernel, t=t, hp=hp, latent=latent),
        grid=(n_seq, MLA_HEADS // hp),
        in_specs=in_specs,
        out_specs=pl.BlockSpec((t, hp * MLA_D_V), lambda s, h: (s, h)),
        out_shape=jax.ShapeDtypeStruct((n_seq * t, MLA_V_W), BF16),
        compiler_params=_params(("arbitrary", "arbitrary")),
        name="attention_lat" if latent else "attention_ctx",
    )(*args)


def _merge_kernel(x_ref, mod_ref, a_ref, rc_ref, rl_ref, mc_ref, ml_ref, sig_ref, wc_ref, wr_ref, wm_ref, wo_ref,
                  g_ref, b_ref, o_ref, wc_bf, wr_bf, wm_bf, wo_bf, *, tm):
    i = pl.program_id(0)
    is_ctx = i < N_WSTEPS + N_CTX // tm

    @pl.when(i < N_WSTEPS)
    def _():
        _stage_chunk(i, wc_ref, wc_bf)
        _stage_chunk(i, wr_ref, wr_bf)
        _stage_chunk(i, wo_ref, wo_bf)
        _stage_chunk(i, wm_ref, wm_bf)

    @pl.when(i >= N_WSTEPS)
    def _():
        for half in range(ROW_HALVES):
            rows = slice(half * tm // ROW_HALVES, (half + 1) * tm // ROW_HALVES)
            x = x_ref[rows, :]
            r = jnp.where(is_ctx, rc_ref[rows, :], rl_ref[rows, :])
            m = jnp.where(is_ctx, mc_ref[rows, :], ml_ref[rows, :])
            merged = sig_ref[rows, 0:D_MODEL] * _dot(a_ref[rows, :], wc_bf[...])
            merged = merged + sig_ref[rows, D_MODEL:2 * D_MODEL] * _dot(r, wr_bf[...])
            merged = merged + sig_ref[rows, 2 * D_MODEL:] * _dot(m, wm_bf[...])
            y = _dot(merged.astype(BF16), wo_bf[...])
            z = DEEPNORM_ALPHA * x + mod_ref[5:6, :] * y
            o_ref[rows, :] = _norm_rows(z) * g_ref[...] + b_ref[...]


def _merge(x, mods_all, a, r_pair, m_pair, sig, wc, wr, wm, wo, ln_g, ln_b, *, layer, tm=512):
    def row(w):
        return _row_spec(tm, w, N_WSTEPS)

    def pair(w):
        return [_ctx_row_spec(tm, w, N_WSTEPS), _lat_row_spec(tm, w, N_WSTEPS)]

    ln_index = layer * 3 + 1
    return pl.pallas_call(
        functools.partial(_merge_kernel, tm=tm),
        grid=(N_WSTEPS + N_TOK // tm,),
        in_specs=[row(D_MODEL), _mod_spec(tm, layer, N_WSTEPS), row(CONV_DIM)] + pair(RET_HEADS * RET_DV)
        + pair(MLA_V_W) + [row(GATE_W), _wchunk_spec(wc, layer), _wchunk_spec(wr, layer), _wchunk_spec(wm, layer),
                         _wchunk_spec(wo, layer), _layer_row_spec(ln_g, ln_index), _layer_row_spec(ln_b, ln_index)],
        out_specs=row(D_MODEL),
        out_shape=jax.ShapeDtypeStruct((N_TOK, D_MODEL), F32),
        scratch_shapes=[pltpu.VMEM((CONV_DIM, D_MODEL), BF16), pltpu.VMEM((RET_HEADS * RET_DV, D_MODEL), BF16),
                        pltpu.VMEM((MLA_V_W, D_MODEL), BF16), pltpu.VMEM((D_MODEL, D_MODEL), BF16)],
        compiler_params=_params(("arbitrary",)),
        name="merge",
    )(x, mods_all, a, *r_pair, *m_pair, sig, wc, wr, wm, wo, ln_g, ln_b)


def _rope_tables():
    rows = DEC_SEQ // GRID_W
    row_id = jnp.repeat(jnp.arange(rows, dtype=F32), GRID_W)
    col_id = jnp.tile(jnp.arange(GRID_W, dtype=F32), rows)
    inv_freq = ROPE_BASE ** (-jnp.arange(ROPE_AXIS_HALF, dtype=F32) / ROPE_AXIS_HALF)
    ang = jnp.stack([row_id[:, None] * inv_freq, col_id[:, None] * inv_freq], axis=1)
    cos = jnp.cos(ang)
    sin = jnp.sin(ang)
    cos32 = jnp.stack([cos, cos], axis=2).reshape(DEC_SEQ, MLA_D_ROPE)
    zero = jnp.zeros_like(sin)
    sin_lo32 = jnp.stack([-sin, zero], axis=2).reshape(DEC_SEQ, MLA_D_ROPE)
    sin_hi32 = jnp.stack([zero, sin], axis=2).reshape(DEC_SEQ, MLA_D_ROPE)
    tail = HEAD_PAD - ROPE_LANE0 - MLA_D_ROPE
    cos_t = jnp.concatenate([jnp.ones((DEC_SEQ, ROPE_LANE0), F32), cos32, jnp.ones((DEC_SEQ, tail), F32)], axis=1)
    cos_t = jnp.concatenate([jnp.ones((DEC_SEQ, HEAD_PAD), F32), cos_t], axis=0)

    def sin_table(s32):
        return jnp.pad(s32, ((DEC_SEQ, 0), (ROPE_LANE0, tail)))

    return cos_t, sin_table(sin_lo32), sin_table(sin_hi32)


def _head_pad_cols(w, width):
    k = w.shape[0]
    w = w.reshape(k, MLA_HEADS, width)
    return jnp.pad(w, ((0, 0), (0, 0), (0, HEAD_PAD - width))).reshape(k, MLA_W)


def kernel(x_prompt, x_sample, cache_mla_ckv, cache_mla_krope, state_ret_fwd, state_ret_bwd, c, c_ctx, ada_w, ada_b, ffn1_w_in, ffn1_w_out, ffn2_w_in, ffn2_w_out, post_ln_g, post_ln_b, mix_w_in, conv_w_dw, conv_b_dw, conv_ln_g, conv_ln_b, conv_w_out, ret_decay_fwd, ret_decay_bwd, ret_w_out, mla_q_norm, mla_w_uq, mla_kv_norm, mla_w_ukv, mla_w_out, mix_w_o):
    assert DEPTH == 2
    cvec = jnp.concatenate([c_ctx[None, :], c, jnp.zeros((N_MOD_ROWS - 1 - DEC_BATCH, D_MODEL), F32)], axis=0)
    mods_all = _ada_mods(cvec, ada_w, ada_b).reshape(DEPTH, N_MOD_ROWS, N_MODS, D_MODEL)
    rope_tabs = _rope_tables()

    w_ukv = mla_w_ukv.reshape(DEPTH, MLA_KV_LORA, MLA_HEADS, MLA_D_NOPE + MLA_D_V)
    wk_all = _head_pad_cols(w_ukv[..., :MLA_D_NOPE].reshape(DEPTH * MLA_KV_LORA, -1), MLA_D_NOPE)
    wk_all = wk_all.reshape(DEPTH, MLA_KV_LORA, MLA_W).astype(BF16)
    wvt_all = jnp.swapaxes(w_ukv[..., MLA_D_NOPE:].reshape(DEPTH, MLA_KV_LORA, MLA_V_W), 1, 2).astype(BF16)
    kr_tail = HEAD_PAD - ROPE_LANE0 - MLA_D_ROPE
    cache_kr_pad = jnp.pad(cache_mla_krope, ((0, 0), (0, 0), (0, 0), (ROPE_LANE0, kr_tail)))
    kcat_c, vt_c = _cache_kv(cache_mla_ckv, cache_kr_pad, wk_all, wvt_all)
    log_g = jnp.stack([jax.nn.log_sigmoid(ret_decay_fwd), jax.nn.log_sigmoid(ret_decay_bwd)], axis=1)
    wt_all = jnp.swapaxes(mix_w_in, 1, 2)
    wuq_all = _head_pad_cols(mla_w_uq.reshape(DEPTH * MLA_Q_LORA, -1), MLA_D_NOPE + MLA_D_ROPE)
    wuq_all = wuq_all.reshape(DEPTH, MLA_Q_LORA, MLA_W).astype(BF16)
    gq_all = mla_q_norm.reshape(DEPTH, 1, MLA_Q_LORA)
    gkv_all = mla_kv_norm.reshape(DEPTH, 1, MLA_KV_LORA)
    conv_w = (conv_w_dw, conv_b_dw.reshape(DEPTH, 1, CONV_DIM), conv_ln_g.reshape(DEPTH, 1, CONV_DIM),
              conv_ln_b.reshape(DEPTH, 1, CONV_DIM))

    ln_g = post_ln_g.reshape(DEPTH * 3, 1, D_MODEL)
    ln_b = post_ln_b.reshape(DEPTH * 3, 1, D_MODEL)
    xs = (x_prompt.reshape(N_CTX, D_MODEL), x_sample.reshape(N_LAT, D_MODEL))
    ctx_carry = None
    state_carry = None
    for l in range(DEPTH):
        last = l == DEPTH - 1
        x = _ffn(xs, mods_all, ffn1_w_in, ffn1_w_out, ln_g, ln_b, layer=l, which=0)[0]

        a, slab, vt, ckv, kr = _proj(x, mods_all, rope_tabs, wt_all, wuq_all, wk_all, wvt_all, gq_all, gkv_all, conv_w,
                                     ctx_carry, layer=l)
        ctx_carry = (ckv, kr)

        r_ctx, sf, sb = _retention(log_g, slab, state_carry, latent=False, layer=l)
        state_carry = (sf, sb)
        r_lat = _retention(log_g, slab, (state_ret_fwd, state_ret_bwd), latent=True, layer=l)[0]
        m_ctx = _attention(slab, vt, None, latent=False, layer=l)
        m_lat = _attention(slab, vt, (kcat_c, vt_c), latent=True, layer=l)

        x = _merge(x, mods_all, a, (r_ctx, r_lat), (m_ctx, m_lat), slab, conv_w_out, ret_w_out, mla_w_out, mix_w_o,
                   ln_g, ln_b, layer=l)
        xs = _ffn((x,), mods_all, ffn2_w_in, ffn2_w_out, ln_g, ln_b, layer=l, which=2, split_out=last)

    y_ctx, y_lat = xs
    return (y_ctx.reshape(BATCH, SEQ, D_MODEL), y_lat.reshape(DEC_BATCH, DEC_SEQ, D_MODEL),
            ctx_carry[0], ctx_carry[1], state_carry[0], state_carry[1])
```

```python
import functools
import math

import jax
import jax.numpy as jnp
from jax import lax
from jax.experimental import pallas as pl
from jax.experimental.pallas import tpu as pltpu

F32 = jnp.float32
BF16 = jnp.bfloat16

D_MODEL = 1024
BATCH = 16
SEQ = 256
DEPTH = 2
DEC_BATCH = 4
DEC_SEQ = 1024
PAST_LEN = 256
GRID_W = 64
D_FF = 2816
N_MODS = 9
CONV_DIM = 512
CONV_WIDTH = 31
RET_HEADS = 4
RET_DK = 128
RET_DV = 256
MLA_HEADS = 8
MLA_Q_LORA = 512
MLA_KV_LORA = 256
MLA_D_NOPE = 64
MLA_D_ROPE = 32
MLA_D_V = 64
ROPE_AXIS_HALF = MLA_D_ROPE // 4
ROPE_BASE = 10000.0
DEEPNORM_ALPHA = (2 * DEPTH) ** 0.25
LN_EPS = 1e-5
RMS_EPS = 1e-6

N_CTX = BATCH * SEQ
N_LAT = DEC_BATCH * DEC_SEQ
N_TOK = N_CTX + N_LAT
N_MOD_ROWS = 8
HEAD_PAD = 128
ROPE_LANE0 = MLA_D_NOPE
MLA_W = MLA_HEADS * HEAD_PAD
MLA_V_W = MLA_HEADS * MLA_D_V
MAIN_W = 2 * CONV_DIM + 2 * RET_HEADS * RET_DK + 2 * RET_HEADS * RET_DV + MLA_Q_LORA + MLA_KV_LORA
N_BRANCHES = 3
GATE_W = N_BRANCHES * D_MODEL
SLAB_COLS = {}
for _name, _w in (("sig", GATE_W), ("rq", RET_HEADS * RET_DK), ("rk", RET_HEADS * RET_DK), ("rv", RET_HEADS * RET_DV),
                  ("rg", RET_HEADS * RET_DV), ("q", MLA_W), ("kcat", MLA_W)):
    SLAB_COLS[_name] = (sum(w for _, w in SLAB_COLS.values()), _w)
SLAB_W = sum(w for _, w in SLAB_COLS.values())
SUBLANES = 8
CONV_HALO = 16
CONV_BLOCK = 256
VMEM_LIMIT = 56 * 1024 * 1024
N_WSTEPS = 8
W_COL_PARTS = 2
PROJ_WSTEPS = 6
MXU_COLS = 256
FFN_CHUNKS = ((0, 6 * MXU_COLS), (6 * MXU_COLS, D_FF))
ROW_HALVES = 2
ATTN_QBLOCK = 256
ATTN_QSCALE = (MLA_D_NOPE + MLA_D_ROPE) ** -0.5 * math.log2(math.e)


def _dot(a, b):
    return jnp.dot(a, b, preferred_element_type=F32)


def _dot_nt(a, b):
    return lax.dot_general(a, b, (((1,), (1,)), ((), ())), preferred_element_type=F32)


def _dot_tn(a, b):
    return lax.dot_general(a, b, (((0,), (0,)), ((), ())), preferred_element_type=F32)


def _sigmoid(x):
    return 1.0 / (1.0 + jnp.exp(-x))


def _norm_rows(z):
    mu = jnp.mean(z, axis=-1, keepdims=True)
    zc = z - mu
    var = jnp.mean(zc * zc, axis=-1, keepdims=True)
    return zc * lax.rsqrt(var + LN_EPS)


def _rms_rows(z):
    return z * lax.rsqrt(jnp.mean(z * z, axis=-1, keepdims=True) + RMS_EPS)


def _resident(shape):
    zeros = (0,) * len(shape)
    return pl.BlockSpec(shape, lambda *_: zeros, pipeline_mode=pl.Buffered(1))


def _layer_resident(w, layer):
    zeros = (0,) * (w.ndim - 1)
    return pl.BlockSpec((None,) + w.shape[1:], lambda *_: (layer,) + zeros, pipeline_mode=pl.Buffered(1))


def _tile(i, n_w, tm):
    return jnp.clip(i - n_w, 0, N_TOK // tm - 1)


def _row_spec(tm, w, n_w):
    return pl.BlockSpec((tm, w), lambda i: (_tile(i, n_w, tm), 0))


def _ctx_row_spec(tm, w, n_w):
    last = N_CTX // tm - 1
    return pl.BlockSpec((tm, w), lambda i: (jnp.minimum(_tile(i, n_w, tm), last), 0))


def _lat_row_spec(tm, w, n_w):
    first = N_CTX // tm
    return pl.BlockSpec((tm, w), lambda i: (jnp.maximum(_tile(i, n_w, tm) - first, 0), 0))


def _mod_spec(tm, layer, n_w):
    n_ctx_tiles = N_CTX // tm
    tiles_per_seq = DEC_SEQ // tm

    def index(i):
        j = _tile(i, n_w, tm)
        return (layer, jnp.where(j < n_ctx_tiles, 0, 1 + (j - n_ctx_tiles) // tiles_per_seq), 0, 0)

    return pl.BlockSpec((None, None, N_MODS, D_MODEL), index)


def _wchunk_spec(w, layer, n_w=N_WSTEPS, col_parts=1, part=0):
    _, rows, cols = w.shape
    return pl.BlockSpec((None, rows // n_w, cols // col_parts), lambda i: (layer, jnp.minimum(i, n_w - 1), part))


def _layer_row_spec(w, index):
    return pl.BlockSpec((None, 1, w.shape[-1]), lambda i: (index, 0, 0), pipeline_mode=pl.Buffered(1))


def _stage_chunk(i, src_ref, dst_ref, part=0):
    rows, cols = src_ref.shape
    dst_ref[pl.ds(pl.multiple_of(i * rows, rows), rows), part * cols:(part + 1) * cols] = src_ref[...].astype(BF16)


def _params(semantics):
    return pltpu.CompilerParams(dimension_semantics=semantics, vmem_limit_bytes=VMEM_LIMIT)


def _ada_kernel(c_ref, w_ref, b_ref, o_ref):
    c = c_ref[...]
    h = (c * _sigmoid(c)).astype(BF16)
    o_ref[...] = _dot(h, w_ref[...].astype(BF16)) + b_ref[...]


def _ada_mods(cvec, ada_w, ada_b):
    tn = D_MODEL
    n_out = N_MODS * D_MODEL
    return pl.pallas_call(
        _ada_kernel,
        grid=(DEPTH, n_out // tn),
        in_specs=[
            pl.BlockSpec((N_MOD_ROWS, D_MODEL), lambda l, j: (0, 0)),
            pl.BlockSpec((None, D_MODEL, tn), lambda l, j: (l, 0, j)),
            pl.BlockSpec((None, 1, tn), lambda l, j: (l, 0, j)),
        ],
        out_specs=pl.BlockSpec((None, N_MOD_ROWS, tn), lambda l, j: (l, 0, j)),
        out_shape=jax.ShapeDtypeStruct((DEPTH, N_MOD_ROWS, n_out), F32),
        compiler_params=_params(("arbitrary", "arbitrary")),
        name="ada_mods",
    )(cvec, ada_w, ada_b.reshape(DEPTH, 1, n_out))


def _ffn_kernel(*refs, base, tm, n_x, n_out):
    x_refs = refs[:n_x]
    mod_ref = refs[n_x]
    win_refs = refs[n_x + 1:n_x + 1 + W_COL_PARTS]
    wout_refs = refs[n_x + 1 + W_COL_PARTS:n_x + 1 + 2 * W_COL_PARTS]
    refs = refs[n_x + 1 + 2 * W_COL_PARTS:]
    g_ref, b_ref = refs[:2]
    o_refs = refs[2:2 + n_out]
    win_bf, wout_bf = refs[2 + n_out:]
    i = pl.program_id(0)
    is_ctx = i < N_WSTEPS + N_CTX // tm

    @pl.when(i < N_WSTEPS)
    def _():
        for part in range(W_COL_PARTS):
            _stage_chunk(i, win_refs[part], win_bf, part)
            _stage_chunk(i, wout_refs[part], wout_bf, part)

    @pl.when(i >= N_WSTEPS)
    def _():
        shift = mod_ref[base:base + 1, :]
        scale = mod_ref[base + 1:base + 2, :]
        gate = mod_ref[base + 2:base + 3, :]
        halves = [slice(k * tm // ROW_HALVES, (k + 1) * tm // ROW_HALVES) for k in range(ROW_HALVES)]
        results = []
        for rows in halves:
            if n_x == 2:
                x = jnp.where(is_ctx, x_refs[0][rows, :], x_refs[1][rows, :])
            else:
                x = x_refs[0][rows, :]
            h = (x * (1.0 + scale) + shift).astype(BF16)
            y = None
            for lo, hi in FFN_CHUNKS:
                g = _dot(h, win_bf[:, lo:hi])
                u = _dot(h, win_bf[:, D_FF + lo:D_FF + hi])
                a = (g * _sigmoid(g) * u).astype(BF16)
                yc = _dot(a, wout_bf[lo:hi, :])
                y = yc if y is None else y + yc
            z = DEEPNORM_ALPHA * x + 0.5 * gate * y
            res = _norm_rows(z) * g_ref[...] + b_ref[...]
            if n_out == 1:
                o_refs[0][rows, :] = res
            results.append(res)
        if n_out == 2:
            @pl.when(is_ctx)
            def _():
                for rows, res in zip(halves, results):
                    o_refs[0][rows, :] = res

            @pl.when(jnp.logical_not(is_ctx))
            def _():
                for rows, res in zip(halves, results):
                    o_refs[1][rows, :] = res


def _ffn(xs, mods_all, w_in, w_out, ln_g, ln_b, *, layer, which, split_out=False, tm=512):
    row = _row_spec(tm, D_MODEL, N_WSTEPS)
    pair = [_ctx_row_spec(tm, D_MODEL, N_WSTEPS), _lat_row_spec(tm, D_MODEL, N_WSTEPS)]
    ln_index = layer * 3 + which
    if split_out:
        out_specs = pair
        out_shape = [jax.ShapeDtypeStruct((N_CTX, D_MODEL), F32), jax.ShapeDtypeStruct((N_LAT, D_MODEL), F32)]
    else:
        out_specs = [row]
        out_shape = [jax.ShapeDtypeStruct((N_TOK, D_MODEL), F32)]
    return pl.pallas_call(
        functools.partial(_ffn_kernel, base=3 * which, tm=tm, n_x=len(xs), n_out=len(out_specs)),
        grid=(N_WSTEPS + N_TOK // tm,),
        in_specs=(pair if len(xs) == 2 else [row]) + [
            _mod_spec(tm, layer, N_WSTEPS)]
        + [_wchunk_spec(w_in, layer, col_parts=W_COL_PARTS, part=p) for p in range(W_COL_PARTS)]
        + [_wchunk_spec(w_out, layer, col_parts=W_COL_PARTS, part=p) for p in range(W_COL_PARTS)]
        + [_layer_row_spec(ln_g, ln_index), _layer_row_spec(ln_b, ln_index)],
        out_specs=out_specs,
        out_shape=out_shape,
        scratch_shapes=[pltpu.VMEM((D_MODEL, 2 * D_FF), BF16), pltpu.VMEM((D_FF, D_MODEL), BF16)],
        compiler_params=_params(("arbitrary",)),
        name="ffn",
    )(*xs, mods_all, *([w_in] * W_COL_PARTS), *([w_out] * W_COL_PARTS), ln_g, ln_b)


def _conv_stages(prev_halo, main, next_halo, w_ref, b_ref, g_ref, beta_ref, o_ref, pad_ref, shift_ref, acc_ref):
    rows = 32
    lanes = 128
    first = CONV_HALO - CONV_WIDTH // 2

    def setup(after):
        del after
        pad_ref[0:CONV_HALO, :] = prev_halo()
        pad_ref[CONV_HALO:CONV_HALO + CONV_BLOCK, :] = main()
        pad_ref[CONV_HALO + CONV_BLOCK:, :] = next_halo()
        span = shift_ref.shape[1]
        for ph in range(SUBLANES):
            shift_ref[ph] = pad_ref[ph:ph + span, :]

    def taps(c, r, start_from):
        cs = slice(c * lanes, (c + 1) * lanes)
        acc = jnp.broadcast_to(b_ref[:, cs], (rows, lanes)) + start_from
        for j in range(CONV_WIDTH):
            ph = (first + j) % SUBLANES
            start = r * rows + (first + j) - ph
            acc = acc + w_ref[j:j + 1, cs] * shift_ref[ph, start:start + rows, cs]
        acc_ref[r * rows:(r + 1) * rows, cs] = acc

    def finish(after):
        del after
        y = _norm_rows(acc_ref[...]) * g_ref[...] + beta_ref[...]
        o_ref[...] = (y * _sigmoid(y)).astype(BF16)

    def lane_group_half(c, half, after):
        if after is None:
            start_from = jnp.zeros((rows, lanes), F32)
        else:
            bits = lax.bitcast_convert_type(after, jnp.int32)
            zero = lax.shift_right_logical(lax.shift_right_logical(bits, 16), 16).astype(F32)
            start_from = jnp.concatenate([zero] * (rows // SUBLANES), axis=0)
        per_half = CONV_BLOCK // rows // 2
        for r in range(half * per_half, (half + 1) * per_half):
            taps(c, r, start_from)

    chunks = [functools.partial(lane_group_half, c, half) for c in range(CONV_DIM // lanes) for half in range(2)]
    return [setup] + chunks + [finish]


def _proj_kernel(*refs, tm, carry):
    (x_ref, mod_ref, cos_ref, sin_lo_ref, sin_hi_ref, wt_ref, wuq_ref, wk_ref, wvt_ref, gq_ref,
     gkv_ref, cw_ref, cb_ref, cg_ref, cbeta_ref) = refs[:15]
    refs = refs[15:]
    if carry:
        ckv_prev_ref, kr_prev_ref = refs[:2]
        refs = refs[2:]
    (a_ref, slab_ref, vt_ref, ckv_ref, kr_ref, glu_ring, pad_ref, shift_ref, acc_ref, wt_bf) = refs
    sig_ref, rq_ref, rk_ref, rv_ref, rg_ref, q_ref, kcat_ref = (
        slab_ref.at[:, lo:lo + w] for lo, w in (SLAB_COLS[n] for n in ("sig", "rq", "rk", "rv", "rg", "q", "kcat")))
    t = pl.program_id(0) - PROJ_WSTEPS
    n_tiles = N_TOK // tm
    n_ctx_tiles = N_CTX // tm
    tiles_per_seq = DEC_SEQ // tm
    is_ctx = t < n_ctx_tiles
    pos = (t - 1 - n_ctx_tiles) % tiles_per_seq
    conv_latent = t - 1 >= n_ctx_tiles
    has_prev = jnp.logical_and(conv_latent, pos != 0)
    has_next = jnp.logical_and(conv_latent, pos != tiles_per_seq - 1)
    slot = t % 2
    no_halo = jnp.zeros((CONV_HALO, CONV_DIM), F32)

    def conv_prev_tile(next_rows):
        return _conv_stages(
            lambda: jnp.where(has_prev, glu_ring[slot, tm - CONV_HALO:tm, :], no_halo),
            lambda: glu_ring[1 - slot],
            lambda: jnp.where(has_next, next_rows, no_halo),
            cw_ref, cb_ref, cg_ref, cbeta_ref, a_ref, pad_ref, shift_ref, acc_ref)

    @pl.when(t < 0)
    def _():
        _stage_chunk(pl.program_id(0), wt_ref, wt_bf)

    @pl.when(t == 0)
    def _():
        glu_ring[...] = jnp.zeros(glu_ring.shape, F32)

    @pl.when(t == n_tiles)
    def _():
        for step in conv_prev_tile(no_halo):
            step(None)

    @pl.when(jnp.logical_and(t >= 0, t < n_tiles))
    def _():
        _proj_tile(x_ref, mod_ref, cos_ref, sin_lo_ref, sin_hi_ref, wt_bf, wuq_ref, wk_ref, wvt_ref, gq_ref,
                   gkv_ref, ckv_prev_ref if carry else None, kr_prev_ref if carry else None,
                   rq_ref, rk_ref, rv_ref, rg_ref, q_ref, kcat_ref, vt_ref, sig_ref, ckv_ref, kr_ref,
                   glu_ring, conv_prev_tile, slot, is_ctx, tm)


def _proj_tile(x_ref, mod_ref, cos_ref, sin_lo_ref, sin_hi_ref, wt_ref, wuq_ref, wk_ref, wvt_ref, gq_ref,
               gkv_ref, ckv_prev_ref, kr_prev_ref,
               rq_ref, rk_ref, rv_ref, rg_ref, q_ref, kcat_ref, vt_ref, sig_ref, ckv_ref, kr_ref,
               glu_ring, conv_prev_tile, slot, is_ctx, tm):
    carry = ckv_prev_ref is not None
    x = x_ref[...]
    u = (x * (1.0 + mod_ref[4:5, :]) + mod_ref[3:4, :]).astype(BF16)
    widths = (CONV_DIM, CONV_DIM, RET_HEADS * RET_DK, RET_HEADS * RET_DK, RET_HEADS * RET_DV, RET_HEADS * RET_DV,
              MLA_Q_LORA, MLA_KV_LORA)
    starts = [sum(widths[:n]) for n in range(len(widths))]

    def proj(n):
        return _dot_nt(u, wt_ref[starts[n]:starts[n] + widths[n], :])

    glu = proj(0) * _sigmoid(proj(1))
    conv_steps = conv_prev_tile(glu[0:CONV_HALO, :])

    def conv(n, result):
        for _ in range(n):
            conv_steps.pop(0)(result[0:SUBLANES, 0:128])

    conv(1, glu)
    glu_ring[slot] = glu

    mq = proj(6)
    mkv = proj(7)
    conv(2, mq)
    cos = cos_ref[...]
    sin_lo = sin_lo_ref[...]
    sin_hi = sin_hi_ref[...]

    def rotary(v):
        up = pltpu.roll(v, HEAD_PAD - ROPE_AXIS_HALF, 1)
        down = pltpu.roll(v, ROPE_AXIS_HALF, 1)
        return v * cos + up * sin_lo + down * sin_hi

    qn = (_rms_rows(mq) * gq_ref[...]).astype(BF16)
    qm = _dot(qn, wuq_ref[...])
    ckv = _rms_rows(mkv) * gkv_ref[...]
    ckvb = ckv.astype(BF16)
    kn = _dot(ckvb, wk_ref[...])
    vt_ref[...] = _dot_nt(wvt_ref[...], ckvb).astype(BF16)
    kr_grp = _dot_nt(u, wt_ref[MAIN_W:MAIN_W + HEAD_PAD, :])
    conv(2, kn)
    lane = lax.broadcasted_iota(jnp.int32, kr_grp.shape, 1)
    in_rope = jnp.logical_and(lane >= ROPE_LANE0, lane < ROPE_LANE0 + MLA_D_ROPE)
    kr = jnp.where(in_rope, pltpu.roll(kr_grp, ROPE_LANE0, 1), 0.0)
    kr_rot = rotary(kr)
    for h in range(MLA_HEADS):
        sl = slice(h * HEAD_PAD, (h + 1) * HEAD_PAD)
        q_ref[:, sl] = (rotary(qm[:, sl]) * ATTN_QSCALE).astype(BF16)
        kcat_ref[:, sl] = (kn[:, sl] + kr_rot).astype(BF16)

    gate0 = MAIN_W + MLA_D_ROPE

    def branch_gate(blk):
        cols = slice(blk * D_MODEL, (blk + 1) * D_MODEL)
        gate = _dot_nt(u, wt_ref[gate0 + cols.start:gate0 + cols.stop, :])
        sig_ref[:, cols] = _sigmoid(gate).astype(BF16)
        return gate

    for blk in range(N_BRANCHES):
        conv(1, branch_gate(blk))
    rg = proj(5)
    rg_ref[...] = (rg * _sigmoid(rg)).astype(BF16)
    conv(2, rg)
    rv_ref[...] = proj(4).astype(BF16)
    rk_ref[...] = (proj(3) * (RET_DK ** -0.5)).astype(BF16)
    rq_ref[...] = proj(2).astype(BF16)

    @pl.when(is_ctx)
    def _():
        seqs = tm // SEQ
        ckv3 = ckv.reshape(seqs, SEQ, MLA_KV_LORA)
        kr3 = kr_grp[:, :MLA_D_ROPE].reshape(seqs, SEQ, MLA_D_ROPE)
        if carry:
            ckv_ref[:, 0] = ckv_prev_ref[...]
            kr_ref[:, 0] = kr_prev_ref[...]
            ckv_ref[:, 1] = ckv3
            kr_ref[:, 1] = kr3
        else:
            ckv_ref[...] = ckv3
            kr_ref[...] = kr3


def _proj(x, mods_all, rope_tabs, wt_all, wuq, wk, wvt, gq, gkv, conv_w, carry, *, layer):
    tm = CONV_BLOCK
    n_w = PROJ_WSTEPS
    n_tiles = N_TOK // tm
    n_ctx_tiles = N_CTX // tm
    tiles_per_seq = DEC_SEQ // tm
    seqs = tm // SEQ

    def rope_index(i):
        j = _tile(i, n_w, tm)
        return (jnp.where(j < n_ctx_tiles, 0, tiles_per_seq + (j - n_ctx_tiles) % tiles_per_seq), 0)

    def row(w):
        return _row_spec(tm, w, n_w)

    def out(w, dt):
        return jax.ShapeDtypeStruct((N_TOK, w), dt)

    def ctx_seq_spec(*tail):
        zeros = (0,) * len(tail)
        return pl.BlockSpec((seqs,) + tail, lambda i: (jnp.minimum(_tile(i, n_w, tm), n_ctx_tiles - 1),) + zeros)

    rope = pl.BlockSpec((tm, HEAD_PAD), rope_index)
    in_specs = [row(D_MODEL), _mod_spec(tm, layer, n_w), rope, rope, rope,
                _wchunk_spec(wt_all, layer, n_w)] + [_layer_resident(w, layer) for w in (wuq, wk, wvt, gq, gkv, *conv_w)]
    args = [x, mods_all, *rope_tabs, wt_all, wuq, wk, wvt, gq, gkv, *conv_w]
    if carry is None:
        ctx_specs = [ctx_seq_spec(SEQ, MLA_KV_LORA), ctx_seq_spec(SEQ, MLA_D_ROPE)]
        ctx_shapes = [jax.ShapeDtypeStruct((BATCH, SEQ, MLA_KV_LORA), F32),
                      jax.ShapeDtypeStruct((BATCH, SEQ, MLA_D_ROPE), F32)]
    else:
        in_specs += [ctx_seq_spec(SEQ, MLA_KV_LORA), ctx_seq_spec(SEQ, MLA_D_ROPE)]
        args += list(carry)
        ctx_specs = [ctx_seq_spec(DEPTH, SEQ, MLA_KV_LORA), ctx_seq_spec(DEPTH, SEQ, MLA_D_ROPE)]
        ctx_shapes = [jax.ShapeDtypeStruct((BATCH, DEPTH, SEQ, MLA_KV_LORA), F32),
                      jax.ShapeDtypeStruct((BATCH, DEPTH, SEQ, MLA_D_ROPE), F32)]
    span = CONV_BLOCK + 2 * CONV_HALO - SUBLANES
    return pl.pallas_call(
        functools.partial(_proj_kernel, tm=tm, carry=carry is not None),
        grid=(n_w + n_tiles + 1,),
        in_specs=in_specs,
        out_specs=[pl.BlockSpec((tm, CONV_DIM), lambda i: (_tile(i, n_w + 1, tm), 0)), row(SLAB_W),
                   pl.BlockSpec((MLA_V_W, tm), lambda i: (0, _tile(i, n_w, tm)))] + ctx_specs,
        out_shape=[out(CONV_DIM, BF16), out(SLAB_W, BF16),
                   jax.ShapeDtypeStruct((MLA_V_W, N_TOK), BF16)] + ctx_shapes,
        scratch_shapes=[pltpu.VMEM((2, tm, CONV_DIM), F32),
                        pltpu.VMEM((CONV_BLOCK + 2 * CONV_HALO, CONV_DIM), F32),
                        pltpu.VMEM((SUBLANES, span, CONV_DIM), F32),
                        pltpu.VMEM((CONV_BLOCK, CONV_DIM), F32),
                        pltpu.VMEM(wt_all.shape[1:], BF16)],
        compiler_params=_params(("arbitrary",)),
        name="mix_proj",
    )(*args)


def _cache_kv_kernel(ckv_ref, kr_ref, wk_ref, wvt_ref, kcat_ref, vt_ref):
    ckvb = ckv_ref[...].astype(BF16)
    kn = _dot(ckvb, wk_ref[...])
    vt_ref[...] = _dot_nt(wvt_ref[...], ckvb).astype(BF16)
    kr = kr_ref[...]
    for h in range(MLA_HEADS):
        sl = slice(h * HEAD_PAD, (h + 1) * HEAD_PAD)
        kcat_ref[:, sl] = (kn[:, sl] + kr).astype(BF16)


def _cache_kv(cache_ckv, cache_kr_pad, wk, wvt):
    n = DEC_BATCH * PAST_LEN
    return pl.pallas_call(
        _cache_kv_kernel,
        grid=(DEPTH, DEC_BATCH),
        in_specs=[pl.BlockSpec((None, None, PAST_LEN, MLA_KV_LORA), lambda l, b: (b, l, 0, 0)),
                  pl.BlockSpec((None, None, PAST_LEN, HEAD_PAD), lambda l, b: (b, l, 0, 0)),
                  pl.BlockSpec((None, MLA_KV_LORA, MLA_W), lambda l, b: (l, 0, 0)),
                  pl.BlockSpec((None, MLA_V_W, MLA_KV_LORA), lambda l, b: (l, 0, 0))],
        out_specs=[pl.BlockSpec((None, PAST_LEN, MLA_W), lambda l, b: (l, b, 0)),
                   pl.BlockSpec((None, MLA_V_W, PAST_LEN), lambda l, b: (l, 0, b))],
        out_shape=[jax.ShapeDtypeStruct((DEPTH, n, MLA_W), BF16), jax.ShapeDtypeStruct((DEPTH, MLA_V_W, n), BF16)],
        compiler_params=_params(("arbitrary", "arbitrary")),
        name="cache_kv",
    )(cache_ckv, cache_kr_pad, wk, wvt)


def _ret_kernel(*refs, t, hp, latent, carry, layer):
    if latent:
        lg_ref, q_ref, k_ref, v_ref, g_ref, s0f_ref, s0b_ref, o_ref, d_ref = refs
    elif carry:
        lg_ref, q_ref, k_ref, v_ref, g_ref, sf_prev_ref, sb_prev_ref, o_ref, sf_ref, sb_ref, d_ref = refs
    else:
        lg_ref, q_ref, k_ref, v_ref, g_ref, o_ref, sf_ref, sb_ref, d_ref = refs
    hblk = pl.program_id(0)

    @pl.when(pl.program_id(1) == 0)
    def _():
        diff = (lax.broadcasted_iota(jnp.int32, (t, t), 0) - lax.broadcasted_iota(jnp.int32, (t, t), 1)).astype(F32)
        for hh in range(hp):
            lgf = lg_ref[layer, 0, hblk * hp + hh]
            lgb = lg_ref[layer, 1, hblk * hp + hh]
            d_ref[hh] = jnp.exp(jnp.where(diff >= 0, diff * lgf, -diff * lgb))

    if carry:
        sf_ref[0] = sf_prev_ref[...]
        sb_ref[0] = sb_prev_ref[...]
    pos = lax.broadcasted_iota(jnp.int32, (t, 1), 0).astype(F32)
    for hh in range(hp):
        lgf = lg_ref[layer, 0, hblk * hp + hh]
        lgb = lg_ref[layer, 1, hblk * hp + hh]
        q = q_ref[:, hh * RET_DK:(hh + 1) * RET_DK]
        k = k_ref[:, hh * RET_DK:(hh + 1) * RET_DK]
        v = v_ref[:, hh * RET_DV:(hh + 1) * RET_DV]
        p = (_dot_nt(q, k) * d_ref[hh]).astype(BF16)
        o = _dot(p, v)
        if latent:
            o = o + jnp.exp((pos + 1.0) * lgf) * _dot(q, s0f_ref[hh].astype(BF16))
            o = o + jnp.exp((t - pos) * lgb) * _dot(q, s0b_ref[hh].astype(BF16))
        else:
            kf = k.astype(F32)
            sf = _dot_tn((kf * jnp.exp((t - 1.0 - pos) * lgf)).astype(BF16), v)
            sb = _dot_tn((kf * jnp.exp(pos * lgb)).astype(BF16), v)
            if carry:
                sf_ref[1, hh] = sf
                sb_ref[1, hh] = sb
            else:
                sf_ref[hh] = sf
                sb_ref[hh] = sb
        o_ref[:, hh * RET_DV:(hh + 1) * RET_DV] = (g_ref[:, hh * RET_DV:(hh + 1) * RET_DV] * _norm_rows(o)).astype(BF16)


def _retention(log_g, slab, states, *, latent, layer):
    t = DEC_SEQ if latent else SEQ
    hp = 2 if latent else RET_HEADS
    n_seq = DEC_BATCH if latent else BATCH
    row0 = (N_CTX // t) if latent else 0

    def row(name, w):
        assert SLAB_COLS[name][0] % (hp * w) == 0
        first = SLAB_COLS[name][0] // (hp * w)
        return pl.BlockSpec((t, hp * w), lambda h, s: (row0 + s, first + h))

    smem = pl.BlockSpec(memory_space=pltpu.SMEM)
    out_shape = [jax.ShapeDtypeStruct((n_seq * t, RET_HEADS * RET_DV), BF16)]
    out_specs = [pl.BlockSpec((t, hp * RET_DV), lambda h, s: (s, h))]
    in_specs = [smem, row("rq", RET_DK), row("rk", RET_DK), row("rv", RET_DV), row("rg", RET_DV)]
    args = [log_g, slab, slab, slab, slab]
    carry = False
    if latent:
        st = pl.BlockSpec((None, None, hp, RET_DK, RET_DV), lambda h, s: (s, layer, h, 0, 0))
        in_specs += [st, st]
        args += list(states)
    else:
        st = pl.BlockSpec((None, hp, RET_DK, RET_DV), lambda h, s: (s, h, 0, 0))
        if states is None:
            out_specs += [st, st]
            out_shape += [jax.ShapeDtypeStruct((BATCH, RET_HEADS, RET_DK, RET_DV), F32)] * 2
        else:
            carry = True
            in_specs += [st, st]
            args += list(states)
            st2 = pl.BlockSpec((None, DEPTH, hp, RET_DK, RET_DV), lambda h, s: (s, 0, h, 0, 0))
            out_specs += [st2, st2]
            out_shape += [jax.ShapeDtypeStruct((BATCH, DEPTH, RET_HEADS, RET_DK, RET_DV), F32)] * 2

    return pl.pallas_call(
        functools.partial(_ret_kernel, t=t, hp=hp, latent=latent, carry=carry, layer=layer),
        grid=(RET_HEADS // hp, n_seq),
        in_specs=in_specs,
        out_specs=out_specs,
        out_shape=out_shape,
        scratch_shapes=[pltpu.VMEM((hp, t, t), F32)],
        compiler_params=_params(("arbitrary", "arbitrary")),
        name="retention_lat" if latent else "retention_ctx",
    )(*args)


def _attn_kernel(*refs, t, hp, latent):
    if latent:
        q_ref, k_ref, vt_ref, kc_ref, vtc_ref, o_ref = refs
    else:
        q_ref, k_ref, vt_ref, o_ref = refs
    qb = min(ATTN_QBLOCK, t)
    units = [(slice(hh * HEAD_PAD, (hh + 1) * HEAD_PAD), slice(hh * MLA_D_V, (hh + 1) * MLA_D_V),
              slice(b * qb, (b + 1) * qb)) for b in range(t // qb) for hh in range(hp)]
    def scores(unit):
        sl, _, rows = unit
        q = q_ref[rows, sl]
        s = [_dot_nt(k_ref[:, sl], q)]
        if latent:
            s.append(_dot_nt(kc_ref[:, sl], q))
        return s

    def softmax(s):
        m = functools.reduce(jnp.maximum, [jnp.max(x, axis=0, keepdims=True) for x in s])
        e = [jnp.exp2(x - m) for x in s]
        den = functools.reduce(jnp.add, [jnp.sum(x, axis=0, keepdims=True) for x in e])
        return [x.astype(BF16) for x in e], den

    def values(unit, e, den):
        _, vs, _ = unit
        o = _dot(vt_ref[vs, :], e[0])
        if latent:
            o = o + _dot(vtc_ref[vs, :], e[1])
        return o / den

    pairs = [units[u:u + 2] for u in range(0, len(units), 2)]
    s_next = [scores(u) for u in pairs[0]]
    sm_prev = None
    for g in range(len(pairs) + 1):
        s_cur = s_next
        if g + 1 < len(pairs):
            s_next = [scores(u) for u in pairs[g + 1]]
        sm_cur = [softmax(s) for s in s_cur] if g < len(pairs) else None
        if sm_prev is not None:
            (_, vs0, rows), (_, vs1, _) = pairs[g - 1]
            outs = [values(u, e, den) for u, (e, den) in zip(pairs[g - 1], sm_prev)]
            o_ref[rows, vs0.start:vs1.stop] = jnp.concatenate(outs, axis=0).T.astype(BF16)
        sm_prev = sm_cur


def _attention(slab, vt, cache, *, latent, layer):
    t = DEC_SEQ if latent else SEQ
    hp = 4 if latent else MLA_HEADS
    n_seq = DEC_BATCH if latent else BATCH
    row0 = (N_CTX // t) if latent else 0

    def row(name):
        assert SLAB_COLS[name][0] % (hp * HEAD_PAD) == 0
        first = SLAB_COLS[name][0] // (hp * HEAD_PAD)
        return pl.BlockSpec((t, hp * HEAD_PAD), lambda s, h: (row0 + s, first + h))

    col = pl.BlockSpec((hp * MLA_D_V, t), lambda s, h: (h, row0 + s))
    in_specs = [row("q"), row("kcat"), col]
    args = [slab, slab, vt]
    if latent:
        in_specs += [pl.BlockSpec((None, PAST_LEN, hp * HEAD_PAD), lambda s, h: (layer, s, h)),
                     pl.BlockSpec((None, hp * MLA_D_V, PAST_LEN), lambda s, h: (layer, h, s))]
        args += list(cache)
    return pl.pallas_call(
        functools.partial(_attn_kernel, t=t, hp=hp, latent=latent),
        grid=(n_seq, MLA_HEADS // hp),
        in_specs=in_specs,
        out_specs=pl.BlockSpec((t, hp * MLA_D_V), lambda s, h: (s, h)),
        out_shape=jax.ShapeDtypeStruct((n_seq * t, MLA_V_W), BF16),
        compiler_params=_params(("arbitrary", "arbitrary")),
        name="attention_lat" if latent else "attention_ctx",
    )(*args)


def _merge_kernel(x_ref, mod_ref, a_ref, rc_ref, rl_ref, mc_ref, ml_ref, sig_ref, wc_ref, wr_ref, wm_ref, wo_ref,
                  g_ref, b_ref, o_ref, wc_bf, wr_bf, wm_bf, wo_bf, *, tm):
    i = pl.program_id(0)
    is_ctx = i < N_WSTEPS + N_CTX // tm

    @pl.when(i < N_WSTEPS)
    def _():
        _stage_chunk(i, wc_ref, wc_bf)
        _stage_chunk(i, wr_ref, wr_bf)
        _stage_chunk(i, wo_ref, wo_bf)
        _stage_chunk(i, wm_ref, wm_bf)

    @pl.when(i >= N_WSTEPS)
    def _():
        for half in range(ROW_HALVES):
            rows = slice(half * tm // ROW_HALVES, (half + 1) * tm // ROW_HALVES)
            x = x_ref[rows, :]
            r = jnp.where(is_ctx, rc_ref[rows, :], rl_ref[rows, :])
            m = jnp.where(is_ctx, mc_ref[rows, :], ml_ref[rows, :])
            merged = sig_ref[rows, 0:D_MODEL] * _dot(a_ref[rows, :], wc_bf[...])
            merged = merged + sig_ref[rows, D_MODEL:2 * D_MODEL] * _dot(r, wr_bf[...])
            merged = merged + sig_ref[rows, 2 * D_MODEL:] * _dot(m, wm_bf[...])
            y = _dot(merged.astype(BF16), wo_bf[...])
            z = DEEPNORM_ALPHA * x + mod_ref[5:6, :] * y
            o_ref[rows, :] = _norm_rows(z) * g_ref[...] + b_ref[...]


def _merge(x, mods_all, a, r_pair, m_pair, sig, wc, wr, wm, wo, ln_g, ln_b, *, layer, tm=512):
    def row(w):
        return _row_spec(tm, w, N_WSTEPS)

    def pair(w):
        return [_ctx_row_spec(tm, w, N_WSTEPS), _lat_row_spec(tm, w, N_WSTEPS)]

    ln_index = layer * 3 + 1
    return pl.pallas_call(
        functools.partial(_merge_kernel, tm=tm),
        grid=(N_WSTEPS + N_TOK // tm,),
        in_specs=[row(D_MODEL), _mod_spec(tm, layer, N_WSTEPS), row(CONV_DIM)] + pair(RET_HEADS * RET_DV)
        + pair(MLA_V_W) + [row(GATE_W), _wchunk_spec(wc, layer), _wchunk_spec(wr, layer), _wchunk_spec(wm, layer),
                         _wchunk_spec(wo, layer), _layer_row_spec(ln_g, ln_index), _layer_row_spec(ln_b, ln_index)],
        out_specs=row(D_MODEL),
        out_shape=jax.ShapeDtypeStruct((N_TOK, D_MODEL), F32),
        scratch_shapes=[pltpu.VMEM((CONV_DIM, D_MODEL), BF16), pltpu.VMEM((RET_HEADS * RET_DV, D_MODEL), BF16),
                        pltpu.VMEM((MLA_V_W, D_MODEL), BF16), pltpu.VMEM((D_MODEL, D_MODEL), BF16)],
        compiler_params=_params(("arbitrary",)),
        name="merge",
    )(x, mods_all, a, *r_pair, *m_pair, sig, wc, wr, wm, wo, ln_g, ln_b)


def _rope_tables():
    rows = DEC_SEQ // GRID_W
    row_id = jnp.repeat(jnp.arange(rows, dtype=F32), GRID_W)
    col_id = jnp.tile(jnp.arange(GRID_W, dtype=F32), rows)
    inv_freq = ROPE_BASE ** (-jnp.arange(ROPE_AXIS_HALF, dtype=F32) / ROPE_AXIS_HALF)
    ang = jnp.stack([row_id[:, None] * inv_freq, col_id[:, None] * inv_freq], axis=1)
    cos = jnp.cos(ang)
    sin = jnp.sin(ang)
    cos32 = jnp.stack([cos, cos], axis=2).reshape(DEC_SEQ, MLA_D_ROPE)
    zero = jnp.zeros_like(sin)
    sin_lo32 = jnp.stack([-sin, zero], axis=2).reshape(DEC_SEQ, MLA_D_ROPE)
    sin_hi32 = jnp.stack([zero, sin], axis=2).reshape(DEC_SEQ, MLA_D_ROPE)
    tail = HEAD_PAD - ROPE_LANE0 - MLA_D_ROPE
    cos_t = jnp.concatenate([jnp.ones((DEC_SEQ, ROPE_LANE0), F32), cos32, jnp.ones((DEC_SEQ, tail), F32)], axis=1)
    cos_t = jnp.concatenate([jnp.ones((DEC_SEQ, HEAD_PAD), F32), cos_t], axis=0)

    def sin_table(s32):
        return jnp.pad(s32, ((DEC_SEQ, 0), (ROPE_LANE0, tail)))

    return cos_t, sin_table(sin_lo32), sin_table(sin_hi32)


def _head_pad_cols(w, width):
    k = w.shape[0]
    w = w.reshape(k, MLA_HEADS, width)
    return jnp.pad(w, ((0, 0), (0, 0), (0, HEAD_PAD - width))).reshape(k, MLA_W)


def kernel(x_prompt, x_sample, cache_mla_ckv, cache_mla_krope, state_ret_fwd, state_ret_bwd, c, c_ctx, ada_w, ada_b, ffn1_w_in, ffn1_w_out, ffn2_w_in, ffn2_w_out, post_ln_g, post_ln_b, mix_w_in, conv_w_dw, conv_b_dw, conv_ln_g, conv_ln_b, conv_w_out, ret_decay_fwd, ret_decay_bwd, ret_w_out, mla_q_norm, mla_w_uq, mla_kv_norm, mla_w_ukv, mla_w_out, mix_w_o):
    assert DEPTH == 2
    cvec = jnp.concatenate([c_ctx[None, :], c, jnp.zeros((N_MOD_ROWS - 1 - DEC_BATCH, D_MODEL), F32)], axis=0)
    mods_all = _ada_mods(cvec, ada_w, ada_b).reshape(DEPTH, N_MOD_ROWS, N_MODS, D_MODEL)
    rope_tabs = _rope_tables()

    w_ukv = mla_w_ukv.reshape(DEPTH, MLA_KV_LORA, MLA_HEADS, MLA_D_NOPE + MLA_D_V)
    wk_all = _head_pad_cols(w_ukv[..., :MLA_D_NOPE].reshape(DEPTH * MLA_KV_LORA, -1), MLA_D_NOPE)
    wk_all = wk_all.reshape(DEPTH, MLA_KV_LORA, MLA_W).astype(BF16)
    wvt_all = jnp.swapaxes(w_ukv[..., MLA_D_NOPE:].reshape(DEPTH, MLA_KV_LORA, MLA_V_W), 1, 2).astype(BF16)
    kr_tail = HEAD_PAD - ROPE_LANE0 - MLA_D_ROPE
    cache_kr_pad = jnp.pad(cache_mla_krope, ((0, 0), (0, 0), (0, 0), (ROPE_LANE0, kr_tail)))
    kcat_c, vt_c = _cache_kv(cache_mla_ckv, cache_kr_pad, wk_all, wvt_all)
    log_g = jnp.stack([jax.nn.log_sigmoid(ret_decay_fwd), jax.nn.log_sigmoid(ret_decay_bwd)], axis=1)
    wt_all = jnp.swapaxes(mix_w_in, 1, 2)
    wuq_all = _head_pad_cols(mla_w_uq.reshape(DEPTH * MLA_Q_LORA, -1), MLA_D_NOPE + MLA_D_ROPE)
    wuq_all = wuq_all.reshape(DEPTH, MLA_Q_LORA, MLA_W).astype(BF16)
    gq_all = mla_q_norm.reshape(DEPTH, 1, MLA_Q_LORA)
    gkv_all = mla_kv_norm.reshape(DEPTH, 1, MLA_KV_LORA)
    conv_w = (conv_w_dw, conv_b_dw.reshape(DEPTH, 1, CONV_DIM), conv_ln_g.reshape(DEPTH, 1, CONV_DIM),
              conv_ln_b.reshape(DEPTH, 1, CONV_DIM))

    ln_g = post_ln_g.reshape(DEPTH * 3, 1, D_MODEL)
    ln_b = post_ln_b.reshape(DEPTH * 3, 1, D_MODEL)
    xs = (x_prompt.reshape(N_CTX, D_MODEL), x_sample.reshape(N_LAT, D_MODEL))
    ctx_carry = None
    state_carry = None
    for l in range(DEPTH):
        last = l == DEPTH - 1
        x = _ffn(xs, mods_all, ffn1_w_in, ffn1_w_out, ln_g, ln_b, layer=l, which=0)[0]

        a, slab, vt, ckv, kr = _proj(x, mods_all, rope_tabs, wt_all, wuq_all, wk_all, wvt_all, gq_all, gkv_all, conv_w,
                                     ctx_carry, layer=l)
        ctx_carry = (ckv, kr)

        r_ctx, sf, sb = _retention(log_g, slab, state_carry, latent=False, layer=l)
        state_carry = (sf, sb)
        r_lat = _retention(log_g, slab, (state_ret_fwd, state_ret_bwd), latent=True, layer=l)[0]
        m_ctx = _attention(slab, vt, None, latent=False, layer=l)
        m_lat = _attention(slab, vt, (kcat_c, vt_c), latent=True, layer=l)

        x = _merge(x, mods_all, a, (r_ctx, r_lat), (m_ctx, m_lat), slab, conv_w_out, ret_w_out, mla_w_out, mix_w_o,
                   ln_g, ln_b, layer=l)
        xs = _ffn((x,), mods_all, ffn2_w_in, ffn2_w_out, ln_g, ln_b, layer=l, which=2, split_out=last)

    y_ctx, y_lat = xs
    return (y_ctx.reshape(BATCH, SEQ, D_MODEL), y_lat.reshape(DEC_BATCH, DEC_SEQ, D_MODEL),
            ctx_carry[0], ctx_carry[1], state_carry[0], state_carry[1])
```

```python
import functools
import math

import jax
import jax.numpy as jnp
from jax import lax
from jax.experimental import pallas as pl
from jax.experimental.pallas import tpu as pltpu

F32 = jnp.float32
BF16 = jnp.bfloat16

D_MODEL = 1024
BATCH = 16
SEQ = 256
DEPTH = 2
DEC_BATCH = 4
DEC_SEQ = 1024
PAST_LEN = 256
GRID_W = 64
D_FF = 2816
N_MODS = 9
CONV_DIM = 512
CONV_WIDTH = 31
RET_HEADS = 4
RET_DK = 128
RET_DV = 256
MLA_HEADS = 8
MLA_Q_LORA = 512
MLA_KV_LORA = 256
MLA_D_NOPE = 64
MLA_D_ROPE = 32
MLA_D_V = 64
ROPE_AXIS_HALF = MLA_D_ROPE // 4
ROPE_BASE = 10000.0
DEEPNORM_ALPHA = (2 * DEPTH) ** 0.25
LN_EPS = 1e-5
RMS_EPS = 1e-6

N_CTX = BATCH * SEQ
N_LAT = DEC_BATCH * DEC_SEQ
N_TOK = N_CTX + N_LAT
N_MOD_ROWS = 8
HEAD_PAD = 128
ROPE_LANE0 = MLA_D_NOPE
MLA_W = MLA_HEADS * HEAD_PAD
MLA_V_W = MLA_HEADS * MLA_D_V
MAIN_W = 2 * CONV_DIM + 2 * RET_HEADS * RET_DK + 2 * RET_HEADS * RET_DV + MLA_Q_LORA + MLA_KV_LORA
N_BRANCHES = 3
GATE_W = N_BRANCHES * D_MODEL
SLAB_COLS = {}
for _name, _w in (("sig", GATE_W), ("rq", RET_HEADS * RET_DK), ("rk", RET_HEADS * RET_DK), ("rv", RET_HEADS * RET_DV),
                  ("rg", RET_HEADS * RET_DV), ("q", MLA_W), ("kcat", MLA_W)):
    SLAB_COLS[_name] = (sum(w for _, w in SLAB_COLS.values()), _w)
SLAB_W = sum(w for _, w in SLAB_COLS.values())
SUBLANES = 8
CONV_HALO = 16
CONV_BLOCK = 256
VMEM_LIMIT = 56 * 1024 * 1024
N_WSTEPS = 8
PROJ_WSTEPS = 6
MXU_COLS = 256
FFN_CHUNKS = ((0, 6 * MXU_COLS), (6 * MXU_COLS, D_FF))
ROW_GROUP = 256
ATTN_QBLOCK = 256
ATTN_QSCALE = (MLA_D_NOPE + MLA_D_ROPE) ** -0.5 * math.log2(math.e)


def _dot(a, b):
    return jnp.dot(a, b, preferred_element_type=F32)


def _dot_nt(a, b):
    return lax.dot_general(a, b, (((1,), (1,)), ((), ())), preferred_element_type=F32)


def _dot_tn(a, b):
    return lax.dot_general(a, b, (((0,), (0,)), ((), ())), preferred_element_type=F32)


def _sigmoid(x):
    return 1.0 / (1.0 + jnp.exp(-x))


def _norm_rows(z):
    mu = jnp.mean(z, axis=-1, keepdims=True)
    zc = z - mu
    var = jnp.mean(zc * zc, axis=-1, keepdims=True)
    return zc * lax.rsqrt(var + LN_EPS)


def _rms_rows(z):
    return z * lax.rsqrt(jnp.mean(z * z, axis=-1, keepdims=True) + RMS_EPS)


def _resident(shape):
    zeros = (0,) * len(shape)
    return pl.BlockSpec(shape, lambda *_: zeros, pipeline_mode=pl.Buffered(1))


def _layer_resident(w, layer):
    zeros = (0,) * (w.ndim - 1)
    return pl.BlockSpec((None,) + w.shape[1:], lambda *_: (layer,) + zeros, pipeline_mode=pl.Buffered(1))


def _tile(i, n_w, tm):
    return jnp.clip(i - n_w, 0, N_TOK // tm - 1)


def _row_spec(tm, w, n_w):
    return pl.BlockSpec((tm, w), lambda i: (_tile(i, n_w, tm), 0))


def _ctx_row_spec(tm, w, n_w):
    last = N_CTX // tm - 1
    return pl.BlockSpec((tm, w), lambda i: (jnp.minimum(_tile(i, n_w, tm), last), 0))


def _lat_row_spec(tm, w, n_w):
    first = N_CTX // tm
    return pl.BlockSpec((tm, w), lambda i: (jnp.maximum(_tile(i, n_w, tm) - first, 0), 0))


def _mod_spec(tm, layer, n_w):
    n_ctx_tiles = N_CTX // tm
    tiles_per_seq = DEC_SEQ // tm

    def index(i):
        j = _tile(i, n_w, tm)
        return (layer, jnp.where(j < n_ctx_tiles, 0, 1 + (j - n_ctx_tiles) // tiles_per_seq), 0, 0)

    return pl.BlockSpec((None, None, N_MODS, D_MODEL), index)


def _wchunk_spec(w, layer, n_w=N_WSTEPS):
    _, rows, cols = w.shape
    return pl.BlockSpec((None, rows // n_w, cols), lambda i: (layer, jnp.minimum(i, n_w - 1), 0))


def _layer_row_spec(w, index):
    return pl.BlockSpec((None, 1, w.shape[-1]), lambda i: (index, 0, 0), pipeline_mode=pl.Buffered(1))


def _stage_chunk(i, src_ref, dst_ref):
    rows = src_ref.shape[0]
    dst_ref[pl.ds(pl.multiple_of(i * rows, rows), rows), :] = src_ref[...].astype(BF16)


def _params(semantics):
    return pltpu.CompilerParams(dimension_semantics=semantics, vmem_limit_bytes=VMEM_LIMIT)


def _ada_kernel(c_ref, w_ref, b_ref, o_ref):
    c = c_ref[...]
    h = (c * _sigmoid(c)).astype(BF16)
    o_ref[...] = _dot(h, w_ref[...].astype(BF16)) + b_ref[...]


def _ada_mods(cvec, ada_w, ada_b):
    tn = D_MODEL
    n_out = N_MODS * D_MODEL
    return pl.pallas_call(
        _ada_kernel,
        grid=(DEPTH, n_out // tn),
        in_specs=[
            pl.BlockSpec((N_MOD_ROWS, D_MODEL), lambda l, j: (0, 0)),
            pl.BlockSpec((None, D_MODEL, tn), lambda l, j: (l, 0, j)),
            pl.BlockSpec((None, 1, tn), lambda l, j: (l, 0, j)),
        ],
        out_specs=pl.BlockSpec((None, N_MOD_ROWS, tn), lambda l, j: (l, 0, j)),
        out_shape=jax.ShapeDtypeStruct((DEPTH, N_MOD_ROWS, n_out), F32),
        compiler_params=_params(("arbitrary", "arbitrary")),
        name="ada_mods",
    )(cvec, ada_w, ada_b.reshape(DEPTH, 1, n_out))


def _ffn_kernel(*refs, base, tm, n_x, n_out):
    x_refs = refs[:n_x]
    mod_ref, win_ref, wout_ref, g_ref, b_ref = refs[n_x:n_x + 5]
    o_refs = refs[n_x + 5:n_x + 5 + n_out]
    win_bf, wout_bf = refs[n_x + 5 + n_out:]
    i = pl.program_id(0)
    is_ctx = i < N_WSTEPS + N_CTX // tm

    @pl.when(i < N_WSTEPS)
    def _():
        _stage_chunk(i, win_ref, win_bf)
        _stage_chunk(i, wout_ref, wout_bf)

    @pl.when(i >= N_WSTEPS)
    def _():
        shift = mod_ref[base:base + 1, :]
        scale = mod_ref[base + 1:base + 2, :]
        gate = mod_ref[base + 2:base + 3, :]
        halves = [slice(r0, r0 + ROW_GROUP) for r0 in range(0, tm, ROW_GROUP)]
        results = []
        for rows in halves:
            if n_x == 2:
                x = jnp.where(is_ctx, x_refs[0][rows, :], x_refs[1][rows, :])
            else:
                x = x_refs[0][rows, :]
            h = (x * (1.0 + scale) + shift).astype(BF16)
            y = None
            for lo, hi in FFN_CHUNKS:
                g = _dot(h, win_bf[:, lo:hi])
                u = _dot(h, win_bf[:, D_FF + lo:D_FF + hi])
                a = (g * _sigmoid(g) * u).astype(BF16)
                yc = _dot(a, wout_bf[lo:hi, :])
                y = yc if y is None else y + yc
            z = DEEPNORM_ALPHA * x + 0.5 * gate * y
            res = _norm_rows(z) * g_ref[...] + b_ref[...]
            if n_out == 1:
                o_refs[0][rows, :] = res
            results.append(res)
        if n_out == 2:
            @pl.when(is_ctx)
            def _():
                for rows, res in zip(halves, results):
                    o_refs[0][rows, :] = res

            @pl.when(jnp.logical_not(is_ctx))
            def _():
                for rows, res in zip(halves, results):
                    o_refs[1][rows, :] = res


def _ffn(xs, mods_all, w_in, w_out, ln_g, ln_b, *, layer, which, split_out=False):
    tm = 512 if (split_out or len(xs) == 2) else 1024
    row = _row_spec(tm, D_MODEL, N_WSTEPS)
    pair = [_ctx_row_spec(tm, D_MODEL, N_WSTEPS), _lat_row_spec(tm, D_MODEL, N_WSTEPS)]
    ln_index = layer * 3 + which
    if split_out:
        out_specs = pair
        out_shape = [jax.ShapeDtypeStruct((N_CTX, D_MODEL), F32), jax.ShapeDtypeStruct((N_LAT, D_MODEL), F32)]
    else:
        out_specs = [row]
        out_shape = [jax.ShapeDtypeStruct((N_TOK, D_MODEL), F32)]
    return pl.pallas_call(
        functools.partial(_ffn_kernel, base=3 * which, tm=tm, n_x=len(xs), n_out=len(out_specs)),
        grid=(N_WSTEPS + N_TOK // tm,),
        in_specs=(pair if len(xs) == 2 else [row]) + [
            _mod_spec(tm, layer, N_WSTEPS), _wchunk_spec(w_in, layer), _wchunk_spec(w_out, layer),
            _layer_row_spec(ln_g, ln_index), _layer_row_spec(ln_b, ln_index)],
        out_specs=out_specs,
        out_shape=out_shape,
        scratch_shapes=[pltpu.VMEM((D_MODEL, 2 * D_FF), BF16), pltpu.VMEM((D_FF, D_MODEL), BF16)],
        compiler_params=_params(("arbitrary",)),
        name="ffn",
    )(*xs, mods_all, w_in, w_out, ln_g, ln_b)


def _conv_stages(prev_halo, main, next_halo, w_ref, b_ref, g_ref, beta_ref, o_ref, pad_ref, shift_ref, acc_ref):
    rows = 32
    lanes = 128
    first = CONV_HALO - CONV_WIDTH // 2

    def setup(after):
        del after
        pad_ref[0:CONV_HALO, :] = prev_halo()
        pad_ref[CONV_HALO:CONV_HALO + CONV_BLOCK, :] = main()
        pad_ref[CONV_HALO + CONV_BLOCK:, :] = next_halo()
        span = shift_ref.shape[1]
        for ph in range(SUBLANES):
            shift_ref[ph] = pad_ref[ph:ph + span, :]

    def taps(c, r, start_from):
        cs = slice(c * lanes, (c + 1) * lanes)
        acc = jnp.broadcast_to(b_ref[:, cs], (rows, lanes)) + start_from
        for j in range(CONV_WIDTH):
            ph = (first + j) % SUBLANES
            start = r * rows + (first + j) - ph
            acc = acc + w_ref[j:j + 1, cs] * shift_ref[ph, start:start + rows, cs]
        acc_ref[r * rows:(r + 1) * rows, cs] = acc

    def finish(after):
        del after
        y = _norm_rows(acc_ref[...]) * g_ref[...] + beta_ref[...]
        o_ref[...] = (y * _sigmoid(y)).astype(BF16)

    def lane_group_half(c, half, after):
        if after is None:
            start_from = jnp.zeros((rows, lanes), F32)
        else:
            bits = lax.bitcast_convert_type(after, jnp.int32)
            zero = lax.shift_right_logical(lax.shift_right_logical(bits, 16), 16).astype(F32)
            start_from = jnp.concatenate([zero] * (rows // SUBLANES), axis=0)
        per_half = CONV_BLOCK // rows // 2
        for r in range(half * per_half, (half + 1) * per_half):
            taps(c, r, start_from)

    chunks = [functools.partial(lane_group_half, c, half) for c in range(CONV_DIM // lanes) for half in range(2)]
    return [setup] + chunks + [finish]


def _proj_kernel(*refs, tm, carry):
    (x_ref, mod_ref, cos_ref, sin_lo_ref, sin_hi_ref, wt_ref, wuq_ref, wk_ref, wvt_ref, gq_ref,
     gkv_ref, cw_ref, cb_ref, cg_ref, cbeta_ref) = refs[:15]
    refs = refs[15:]
    if carry:
        ckv_prev_ref, kr_prev_ref = refs[:2]
        refs = refs[2:]
    (a_ref, slab_ref, vt_ref, ckv_ref, kr_ref, glu_ring, pad_ref, shift_ref, acc_ref, wt_bf) = refs
    sig_ref, rq_ref, rk_ref, rv_ref, rg_ref, q_ref, kcat_ref = (
        slab_ref.at[:, lo:lo + w] for lo, w in (SLAB_COLS[n] for n in ("sig", "rq", "rk", "rv", "rg", "q", "kcat")))
    t = pl.program_id(0) - PROJ_WSTEPS
    n_tiles = N_TOK // tm
    n_ctx_tiles = N_CTX // tm
    tiles_per_seq = DEC_SEQ // tm
    is_ctx = t < n_ctx_tiles
    pos = (t - 1 - n_ctx_tiles) % tiles_per_seq
    conv_latent = t - 1 >= n_ctx_tiles
    has_prev = jnp.logical_and(conv_latent, pos != 0)
    has_next = jnp.logical_and(conv_latent, pos != tiles_per_seq - 1)
    slot = t % 2
    no_halo = jnp.zeros((CONV_HALO, CONV_DIM), F32)

    def conv_prev_tile(next_rows):
        return _conv_stages(
            lambda: jnp.where(has_prev, glu_ring[slot, tm - CONV_HALO:tm, :], no_halo),
            lambda: glu_ring[1 - slot],
            lambda: jnp.where(has_next, next_rows, no_halo),
            cw_ref, cb_ref, cg_ref, cbeta_ref, a_ref, pad_ref, shift_ref, acc_ref)

    @pl.when(t < 0)
    def _():
        _stage_chunk(pl.program_id(0), wt_ref, wt_bf)

    @pl.when(t == 0)
    def _():
        glu_ring[...] = jnp.zeros(glu_ring.shape, F32)

    @pl.when(t == n_tiles)
    def _():
        for step in conv_prev_tile(no_halo):
            step(None)

    @pl.when(jnp.logical_and(t >= 0, t < n_tiles))
    def _():
        _proj_tile(x_ref, mod_ref, cos_ref, sin_lo_ref, sin_hi_ref, wt_bf, wuq_ref, wk_ref, wvt_ref, gq_ref,
                   gkv_ref, ckv_prev_ref if carry else None, kr_prev_ref if carry else None,
                   rq_ref, rk_ref, rv_ref, rg_ref, q_ref, kcat_ref, vt_ref, sig_ref, ckv_ref, kr_ref,
                   glu_ring, conv_prev_tile, slot, is_ctx, tm)


def _proj_tile(x_ref, mod_ref, cos_ref, sin_lo_ref, sin_hi_ref, wt_ref, wuq_ref, wk_ref, wvt_ref, gq_ref,
               gkv_ref, ckv_prev_ref, kr_prev_ref,
               rq_ref, rk_ref, rv_ref, rg_ref, q_ref, kcat_ref, vt_ref, sig_ref, ckv_ref, kr_ref,
               glu_ring, conv_prev_tile, slot, is_ctx, tm):
    carry = ckv_prev_ref is not None
    x = x_ref[...]
    u = (x * (1.0 + mod_ref[4:5, :]) + mod_ref[3:4, :]).astype(BF16)
    widths = (CONV_DIM, CONV_DIM, RET_HEADS * RET_DK, RET_HEADS * RET_DK, RET_HEADS * RET_DV, RET_HEADS * RET_DV,
              MLA_Q_LORA, MLA_KV_LORA)
    starts = [sum(widths[:n]) for n in range(len(widths))]

    def proj(n):
        return _dot_nt(u, wt_ref[starts[n]:starts[n] + widths[n], :])

    glu = proj(0) * _sigmoid(proj(1))
    conv_steps = conv_prev_tile(glu[0:CONV_HALO, :])

    def conv(n, result):
        for _ in range(n):
            conv_steps.pop(0)(result[0:SUBLANES, 0:128])

    conv(1, glu)
    glu_ring[slot] = glu

    mq = proj(6)
    mkv = proj(7)
    conv(2, mq)
    cos = cos_ref[...]
    sin_lo = sin_lo_ref[...]
    sin_hi = sin_hi_ref[...]

    def rotary(v):
        up = pltpu.roll(v, HEAD_PAD - ROPE_AXIS_HALF, 1)
        down = pltpu.roll(v, ROPE_AXIS_HALF, 1)
        return v * cos + up * sin_lo + down * sin_hi

    qn = (_rms_rows(mq) * gq_ref[...]).astype(BF16)
    qm = _dot(qn, wuq_ref[...])
    ckv = _rms_rows(mkv) * gkv_ref[...]
    ckvb = ckv.astype(BF16)
    kn = _dot(ckvb, wk_ref[...])
    vt_ref[...] = _dot_nt(wvt_ref[...], ckvb).astype(BF16)
    kr_grp = _dot_nt(u, wt_ref[MAIN_W:MAIN_W + HEAD_PAD, :])
    conv(2, kn)
    lane = lax.broadcasted_iota(jnp.int32, kr_grp.shape, 1)
    in_rope = jnp.logical_and(lane >= ROPE_LANE0, lane < ROPE_LANE0 + MLA_D_ROPE)
    kr = jnp.where(in_rope, pltpu.roll(kr_grp, ROPE_LANE0, 1), 0.0)
    kr_rot = rotary(kr)
    for h in range(MLA_HEADS):
        sl = slice(h * HEAD_PAD, (h + 1) * HEAD_PAD)
        q_ref[:, sl] = (rotary(qm[:, sl]) * ATTN_QSCALE).astype(BF16)
        kcat_ref[:, sl] = (kn[:, sl] + kr_rot).astype(BF16)

    gate0 = MAIN_W + MLA_D_ROPE

    def branch_gate(blk):
        cols = slice(blk * D_MODEL, (blk + 1) * D_MODEL)
        gate = _dot_nt(u, wt_ref[gate0 + cols.start:gate0 + cols.stop, :])
        sig_ref[:, cols] = _sigmoid(gate).astype(BF16)
        return gate

    for blk in range(N_BRANCHES):
        conv(1, branch_gate(blk))
    rg = proj(5)
    rg_ref[...] = (rg * _sigmoid(rg)).astype(BF16)
    conv(2, rg)
    rv_ref[...] = proj(4).astype(BF16)
    rk_ref[...] = (proj(3) * (RET_DK ** -0.5)).astype(BF16)
    rq_ref[...] = proj(2).astype(BF16)

    @pl.when(is_ctx)
    def _():
        seqs = tm // SEQ
        ckv3 = ckv.reshape(seqs, SEQ, MLA_KV_LORA)
        kr3 = kr_grp[:, :MLA_D_ROPE].reshape(seqs, SEQ, MLA_D_ROPE)
        if carry:
            ckv_ref[:, 0] = ckv_prev_ref[...]
            kr_ref[:, 0] = kr_prev_ref[...]
            ckv_ref[:, 1] = ckv3
            kr_ref[:, 1] = kr3
        else:
            ckv_ref[...] = ckv3
            kr_ref[...] = kr3


def _proj(x, mods_all, rope_tabs, wt_all, wuq, wk, wvt, gq, gkv, conv_w, carry, *, layer):
    tm = CONV_BLOCK
    n_w = PROJ_WSTEPS
    n_tiles = N_TOK // tm
    n_ctx_tiles = N_CTX // tm
    tiles_per_seq = DEC_SEQ // tm
    seqs = tm // SEQ

    def rope_index(i):
        j = _tile(i, n_w, tm)
        return (jnp.where(j < n_ctx_tiles, 0, tiles_per_seq + (j - n_ctx_tiles) % tiles_per_seq), 0)

    def row(w):
        return _row_spec(tm, w, n_w)

    def out(w, dt):
        return jax.ShapeDtypeStruct((N_TOK, w), dt)

    def ctx_seq_spec(*tail):
        zeros = (0,) * len(tail)
        return pl.BlockSpec((seqs,) + tail, lambda i: (jnp.minimum(_tile(i, n_w, tm), n_ctx_tiles - 1),) + zeros)

    rope = pl.BlockSpec((tm, HEAD_PAD), rope_index)
    in_specs = [row(D_MODEL), _mod_spec(tm, layer, n_w), rope, rope, rope,
                _wchunk_spec(wt_all, layer, n_w)] + [_layer_resident(w, layer) for w in (wuq, wk, wvt, gq, gkv, *conv_w)]
    args = [x, mods_all, *rope_tabs, wt_all, wuq, wk, wvt, gq, gkv, *conv_w]
    if carry is None:
        ctx_specs = [ctx_seq_spec(SEQ, MLA_KV_LORA), ctx_seq_spec(SEQ, MLA_D_ROPE)]
        ctx_shapes = [jax.ShapeDtypeStruct((BATCH, SEQ, MLA_KV_LORA), F32),
                      jax.ShapeDtypeStruct((BATCH, SEQ, MLA_D_ROPE), F32)]
    else:
        in_specs += [ctx_seq_spec(SEQ, MLA_KV_LORA), ctx_seq_spec(SEQ, MLA_D_ROPE)]
        args += list(carry)
        ctx_specs = [ctx_seq_spec(DEPTH, SEQ, MLA_KV_LORA), ctx_seq_spec(DEPTH, SEQ, MLA_D_ROPE)]
        ctx_shapes = [jax.ShapeDtypeStruct((BATCH, DEPTH, SEQ, MLA_KV_LORA), F32),
                      jax.ShapeDtypeStruct((BATCH, DEPTH, SEQ, MLA_D_ROPE), F32)]
    span = CONV_BLOCK + 2 * CONV_HALO - SUBLANES
    return pl.pallas_call(
        functools.partial(_proj_kernel, tm=tm, carry=carry is not None),
        grid=(n_w + n_tiles + 1,),
        in_specs=in_specs,
        out_specs=[pl.BlockSpec((tm, CONV_DIM), lambda i: (_tile(i, n_w + 1, tm), 0)), row(SLAB_W),
                   pl.BlockSpec((MLA_V_W, tm), lambda i: (0, _tile(i, n_w, tm)))] + ctx_specs,
        out_shape=[out(CONV_DIM, BF16), out(SLAB_W, BF16),
                   jax.ShapeDtypeStruct((MLA_V_W, N_TOK), BF16)] + ctx_shapes,
        scratch_shapes=[pltpu.VMEM((2, tm, CONV_DIM), F32),
                        pltpu.VMEM((CONV_BLOCK + 2 * CONV_HALO, CONV_DIM), F32),
                        pltpu.VMEM((SUBLANES, span, CONV_DIM), F32),
                        pltpu.VMEM((CONV_BLOCK, CONV_DIM), F32),
                        pltpu.VMEM(wt_all.shape[1:], BF16)],
        compiler_params=_params(("arbitrary",)),
        name="mix_proj",
    )(*args)


def _cache_kv_kernel(ckv_ref, kr_ref, wk_ref, wvt_ref, kcat_ref, vt_ref):
    ckvb = ckv_ref[...].astype(BF16)
    kn = _dot(ckvb, wk_ref[...])
    vt_ref[...] = _dot_nt(wvt_ref[...], ckvb).astype(BF16)
    kr = kr_ref[...]
    for h in range(MLA_HEADS):
        sl = slice(h * HEAD_PAD, (h + 1) * HEAD_PAD)
        kcat_ref[:, sl] = (kn[:, sl] + kr).astype(BF16)


def _cache_kv(cache_ckv, cache_kr_pad, wk, wvt):
    n = DEC_BATCH * PAST_LEN
    return pl.pallas_call(
        _cache_kv_kernel,
        grid=(DEPTH, DEC_BATCH),
        in_specs=[pl.BlockSpec((None, None, PAST_LEN, MLA_KV_LORA), lambda l, b: (b, l, 0, 0)),
                  pl.BlockSpec((None, None, PAST_LEN, HEAD_PAD), lambda l, b: (b, l, 0, 0)),
                  pl.BlockSpec((None, MLA_KV_LORA, MLA_W), lambda l, b: (l, 0, 0)),
                  pl.BlockSpec((None, MLA_V_W, MLA_KV_LORA), lambda l, b: (l, 0, 0))],
        out_specs=[pl.BlockSpec((None, PAST_LEN, MLA_W), lambda l, b: (l, b, 0)),
                   pl.BlockSpec((None, MLA_V_W, PAST_LEN), lambda l, b: (l, 0, b))],
        out_shape=[jax.ShapeDtypeStruct((DEPTH, n, MLA_W), BF16), jax.ShapeDtypeStruct((DEPTH, MLA_V_W, n), BF16)],
        compiler_params=_params(("arbitrary", "arbitrary")),
        name="cache_kv",
    )(cache_ckv, cache_kr_pad, wk, wvt)


def _ret_kernel(*refs, t, hp, latent, carry, layer):
    if latent:
        lg_ref, q_ref, k_ref, v_ref, g_ref, s0f_ref, s0b_ref, o_ref, d_ref = refs
    elif carry:
        lg_ref, q_ref, k_ref, v_ref, g_ref, sf_prev_ref, sb_prev_ref, o_ref, sf_ref, sb_ref, d_ref = refs
    else:
        lg_ref, q_ref, k_ref, v_ref, g_ref, o_ref, sf_ref, sb_ref, d_ref = refs
    hblk = pl.program_id(0)

    @pl.when(pl.program_id(1) == 0)
    def _():
        diff = (lax.broadcasted_iota(jnp.int32, (t, t), 0) - lax.broadcasted_iota(jnp.int32, (t, t), 1)).astype(F32)
        for hh in range(hp):
            lgf = lg_ref[layer, 0, hblk * hp + hh]
            lgb = lg_ref[layer, 1, hblk * hp + hh]
            d_ref[hh] = jnp.exp(jnp.where(diff >= 0, diff * lgf, -diff * lgb))

    if carry:
        sf_ref[0] = sf_prev_ref[...]
        sb_ref[0] = sb_prev_ref[...]
    pos = lax.broadcasted_iota(jnp.int32, (t, 1), 0).astype(F32)
    for hh in range(hp):
        lgf = lg_ref[layer, 0, hblk * hp + hh]
        lgb = lg_ref[layer, 1, hblk * hp + hh]
        q = q_ref[:, hh * RET_DK:(hh + 1) * RET_DK]
        k = k_ref[:, hh * RET_DK:(hh + 1) * RET_DK]
        v = v_ref[:, hh * RET_DV:(hh + 1) * RET_DV]
        p = (_dot_nt(q, k) * d_ref[hh]).astype(BF16)
        o = _dot(p, v)
        if latent:
            o = o + jnp.exp((pos + 1.0) * lgf) * _dot(q, s0f_ref[hh].astype(BF16))
            o = o + jnp.exp((t - pos) * lgb) * _dot(q, s0b_ref[hh].astype(BF16))
        else:
            kf = k.astype(F32)
            sf = _dot_tn((kf * jnp.exp((t - 1.0 - pos) * lgf)).astype(BF16), v)
            sb = _dot_tn((kf * jnp.exp(pos * lgb)).astype(BF16), v)
            if carry:
                sf_ref[1, hh] = sf
                sb_ref[1, hh] = sb
            else:
                sf_ref[hh] = sf
                sb_ref[hh] = sb
        o_ref[:, hh * RET_DV:(hh + 1) * RET_DV] = (g_ref[:, hh * RET_DV:(hh + 1) * RET_DV] * _norm_rows(o)).astype(BF16)


def _retention(log_g, slab, states, *, latent, layer):
    t = DEC_SEQ if latent else SEQ
    hp = 2 if latent else RET_HEADS
    n_seq = DEC_BATCH if latent else BATCH
    row0 = (N_CTX // t) if latent else 0

    def row(name, w):
        assert SLAB_COLS[name][0] % (hp * w) == 0
        first = SLAB_COLS[name][0] // (hp * w)
        return pl.BlockSpec((t, hp * w), lambda h, s: (row0 + s, first + h))

    smem = pl.BlockSpec(memory_space=pltpu.SMEM)
    out_shape = [jax.ShapeDtypeStruct((n_seq * t, RET_HEADS * RET_DV), BF16)]
    out_specs = [pl.BlockSpec((t, hp * RET_DV), lambda h, s: (s, h))]
    in_specs = [smem, row("rq", RET_DK), row("rk", RET_DK), row("rv", RET_DV), row("rg", RET_DV)]
    args = [log_g, slab, slab, slab, slab]
    carry = False
    if latent:
        st = pl.BlockSpec((None, None, hp, RET_DK, RET_DV), lambda h, s: (s, layer, h, 0, 0))
        in_specs += [st, st]
        args += list(states)
    else:
        st = pl.BlockSpec((None, hp, RET_DK, RET_DV), lambda h, s: (s, h, 0, 0))
        if states is None:
            out_specs += [st, st]
            out_shape += [jax.ShapeDtypeStruct((BATCH, RET_HEADS, RET_DK, RET_DV), F32)] * 2
        else:
            carry = True
            in_specs += [st, st]
            args += list(states)
            st2 = pl.BlockSpec((None, DEPTH, hp, RET_DK, RET_DV), lambda h, s: (s, 0, h, 0, 0))
            out_specs += [st2, st2]
            out_shape += [jax.ShapeDtypeStruct((BATCH, DEPTH, RET_HEADS, RET_DK, RET_DV), F32)] * 2

    return pl.pallas_call(
        functools.partial(_ret_kernel, t=t, hp=hp, latent=latent, carry=carry, layer=layer),
        grid=(RET_HEADS // hp, n_seq),
        in_specs=in_specs,
        out_specs=out_specs,
        out_shape=out_shape,
        scratch_shapes=[pltpu.VMEM((hp, t, t), F32)],
        compiler_params=_params(("arbitrary", "arbitrary")),
        name="retention_lat" if latent else "retention_ctx",
    )(*args)


def _attn_kernel(*refs, t, hp, latent):
    if latent:
        q_ref, k_ref, vt_ref, kc_ref, vtc_ref, o_ref = refs
    else:
        q_ref, k_ref, vt_ref, o_ref = refs
    qb = min(ATTN_QBLOCK, t)
    units = [(slice(hh * HEAD_PAD, (hh + 1) * HEAD_PAD), slice(hh * MLA_D_V, (hh + 1) * MLA_D_V),
              slice(b * qb, (b + 1) * qb)) for b in range(t // qb) for hh in range(hp)]
    def scores(unit):
        sl, _, rows = unit
        q = q_ref[rows, sl]
        s = [_dot_nt(k_ref[:, sl], q)]
        if latent:
            s.append(_dot_nt(kc_ref[:, sl], q))
        return s

    def softmax(s):
        m = functools.reduce(jnp.maximum, [jnp.max(x, axis=0, keepdims=True) for x in s])
        e = [jnp.exp2(x - m) for x in s]
        den = functools.reduce(jnp.add, [jnp.sum(x, axis=0, keepdims=True) for x in e])
        return [x.astype(BF16) for x in e], den

    def values(unit, e, den):
        _, vs, _ = unit
        o = _dot(vt_ref[vs, :], e[0])
        if latent:
            o = o + _dot(vtc_ref[vs, :], e[1])
        return o / den

    pairs = [units[u:u + 2] for u in range(0, len(units), 2)]
    s_next = [scores(u) for u in pairs[0]]
    sm_prev = None
    for g in range(len(pairs) + 1):
        s_cur = s_next
        if g + 1 < len(pairs):
            s_next = [scores(u) for u in pairs[g + 1]]
        sm_cur = [softmax(s) for s in s_cur] if g < len(pairs) else None
        if sm_prev is not None:
            (_, vs0, rows), (_, vs1, _) = pairs[g - 1]
            outs = [values(u, e, den) for u, (e, den) in zip(pairs[g - 1], sm_prev)]
            o_ref[rows, vs0.start:vs1.stop] = jnp.concatenate(outs, axis=0).T.astype(BF16)
        sm_prev = sm_cur


def _attention(slab, vt, cache, *, latent, layer):
    t = DEC_SEQ if latent else SEQ
    hp = 4 if latent else MLA_HEADS
    n_seq = DEC_BATCH if latent else BATCH
    row0 = (N_CTX // t) if latent else 0

    def row(name):
        assert SLAB_COLS[name][0] % (hp * HEAD_PAD) == 0
        first = SLAB_COLS[name][0] // (hp * HEAD_PAD)
        return pl.BlockSpec((t, hp * HEAD_PAD), lambda s, h: (row0 + s, first + h))

    col = pl.BlockSpec((hp * MLA_D_V, t), lambda s, h: (h, row0 + s))
    in_specs = [row("q"), row("kcat"), col]
    args = [slab, slab, vt]
    if latent:
        in_specs += [pl.BlockSpec((None, PAST_LEN, hp * HEAD_PAD), lambda s, h: (layer, s, h)),
                     pl.BlockSpec((None, hp * MLA_D_V, PAST_LEN), lambda s, h: (layer, h, s))]
        args += list(cache)
    return pl.pallas_call(
        functools.partial(_attn_kernel, t=t, hp=hp, latent=latent),
        grid=(n_seq, MLA_HEADS // hp),
        in_specs=in_specs,
        out_specs=pl.BlockSpec((t, hp * MLA_D_V), lambda s, h: (s, h)),
        out_shape=jax.ShapeDtypeStruct((n_seq * t, MLA_V_W), BF16),
        compiler_params=_params(("arbitrary", "arbitrary")),
        name="attention_lat" if latent else "attention_ctx",
    )(*args)


def _merge_kernel(x_ref, mod_ref, a_ref, rc_ref, rl_ref, mc_ref, ml_ref, sig_ref, wc_ref, wr_ref, wm_ref, wo_ref,
                  g_ref, b_ref, o_ref, wc_bf, wr_bf, wm_bf, wo_bf, *, tm):
    i = pl.program_id(0)
    is_ctx = i < N_WSTEPS + N_CTX // tm

    @pl.when(i < N_WSTEPS)
    def _():
        _stage_chunk(i, wc_ref, wc_bf)
        _stage_chunk(i, wr_ref, wr_bf)
        _stage_chunk(i, wo_ref, wo_bf)
        _stage_chunk(i, wm_ref, wm_bf)

    @pl.when(i >= N_WSTEPS)
    def _():
        for r0 in range(0, tm, ROW_GROUP):
            rows = slice(r0, r0 + ROW_GROUP)
            x = x_ref[rows, :]
            r = jnp.where(is_ctx, rc_ref[rows, :], rl_ref[rows, :])
            m = jnp.where(is_ctx, mc_ref[rows, :], ml_ref[rows, :])
            merged = sig_ref[rows, 0:D_MODEL] * _dot(a_ref[rows, :], wc_bf[...])
            merged = merged + sig_ref[rows, D_MODEL:2 * D_MODEL] * _dot(r, wr_bf[...])
            merged = merged + sig_ref[rows, 2 * D_MODEL:] * _dot(m, wm_bf[...])
            y = _dot(merged.astype(BF16), wo_bf[...])
            z = DEEPNORM_ALPHA * x + mod_ref[5:6, :] * y
            o_ref[rows, :] = _norm_rows(z) * g_ref[...] + b_ref[...]


def _merge(x, mods_all, a, r_pair, m_pair, sig, wc, wr, wm, wo, ln_g, ln_b, *, layer, tm=512):
    def row(w):
        return _row_spec(tm, w, N_WSTEPS)

    def pair(w):
        return [_ctx_row_spec(tm, w, N_WSTEPS), _lat_row_spec(tm, w, N_WSTEPS)]

    ln_index = layer * 3 + 1
    return pl.pallas_call(
        functools.partial(_merge_kernel, tm=tm),
        grid=(N_WSTEPS + N_TOK // tm,),
        in_specs=[row(D_MODEL), _mod_spec(tm, layer, N_WSTEPS), row(CONV_DIM)] + pair(RET_HEADS * RET_DV)
        + pair(MLA_V_W) + [row(GATE_W), _wchunk_spec(wc, layer), _wchunk_spec(wr, layer), _wchunk_spec(wm, layer),
                         _wchunk_spec(wo, layer), _layer_row_spec(ln_g, ln_index), _layer_row_spec(ln_b, ln_index)],
        out_specs=row(D_MODEL),
        out_shape=jax.ShapeDtypeStruct((N_TOK, D_MODEL), F32),
        scratch_shapes=[pltpu.VMEM((CONV_DIM, D_MODEL), BF16), pltpu.VMEM((RET_HEADS * RET_DV, D_MODEL), BF16),
                        pltpu.VMEM((MLA_V_W, D_MODEL), BF16), pltpu.VMEM((D_MODEL, D_MODEL), BF16)],
        compiler_params=_params(("arbitrary",)),
        name="merge",
    )(x, mods_all, a, *r_pair, *m_pair, sig, wc, wr, wm, wo, ln_g, ln_b)


def _rope_tables():
    rows = DEC_SEQ // GRID_W
    row_id = jnp.repeat(jnp.arange(rows, dtype=F32), GRID_W)
    col_id = jnp.tile(jnp.arange(GRID_W, dtype=F32), rows)
    inv_freq = ROPE_BASE ** (-jnp.arange(ROPE_AXIS_HALF, dtype=F32) / ROPE_AXIS_HALF)
    ang = jnp.stack([row_id[:, None] * inv_freq, col_id[:, None] * inv_freq], axis=1)
    cos = jnp.cos(ang)
    sin = jnp.sin(ang)
    cos32 = jnp.stack([cos, cos], axis=2).reshape(DEC_SEQ, MLA_D_ROPE)
    zero = jnp.zeros_like(sin)
    sin_lo32 = jnp.stack([-sin, zero], axis=2).reshape(DEC_SEQ, MLA_D_ROPE)
    sin_hi32 = jnp.stack([zero, sin], axis=2).reshape(DEC_SEQ, MLA_D_ROPE)
    tail = HEAD_PAD - ROPE_LANE0 - MLA_D_ROPE
    cos_t = jnp.concatenate([jnp.ones((DEC_SEQ, ROPE_LANE0), F32), cos32, jnp.ones((DEC_SEQ, tail), F32)], axis=1)
    cos_t = jnp.concatenate([jnp.ones((DEC_SEQ, HEAD_PAD), F32), cos_t], axis=0)

    def sin_table(s32):
        return jnp.pad(s32, ((DEC_SEQ, 0), (ROPE_LANE0, tail)))

    return cos_t, sin_table(sin_lo32), sin_table(sin_hi32)


def _head_pad_cols(w, width):
    k = w.shape[0]
    w = w.reshape(k, MLA_HEADS, width)
    return jnp.pad(w, ((0, 0), (0, 0), (0, HEAD_PAD - width))).reshape(k, MLA_W)


def kernel(x_prompt, x_sample, cache_mla_ckv, cache_mla_krope, state_ret_fwd, state_ret_bwd, c, c_ctx, ada_w, ada_b, ffn1_w_in, ffn1_w_out, ffn2_w_in, ffn2_w_out, post_ln_g, post_ln_b, mix_w_in, conv_w_dw, conv_b_dw, conv_ln_g, conv_ln_b, conv_w_out, ret_decay_fwd, ret_decay_bwd, ret_w_out, mla_q_norm, mla_w_uq, mla_kv_norm, mla_w_ukv, mla_w_out, mix_w_o):
    assert DEPTH == 2
    cvec = jnp.concatenate([c_ctx[None, :], c, jnp.zeros((N_MOD_ROWS - 1 - DEC_BATCH, D_MODEL), F32)], axis=0)
    mods_all = _ada_mods(cvec, ada_w, ada_b).reshape(DEPTH, N_MOD_ROWS, N_MODS, D_MODEL)
    rope_tabs = _rope_tables()

    w_ukv = mla_w_ukv.reshape(DEPTH, MLA_KV_LORA, MLA_HEADS, MLA_D_NOPE + MLA_D_V)
    wk_all = _head_pad_cols(w_ukv[..., :MLA_D_NOPE].reshape(DEPTH * MLA_KV_LORA, -1), MLA_D_NOPE)
    wk_all = wk_all.reshape(DEPTH, MLA_KV_LORA, MLA_W).astype(BF16)
    wvt_all = jnp.swapaxes(w_ukv[..., MLA_D_NOPE:].reshape(DEPTH, MLA_KV_LORA, MLA_V_W), 1, 2).astype(BF16)
    kr_tail = HEAD_PAD - ROPE_LANE0 - MLA_D_ROPE
    cache_kr_pad = jnp.pad(cache_mla_krope, ((0, 0), (0, 0), (0, 0), (ROPE_LANE0, kr_tail)))
    kcat_c, vt_c = _cache_kv(cache_mla_ckv, cache_kr_pad, wk_all, wvt_all)
    log_g = jnp.stack([jax.nn.log_sigmoid(ret_decay_fwd), jax.nn.log_sigmoid(ret_decay_bwd)], axis=1)
    wt_all = jnp.swapaxes(mix_w_in, 1, 2)
    wuq_all = _head_pad_cols(mla_w_uq.reshape(DEPTH * MLA_Q_LORA, -1), MLA_D_NOPE + MLA_D_ROPE)
    wuq_all = wuq_all.reshape(DEPTH, MLA_Q_LORA, MLA_W).astype(BF16)
    gq_all = mla_q_norm.reshape(DEPTH, 1, MLA_Q_LORA)
    gkv_all = mla_kv_norm.reshape(DEPTH, 1, MLA_KV_LORA)
    conv_w = (conv_w_dw, conv_b_dw.reshape(DEPTH, 1, CONV_DIM), conv_ln_g.reshape(DEPTH, 1, CONV_DIM),
              conv_ln_b.reshape(DEPTH, 1, CONV_DIM))

    ln_g = post_ln_g.reshape(DEPTH * 3, 1, D_MODEL)
    ln_b = post_ln_b.reshape(DEPTH * 3, 1, D_MODEL)
    xs = (x_prompt.reshape(N_CTX, D_MODEL), x_sample.reshape(N_LAT, D_MODEL))
    ctx_carry = None
    state_carry = None
    for l in range(DEPTH):
        last = l == DEPTH - 1
        x = _ffn(xs, mods_all, ffn1_w_in, ffn1_w_out, ln_g, ln_b, layer=l, which=0)[0]

        a, slab, vt, ckv, kr = _proj(x, mods_all, rope_tabs, wt_all, wuq_all, wk_all, wvt_all, gq_all, gkv_all, conv_w,
                                     ctx_carry, layer=l)
        ctx_carry = (ckv, kr)

        r_ctx, sf, sb = _retention(log_g, slab, state_carry, latent=False, layer=l)
        state_carry = (sf, sb)
        r_lat = _retention(log_g, slab, (state_ret_fwd, state_ret_bwd), latent=True, layer=l)[0]
        m_ctx = _attention(slab, vt, None, latent=False, layer=l)
        m_lat = _attention(slab, vt, (kcat_c, vt_c), latent=True, layer=l)

        x = _merge(x, mods_all, a, (r_ctx, r_lat), (m_ctx, m_lat), slab, conv_w_out, ret_w_out, mla_w_out, mix_w_o,
                   ln_g, ln_b, layer=l)
        xs = _ffn((x,), mods_all, ffn2_w_in, ffn2_w_out, ln_g, ln_b, layer=l, which=2, split_out=last)

    y_ctx, y_lat = xs
    return (y_ctx.reshape(BATCH, SEQ, D_MODEL), y_lat.reshape(DEC_BATCH, DEC_SEQ, D_MODEL),
            ctx_carry[0], ctx_carry[1], state_carry[0], state_carry[1])
```

```python
import functools
import math

import jax
import jax.numpy as jnp
import numpy as np
from jax import lax
from jax.experimental import pallas as pl
from jax.experimental.pallas import tpu as pltpu

F32 = jnp.float32
BF16 = jnp.bfloat16

D_MODEL = 1024
BATCH = 16
SEQ = 256
DEPTH = 2
DEC_BATCH = 4
DEC_SEQ = 1024
PAST_LEN = 256
GRID_W = 64
D_FF = 2816
N_MODS = 9
CONV_DIM = 512
CONV_WIDTH = 31
RET_HEADS = 4
RET_DK = 128
RET_DV = 256
MLA_HEADS = 8
MLA_Q_LORA = 512
MLA_KV_LORA = 256
MLA_D_NOPE = 64
MLA_D_ROPE = 32
MLA_D_V = 64
ROPE_AXIS_HALF = MLA_D_ROPE // 4
ROPE_BASE = 10000.0
DEEPNORM_ALPHA = (2 * DEPTH) ** 0.25
LN_EPS = 1e-5
RMS_EPS = 1e-6

N_CTX = BATCH * SEQ
N_LAT = DEC_BATCH * DEC_SEQ
N_TOK = N_CTX + N_LAT
N_MOD_ROWS = 8
HEAD_PAD = 128
ROPE_LANE0 = MLA_D_NOPE
MLA_W = MLA_HEADS * HEAD_PAD
MLA_V_W = MLA_HEADS * MLA_D_V
MAIN_W = 2 * CONV_DIM + 2 * RET_HEADS * RET_DK + 2 * RET_HEADS * RET_DV + MLA_Q_LORA + MLA_KV_LORA
N_BRANCHES = 3
GATE_W = N_BRANCHES * D_MODEL
SLAB_COLS = {}
for _name, _w in (("sig", GATE_W), ("rq", RET_HEADS * RET_DK), ("rk", RET_HEADS * RET_DK), ("rv", RET_HEADS * RET_DV),
                  ("rg", RET_HEADS * RET_DV), ("q", MLA_W), ("kcat", MLA_W)):
    SLAB_COLS[_name] = (sum(w for _, w in SLAB_COLS.values()), _w)
SLAB_W = sum(w for _, w in SLAB_COLS.values())
SUBLANES = 8
CONV_HALO = 16
CONV_BLOCK = 256
VMEM_LIMIT = 56 * 1024 * 1024
N_WSTEPS = 8
PROJ_WSTEPS = 6
MXU_COLS = 256
FFN_CHUNKS = ((0, 6 * MXU_COLS), (6 * MXU_COLS, D_FF))
ROW_HALVES = 2
ATTN_QBLOCK = 256
ATTN_QSCALE = (MLA_D_NOPE + MLA_D_ROPE) ** -0.5 * math.log2(math.e)


def _dot(a, b):
    return jnp.dot(a, b, preferred_element_type=F32)


def _dot_nt(a, b):
    return lax.dot_general(a, b, (((1,), (1,)), ((), ())), preferred_element_type=F32)


def _dot_tn(a, b):
    return lax.dot_general(a, b, (((0,), (0,)), ((), ())), preferred_element_type=F32)


def _sigmoid(x):
    return 1.0 / (1.0 + jnp.exp(-x))


def _norm_rows(z):
    mu = jnp.mean(z, axis=-1, keepdims=True)
    zc = z - mu
    var = jnp.mean(zc * zc, axis=-1, keepdims=True)
    return zc * lax.rsqrt(var + LN_EPS)


def _rms_rows(z):
    return z * lax.rsqrt(jnp.mean(z * z, axis=-1, keepdims=True) + RMS_EPS)


def _resident(shape):
    zeros = (0,) * len(shape)
    return pl.BlockSpec(shape, lambda *_: zeros, pipeline_mode=pl.Buffered(1))


def _layer_resident(w, layer):
    zeros = (0,) * (w.ndim - 1)
    return pl.BlockSpec((None,) + w.shape[1:], lambda *_: (layer,) + zeros, pipeline_mode=pl.Buffered(1))


def _tile(i, n_w, tm):
    return jnp.clip(i - n_w, 0, N_TOK // tm - 1)


def _row_spec(tm, w, n_w):
    return pl.BlockSpec((tm, w), lambda i: (_tile(i, n_w, tm), 0))


def _ctx_row_spec(tm, w, n_w):
    last = N_CTX // tm - 1
    return pl.BlockSpec((tm, w), lambda i: (jnp.minimum(_tile(i, n_w, tm), last), 0))


def _lat_row_spec(tm, w, n_w):
    first = N_CTX // tm
    return pl.BlockSpec((tm, w), lambda i: (jnp.maximum(_tile(i, n_w, tm) - first, 0), 0))


def _mod_spec(tm, layer, n_w):
    n_ctx_tiles = N_CTX // tm
    tiles_per_seq = DEC_SEQ // tm

    def index(i):
        j = _tile(i, n_w, tm)
        return (layer, jnp.where(j < n_ctx_tiles, 0, 1 + (j - n_ctx_tiles) // tiles_per_seq), 0, 0)

    return pl.BlockSpec((None, None, N_MODS, D_MODEL), index)


def _wchunk_spec(w, layer, n_w=N_WSTEPS):
    _, rows, cols = w.shape
    return pl.BlockSpec((None, rows // n_w, cols), lambda i: (layer, jnp.minimum(i, n_w - 1), 0))


def _layer_row_spec(w, index):
    return pl.BlockSpec((None, 1, w.shape[-1]), lambda i: (index, 0, 0), pipeline_mode=pl.Buffered(1))


def _stage_chunk(i, src_ref, dst_ref):
    rows = src_ref.shape[0]
    dst_ref[pl.ds(pl.multiple_of(i * rows, rows), rows), :] = src_ref[...].astype(BF16)


def _params(semantics):
    return pltpu.CompilerParams(dimension_semantics=semantics, vmem_limit_bytes=VMEM_LIMIT)


def _ada_kernel(c_ref, w_ref, b_ref, o_ref):
    c = c_ref[...]
    h = (c * _sigmoid(c)).astype(BF16)
    o_ref[...] = _dot(h, w_ref[...].astype(BF16)) + b_ref[...]


def _ada_mods(cvec, ada_w, ada_b):
    n_out = N_MODS * D_MODEL
    tn = n_out // 4
    return pl.pallas_call(
        _ada_kernel,
        grid=(DEPTH, n_out // tn),
        in_specs=[
            pl.BlockSpec((N_MOD_ROWS, D_MODEL), lambda l, j: (0, 0)),
            pl.BlockSpec((None, D_MODEL, tn), lambda l, j: (l, 0, j)),
            pl.BlockSpec((None, 1, tn), lambda l, j: (l, 0, j)),
        ],
        out_specs=pl.BlockSpec((None, N_MOD_ROWS, tn), lambda l, j: (l, 0, j)),
        out_shape=jax.ShapeDtypeStruct((DEPTH, N_MOD_ROWS, n_out), F32),
        compiler_params=_params(("arbitrary", "arbitrary")),
        name="ada_mods",
    )(cvec, ada_w, ada_b.reshape(DEPTH, 1, n_out))


def _ffn_kernel(*refs, base, tm, n_x, n_out):
    x_refs = refs[:n_x]
    mod_ref, win_ref, wout_ref, g_ref, b_ref = refs[n_x:n_x + 5]
    o_refs = refs[n_x + 5:n_x + 5 + n_out]
    win_bf, wout_bf = refs[n_x + 5 + n_out:]
    i = pl.program_id(0)
    is_ctx = i < N_WSTEPS + N_CTX // tm

    @pl.when(i < N_WSTEPS)
    def _():
        _stage_chunk(i, win_ref, win_bf)
        _stage_chunk(i, wout_ref, wout_bf)

    @pl.when(i >= N_WSTEPS)
    def _():
        shift = mod_ref[base:base + 1, :]
        scale = mod_ref[base + 1:base + 2, :]
        gate = mod_ref[base + 2:base + 3, :]
        halves = [slice(k * tm // ROW_HALVES, (k + 1) * tm // ROW_HALVES) for k in range(ROW_HALVES)]
        results = []
        for rows in halves:
            if n_x == 2:
                x = jnp.where(is_ctx, x_refs[0][rows, :], x_refs[1][rows, :])
            else:
                x = x_refs[0][rows, :]
            h = (x * (1.0 + scale) + shift).astype(BF16)
            y = None
            for lo, hi in FFN_CHUNKS:
                g = _dot(h, win_bf[:, lo:hi])
                u = _dot(h, win_bf[:, D_FF + lo:D_FF + hi])
                a = (g * _sigmoid(g) * u).astype(BF16)
                yc = _dot(a, wout_bf[lo:hi, :])
                y = yc if y is None else y + yc
            z = DEEPNORM_ALPHA * x + 0.5 * gate * y
            res = _norm_rows(z) * g_ref[...] + b_ref[...]
            if n_out == 1:
                o_refs[0][rows, :] = res
            results.append(res)
        if n_out == 2:
            @pl.when(is_ctx)
            def _():
                for rows, res in zip(halves, results):
                    o_refs[0][rows, :] = res

            @pl.when(jnp.logical_not(is_ctx))
            def _():
                for rows, res in zip(halves, results):
                    o_refs[1][rows, :] = res


def _ffn(xs, mods_all, w_in, w_out, ln_g, ln_b, *, layer, which, split_out=False, tm=512):
    row = _row_spec(tm, D_MODEL, N_WSTEPS)
    pair = [_ctx_row_spec(tm, D_MODEL, N_WSTEPS), _lat_row_spec(tm, D_MODEL, N_WSTEPS)]
    ln_index = layer * 3 + which
    if split_out:
        out_specs = pair
        out_shape = [jax.ShapeDtypeStruct((N_CTX, D_MODEL), F32), jax.ShapeDtypeStruct((N_LAT, D_MODEL), F32)]
    else:
        out_specs = [row]
        out_shape = [jax.ShapeDtypeStruct((N_TOK, D_MODEL), F32)]
    return pl.pallas_call(
        functools.partial(_ffn_kernel, base=3 * which, tm=tm, n_x=len(xs), n_out=len(out_specs)),
        grid=(N_WSTEPS + N_TOK // tm,),
        in_specs=(pair if len(xs) == 2 else [row]) + [
            _mod_spec(tm, layer, N_WSTEPS), _wchunk_spec(w_in, layer), _wchunk_spec(w_out, layer),
            _layer_row_spec(ln_g, ln_index), _layer_row_spec(ln_b, ln_index)],
        out_specs=out_specs,
        out_shape=out_shape,
        scratch_shapes=[pltpu.VMEM((D_MODEL, 2 * D_FF), BF16), pltpu.VMEM((D_FF, D_MODEL), BF16)],
        compiler_params=_params(("arbitrary",)),
        name="ffn",
    )(*xs, mods_all, w_in, w_out, ln_g, ln_b)


def _conv_stages(prev_halo, main, next_halo, w_ref, b_ref, g_ref, beta_ref, o_ref, pad_ref, shift_ref, acc_ref):
    rows = 32
    lanes = 128
    first = CONV_HALO - CONV_WIDTH // 2

    def setup(after):
        del after
        pad_ref[0:CONV_HALO, :] = prev_halo()
        pad_ref[CONV_HALO:CONV_HALO + CONV_BLOCK, :] = main()
        pad_ref[CONV_HALO + CONV_BLOCK:, :] = next_halo()
        span = shift_ref.shape[1]
        for ph in range(SUBLANES):
            shift_ref[ph] = pad_ref[ph:ph + span, :]

    def taps(c, r, start_from):
        cs = slice(c * lanes, (c + 1) * lanes)
        acc = jnp.broadcast_to(b_ref[:, cs], (rows, lanes)) + start_from
        for j in range(CONV_WIDTH):
            ph = (first + j) % SUBLANES
            start = r * rows + (first + j) - ph
            acc = acc + w_ref[j:j + 1, cs] * shift_ref[ph, start:start + rows, cs]
        acc_ref[r * rows:(r + 1) * rows, cs] = acc

    def finish(after):
        del after
        y = _norm_rows(acc_ref[...]) * g_ref[...] + beta_ref[...]
        o_ref[...] = (y * _sigmoid(y)).astype(BF16)

    def lane_group_half(c, half, after):
        if after is None:
            start_from = jnp.zeros((rows, lanes), F32)
        else:
            bits = lax.bitcast_convert_type(after, jnp.int32)
            zero = lax.shift_right_logical(lax.shift_right_logical(bits, 16), 16).astype(F32)
            start_from = jnp.concatenate([zero] * (rows // SUBLANES), axis=0)
        per_half = CONV_BLOCK // rows // 2
        for r in range(half * per_half, (half + 1) * per_half):
            taps(c, r, start_from)

    chunks = [functools.partial(lane_group_half, c, half) for c in range(CONV_DIM // lanes) for half in range(2)]
    return [setup] + chunks + [finish]


def _proj_kernel(*refs, tm, carry):
    (x_ref, mod_ref, cos_ref, sin_lo_ref, sin_hi_ref, wt_ref, wuq_ref, wk_ref, wvt_ref, gq_ref,
     gkv_ref, cw_ref, cb_ref, cg_ref, cbeta_ref) = refs[:15]
    refs = refs[15:]
    if carry:
        ckv_prev_ref, kr_prev_ref = refs[:2]
        refs = refs[2:]
    (a_ref, slab_ref, vt_ref, ckv_ref, kr_ref, glu_ring, pad_ref, shift_ref, acc_ref, wt_bf) = refs
    sig_ref, rq_ref, rk_ref, rv_ref, rg_ref, q_ref, kcat_ref = (
        slab_ref.at[:, lo:lo + w] for lo, w in (SLAB_COLS[n] for n in ("sig", "rq", "rk", "rv", "rg", "q", "kcat")))
    t = pl.program_id(0) - PROJ_WSTEPS
    n_tiles = N_TOK // tm
    n_ctx_tiles = N_CTX // tm
    tiles_per_seq = DEC_SEQ // tm
    is_ctx = t < n_ctx_tiles
    pos = (t - 1 - n_ctx_tiles) % tiles_per_seq
    conv_latent = t - 1 >= n_ctx_tiles
    has_prev = jnp.logical_and(conv_latent, pos != 0)
    has_next = jnp.logical_and(conv_latent, pos != tiles_per_seq - 1)
    slot = t % 2
    no_halo = jnp.zeros((CONV_HALO, CONV_DIM), F32)

    def conv_prev_tile(next_rows):
        return _conv_stages(
            lambda: jnp.where(has_prev, glu_ring[slot, tm - CONV_HALO:tm, :], no_halo),
            lambda: glu_ring[1 - slot],
            lambda: jnp.where(has_next, next_rows, no_halo),
            cw_ref, cb_ref, cg_ref, cbeta_ref, a_ref, pad_ref, shift_ref, acc_ref)

    @pl.when(t < 0)
    def _():
        _stage_chunk(pl.program_id(0), wt_ref, wt_bf)

    @pl.when(t == 0)
    def _():
        glu_ring[...] = jnp.zeros(glu_ring.shape, F32)

    @pl.when(t == n_tiles)
    def _():
        for step in conv_prev_tile(no_halo):
            step(None)

    @pl.when(jnp.logical_and(t >= 0, t < n_tiles))
    def _():
        _proj_tile(x_ref, mod_ref, cos_ref, sin_lo_ref, sin_hi_ref, wt_bf, wuq_ref, wk_ref, wvt_ref, gq_ref,
                   gkv_ref, ckv_prev_ref if carry else None, kr_prev_ref if carry else None,
                   rq_ref, rk_ref, rv_ref, rg_ref, q_ref, kcat_ref, vt_ref, sig_ref, ckv_ref, kr_ref,
                   glu_ring, conv_prev_tile, slot, is_ctx, tm)


def _proj_tile(x_ref, mod_ref, cos_ref, sin_lo_ref, sin_hi_ref, wt_ref, wuq_ref, wk_ref, wvt_ref, gq_ref,
               gkv_ref, ckv_prev_ref, kr_prev_ref,
               rq_ref, rk_ref, rv_ref, rg_ref, q_ref, kcat_ref, vt_ref, sig_ref, ckv_ref, kr_ref,
               glu_ring, conv_prev_tile, slot, is_ctx, tm):
    carry = ckv_prev_ref is not None
    x = x_ref[...]
    u = (x * (1.0 + mod_ref[4:5, :]) + mod_ref[3:4, :]).astype(BF16)
    widths = (CONV_DIM, CONV_DIM, RET_HEADS * RET_DK, RET_HEADS * RET_DK, RET_HEADS * RET_DV, RET_HEADS * RET_DV,
              MLA_Q_LORA, MLA_KV_LORA)
    starts = [sum(widths[:n]) for n in range(len(widths))]

    def proj(n):
        return _dot_nt(u, wt_ref[starts[n]:starts[n] + widths[n], :])

    glu = proj(0) * _sigmoid(proj(1))
    conv_steps = conv_prev_tile(glu[0:CONV_HALO, :])

    def conv(n, result):
        for _ in range(n):
            conv_steps.pop(0)(result[0:SUBLANES, 0:128])

    conv(1, glu)
    glu_ring[slot] = glu

    mq = proj(6)
    mkv = proj(7)
    conv(2, mq)
    cos = cos_ref[...]
    sin_lo = sin_lo_ref[...]
    sin_hi = sin_hi_ref[...]

    def rotary(v):
        up = pltpu.roll(v, HEAD_PAD - ROPE_AXIS_HALF, 1)
        down = pltpu.roll(v, ROPE_AXIS_HALF, 1)
        return v * cos + up * sin_lo + down * sin_hi

    qn = (_rms_rows(mq) * gq_ref[...]).astype(BF16)
    qm = _dot(qn, wuq_ref[...])
    ckv = _rms_rows(mkv) * gkv_ref[...]
    ckvb = ckv.astype(BF16)
    kn = _dot(ckvb, wk_ref[...])
    vt_ref[...] = _dot_nt(wvt_ref[...], ckvb).astype(BF16)
    kr_grp = _dot_nt(u, wt_ref[MAIN_W:MAIN_W + HEAD_PAD, :])
    conv(2, kn)
    lane = lax.broadcasted_iota(jnp.int32, kr_grp.shape, 1)
    in_rope = jnp.logical_and(lane >= ROPE_LANE0, lane < ROPE_LANE0 + MLA_D_ROPE)
    kr = jnp.where(in_rope, pltpu.roll(kr_grp, ROPE_LANE0, 1), 0.0)
    kr_rot = rotary(kr)
    for h in range(MLA_HEADS):
        sl = slice(h * HEAD_PAD, (h + 1) * HEAD_PAD)
        q_ref[:, sl] = (rotary(qm[:, sl]) * ATTN_QSCALE).astype(BF16)
        kcat_ref[:, sl] = (kn[:, sl] + kr_rot).astype(BF16)

    gate0 = MAIN_W + MLA_D_ROPE

    def branch_gate(blk):
        cols = slice(blk * D_MODEL, (blk + 1) * D_MODEL)
        gate = _dot_nt(u, wt_ref[gate0 + cols.start:gate0 + cols.stop, :])
        sig_ref[:, cols] = _sigmoid(gate).astype(BF16)
        return gate

    for blk in range(N_BRANCHES):
        conv(1, branch_gate(blk))
    rg = proj(5)
    rg_ref[...] = (rg * _sigmoid(rg)).astype(BF16)
    conv(2, rg)
    rv_ref[...] = proj(4).astype(BF16)
    rk_ref[...] = (proj(3) * (RET_DK ** -0.5)).astype(BF16)
    rq_ref[...] = proj(2).astype(BF16)

    @pl.when(is_ctx)
    def _():
        seqs = tm // SEQ
        ckv3 = ckv.reshape(seqs, SEQ, MLA_KV_LORA)
        kr3 = kr_grp[:, :MLA_D_ROPE].reshape(seqs, SEQ, MLA_D_ROPE)
        if carry:
            ckv_ref[:, 0] = ckv_prev_ref[...]
            kr_ref[:, 0] = kr_prev_ref[...]
            ckv_ref[:, 1] = ckv3
            kr_ref[:, 1] = kr3
        else:
            ckv_ref[...] = ckv3
            kr_ref[...] = kr3


def _proj(x, mods_all, rope_tabs, wt_all, wuq, wk, wvt, gq, gkv, conv_w, carry, *, layer):
    tm = CONV_BLOCK
    n_w = PROJ_WSTEPS
    n_tiles = N_TOK // tm
    n_ctx_tiles = N_CTX // tm
    tiles_per_seq = DEC_SEQ // tm
    seqs = tm // SEQ

    def rope_index(i):
        j = _tile(i, n_w, tm)
        return (jnp.where(j < n_ctx_tiles, 0, tiles_per_seq + (j - n_ctx_tiles) % tiles_per_seq), 0)

    def row(w):
        return _row_spec(tm, w, n_w)

    def out(w, dt):
        return jax.ShapeDtypeStruct((N_TOK, w), dt)

    def ctx_seq_spec(*tail):
        zeros = (0,) * len(tail)
        return pl.BlockSpec((seqs,) + tail, lambda i: (jnp.minimum(_tile(i, n_w, tm), n_ctx_tiles - 1),) + zeros)

    rope = pl.BlockSpec((tm, HEAD_PAD), rope_index)
    in_specs = [row(D_MODEL), _mod_spec(tm, layer, n_w), rope, rope, rope,
                _wchunk_spec(wt_all, layer, n_w)] + [_layer_resident(w, layer) for w in (wuq, wk, wvt, gq, gkv, *conv_w)]
    args = [x, mods_all, *rope_tabs, wt_all, wuq, wk, wvt, gq, gkv, *conv_w]
    if carry is None:
        ctx_specs = [ctx_seq_spec(SEQ, MLA_KV_LORA), ctx_seq_spec(SEQ, MLA_D_ROPE)]
        ctx_shapes = [jax.ShapeDtypeStruct((BATCH, SEQ, MLA_KV_LORA), F32),
                      jax.ShapeDtypeStruct((BATCH, SEQ, MLA_D_ROPE), F32)]
    else:
        in_specs += [ctx_seq_spec(SEQ, MLA_KV_LORA), ctx_seq_spec(SEQ, MLA_D_ROPE)]
        args += list(carry)
        ctx_specs = [ctx_seq_spec(DEPTH, SEQ, MLA_KV_LORA), ctx_seq_spec(DEPTH, SEQ, MLA_D_ROPE)]
        ctx_shapes = [jax.ShapeDtypeStruct((BATCH, DEPTH, SEQ, MLA_KV_LORA), F32),
                      jax.ShapeDtypeStruct((BATCH, DEPTH, SEQ, MLA_D_ROPE), F32)]
    span = CONV_BLOCK + 2 * CONV_HALO - SUBLANES
    return pl.pallas_call(
        functools.partial(_proj_kernel, tm=tm, carry=carry is not None),
        grid=(n_w + n_tiles + 1,),
        in_specs=in_specs,
        out_specs=[pl.BlockSpec((tm, CONV_DIM), lambda i: (_tile(i, n_w + 1, tm), 0)), row(SLAB_W),
                   pl.BlockSpec((MLA_V_W, tm), lambda i: (0, _tile(i, n_w, tm)))] + ctx_specs,
        out_shape=[out(CONV_DIM, BF16), out(SLAB_W, BF16),
                   jax.ShapeDtypeStruct((MLA_V_W, N_TOK), BF16)] + ctx_shapes,
        scratch_shapes=[pltpu.VMEM((2, tm, CONV_DIM), F32),
                        pltpu.VMEM((CONV_BLOCK + 2 * CONV_HALO, CONV_DIM), F32),
                        pltpu.VMEM((SUBLANES, span, CONV_DIM), F32),
                        pltpu.VMEM((CONV_BLOCK, CONV_DIM), F32),
                        pltpu.VMEM(wt_all.shape[1:], BF16)],
        compiler_params=_params(("arbitrary",)),
        name="mix_proj",
    )(*args)


def _cache_kv_kernel(ckv_ref, kr_ref, wk_ref, wvt_ref, kcat_ref, vt_ref):
    n = DEC_BATCH * PAST_LEN
    ckvb = ckv_ref[...].reshape(n, MLA_KV_LORA).astype(BF16)
    kn = _dot(ckvb, wk_ref[...])
    vt_ref[...] = _dot_nt(wvt_ref[...], ckvb).astype(BF16)
    kr = kr_ref[...].reshape(n, HEAD_PAD)
    for h in range(MLA_HEADS):
        sl = slice(h * HEAD_PAD, (h + 1) * HEAD_PAD)
        kcat_ref[:, sl] = (kn[:, sl] + kr).astype(BF16)


def _cache_kv(cache_ckv, cache_kr_pad, wk, wvt):
    n = DEC_BATCH * PAST_LEN
    return pl.pallas_call(
        _cache_kv_kernel,
        grid=(DEPTH,),
        in_specs=[pl.BlockSpec((DEC_BATCH, None, PAST_LEN, MLA_KV_LORA), lambda l: (0, l, 0, 0)),
                  pl.BlockSpec((DEC_BATCH, None, PAST_LEN, HEAD_PAD), lambda l: (0, l, 0, 0)),
                  pl.BlockSpec((None, MLA_KV_LORA, MLA_W), lambda l: (l, 0, 0)),
                  pl.BlockSpec((None, MLA_V_W, MLA_KV_LORA), lambda l: (l, 0, 0))],
        out_specs=[pl.BlockSpec((None, n, MLA_W), lambda l: (l, 0, 0)),
                   pl.BlockSpec((None, MLA_V_W, n), lambda l: (l, 0, 0))],
        out_shape=[jax.ShapeDtypeStruct((DEPTH, n, MLA_W), BF16), jax.ShapeDtypeStruct((DEPTH, MLA_V_W, n), BF16)],
        compiler_params=_params(("arbitrary",)),
        name="cache_kv",
    )(cache_ckv, cache_kr_pad, wk, wvt)


def _ret_kernel(*refs, t, hp, latent, carry, layer):
    if latent:
        lg_ref, q_ref, k_ref, v_ref, g_ref, s0f_ref, s0b_ref, o_ref, d_ref = refs
    elif carry:
        lg_ref, q_ref, k_ref, v_ref, g_ref, sf_prev_ref, sb_prev_ref, o_ref, sf_ref, sb_ref, d_ref = refs
    else:
        lg_ref, q_ref, k_ref, v_ref, g_ref, o_ref, sf_ref, sb_ref, d_ref = refs
    hblk = pl.program_id(0)

    @pl.when(pl.program_id(1) == 0)
    def _():
        diff = (lax.broadcasted_iota(jnp.int32, (t, t), 0) - lax.broadcasted_iota(jnp.int32, (t, t), 1)).astype(F32)
        for hh in range(hp):
            lgf = lg_ref[layer, 0, hblk * hp + hh]
            lgb = lg_ref[layer, 1, hblk * hp + hh]
            d_ref[hh] = jnp.exp(jnp.where(diff >= 0, diff * lgf, -diff * lgb))

    if carry:
        sf_ref[0] = sf_prev_ref[...]
        sb_ref[0] = sb_prev_ref[...]
    pos = lax.broadcasted_iota(jnp.int32, (t, 1), 0).astype(F32)
    for hh in range(hp):
        lgf = lg_ref[layer, 0, hblk * hp + hh]
        lgb = lg_ref[layer, 1, hblk * hp + hh]
        q = q_ref[:, hh * RET_DK:(hh + 1) * RET_DK]
        k = k_ref[:, hh * RET_DK:(hh + 1) * RET_DK]
        v = v_ref[:, hh * RET_DV:(hh + 1) * RET_DV]
        p = (_dot_nt(q, k) * d_ref[hh]).astype(BF16)
        o = _dot(p, v)
        if latent:
            o = o + jnp.exp((pos + 1.0) * lgf) * _dot(q, s0f_ref[hh].astype(BF16))
            o = o + jnp.exp((t - pos) * lgb) * _dot(q, s0b_ref[hh].astype(BF16))
        else:
            kf = k.astype(F32)
            sf = _dot_tn((kf * jnp.exp((t - 1.0 - pos) * lgf)).astype(BF16), v)
            sb = _dot_tn((kf * jnp.exp(pos * lgb)).astype(BF16), v)
            if carry:
                sf_ref[1, hh] = sf
                sb_ref[1, hh] = sb
            else:
                sf_ref[hh] = sf
                sb_ref[hh] = sb
        o_ref[:, hh * RET_DV:(hh + 1) * RET_DV] = (g_ref[:, hh * RET_DV:(hh + 1) * RET_DV] * _norm_rows(o)).astype(BF16)


def _retention(log_g, slab, states, *, latent, layer):
    t = DEC_SEQ if latent else SEQ
    hp = 2 if latent else RET_HEADS
    n_seq = DEC_BATCH if latent else BATCH
    row0 = (N_CTX // t) if latent else 0

    def row(name, w):
        assert SLAB_COLS[name][0] % (hp * w) == 0
        first = SLAB_COLS[name][0] // (hp * w)
        return pl.BlockSpec((t, hp * w), lambda h, s: (row0 + s, first + h))

    smem = pl.BlockSpec(memory_space=pltpu.SMEM)
    out_shape = [jax.ShapeDtypeStruct((n_seq * t, RET_HEADS * RET_DV), BF16)]
    out_specs = [pl.BlockSpec((t, hp * RET_DV), lambda h, s: (s, h))]
    in_specs = [smem, row("rq", RET_DK), row("rk", RET_DK), row("rv", RET_DV), row("rg", RET_DV)]
    args = [log_g, slab, slab, slab, slab]
    carry = False
    if latent:
        st = pl.BlockSpec((None, None, hp, RET_DK, RET_DV), lambda h, s: (s, layer, h, 0, 0))
        in_specs += [st, st]
        args += list(states)
    else:
        st = pl.BlockSpec((None, hp, RET_DK, RET_DV), lambda h, s: (s, h, 0, 0))
        if states is None:
            out_specs += [st, st]
            out_shape += [jax.ShapeDtypeStruct((BATCH, RET_HEADS, RET_DK, RET_DV), F32)] * 2
        else:
            carry = True
            in_specs += [st, st]
            args += list(states)
            st2 = pl.BlockSpec((None, DEPTH, hp, RET_DK, RET_DV), lambda h, s: (s, 0, h, 0, 0))
            out_specs += [st2, st2]
            out_shape += [jax.ShapeDtypeStruct((BATCH, DEPTH, RET_HEADS, RET_DK, RET_DV), F32)] * 2

    return pl.pallas_call(
        functools.partial(_ret_kernel, t=t, hp=hp, latent=latent, carry=carry, layer=layer),
        grid=(RET_HEADS // hp, n_seq),
        in_specs=in_specs,
        out_specs=out_specs,
        out_shape=out_shape,
        scratch_shapes=[pltpu.VMEM((hp, t, t), F32)],
        compiler_params=_params(("arbitrary", "arbitrary")),
        name="retention_lat" if latent else "retention_ctx",
    )(*args)


def _attn_kernel(*refs, t, hp, latent):
    if latent:
        q_ref, k_ref, vt_ref, kc_ref, vtc_ref, o_ref = refs
    else:
        q_ref, k_ref, vt_ref, o_ref = refs
    qb = min(ATTN_QBLOCK, t)
    units = [(slice(hh * HEAD_PAD, (hh + 1) * HEAD_PAD), slice(hh * MLA_D_V, (hh + 1) * MLA_D_V),
              slice(b * qb, (b + 1) * qb)) for b in range(t // qb) for hh in range(hp)]
    def scores(unit):
        sl, _, rows = unit
        q = q_ref[rows, sl]
        s = [_dot_nt(k_ref[:, sl], q)]
        if latent:
            s.append(_dot_nt(kc_ref[:, sl], q))
        return s

    def softmax(s):
        m = functools.reduce(jnp.maximum, [jnp.max(x, axis=0, keepdims=True) for x in s])
        e = [jnp.exp2(x - m) for x in s]
        den = functools.reduce(jnp.add, [jnp.sum(x, axis=0, keepdims=True) for x in e])
        return [x.astype(BF16) for x in e], den

    def values(unit, e, den):
        _, vs, _ = unit
        o = _dot(vt_ref[vs, :], e[0])
        if latent:
            o = o + _dot(vtc_ref[vs, :], e[1])
        return o / den

    pairs = [units[u:u + 2] for u in range(0, len(units), 2)]
    s_next = [scores(u) for u in pairs[0]]
    sm_prev = None
    for g in range(len(pairs) + 1):
        s_cur = s_next
        if g + 1 < len(pairs):
            s_next = [scores(u) for u in pairs[g + 1]]
        sm_cur = [softmax(s) for s in s_cur] if g < len(pairs) else None
        if sm_prev is not None:
            (_, vs0, rows), (_, vs1, _) = pairs[g - 1]
            outs = [values(u, e, den) for u, (e, den) in zip(pairs[g - 1], sm_prev)]
            o_ref[rows, vs0.start:vs1.stop] = jnp.concatenate(outs, axis=0).T.astype(BF16)
        sm_prev = sm_cur


def _attention(slab, vt, cache, *, latent, layer):
    t = DEC_SEQ if latent else SEQ
    hp = 4 if latent else MLA_HEADS
    n_seq = DEC_BATCH if latent else BATCH
    row0 = (N_CTX // t) if latent else 0

    def row(name):
        assert SLAB_COLS[name][0] % (hp * HEAD_PAD) == 0
        first = SLAB_COLS[name][0] // (hp * HEAD_PAD)
        return pl.BlockSpec((t, hp * HEAD_PAD), lambda s, h: (row0 + s, first + h))

    col = pl.BlockSpec((hp * MLA_D_V, t), lambda s, h: (h, row0 + s))
    in_specs = [row("q"), row("kcat"), col]
    args = [slab, slab, vt]
    if latent:
        in_specs += [pl.BlockSpec((None, PAST_LEN, hp * HEAD_PAD), lambda s, h: (layer, s, h)),
                     pl.BlockSpec((None, hp * MLA_D_V, PAST_LEN), lambda s, h: (layer, h, s))]
        args += list(cache)
    return pl.pallas_call(
        functools.partial(_attn_kernel, t=t, hp=hp, latent=latent),
        grid=(n_seq, MLA_HEADS // hp),
        in_specs=in_specs,
        out_specs=pl.BlockSpec((t, hp * MLA_D_V), lambda s, h: (s, h)),
        out_shape=jax.ShapeDtypeStruct((n_seq * t, MLA_V_W), BF16),
        compiler_params=_params(("arbitrary", "arbitrary")),
        name="attention_lat" if latent else "attention_ctx",
    )(*args)


def _merge_kernel(x_ref, mod_ref, a_ref, rc_ref, rl_ref, mc_ref, ml_ref, sig_ref, wc_ref, wr_ref, wm_ref, wo_ref,
                  g_ref, b_ref, o_ref, wc_bf, wr_bf, wm_bf, wo_bf, *, tm):
    i = pl.program_id(0)
    is_ctx = i < N_WSTEPS + N_CTX // tm

    @pl.when(i < N_WSTEPS)
    def _():
        _stage_chunk(i, wc_ref, wc_bf)
        _stage_chunk(i, wr_ref, wr_bf)
        _stage_chunk(i, wo_ref, wo_bf)
        _stage_chunk(i, wm_ref, wm_bf)

    @pl.when(i >= N_WSTEPS)
    def _():
        for half in range(ROW_HALVES):
            rows = slice(half * tm // ROW_HALVES, (half + 1) * tm // ROW_HALVES)
            x = x_ref[rows, :]
            r = jnp.where(is_ctx, rc_ref[rows, :], rl_ref[rows, :])
            m = jnp.where(is_ctx, mc_ref[rows, :], ml_ref[rows, :])
            merged = sig_ref[rows, 0:D_MODEL] * _dot(a_ref[rows, :], wc_bf[...])
            merged = merged + sig_ref[rows, D_MODEL:2 * D_MODEL] * _dot(r, wr_bf[...])
            merged = merged + sig_ref[rows, 2 * D_MODEL:] * _dot(m, wm_bf[...])
            y = _dot(merged.astype(BF16), wo_bf[...])
            z = DEEPNORM_ALPHA * x + mod_ref[5:6, :] * y
            o_ref[rows, :] = _norm_rows(z) * g_ref[...] + b_ref[...]


def _merge(x, mods_all, a, r_pair, m_pair, sig, wc, wr, wm, wo, ln_g, ln_b, *, layer, tm=512):
    def row(w):
        return _row_spec(tm, w, N_WSTEPS)

    def pair(w):
        return [_ctx_row_spec(tm, w, N_WSTEPS), _lat_row_spec(tm, w, N_WSTEPS)]

    ln_index = layer * 3 + 1
    return pl.pallas_call(
        functools.partial(_merge_kernel, tm=tm),
        grid=(N_WSTEPS + N_TOK // tm,),
        in_specs=[row(D_MODEL), _mod_spec(tm, layer, N_WSTEPS), row(CONV_DIM)] + pair(RET_HEADS * RET_DV)
        + pair(MLA_V_W) + [row(GATE_W), _wchunk_spec(wc, layer), _wchunk_spec(wr, layer), _wchunk_spec(wm, layer),
                         _wchunk_spec(wo, layer), _layer_row_spec(ln_g, ln_index), _layer_row_spec(ln_b, ln_index)],
        out_specs=row(D_MODEL),
        out_shape=jax.ShapeDtypeStruct((N_TOK, D_MODEL), F32),
        scratch_shapes=[pltpu.VMEM((CONV_DIM, D_MODEL), BF16), pltpu.VMEM((RET_HEADS * RET_DV, D_MODEL), BF16),
                        pltpu.VMEM((MLA_V_W, D_MODEL), BF16), pltpu.VMEM((D_MODEL, D_MODEL), BF16)],
        compiler_params=_params(("arbitrary",)),
        name="merge",
    )(x, mods_all, a, *r_pair, *m_pair, sig, wc, wr, wm, wo, ln_g, ln_b)


def _rope_tables():
    f32 = np.float32
    rows = DEC_SEQ // GRID_W
    row_id = np.repeat(np.arange(rows, dtype=f32), GRID_W)
    col_id = np.tile(np.arange(GRID_W, dtype=f32), rows)
    inv_freq = (f32(ROPE_BASE) ** (-np.arange(ROPE_AXIS_HALF, dtype=f32) / f32(ROPE_AXIS_HALF))).astype(f32)
    ang = np.stack([row_id[:, None] * inv_freq, col_id[:, None] * inv_freq], axis=1).astype(f32)
    cos = np.cos(ang).astype(f32)
    sin = np.sin(ang).astype(f32)
    cos32 = np.stack([cos, cos], axis=2).reshape(DEC_SEQ, MLA_D_ROPE)
    zero = np.zeros_like(sin)
    sin_lo32 = np.stack([-sin, zero], axis=2).reshape(DEC_SEQ, MLA_D_ROPE)
    sin_hi32 = np.stack([zero, sin], axis=2).reshape(DEC_SEQ, MLA_D_ROPE)
    tail = HEAD_PAD - ROPE_LANE0 - MLA_D_ROPE
    cos_t = np.concatenate([np.ones((DEC_SEQ, ROPE_LANE0), f32), cos32, np.ones((DEC_SEQ, tail), f32)], axis=1)
    cos_t = np.concatenate([np.ones((DEC_SEQ, HEAD_PAD), f32), cos_t], axis=0)

    def sin_table(s32):
        return np.pad(s32, ((DEC_SEQ, 0), (ROPE_LANE0, tail)))

    return tuple(jnp.asarray(t, dtype=F32) for t in (cos_t, sin_table(sin_lo32), sin_table(sin_hi32)))


def _head_pad_cols(w, width):
    k = w.shape[0]
    w = w.reshape(k, MLA_HEADS, width)
    return jnp.pad(w, ((0, 0), (0, 0), (0, HEAD_PAD - width))).reshape(k, MLA_W)


def kernel(x_prompt, x_sample, cache_mla_ckv, cache_mla_krope, state_ret_fwd, state_ret_bwd, c, c_ctx, ada_w, ada_b, ffn1_w_in, ffn1_w_out, ffn2_w_in, ffn2_w_out, post_ln_g, post_ln_b, mix_w_in, conv_w_dw, conv_b_dw, conv_ln_g, conv_ln_b, conv_w_out, ret_decay_fwd, ret_decay_bwd, ret_w_out, mla_q_norm, mla_w_uq, mla_kv_norm, mla_w_ukv, mla_w_out, mix_w_o):
    assert DEPTH == 2
    cvec = jnp.concatenate([c_ctx[None, :], c, jnp.zeros((N_MOD_ROWS - 1 - DEC_BATCH, D_MODEL), F32)], axis=0)
    mods_all = _ada_mods(cvec, ada_w, ada_b).reshape(DEPTH, N_MOD_ROWS, N_MODS, D_MODEL)
    rope_tabs = _rope_tables()

    w_ukv = mla_w_ukv.reshape(DEPTH, MLA_KV_LORA, MLA_HEADS, MLA_D_NOPE + MLA_D_V)
    wk_all = _head_pad_cols(w_ukv[..., :MLA_D_NOPE].reshape(DEPTH * MLA_KV_LORA, -1), MLA_D_NOPE)
    wk_all = wk_all.reshape(DEPTH, MLA_KV_LORA, MLA_W).astype(BF16)
    wvt_all = jnp.swapaxes(w_ukv[..., MLA_D_NOPE:].reshape(DEPTH, MLA_KV_LORA, MLA_V_W), 1, 2).astype(BF16)
    kr_tail = HEAD_PAD - ROPE_LANE0 - MLA_D_ROPE
    cache_kr_pad = jnp.pad(cache_mla_krope, ((0, 0), (0, 0), (0, 0), (ROPE_LANE0, kr_tail)))
    kcat_c, vt_c = _cache_kv(cache_mla_ckv, cache_kr_pad, wk_all, wvt_all)
    log_g = jnp.stack([jax.nn.log_sigmoid(ret_decay_fwd), jax.nn.log_sigmoid(ret_decay_bwd)], axis=1)
    wt_all = jnp.swapaxes(mix_w_in, 1, 2)
    wuq_all = _head_pad_cols(mla_w_uq.reshape(DEPTH * MLA_Q_LORA, -1), MLA_D_NOPE + MLA_D_ROPE)
    wuq_all = wuq_all.reshape(DEPTH, MLA_Q_LORA, MLA_W).astype(BF16)
    gq_all = mla_q_norm.reshape(DEPTH, 1, MLA_Q_LORA)
    gkv_all = mla_kv_norm.reshape(DEPTH, 1, MLA_KV_LORA)
    conv_w = (conv_w_dw, conv_b_dw.reshape(DEPTH, 1, CONV_DIM), conv_ln_g.reshape(DEPTH, 1, CONV_DIM),
              conv_ln_b.reshape(DEPTH, 1, CONV_DIM))

    ln_g = post_ln_g.reshape(DEPTH * 3, 1, D_MODEL)
    ln_b = post_ln_b.reshape(DEPTH * 3, 1, D_MODEL)
    xs = (x_prompt.reshape(N_CTX, D_MODEL), x_sample.reshape(N_LAT, D_MODEL))
    ctx_carry = None
    state_carry = None
    for l in range(DEPTH):
        last = l == DEPTH - 1
        x = _ffn(xs, mods_all, ffn1_w_in, ffn1_w_out, ln_g, ln_b, layer=l, which=0)[0]

        a, slab, vt, ckv, kr = _proj(x, mods_all, rope_tabs, wt_all, wuq_all, wk_all, wvt_all, gq_all, gkv_all, conv_w,
                                     ctx_carry, layer=l)
        ctx_carry = (ckv, kr)

        r_ctx, sf, sb = _retention(log_g, slab, state_carry, latent=False, layer=l)
        state_carry = (sf, sb)
        r_lat = _retention(log_g, slab, (state_ret_fwd, state_ret_bwd), latent=True, layer=l)[0]
        m_ctx = _attention(slab, vt, None, latent=False, layer=l)
        m_lat = _attention(slab, vt, (kcat_c, vt_c), latent=True, layer=l)

        x = _merge(x, mods_all, a, (r_ctx, r_lat), (m_ctx, m_lat), slab, conv_w_out, ret_w_out, mla_w_out, mix_w_o,
                   ln_g, ln_b, layer=l)
        xs = _ffn((x,), mods_all, ffn2_w_in, ffn2_w_out, ln_g, ln_b, layer=l, which=2, split_out=last)

    y_ctx, y_lat = xs
    return (y_ctx.reshape(BATCH, SEQ, D_MODEL), y_lat.reshape(DEC_BATCH, DEC_SEQ, D_MODEL),
            ctx_carry[0], ctx_carry[1], state_carry[0], state_carry[1])
```

```python
import functools
import math

import jax
import jax.numpy as jnp
import numpy as np
from jax import lax
from jax.experimental import pallas as pl
from jax.experimental.pallas import tpu as pltpu

F32 = jnp.float32
BF16 = jnp.bfloat16

D_MODEL = 1024
BATCH = 16
SEQ = 256
DEPTH = 2
DEC_BATCH = 4
DEC_SEQ = 1024
PAST_LEN = 256
GRID_W = 64
D_FF = 2816
N_MODS = 9
CONV_DIM = 512
CONV_WIDTH = 31
RET_HEADS = 4
RET_DK = 128
RET_DV = 256
MLA_HEADS = 8
MLA_Q_LORA = 512
MLA_KV_LORA = 256
MLA_D_NOPE = 64
MLA_D_ROPE = 32
MLA_D_V = 64
ROPE_AXIS_HALF = MLA_D_ROPE // 4
ROPE_BASE = 10000.0
DEEPNORM_ALPHA = (2 * DEPTH) ** 0.25
LN_EPS = 1e-5
RMS_EPS = 1e-6

N_CTX = BATCH * SEQ
N_LAT = DEC_BATCH * DEC_SEQ
N_TOK = N_CTX + N_LAT
N_MOD_ROWS = 8
HEAD_PAD = 128
ROPE_LANE0 = MLA_D_NOPE
MLA_W = MLA_HEADS * HEAD_PAD
MLA_V_W = MLA_HEADS * MLA_D_V
MAIN_W = 2 * CONV_DIM + 2 * RET_HEADS * RET_DK + 2 * RET_HEADS * RET_DV + MLA_Q_LORA + MLA_KV_LORA
N_BRANCHES = 3
GATE_W = N_BRANCHES * D_MODEL
SLAB_COLS = {}
for _name, _w in (("sig", GATE_W), ("rq", RET_HEADS * RET_DK), ("rk", RET_HEADS * RET_DK), ("rv", RET_HEADS * RET_DV),
                  ("rg", RET_HEADS * RET_DV), ("q", MLA_W), ("kcat", MLA_W)):
    SLAB_COLS[_name] = (sum(w for _, w in SLAB_COLS.values()), _w)
SLAB_W = sum(w for _, w in SLAB_COLS.values())
SUBLANES = 8
CONV_HALO = 16
CONV_BLOCK = 256
VMEM_LIMIT = 56 * 1024 * 1024
N_WSTEPS = 8
PROJ_WSTEPS = 6
MXU_COLS = 256
FFN_CHUNKS = ((0, 6 * MXU_COLS), (6 * MXU_COLS, D_FF))
ROW_HALVES = 2
ATTN_QBLOCK = 256
ATTN_QSCALE = (MLA_D_NOPE + MLA_D_ROPE) ** -0.5 * math.log2(math.e)


def _dot(a, b):
    return jnp.dot(a, b, preferred_element_type=F32)


def _dot_nt(a, b):
    return lax.dot_general(a, b, (((1,), (1,)), ((), ())), preferred_element_type=F32)


def _dot_tn(a, b):
    return lax.dot_general(a, b, (((0,), (0,)), ((), ())), preferred_element_type=F32)


def _sigmoid(x):
    return 1.0 / (1.0 + jnp.exp(-x))


def _norm_rows(z):
    mu = jnp.mean(z, axis=-1, keepdims=True)
    zc = z - mu
    var = jnp.mean(zc * zc, axis=-1, keepdims=True)
    return zc * lax.rsqrt(var + LN_EPS)


def _rms_rows(z):
    return z * lax.rsqrt(jnp.mean(z * z, axis=-1, keepdims=True) + RMS_EPS)


def _resident(shape):
    zeros = (0,) * len(shape)
    return pl.BlockSpec(shape, lambda *_: zeros, pipeline_mode=pl.Buffered(1))


def _layer_resident(w, layer):
    zeros = (0,) * (w.ndim - 1)
    return pl.BlockSpec((None,) + w.shape[1:], lambda *_: (layer,) + zeros, pipeline_mode=pl.Buffered(1))


def _tile(i, n_w, tm):
    return jnp.clip(i - n_w, 0, N_TOK // tm - 1)


def _row_spec(tm, w, n_w):
    return pl.BlockSpec((tm, w), lambda i: (_tile(i, n_w, tm), 0))


def _ctx_row_spec(tm, w, n_w):
    last = N_CTX // tm - 1
    return pl.BlockSpec((tm, w), lambda i: (jnp.minimum(_tile(i, n_w, tm), last), 0))


def _lat_row_spec(tm, w, n_w):
    first = N_CTX // tm
    return pl.BlockSpec((tm, w), lambda i: (jnp.maximum(_tile(i, n_w, tm) - first, 0), 0))


def _mod_spec(tm, layer, n_w):
    n_ctx_tiles = N_CTX // tm
    tiles_per_seq = DEC_SEQ // tm

    def index(i):
        j = _tile(i, n_w, tm)
        return (layer, jnp.where(j < n_ctx_tiles, 0, 1 + (j - n_ctx_tiles) // tiles_per_seq), 0, 0)

    return pl.BlockSpec((None, None, N_MODS, D_MODEL), index)


def _wchunk_spec(w, layer, n_w=N_WSTEPS):
    _, rows, cols = w.shape
    return pl.BlockSpec((None, rows // n_w, cols), lambda i: (layer, jnp.minimum(i, n_w - 1), 0))


def _stage_chunk(i, src_ref, dst_ref):
    rows = src_ref.shape[0]
    dst_ref[pl.ds(pl.multiple_of(i * rows, rows), rows), :] = src_ref[...].astype(BF16)


def _params(semantics):
    return pltpu.CompilerParams(dimension_semantics=semantics, vmem_limit_bytes=VMEM_LIMIT)


def _ada_kernel(c_ref, w_ref, b_ref, o_ref):
    c = c_ref[...]
    h = (c * _sigmoid(c)).astype(BF16)
    o_ref[...] = _dot(h, w_ref[...].astype(BF16)) + b_ref[pl.ds(pl.program_id(0), 1), :]


def _ada_mods(cvec, ada_w, ada_b):
    n_out = N_MODS * D_MODEL
    tn = n_out // 4
    return pl.pallas_call(
        _ada_kernel,
        grid=(DEPTH, n_out // tn),
        in_specs=[
            pl.BlockSpec((N_MOD_ROWS, D_MODEL), lambda l, j: (0, 0)),
            pl.BlockSpec((None, D_MODEL, tn), lambda l, j: (l, 0, j)),
            pl.BlockSpec((DEPTH, tn), lambda l, j: (0, j)),
        ],
        out_specs=pl.BlockSpec((None, N_MOD_ROWS, tn), lambda l, j: (l, 0, j)),
        out_shape=jax.ShapeDtypeStruct((DEPTH, N_MOD_ROWS, n_out), F32),
        compiler_params=_params(("arbitrary", "arbitrary")),
        name="ada_mods",
    )(cvec, ada_w, ada_b)


def _ffn_kernel(*refs, layer, which, tm, n_x, n_out):
    base = N_MODS // 3 * which
    x_refs = refs[:n_x]
    mod_ref, win_ref, wout_ref, g_ref, b_ref = refs[n_x:n_x + 5]
    g_ref, b_ref = (r.at[layer, which:which + 1, :] for r in (g_ref, b_ref))
    o_refs = refs[n_x + 5:n_x + 5 + n_out]
    win_bf, wout_bf = refs[n_x + 5 + n_out:]
    i = pl.program_id(0)
    is_ctx = i < N_WSTEPS + N_CTX // tm

    @pl.when(i < N_WSTEPS)
    def _():
        _stage_chunk(i, win_ref, win_bf)
        _stage_chunk(i, wout_ref, wout_bf)

    @pl.when(i >= N_WSTEPS)
    def _():
        shift = mod_ref[base:base + 1, :]
        scale = mod_ref[base + 1:base + 2, :]
        gate = mod_ref[base + 2:base + 3, :]
        halves = [slice(k * tm // ROW_HALVES, (k + 1) * tm // ROW_HALVES) for k in range(ROW_HALVES)]
        results = []
        for rows in halves:
            if n_x == 2:
                x = jnp.where(is_ctx, x_refs[0][rows, :], x_refs[1][rows, :])
            else:
                x = x_refs[0][rows, :]
            h = (x * (1.0 + scale) + shift).astype(BF16)
            y = None
            for lo, hi in FFN_CHUNKS:
                g = _dot(h, win_bf[:, lo:hi])
                u = _dot(h, win_bf[:, D_FF + lo:D_FF + hi])
                a = (g * _sigmoid(g) * u).astype(BF16)
                yc = _dot(a, wout_bf[lo:hi, :])
                y = yc if y is None else y + yc
            z = DEEPNORM_ALPHA * x + 0.5 * gate * y
            res = _norm_rows(z) * g_ref[...] + b_ref[...]
            if n_out == 1:
                o_refs[0][rows, :] = res
            results.append(res)
        if n_out == 2:
            @pl.when(is_ctx)
            def _():
                for rows, res in zip(halves, results):
                    o_refs[0][rows, :] = res

            @pl.when(jnp.logical_not(is_ctx))
            def _():
                for rows, res in zip(halves, results):
                    o_refs[1][rows, :] = res


def _ffn(xs, mods_all, w_in, w_out, ln_g, ln_b, *, layer, which, split_out=False, tm=512):
    row = _row_spec(tm, D_MODEL, N_WSTEPS)
    pair = [_ctx_row_spec(tm, D_MODEL, N_WSTEPS), _lat_row_spec(tm, D_MODEL, N_WSTEPS)]
    if split_out:
        out_specs = pair
        out_shape = [jax.ShapeDtypeStruct((N_CTX, D_MODEL), F32), jax.ShapeDtypeStruct((N_LAT, D_MODEL), F32)]
    else:
        out_specs = [row]
        out_shape = [jax.ShapeDtypeStruct((N_TOK, D_MODEL), F32)]
    return pl.pallas_call(
        functools.partial(_ffn_kernel, layer=layer, which=which, tm=tm, n_x=len(xs), n_out=len(out_specs)),
        grid=(N_WSTEPS + N_TOK // tm,),
        in_specs=(pair if len(xs) == 2 else [row]) + [
            _mod_spec(tm, layer, N_WSTEPS), _wchunk_spec(w_in, layer), _wchunk_spec(w_out, layer),
            _resident(ln_g.shape), _resident(ln_b.shape)],
        out_specs=out_specs,
        out_shape=out_shape,
        scratch_shapes=[pltpu.VMEM((D_MODEL, 2 * D_FF), BF16), pltpu.VMEM((D_FF, D_MODEL), BF16)],
        compiler_params=_params(("arbitrary",)),
        name="ffn",
    )(*xs, mods_all, w_in, w_out, ln_g, ln_b)


def _conv_stages(prev_halo, main, next_halo, w_ref, b_ref, g_ref, beta_ref, o_ref, pad_ref, shift_ref, acc_ref):
    rows = 32
    lanes = 128
    first = CONV_HALO - CONV_WIDTH // 2

    def setup(after):
        del after
        pad_ref[0:CONV_HALO, :] = prev_halo()
        pad_ref[CONV_HALO:CONV_HALO + CONV_BLOCK, :] = main()
        pad_ref[CONV_HALO + CONV_BLOCK:, :] = next_halo()
        span = shift_ref.shape[1]
        for ph in range(SUBLANES):
            shift_ref[ph] = pad_ref[ph:ph + span, :]

    def taps(c, r, start_from):
        cs = slice(c * lanes, (c + 1) * lanes)
        acc = jnp.broadcast_to(b_ref[:, cs], (rows, lanes)) + start_from
        for j in range(CONV_WIDTH):
            ph = (first + j) % SUBLANES
            start = r * rows + (first + j) - ph
            acc = acc + w_ref[j:j + 1, cs] * shift_ref[ph, start:start + rows, cs]
        acc_ref[r * rows:(r + 1) * rows, cs] = acc

    def finish(after):
        del after
        y = _norm_rows(acc_ref[...]) * g_ref[...] + beta_ref[...]
        o_ref[...] = (y * _sigmoid(y)).astype(BF16)

    def lane_group_half(c, half, after):
        if after is None:
            start_from = jnp.zeros((rows, lanes), F32)
        else:
            bits = lax.bitcast_convert_type(after, jnp.int32)
            zero = lax.shift_right_logical(lax.shift_right_logical(bits, 16), 16).astype(F32)
            start_from = jnp.concatenate([zero] * (rows // SUBLANES), axis=0)
        per_half = CONV_BLOCK // rows // 2
        for r in range(half * per_half, (half + 1) * per_half):
            taps(c, r, start_from)

    chunks = [functools.partial(lane_group_half, c, half) for c in range(CONV_DIM // lanes) for half in range(2)]
    return [setup] + chunks + [finish]


def _proj_kernel(*refs, tm, carry, layer):
    (x_ref, mod_ref, cos_ref, sin_lo_ref, sin_hi_ref, wt_ref, wuq_ref, place_ref, wk_ref, wvt_ref, gq_ref,
     gkv_ref, cw_ref, cb_ref, cg_ref, cbeta_ref) = refs[:16]
    refs = refs[16:]
    gq_ref, gkv_ref, cb_ref, cg_ref, cbeta_ref = (
        r.at[layer:layer + 1, :] for r in (gq_ref, gkv_ref, cb_ref, cg_ref, cbeta_ref))
    if carry:
        ckv_prev_ref, kr_prev_ref = refs[:2]
        refs = refs[2:]
    (a_ref, slab_ref, vt_ref, ckv_ref, kr_ref, glu_ring, pad_ref, shift_ref, acc_ref, wt_bf, wuq_bf) = refs
    sig_ref, rq_ref, rk_ref, rv_ref, rg_ref, q_ref, kcat_ref = (
        slab_ref.at[:, lo:lo + w] for lo, w in (SLAB_COLS[n] for n in ("sig", "rq", "rk", "rv", "rg", "q", "kcat")))
    t = pl.program_id(0) - PROJ_WSTEPS
    n_tiles = N_TOK // tm
    n_ctx_tiles = N_CTX // tm
    tiles_per_seq = DEC_SEQ // tm
    is_ctx = t < n_ctx_tiles
    pos = (t - 1 - n_ctx_tiles) % tiles_per_seq
    conv_latent = t - 1 >= n_ctx_tiles
    has_prev = jnp.logical_and(conv_latent, pos != 0)
    has_next = jnp.logical_and(conv_latent, pos != tiles_per_seq - 1)
    slot = t % 2
    no_halo = jnp.zeros((CONV_HALO, CONV_DIM), F32)

    def conv_prev_tile(next_rows):
        return _conv_stages(
            lambda: jnp.where(has_prev, glu_ring[slot, tm - CONV_HALO:tm, :], no_halo),
            lambda: glu_ring[1 - slot],
            lambda: jnp.where(has_next, next_rows, no_halo),
            cw_ref, cb_ref, cg_ref, cbeta_ref, a_ref, pad_ref, shift_ref, acc_ref)

    @pl.when(t < 0)
    def _():
        _stage_chunk(pl.program_id(0), wt_ref, wt_bf)

    @pl.when(t == 0)
    def _():
        glu_ring[...] = jnp.zeros(glu_ring.shape, F32)
        wuq_bf[...] = _dot(wuq_ref[...].astype(BF16), place_ref[...]).astype(BF16)

    @pl.when(t == n_tiles)
    def _():
        for step in conv_prev_tile(no_halo):
            step(None)

    @pl.when(jnp.logical_and(t >= 0, t < n_tiles))
    def _():
        _proj_tile(x_ref, mod_ref, cos_ref, sin_lo_ref, sin_hi_ref, wt_bf, wuq_bf, wk_ref, wvt_ref, gq_ref,
                   gkv_ref, ckv_prev_ref if carry else None, kr_prev_ref if carry else None,
                   rq_ref, rk_ref, rv_ref, rg_ref, q_ref, kcat_ref, vt_ref, sig_ref, ckv_ref, kr_ref,
                   glu_ring, conv_prev_tile, slot, is_ctx, tm)


def _proj_tile(x_ref, mod_ref, cos_ref, sin_lo_ref, sin_hi_ref, wt_ref, wuq_ref, wk_ref, wvt_ref, gq_ref,
               gkv_ref, ckv_prev_ref, kr_prev_ref,
               rq_ref, rk_ref, rv_ref, rg_ref, q_ref, kcat_ref, vt_ref, sig_ref, ckv_ref, kr_ref,
               glu_ring, conv_prev_tile, slot, is_ctx, tm):
    carry = ckv_prev_ref is not None
    x = x_ref[...]
    u = (x * (1.0 + mod_ref[4:5, :]) + mod_ref[3:4, :]).astype(BF16)
    widths = (CONV_DIM, CONV_DIM, RET_HEADS * RET_DK, RET_HEADS * RET_DK, RET_HEADS * RET_DV, RET_HEADS * RET_DV,
              MLA_Q_LORA, MLA_KV_LORA)
    starts = [sum(widths[:n]) for n in range(len(widths))]

    def proj(n):
        return _dot_nt(u, wt_ref[starts[n]:starts[n] + widths[n], :])

    glu = proj(0) * _sigmoid(proj(1))
    conv_steps = conv_prev_tile(glu[0:CONV_HALO, :])

    def conv(n, result):
        for _ in range(n):
            conv_steps.pop(0)(result[0:SUBLANES, 0:128])

    conv(1, glu)
    glu_ring[slot] = glu

    mq = proj(6)
    mkv = proj(7)
    conv(2, mq)
    cos = cos_ref[...]
    sin_lo = sin_lo_ref[...]
    sin_hi = sin_hi_ref[...]

    def rotary(v):
        up = pltpu.roll(v, HEAD_PAD - ROPE_AXIS_HALF, 1)
        down = pltpu.roll(v, ROPE_AXIS_HALF, 1)
        return v * cos + up * sin_lo + down * sin_hi

    qn = (_rms_rows(mq) * gq_ref[...]).astype(BF16)
    qm = _dot(qn, wuq_ref[...])
    ckv = _rms_rows(mkv) * gkv_ref[...]
    ckvb = ckv.astype(BF16)
    kn = _dot(ckvb, wk_ref[...])
    vt_ref[...] = _dot_nt(wvt_ref[...], ckvb).astype(BF16)
    kr_grp = _dot_nt(u, wt_ref[MAIN_W:MAIN_W + HEAD_PAD, :])
    conv(2, kn)
    lane = lax.broadcasted_iota(jnp.int32, kr_grp.shape, 1)
    in_rope = jnp.logical_and(lane >= ROPE_LANE0, lane < ROPE_LANE0 + MLA_D_ROPE)
    kr = jnp.where(in_rope, pltpu.roll(kr_grp, ROPE_LANE0, 1), 0.0)
    kr_rot = rotary(kr)
    for h in range(MLA_HEADS):
        sl = slice(h * HEAD_PAD, (h + 1) * HEAD_PAD)
        q_ref[:, sl] = (rotary(qm[:, sl]) * ATTN_QSCALE).astype(BF16)
        kcat_ref[:, sl] = (kn[:, sl] + kr_rot).astype(BF16)

    gate0 = MAIN_W + MLA_D_ROPE

    def branch_gate(blk):
        cols = slice(blk * D_MODEL, (blk + 1) * D_MODEL)
        gate = _dot_nt(u, wt_ref[gate0 + cols.start:gate0 + cols.stop, :])
        sig_ref[:, cols] = _sigmoid(gate).astype(BF16)
        return gate

    for blk in range(N_BRANCHES):
        conv(1, branch_gate(blk))
    rg = proj(5)
    rg_ref[...] = (rg * _sigmoid(rg)).astype(BF16)
    conv(2, rg)
    rv_ref[...] = proj(4).astype(BF16)
    rk_ref[...] = (proj(3) * (RET_DK ** -0.5)).astype(BF16)
    rq_ref[...] = proj(2).astype(BF16)

    @pl.when(is_ctx)
    def _():
        seqs = tm // SEQ
        ckv3 = ckv.reshape(seqs, SEQ, MLA_KV_LORA)
        kr3 = kr_grp[:, :MLA_D_ROPE].reshape(seqs, SEQ, MLA_D_ROPE)
        if carry:
            ckv_ref[:, 0] = ckv_prev_ref[...]
            kr_ref[:, 0] = kr_prev_ref[...]
            ckv_ref[:, 1] = ckv3
            kr_ref[:, 1] = kr3
        else:
            ckv_ref[...] = ckv3
            kr_ref[...] = kr3


def _proj(x, mods_all, rope_tabs, wt_all, wuq, wk, wvt, gq, gkv, conv_w, carry, *, layer):
    tm = CONV_BLOCK
    n_w = PROJ_WSTEPS
    qk = MLA_D_NOPE + MLA_D_ROPE
    place_np = np.zeros((MLA_HEADS * qk, MLA_W), np.float32)
    cols = np.arange(MLA_HEADS * qk)
    place_np[cols, cols // qk * HEAD_PAD + cols % qk] = 1.0
    place = jnp.asarray(place_np, dtype=BF16)
    n_tiles = N_TOK // tm
    n_ctx_tiles = N_CTX // tm
    tiles_per_seq = DEC_SEQ // tm
    seqs = tm // SEQ

    def rope_index(i):
        j = _tile(i, n_w, tm)
        return (jnp.where(j < n_ctx_tiles, 0, tiles_per_seq + (j - n_ctx_tiles) % tiles_per_seq), 0)

    def row(w):
        return _row_spec(tm, w, n_w)

    def out(w, dt):
        return jax.ShapeDtypeStruct((N_TOK, w), dt)

    def ctx_seq_spec(*tail):
        zeros = (0,) * len(tail)
        return pl.BlockSpec((seqs,) + tail, lambda i: (jnp.minimum(_tile(i, n_w, tm), n_ctx_tiles - 1),) + zeros)

    rope = pl.BlockSpec((tm, HEAD_PAD), rope_index)
    in_specs = [row(D_MODEL), _mod_spec(tm, layer, n_w), rope, rope, rope,
                _wchunk_spec(wt_all, layer, n_w), _layer_resident(wuq, layer), _resident(place.shape),
                _layer_resident(wk, layer), _layer_resident(wvt, layer), _resident(gq.shape), _resident(gkv.shape),
                _layer_resident(conv_w[0], layer)] + [_resident(w.shape) for w in conv_w[1:]]
    args = [x, mods_all, *rope_tabs, wt_all, wuq, place, wk, wvt, gq, gkv, *conv_w]
    if carry is None:
        ctx_specs = [ctx_seq_spec(SEQ, MLA_KV_LORA), ctx_seq_spec(SEQ, MLA_D_ROPE)]
        ctx_shapes = [jax.ShapeDtypeStruct((BATCH, SEQ, MLA_KV_LORA), F32),
                      jax.ShapeDtypeStruct((BATCH, SEQ, MLA_D_ROPE), F32)]
    else:
        in_specs += [ctx_seq_spec(SEQ, MLA_KV_LORA), ctx_seq_spec(SEQ, MLA_D_ROPE)]
        args += list(carry)
        ctx_specs = [ctx_seq_spec(DEPTH, SEQ, MLA_KV_LORA), ctx_seq_spec(DEPTH, SEQ, MLA_D_ROPE)]
        ctx_shapes = [jax.ShapeDtypeStruct((BATCH, DEPTH, SEQ, MLA_KV_LORA), F32),
                      jax.ShapeDtypeStruct((BATCH, DEPTH, SEQ, MLA_D_ROPE), F32)]
    span = CONV_BLOCK + 2 * CONV_HALO - SUBLANES
    return pl.pallas_call(
        functools.partial(_proj_kernel, tm=tm, carry=carry is not None, layer=layer),
        grid=(n_w + n_tiles + 1,),
        in_specs=in_specs,
        out_specs=[pl.BlockSpec((tm, CONV_DIM), lambda i: (_tile(i, n_w + 1, tm), 0)), row(SLAB_W),
                   pl.BlockSpec((MLA_V_W, tm), lambda i: (0, _tile(i, n_w, tm)))] + ctx_specs,
        out_shape=[out(CONV_DIM, BF16), out(SLAB_W, BF16),
                   jax.ShapeDtypeStruct((MLA_V_W, N_TOK), BF16)] + ctx_shapes,
        scratch_shapes=[pltpu.VMEM((2, tm, CONV_DIM), F32),
                        pltpu.VMEM((CONV_BLOCK + 2 * CONV_HALO, CONV_DIM), F32),
                        pltpu.VMEM((SUBLANES, span, CONV_DIM), F32),
                        pltpu.VMEM((CONV_BLOCK, CONV_DIM), F32),
                        pltpu.VMEM(wt_all.shape[1:], BF16), pltpu.VMEM((MLA_Q_LORA, MLA_W), BF16)],
        compiler_params=_params(("arbitrary",)),
        name="mix_proj",
    )(*args)


def _cache_kv_kernel(ckv_ref, kr_ref, wk_ref, wvt_ref, kcat_ref, vt_ref):
    n = DEC_BATCH * PAST_LEN
    ckvb = ckv_ref[...].reshape(n, MLA_KV_LORA).astype(BF16)
    kn = _dot(ckvb, wk_ref[...])
    vt_ref[...] = _dot_nt(wvt_ref[...], ckvb).astype(BF16)
    kr = kr_ref[...].reshape(n, HEAD_PAD)
    for h in range(MLA_HEADS):
        sl = slice(h * HEAD_PAD, (h + 1) * HEAD_PAD)
        kcat_ref[:, sl] = (kn[:, sl] + kr).astype(BF16)


def _cache_kv(cache_ckv, cache_kr_pad, wk, wvt):
    n = DEC_BATCH * PAST_LEN
    return pl.pallas_call(
        _cache_kv_kernel,
        grid=(DEPTH,),
        in_specs=[pl.BlockSpec((DEC_BATCH, None, PAST_LEN, MLA_KV_LORA), lambda l: (0, l, 0, 0)),
                  pl.BlockSpec((DEC_BATCH, None, PAST_LEN, HEAD_PAD), lambda l: (0, l, 0, 0)),
                  pl.BlockSpec((None, MLA_KV_LORA, MLA_W), lambda l: (l, 0, 0)),
                  pl.BlockSpec((None, MLA_V_W, MLA_KV_LORA), lambda l: (l, 0, 0))],
        out_specs=[pl.BlockSpec((None, n, MLA_W), lambda l: (l, 0, 0)),
                   pl.BlockSpec((None, MLA_V_W, n), lambda l: (l, 0, 0))],
        out_shape=[jax.ShapeDtypeStruct((DEPTH, n, MLA_W), BF16), jax.ShapeDtypeStruct((DEPTH, MLA_V_W, n), BF16)],
        compiler_params=_params(("arbitrary",)),
        name="cache_kv",
    )(cache_ckv, cache_kr_pad, wk, wvt)


def _ret_kernel(*refs, t, hp, latent, carry, layer):
    if latent:
        lg_ref, q_ref, k_ref, v_ref, g_ref, s0f_ref, s0b_ref, o_ref, d_ref = refs
    elif carry:
        lg_ref, q_ref, k_ref, v_ref, g_ref, sf_prev_ref, sb_prev_ref, o_ref, sf_ref, sb_ref, d_ref = refs
    else:
        lg_ref, q_ref, k_ref, v_ref, g_ref, o_ref, sf_ref, sb_ref, d_ref = refs
    hblk = pl.program_id(0)

    @pl.when(pl.program_id(1) == 0)
    def _():
        diff = (lax.broadcasted_iota(jnp.int32, (t, t), 0) - lax.broadcasted_iota(jnp.int32, (t, t), 1)).astype(F32)
        for hh in range(hp):
            lgf = lg_ref[layer, 0, hblk * hp + hh]
            lgb = lg_ref[layer, 1, hblk * hp + hh]
            d_ref[hh] = jnp.exp(jnp.where(diff >= 0, diff * lgf, -diff * lgb))

    if carry:
        sf_ref[0] = sf_prev_ref[...]
        sb_ref[0] = sb_prev_ref[...]
    pos = lax.broadcasted_iota(jnp.int32, (t, 1), 0).astype(F32)
    for hh in range(hp):
        lgf = lg_ref[layer, 0, hblk * hp + hh]
        lgb = lg_ref[layer, 1, hblk * hp + hh]
        q = q_ref[:, hh * RET_DK:(hh + 1) * RET_DK]
        k = k_ref[:, hh * RET_DK:(hh + 1) * RET_DK]
        v = v_ref[:, hh * RET_DV:(hh + 1) * RET_DV]
        p = (_dot_nt(q, k) * d_ref[hh]).astype(BF16)
        o = _dot(p, v)
        if latent:
            o = o + jnp.exp((pos + 1.0) * lgf) * _dot(q, s0f_ref[hh].astype(BF16))
            o = o + jnp.exp((t - pos) * lgb) * _dot(q, s0b_ref[hh].astype(BF16))
        else:
            kf = k.astype(F32)
            sf = _dot_tn((kf * jnp.exp((t - 1.0 - pos) * lgf)).astype(BF16), v)
            sb = _dot_tn((kf * jnp.exp(pos * lgb)).astype(BF16), v)
            if carry:
                sf_ref[1, hh] = sf
                sb_ref[1, hh] = sb
            else:
                sf_ref[hh] = sf
                sb_ref[hh] = sb
        o_ref[:, hh * RET_DV:(hh + 1) * RET_DV] = (g_ref[:, hh * RET_DV:(hh + 1) * RET_DV] * _norm_rows(o)).astype(BF16)


def _retention(log_g, slab, states, *, latent, layer):
    t = DEC_SEQ if latent else SEQ
    hp = 2 if latent else RET_HEADS
    n_seq = DEC_BATCH if latent else BATCH
    row0 = (N_CTX // t) if latent else 0

    def row(name, w):
        assert SLAB_COLS[name][0] % (hp * w) == 0
        first = SLAB_COLS[name][0] // (hp * w)
        return pl.BlockSpec((t, hp * w), lambda h, s: (row0 + s, first + h))

    smem = pl.BlockSpec(memory_space=pltpu.SMEM)
    out_shape = [jax.ShapeDtypeStruct((n_seq * t, RET_HEADS * RET_DV), BF16)]
    out_specs = [pl.BlockSpec((t, hp * RET_DV), lambda h, s: (s, h))]
    in_specs = [smem, row("rq", RET_DK), row("rk", RET_DK), row("rv", RET_DV), row("rg", RET_DV)]
    args = [log_g, slab, slab, slab, slab]
    carry = False
    if latent:
        st = pl.BlockSpec((None, None, hp, RET_DK, RET_DV), lambda h, s: (s, layer, h, 0, 0))
        in_specs += [st, st]
        args += list(states)
    else:
        st = pl.BlockSpec((None, hp, RET_DK, RET_DV), lambda h, s: (s, h, 0, 0))
        if states is None:
            out_specs += [st, st]
            out_shape += [jax.ShapeDtypeStruct((BATCH, RET_HEADS, RET_DK, RET_DV), F32)] * 2
        else:
            carry = True
            in_specs += [st, st]
            args += list(states)
            st2 = pl.BlockSpec((None, DEPTH, hp, RET_DK, RET_DV), lambda h, s: (s, 0, h, 0, 0))
            out_specs += [st2, st2]
            out_shape += [jax.ShapeDtypeStruct((BATCH, DEPTH, RET_HEADS, RET_DK, RET_DV), F32)] * 2

    return pl.pallas_call(
        functools.partial(_ret_kernel, t=t, hp=hp, latent=latent, carry=carry, layer=layer),
        grid=(RET_HEADS // hp, n_seq),
        in_specs=in_specs,
        out_specs=out_specs,
        out_shape=out_shape,
        scratch_shapes=[pltpu.VMEM((hp, t, t), F32)],
        compiler_params=_params(("arbitrary", "arbitrary")),
        name="retention_lat" if latent else "retention_ctx",
    )(*args)


def _attn_kernel(*refs, t, hp, latent):
    if latent:
        q_ref, k_ref, vt_ref, kc_ref, vtc_ref, o_ref = refs
    else:
        q_ref, k_ref, vt_ref, o_ref = refs
    qb = min(ATTN_QBLOCK, t)
    units = [(slice(hh * HEAD_PAD, (hh + 1) * HEAD_PAD), slice(hh * MLA_D_V, (hh + 1) * MLA_D_V),
              slice(b * qb, (b + 1) * qb)) for b in range(t // qb) for hh in range(hp)]
    def scores(unit):
        sl, _, rows = unit
        q = q_ref[rows, sl]
        s = [_dot_nt(k_ref[:, sl], q)]
        if latent:
            s.append(_dot_nt(kc_ref[:, sl], q))
        return s

    def softmax(s):
        m = functools.reduce(jnp.maximum, [jnp.max(x, axis=0, keepdims=True) for x in s])
        e = [jnp.exp2(x - m) for x in s]
        den = functools.reduce(jnp.add, [jnp.sum(x, axis=0, keepdims=True) for x in e])
        return [x.astype(BF16) for x in e], den

    def values(unit, e, den):
        _, vs, _ = unit
        o = _dot(vt_ref[vs, :], e[0])
        if latent:
            o = o + _dot(vtc_ref[vs, :], e[1])
        return o / den

    pairs = [units[u:u + 2] for u in range(0, len(units), 2)]
    s_next = [scores(u) for u in pairs[0]]
    sm_prev = None
    for g in range(len(pairs) + 1):
        s_cur = s_next
        if g + 1 < len(pairs):
            s_next = [scores(u) for u in pairs[g + 1]]
        sm_cur = [softmax(s) for s in s_cur] if g < len(pairs) else None
        if sm_prev is not None:
            (_, vs0, rows), (_, vs1, _) = pairs[g - 1]
            outs = [values(u, e, den) for u, (e, den) in zip(pairs[g - 1], sm_prev)]
            o_ref[rows, vs0.start:vs1.stop] = jnp.concatenate(outs, axis=0).T.astype(BF16)
        sm_prev = sm_cur


def _attention(slab, vt, cache, *, latent, layer):
    t = DEC_SEQ if latent else SEQ
    hp = 4 if latent else MLA_HEADS
    n_seq = DEC_BATCH if latent else BATCH
    row0 = (N_CTX // t) if latent else 0

    def row(name):
        assert SLAB_COLS[name][0] % (hp * HEAD_PAD) == 0
        first = SLAB_COLS[name][0] // (hp * HEAD_PAD)
        return pl.BlockSpec((t, hp * HEAD_PAD), lambda s, h: (row0 + s, first + h))

    col = pl.BlockSpec((hp * MLA_D_V, t), lambda s, h: (h, row0 + s))
    in_specs = [row("q"), row("kcat"), col]
    args = [slab, slab, vt]
    if latent:
        in_specs += [pl.BlockSpec((None, PAST_LEN, hp * HEAD_PAD), lambda s, h: (layer, s, h)),
                     pl.BlockSpec((None, hp * MLA_D_V, PAST_LEN), lambda s, h: (layer, h, s))]
        args += list(cache)
    return pl.pallas_call(
        functools.partial(_attn_kernel, t=t, hp=hp, latent=latent),
        grid=(n_seq, MLA_HEADS // hp),
        in_specs=in_specs,
        out_specs=pl.BlockSpec((t, hp * MLA_D_V), lambda s, h: (s, h)),
        out_shape=jax.ShapeDtypeStruct((n_seq * t, MLA_V_W), BF16),
        compiler_params=_params(("arbitrary", "arbitrary")),
        name="attention_lat" if latent else "attention_ctx",
    )(*args)


def _merge_kernel(x_ref, mod_ref, a_ref, rc_ref, rl_ref, mc_ref, ml_ref, sig_ref, wc_ref, wr_ref, wm_ref, wo_ref,
                  g_ref, b_ref, o_ref, wc_bf, wr_bf, wm_bf, wo_bf, *, tm, layer):
    g_ref, b_ref = (r.at[layer, 1:2, :] for r in (g_ref, b_ref))
    i = pl.program_id(0)
    is_ctx = i < N_WSTEPS + N_CTX // tm

    @pl.when(i < N_WSTEPS)
    def _():
        _stage_chunk(i, wc_ref, wc_bf)
        _stage_chunk(i, wr_ref, wr_bf)
        _stage_chunk(i, wo_ref, wo_bf)
        _stage_chunk(i, wm_ref, wm_bf)

    @pl.when(i >= N_WSTEPS)
    def _():
        for half in range(ROW_HALVES):
            rows = slice(half * tm // ROW_HALVES, (half + 1) * tm // ROW_HALVES)
            x = x_ref[rows, :]
            r = jnp.where(is_ctx, rc_ref[rows, :], rl_ref[rows, :])
            m = jnp.where(is_ctx, mc_ref[rows, :], ml_ref[rows, :])
            merged = sig_ref[rows, 0:D_MODEL] * _dot(a_ref[rows, :], wc_bf[...])
            merged = merged + sig_ref[rows, D_MODEL:2 * D_MODEL] * _dot(r, wr_bf[...])
            merged = merged + sig_ref[rows, 2 * D_MODEL:] * _dot(m, wm_bf[...])
            y = _dot(merged.astype(BF16), wo_bf[...])
            z = DEEPNORM_ALPHA * x + mod_ref[5:6, :] * y
            o_ref[rows, :] = _norm_rows(z) * g_ref[...] + b_ref[...]


def _merge(x, mods_all, a, r_pair, m_pair, sig, wc, wr, wm, wo, ln_g, ln_b, *, layer, tm=512):
    def row(w):
        return _row_spec(tm, w, N_WSTEPS)

    def pair(w):
        return [_ctx_row_spec(tm, w, N_WSTEPS), _lat_row_spec(tm, w, N_WSTEPS)]

    return pl.pallas_call(
        functools.partial(_merge_kernel, tm=tm, layer=layer),
        grid=(N_WSTEPS + N_TOK // tm,),
        in_specs=[row(D_MODEL), _mod_spec(tm, layer, N_WSTEPS), row(CONV_DIM)] + pair(RET_HEADS * RET_DV)
        + pair(MLA_V_W) + [row(GATE_W), _wchunk_spec(wc, layer), _wchunk_spec(wr, layer), _wchunk_spec(wm, layer),
                         _wchunk_spec(wo, layer), _resident(ln_g.shape), _resident(ln_b.shape)],
        out_specs=row(D_MODEL),
        out_shape=jax.ShapeDtypeStruct((N_TOK, D_MODEL), F32),
        scratch_shapes=[pltpu.VMEM((CONV_DIM, D_MODEL), BF16), pltpu.VMEM((RET_HEADS * RET_DV, D_MODEL), BF16),
                        pltpu.VMEM((MLA_V_W, D_MODEL), BF16), pltpu.VMEM((D_MODEL, D_MODEL), BF16)],
        compiler_params=_params(("arbitrary",)),
        name="merge",
    )(x, mods_all, a, *r_pair, *m_pair, sig, wc, wr, wm, wo, ln_g, ln_b)


def _rope_tables():
    f32 = np.float32
    rows = DEC_SEQ // GRID_W
    row_id = np.repeat(np.arange(rows, dtype=f32), GRID_W)
    col_id = np.tile(np.arange(GRID_W, dtype=f32), rows)
    inv_freq = (f32(ROPE_BASE) ** (-np.arange(ROPE_AXIS_HALF, dtype=f32) / f32(ROPE_AXIS_HALF))).astype(f32)
    ang = np.stack([row_id[:, None] * inv_freq, col_id[:, None] * inv_freq], axis=1).astype(f32)
    cos = np.cos(ang).astype(f32)
    sin = np.sin(ang).astype(f32)
    cos32 = np.stack([cos, cos], axis=2).reshape(DEC_SEQ, MLA_D_ROPE)
    zero = np.zeros_like(sin)
    sin_lo32 = np.stack([-sin, zero], axis=2).reshape(DEC_SEQ, MLA_D_ROPE)
    sin_hi32 = np.stack([zero, sin], axis=2).reshape(DEC_SEQ, MLA_D_ROPE)
    tail = HEAD_PAD - ROPE_LANE0 - MLA_D_ROPE
    cos_t = np.concatenate([np.ones((DEC_SEQ, ROPE_LANE0), f32), cos32, np.ones((DEC_SEQ, tail), f32)], axis=1)
    cos_t = np.concatenate([np.ones((DEC_SEQ, HEAD_PAD), f32), cos_t], axis=0)

    def sin_table(s32):
        return np.pad(s32, ((DEC_SEQ, 0), (ROPE_LANE0, tail)))

    return tuple(jnp.asarray(t, dtype=F32) for t in (cos_t, sin_table(sin_lo32), sin_table(sin_hi32)))


def _head_pad_cols(w, width):
    k = w.shape[0]
    w = w.reshape(k, MLA_HEADS, width)
    return jnp.pad(w, ((0, 0), (0, 0), (0, HEAD_PAD - width))).reshape(k, MLA_W)


def kernel(x_prompt, x_sample, cache_mla_ckv, cache_mla_krope, state_ret_fwd, state_ret_bwd, c, c_ctx, ada_w, ada_b, ffn1_w_in, ffn1_w_out, ffn2_w_in, ffn2_w_out, post_ln_g, post_ln_b, mix_w_in, conv_w_dw, conv_b_dw, conv_ln_g, conv_ln_b, conv_w_out, ret_decay_fwd, ret_decay_bwd, ret_w_out, mla_q_norm, mla_w_uq, mla_kv_norm, mla_w_ukv, mla_w_out, mix_w_o):
    assert DEPTH == 2
    cvec = jnp.concatenate([c_ctx[None, :], c, jnp.zeros((N_MOD_ROWS - 1 - DEC_BATCH, D_MODEL), F32)], axis=0)
    mods_all = _ada_mods(cvec, ada_w, ada_b).reshape(DEPTH, N_MOD_ROWS, N_MODS, D_MODEL)
    rope_tabs = _rope_tables()

    w_ukv = mla_w_ukv.reshape(DEPTH, MLA_KV_LORA, MLA_HEADS, MLA_D_NOPE + MLA_D_V)
    wk_all = _head_pad_cols(w_ukv[..., :MLA_D_NOPE].reshape(DEPTH * MLA_KV_LORA, -1), MLA_D_NOPE)
    wk_all = wk_all.reshape(DEPTH, MLA_KV_LORA, MLA_W).astype(BF16)
    wvt_all = jnp.swapaxes(w_ukv[..., MLA_D_NOPE:].reshape(DEPTH, MLA_KV_LORA, MLA_V_W), 1, 2).astype(BF16)
    kr_tail = HEAD_PAD - ROPE_LANE0 - MLA_D_ROPE
    cache_kr_pad = jnp.pad(cache_mla_krope, ((0, 0), (0, 0), (0, 0), (ROPE_LANE0, kr_tail)))
    kcat_c, vt_c = _cache_kv(cache_mla_ckv, cache_kr_pad, wk_all, wvt_all)
    log_g = jnp.stack([jax.nn.log_sigmoid(ret_decay_fwd), jax.nn.log_sigmoid(ret_decay_bwd)], axis=1)
    wt_all = jnp.swapaxes(mix_w_in, 1, 2)
    conv_w = (conv_w_dw, conv_b_dw, conv_ln_g, conv_ln_b)
    ln_g, ln_b = post_ln_g, post_ln_b
    xs = (x_prompt.reshape(N_CTX, D_MODEL), x_sample.reshape(N_LAT, D_MODEL))
    ctx_carry = None
    state_carry = None
    for l in range(DEPTH):
        last = l == DEPTH - 1
        x = _ffn(xs, mods_all, ffn1_w_in, ffn1_w_out, ln_g, ln_b, layer=l, which=0)[0]

        a, slab, vt, ckv, kr = _proj(x, mods_all, rope_tabs, wt_all, mla_w_uq, wk_all, wvt_all, mla_q_norm, mla_kv_norm,
                                     conv_w, ctx_carry, layer=l)
        ctx_carry = (ckv, kr)

        r_ctx, sf, sb = _retention(log_g, slab, state_carry, latent=False, layer=l)
        state_carry = (sf, sb)
        r_lat = _retention(log_g, slab, (state_ret_fwd, state_ret_bwd), latent=True, layer=l)[0]
        m_ctx = _attention(slab, vt, None, latent=False, layer=l)
        m_lat = _attention(slab, vt, (kcat_c, vt_c), latent=True, layer=l)

        x = _merge(x, mods_all, a, (r_ctx, r_lat), (m_ctx, m_lat), slab, conv_w_out, ret_w_out, mla_w_out, mix_w_o,
                   ln_g, ln_b, layer=l)
        xs = _ffn((x,), mods_all, ffn2_w_in, ffn2_w_out, ln_g, ln_b, layer=l, which=2, split_out=last)

    y_ctx, y_lat = xs
    return (y_ctx.reshape(BATCH, SEQ, D_MODEL), y_lat.reshape(DEC_BATCH, DEC_SEQ, D_MODEL),
            ctx_carry[0], ctx_carry[1], state_carry[0], state_carry[1])
```

```python
import functools
import math

import jax
import jax.numpy as jnp
import numpy as np
from jax import lax
from jax.experimental import pallas as pl
from jax.experimental.pallas import tpu as pltpu

F32 = jnp.float32
BF16 = jnp.bfloat16

D_MODEL = 1024
BATCH = 16
SEQ = 256
DEPTH = 2
DEC_BATCH = 4
DEC_SEQ = 1024
PAST_LEN = 256
GRID_W = 64
D_FF = 2816
N_MODS = 9
CONV_DIM = 512
CONV_WIDTH = 31
RET_HEADS = 4
RET_DK = 128
RET_DV = 256
MLA_HEADS = 8
MLA_Q_LORA = 512
MLA_KV_LORA = 256
MLA_D_NOPE = 64
MLA_D_ROPE = 32
MLA_D_V = 64
ROPE_AXIS_HALF = MLA_D_ROPE // 4
ROPE_BASE = 10000.0
DEEPNORM_ALPHA = (2 * DEPTH) ** 0.25
LN_EPS = 1e-5
RMS_EPS = 1e-6

N_CTX = BATCH * SEQ
N_LAT = DEC_BATCH * DEC_SEQ
N_TOK = N_CTX + N_LAT
N_MOD_ROWS = 8
HEAD_PAD = 128
ROPE_LANE0 = MLA_D_NOPE
MLA_W = MLA_HEADS * HEAD_PAD
MLA_V_W = MLA_HEADS * MLA_D_V
MAIN_W = 2 * CONV_DIM + 2 * RET_HEADS * RET_DK + 2 * RET_HEADS * RET_DV + MLA_Q_LORA + MLA_KV_LORA
N_BRANCHES = 3
GATE_W = N_BRANCHES * D_MODEL
SLAB_COLS = {}
for _name, _w in (("sig", GATE_W), ("rq", RET_HEADS * RET_DK), ("rk", RET_HEADS * RET_DK), ("rv", RET_HEADS * RET_DV),
                  ("rg", RET_HEADS * RET_DV), ("q", MLA_W), ("kcat", MLA_W)):
    SLAB_COLS[_name] = (sum(w for _, w in SLAB_COLS.values()), _w)
SLAB_W = sum(w for _, w in SLAB_COLS.values())
SUBLANES = 8
CONV_HALO = 16
CONV_BLOCK = 256
VMEM_LIMIT = 56 * 1024 * 1024
N_WSTEPS = 8
PROJ_WSTEPS = 6
MXU_COLS = 256
FFN_CHUNKS = ((0, 6 * MXU_COLS), (6 * MXU_COLS, D_FF))
ROW_HALVES = 2
ATTN_QBLOCK = 256
ATTN_QSCALE = (MLA_D_NOPE + MLA_D_ROPE) ** -0.5 * math.log2(math.e)


def _dot(a, b):
    return jnp.dot(a, b, preferred_element_type=F32)


def _dot_nt(a, b):
    return lax.dot_general(a, b, (((1,), (1,)), ((), ())), preferred_element_type=F32)


def _dot_tn(a, b):
    return lax.dot_general(a, b, (((0,), (0,)), ((), ())), preferred_element_type=F32)


def _sigmoid(x):
    return 1.0 / (1.0 + jnp.exp(-x))


def _norm_rows(z):
    mu = jnp.mean(z, axis=-1, keepdims=True)
    zc = z - mu
    var = jnp.mean(zc * zc, axis=-1, keepdims=True)
    return zc * lax.rsqrt(var + LN_EPS)


def _rms_rows(z):
    return z * lax.rsqrt(jnp.mean(z * z, axis=-1, keepdims=True) + RMS_EPS)


def _resident(shape):
    zeros = (0,) * len(shape)
    return pl.BlockSpec(shape, lambda *_: zeros, pipeline_mode=pl.Buffered(1))


def _layer_resident(w, layer):
    zeros = (0,) * (w.ndim - 1)
    return pl.BlockSpec((None,) + w.shape[1:], lambda *_: (layer,) + zeros, pipeline_mode=pl.Buffered(1))


def _tile(i, n_w, tm):
    return jnp.clip(i - n_w, 0, N_TOK // tm - 1)


def _row_spec(tm, w, n_w):
    return pl.BlockSpec((tm, w), lambda i: (_tile(i, n_w, tm), 0))


def _ctx_row_spec(tm, w, n_w):
    last = N_CTX // tm - 1
    return pl.BlockSpec((tm, w), lambda i: (jnp.minimum(_tile(i, n_w, tm), last), 0))


def _lat_row_spec(tm, w, n_w):
    first = N_CTX // tm
    return pl.BlockSpec((tm, w), lambda i: (jnp.maximum(_tile(i, n_w, tm) - first, 0), 0))


def _mod_spec(tm, layer, n_w):
    n_ctx_tiles = N_CTX // tm
    tiles_per_seq = DEC_SEQ // tm

    def index(i):
        j = _tile(i, n_w, tm)
        return (layer, jnp.where(j < n_ctx_tiles, 0, 1 + (j - n_ctx_tiles) // tiles_per_seq), 0, 0)

    return pl.BlockSpec((None, None, N_MODS, D_MODEL), index)


def _wchunk_spec(w, layer, n_w=N_WSTEPS):
    _, rows, cols = w.shape
    return pl.BlockSpec((None, rows // n_w, cols), lambda i: (layer, jnp.minimum(i, n_w - 1), 0))


def _stage_chunk(i, src_ref, dst_ref):
    rows = src_ref.shape[0]
    dst_ref[pl.ds(pl.multiple_of(i * rows, rows), rows), :] = src_ref[...].astype(BF16)


def _params(semantics):
    return pltpu.CompilerParams(dimension_semantics=semantics, vmem_limit_bytes=VMEM_LIMIT)


def _ada_kernel(c_ref, w_ref, b_ref, o_ref):
    c = c_ref[...]
    h = (c * _sigmoid(c)).astype(BF16)
    o_ref[...] = _dot(h, w_ref[...].astype(BF16)) + b_ref[pl.ds(pl.program_id(0), 1), :]


def _ada_mods(cvec, ada_w, ada_b):
    n_out = N_MODS * D_MODEL
    tn = n_out // 4
    return pl.pallas_call(
        _ada_kernel,
        grid=(DEPTH, n_out // tn),
        in_specs=[
            pl.BlockSpec((N_MOD_ROWS, D_MODEL), lambda l, j: (0, 0)),
            pl.BlockSpec((None, D_MODEL, tn), lambda l, j: (l, 0, j)),
            pl.BlockSpec((DEPTH, tn), lambda l, j: (0, j)),
        ],
        out_specs=pl.BlockSpec((None, N_MOD_ROWS, tn), lambda l, j: (l, 0, j)),
        out_shape=jax.ShapeDtypeStruct((DEPTH, N_MOD_ROWS, n_out), F32),
        compiler_params=_params(("arbitrary", "arbitrary")),
        name="ada_mods",
    )(cvec, ada_w, ada_b)


def _ffn_kernel(*refs, layer, which, tm, n_x, n_out):
    base = N_MODS // 3 * which
    x_refs = refs[:n_x]
    mod_ref, win_ref, wout_ref, g_ref, b_ref = refs[n_x:n_x + 5]
    g_ref, b_ref = (r.at[layer, which:which + 1, :] for r in (g_ref, b_ref))
    o_refs = refs[n_x + 5:n_x + 5 + n_out]
    win_bf, wout_bf = refs[n_x + 5 + n_out:]
    i = pl.program_id(0)
    is_ctx = i < N_WSTEPS + N_CTX // tm

    @pl.when(i < N_WSTEPS)
    def _():
        _stage_chunk(i, win_ref, win_bf)
        _stage_chunk(i, wout_ref, wout_bf)

    @pl.when(i >= N_WSTEPS)
    def _():
        shift = mod_ref[base:base + 1, :]
        scale = mod_ref[base + 1:base + 2, :]
        gate = mod_ref[base + 2:base + 3, :]
        halves = [slice(k * tm // ROW_HALVES, (k + 1) * tm // ROW_HALVES) for k in range(ROW_HALVES)]
        results = []
        for rows in halves:
            if n_x == 2:
                x = jnp.where(is_ctx, x_refs[0][rows, :], x_refs[1][rows, :])
            else:
                x = x_refs[0][rows, :]
            h = (x * (1.0 + scale) + shift).astype(BF16)
            y = None
            for lo, hi in FFN_CHUNKS:
                g = _dot(h, win_bf[:, lo:hi])
                u = _dot(h, win_bf[:, D_FF + lo:D_FF + hi])
                a = (g * _sigmoid(g) * u).astype(BF16)
                yc = _dot(a, wout_bf[lo:hi, :])
                y = yc if y is None else y + yc
            z = DEEPNORM_ALPHA * x + 0.5 * gate * y
            res = _norm_rows(z) * g_ref[...] + b_ref[...]
            if n_out == 1:
                o_refs[0][rows, :] = res
            results.append(res)
        if n_out == 2:
            @pl.when(is_ctx)
            def _():
                for rows, res in zip(halves, results):
                    o_refs[0][rows, :] = res

            @pl.when(jnp.logical_not(is_ctx))
            def _():
                for rows, res in zip(halves, results):
                    o_refs[1][rows, :] = res


def _ffn(xs, mods_all, w_in, w_out, ln_g, ln_b, *, layer, which, split_out=False, tm=512):
    row = _row_spec(tm, D_MODEL, N_WSTEPS)
    pair = [_ctx_row_spec(tm, D_MODEL, N_WSTEPS), _lat_row_spec(tm, D_MODEL, N_WSTEPS)]
    if split_out:
        out_specs = pair
        out_shape = [jax.ShapeDtypeStruct((N_CTX, D_MODEL), F32), jax.ShapeDtypeStruct((N_LAT, D_MODEL), F32)]
    else:
        out_specs = [row]
        out_shape = [jax.ShapeDtypeStruct((N_TOK, D_MODEL), F32)]
    return pl.pallas_call(
        functools.partial(_ffn_kernel, layer=layer, which=which, tm=tm, n_x=len(xs), n_out=len(out_specs)),
        grid=(N_WSTEPS + N_TOK // tm,),
        in_specs=(pair if len(xs) == 2 else [row]) + [
            _mod_spec(tm, layer, N_WSTEPS), _wchunk_spec(w_in, layer), _wchunk_spec(w_out, layer),
            _resident(ln_g.shape), _resident(ln_b.shape)],
        out_specs=out_specs,
        out_shape=out_shape,
        scratch_shapes=[pltpu.VMEM((D_MODEL, 2 * D_FF), BF16), pltpu.VMEM((D_FF, D_MODEL), BF16)],
        compiler_params=_params(("arbitrary",)),
        name="ffn",
    )(*xs, mods_all, w_in, w_out, ln_g, ln_b)


def _conv_stages(prev_halo, main, next_halo, w_ref, b_ref, g_ref, beta_ref, o_ref, pad_ref, shift_ref, acc_ref):
    rows = 32
    lanes = 128
    first = CONV_HALO - CONV_WIDTH // 2

    def setup(after):
        del after
        pad_ref[0:CONV_HALO, :] = prev_halo()
        pad_ref[CONV_HALO:CONV_HALO + CONV_BLOCK, :] = main()
        pad_ref[CONV_HALO + CONV_BLOCK:, :] = next_halo()
        span = shift_ref.shape[1]
        for ph in range(SUBLANES):
            shift_ref[ph] = pad_ref[ph:ph + span, :]

    def taps(c, r, start_from):
        cs = slice(c * lanes, (c + 1) * lanes)
        acc = jnp.broadcast_to(b_ref[:, cs], (rows, lanes)) + start_from
        for j in range(CONV_WIDTH):
            ph = (first + j) % SUBLANES
            start = r * rows + (first + j) - ph
            acc = acc + w_ref[j:j + 1, cs] * shift_ref[ph, start:start + rows, cs]
        acc_ref[r * rows:(r + 1) * rows, cs] = acc

    def finish(after):
        del after
        y = _norm_rows(acc_ref[...]) * g_ref[...] + beta_ref[...]
        o_ref[...] = (y * _sigmoid(y)).astype(BF16)

    def lane_group_half(c, half, after):
        if after is None:
            start_from = jnp.zeros((rows, lanes), F32)
        else:
            bits = lax.bitcast_convert_type(after, jnp.int32)
            zero = lax.shift_right_logical(lax.shift_right_logical(bits, 16), 16).astype(F32)
            start_from = jnp.concatenate([zero] * (rows // SUBLANES), axis=0)
        per_half = CONV_BLOCK // rows // 2
        for r in range(half * per_half, (half + 1) * per_half):
            taps(c, r, start_from)

    chunks = [functools.partial(lane_group_half, c, half) for c in range(CONV_DIM // lanes) for half in range(2)]
    return [setup] + chunks + [finish]


def _proj_kernel(*refs, tm, carry, layer):
    (x_ref, mod_ref, cos_ref, sin_lo_ref, sin_hi_ref, wt_ref, wuq_ref, place_ref, wk_ref, wvt_ref, gq_ref,
     gkv_ref, cw_ref, cb_ref, cg_ref, cbeta_ref) = refs[:16]
    refs = refs[16:]
    gq_ref, gkv_ref, cb_ref, cg_ref, cbeta_ref = (
        r.at[layer:layer + 1, :] for r in (gq_ref, gkv_ref, cb_ref, cg_ref, cbeta_ref))
    if carry:
        ckv_prev_ref, kr_prev_ref = refs[:2]
        refs = refs[2:]
    (a_ref, slab_ref, vt_ref, ckv_ref, kr_ref, glu_ring, pad_ref, shift_ref, acc_ref, wt_bf, wuq_bf) = refs
    sig_ref, rq_ref, rk_ref, rv_ref, rg_ref, q_ref, kcat_ref = (
        slab_ref.at[:, lo:lo + w] for lo, w in (SLAB_COLS[n] for n in ("sig", "rq", "rk", "rv", "rg", "q", "kcat")))
    t = pl.program_id(0) - PROJ_WSTEPS
    n_tiles = N_TOK // tm
    n_ctx_tiles = N_CTX // tm
    tiles_per_seq = DEC_SEQ // tm
    is_ctx = t < n_ctx_tiles
    pos = (t - 1 - n_ctx_tiles) % tiles_per_seq
    conv_latent = t - 1 >= n_ctx_tiles
    has_prev = jnp.logical_and(conv_latent, pos != 0)
    has_next = jnp.logical_and(conv_latent, pos != tiles_per_seq - 1)
    slot = t % 2
    no_halo = jnp.zeros((CONV_HALO, CONV_DIM), F32)

    def conv_prev_tile(next_rows):
        return _conv_stages(
            lambda: jnp.where(has_prev, glu_ring[slot, tm - CONV_HALO:tm, :], no_halo),
            lambda: glu_ring[1 - slot],
            lambda: jnp.where(has_next, next_rows, no_halo),
            cw_ref, cb_ref, cg_ref, cbeta_ref, a_ref, pad_ref, shift_ref, acc_ref)

    @pl.when(t < 0)
    def _():
        _stage_chunk(pl.program_id(0), wt_ref, wt_bf)

    @pl.when(t == 0)
    def _():
        glu_ring[...] = jnp.zeros(glu_ring.shape, F32)
        wuq_bf[...] = _dot(wuq_ref[...].astype(BF16), place_ref[...]).astype(BF16)

    @pl.when(t == n_tiles)
    def _():
        for step in conv_prev_tile(no_halo):
            step(None)

    @pl.when(jnp.logical_and(t >= 0, t < n_tiles))
    def _():
        _proj_tile(x_ref, mod_ref, cos_ref, sin_lo_ref, sin_hi_ref, wt_bf, wuq_bf, wk_ref, wvt_ref, gq_ref,
                   gkv_ref, ckv_prev_ref if carry else None, kr_prev_ref if carry else None,
                   rq_ref, rk_ref, rv_ref, rg_ref, q_ref, kcat_ref, vt_ref, sig_ref, ckv_ref, kr_ref,
                   glu_ring, conv_prev_tile, slot, is_ctx, tm)


def _proj_tile(x_ref, mod_ref, cos_ref, sin_lo_ref, sin_hi_ref, wt_ref, wuq_ref, wk_ref, wvt_ref, gq_ref,
               gkv_ref, ckv_prev_ref, kr_prev_ref,
               rq_ref, rk_ref, rv_ref, rg_ref, q_ref, kcat_ref, vt_ref, sig_ref, ckv_ref, kr_ref,
               glu_ring, conv_prev_tile, slot, is_ctx, tm):
    carry = ckv_prev_ref is not None
    x = x_ref[...]
    u = (x * (1.0 + mod_ref[4:5, :]) + mod_ref[3:4, :]).astype(BF16)
    widths = (CONV_DIM, CONV_DIM, RET_HEADS * RET_DK, RET_HEADS * RET_DK, RET_HEADS * RET_DV, RET_HEADS * RET_DV,
              MLA_Q_LORA, MLA_KV_LORA)
    starts = [sum(widths[:n]) for n in range(len(widths))]

    def proj(n):
        return _dot_nt(u, wt_ref[starts[n]:starts[n] + widths[n], :])

    glu = proj(0) * _sigmoid(proj(1))
    conv_steps = conv_prev_tile(glu[0:CONV_HALO, :])

    def conv(n, result):
        for _ in range(n):
            conv_steps.pop(0)(result[0:SUBLANES, 0:128])

    conv(1, glu)
    glu_ring[slot] = glu

    mq = proj(6)
    mkv = proj(7)
    conv(2, mq)
    cos = cos_ref[...]
    sin_lo = sin_lo_ref[...]
    sin_hi = sin_hi_ref[...]

    def rotary(v):
        up = pltpu.roll(v, HEAD_PAD - ROPE_AXIS_HALF, 1)
        down = pltpu.roll(v, ROPE_AXIS_HALF, 1)
        return v * cos + up * sin_lo + down * sin_hi

    qn = (_rms_rows(mq) * gq_ref[...]).astype(BF16)
    qm = _dot(qn, wuq_ref[...])
    ckv = _rms_rows(mkv) * gkv_ref[...]
    ckvb = ckv.astype(BF16)
    kn = _dot(ckvb, wk_ref[...])
    vt_ref[...] = _dot_nt(wvt_ref[...], ckvb).astype(BF16)
    kr_grp = _dot_nt(u, wt_ref[MAIN_W:MAIN_W + HEAD_PAD, :])
    conv(2, kn)
    lane = lax.broadcasted_iota(jnp.int32, kr_grp.shape, 1)
    in_rope = jnp.logical_and(lane >= ROPE_LANE0, lane < ROPE_LANE0 + MLA_D_ROPE)
    kr = jnp.where(in_rope, pltpu.roll(kr_grp, ROPE_LANE0, 1), 0.0)
    kr_rot = rotary(kr)
    for h in range(MLA_HEADS):
        sl = slice(h * HEAD_PAD, (h + 1) * HEAD_PAD)
        q_ref[:, sl] = (rotary(qm[:, sl]) * ATTN_QSCALE).astype(BF16)
        kcat_ref[:, sl] = (kn[:, sl] + kr_rot).astype(BF16)

    gate0 = MAIN_W + MLA_D_ROPE

    def branch_gate(blk):
        cols = slice(blk * D_MODEL, (blk + 1) * D_MODEL)
        gate = _dot_nt(u, wt_ref[gate0 + cols.start:gate0 + cols.stop, :])
        sig_ref[:, cols] = _sigmoid(gate).astype(BF16)
        return gate

    for blk in range(N_BRANCHES):
        conv(1, branch_gate(blk))
    rg = proj(5)
    rg_ref[...] = (rg * _sigmoid(rg)).astype(BF16)
    conv(2, rg)
    rv_ref[...] = proj(4).astype(BF16)
    rk_ref[...] = (proj(3) * (RET_DK ** -0.5)).astype(BF16)
    rq_ref[...] = proj(2).astype(BF16)

    @pl.when(is_ctx)
    def _():
        seqs = tm // SEQ
        ckv3 = ckv.reshape(seqs, SEQ, MLA_KV_LORA)
        kr3 = kr_grp[:, :MLA_D_ROPE].reshape(seqs, SEQ, MLA_D_ROPE)
        if carry:
            ckv_ref[:, 0] = ckv_prev_ref[...]
            kr_ref[:, 0] = kr_prev_ref[...]
            ckv_ref[:, 1] = ckv3
            kr_ref[:, 1] = kr3
        else:
            ckv_ref[...] = ckv3
            kr_ref[...] = kr3


def _proj(x, mods_all, rope_tabs, wt_all, wuq, wk, wvt, gq, gkv, conv_w, carry, *, layer):
    tm = CONV_BLOCK
    n_w = PROJ_WSTEPS
    qk = MLA_D_NOPE + MLA_D_ROPE
    place_np = np.zeros((MLA_HEADS * qk, MLA_W), np.float32)
    cols = np.arange(MLA_HEADS * qk)
    place_np[cols, cols // qk * HEAD_PAD + cols % qk] = 1.0
    place = jnp.asarray(place_np, dtype=BF16)
    n_tiles = N_TOK // tm
    n_ctx_tiles = N_CTX // tm
    tiles_per_seq = DEC_SEQ // tm
    seqs = tm // SEQ

    def rope_index(i):
        j = _tile(i, n_w, tm)
        return (jnp.where(j < n_ctx_tiles, 0, tiles_per_seq + (j - n_ctx_tiles) % tiles_per_seq), 0)

    def row(w):
        return _row_spec(tm, w, n_w)

    def out(w, dt):
        return jax.ShapeDtypeStruct((N_TOK, w), dt)

    def ctx_seq_spec(*tail):
        zeros = (0,) * len(tail)
        return pl.BlockSpec((seqs,) + tail, lambda i: (jnp.minimum(_tile(i, n_w, tm), n_ctx_tiles - 1),) + zeros)

    rope = pl.BlockSpec((tm, HEAD_PAD), rope_index)
    in_specs = [row(D_MODEL), _mod_spec(tm, layer, n_w), rope, rope, rope,
                _wchunk_spec(wt_all, layer, n_w), _layer_resident(wuq, layer), _resident(place.shape),
                _layer_resident(wk, layer), _layer_resident(wvt, layer), _resident(gq.shape), _resident(gkv.shape),
                _layer_resident(conv_w[0], layer)] + [_resident(w.shape) for w in conv_w[1:]]
    args = [x, mods_all, *rope_tabs, wt_all, wuq, place, wk, wvt, gq, gkv, *conv_w]
    if carry is None:
        ctx_specs = [ctx_seq_spec(SEQ, MLA_KV_LORA), ctx_seq_spec(SEQ, MLA_D_ROPE)]
        ctx_shapes = [jax.ShapeDtypeStruct((BATCH, SEQ, MLA_KV_LORA), F32),
                      jax.ShapeDtypeStruct((BATCH, SEQ, MLA_D_ROPE), F32)]
    else:
        in_specs += [ctx_seq_spec(SEQ, MLA_KV_LORA), ctx_seq_spec(SEQ, MLA_D_ROPE)]
        args += list(carry)
        ctx_specs = [ctx_seq_spec(DEPTH, SEQ, MLA_KV_LORA), ctx_seq_spec(DEPTH, SEQ, MLA_D_ROPE)]
        ctx_shapes = [jax.ShapeDtypeStruct((BATCH, DEPTH, SEQ, MLA_KV_LORA), F32),
                      jax.ShapeDtypeStruct((BATCH, DEPTH, SEQ, MLA_D_ROPE), F32)]
    span = CONV_BLOCK + 2 * CONV_HALO - SUBLANES
    return pl.pallas_call(
        functools.partial(_proj_kernel, tm=tm, carry=carry is not None, layer=layer),
        grid=(n_w + n_tiles + 1,),
        in_specs=in_specs,
        out_specs=[pl.BlockSpec((tm, CONV_DIM), lambda i: (_tile(i, n_w + 1, tm), 0)), row(SLAB_W),
                   pl.BlockSpec((MLA_V_W, tm), lambda i: (0, _tile(i, n_w, tm)))] + ctx_specs,
        out_shape=[out(CONV_DIM, BF16), out(SLAB_W, BF16),
                   jax.ShapeDtypeStruct((MLA_V_W, N_TOK), BF16)] + ctx_shapes,
        scratch_shapes=[pltpu.VMEM((2, tm, CONV_DIM), F32),
                        pltpu.VMEM((CONV_BLOCK + 2 * CONV_HALO, CONV_DIM), F32),
                        pltpu.VMEM((SUBLANES, span, CONV_DIM), F32),
                        pltpu.VMEM((CONV_BLOCK, CONV_DIM), F32),
                        pltpu.VMEM(wt_all.shape[1:], BF16), pltpu.VMEM((MLA_Q_LORA, MLA_W), BF16)],
        compiler_params=_params(("arbitrary",)),
        name="mix_proj",
    )(*args)


def _cache_kv_kernel(ckv_ref, kr_ref, wk_ref, wvt_ref, kcat_ref, vt_ref):
    n = DEC_BATCH * PAST_LEN
    ckvb = ckv_ref[...].reshape(n, MLA_KV_LORA).astype(BF16)
    kn = _dot(ckvb, wk_ref[...])
    vt_ref[...] = _dot_nt(wvt_ref[...], ckvb).astype(BF16)
    kr = kr_ref[...].reshape(n, HEAD_PAD)
    for h in range(MLA_HEADS):
        sl = slice(h * HEAD_PAD, (h + 1) * HEAD_PAD)
        kcat_ref[:, sl] = (kn[:, sl] + kr).astype(BF16)


def _cache_kv(cache_ckv, cache_kr_pad, wk, wvt):
    n = DEC_BATCH * PAST_LEN
    return pl.pallas_call(
        _cache_kv_kernel,
        grid=(DEPTH,),
        in_specs=[pl.BlockSpec((DEC_BATCH, None, PAST_LEN, MLA_KV_LORA), lambda l: (0, l, 0, 0)),
                  pl.BlockSpec((DEC_BATCH, None, PAST_LEN, HEAD_PAD), lambda l: (0, l, 0, 0)),
                  pl.BlockSpec((None, MLA_KV_LORA, MLA_W), lambda l: (l, 0, 0)),
                  pl.BlockSpec((None, MLA_V_W, MLA_KV_LORA), lambda l: (l, 0, 0))],
        out_specs=[pl.BlockSpec((None, n, MLA_W), lambda l: (l, 0, 0)),
                   pl.BlockSpec((None, MLA_V_W, n), lambda l: (l, 0, 0))],
        out_shape=[jax.ShapeDtypeStruct((DEPTH, n, MLA_W), BF16), jax.ShapeDtypeStruct((DEPTH, MLA_V_W, n), BF16)],
        compiler_params=_params(("arbitrary",)),
        name="cache_kv",
    )(cache_ckv, cache_kr_pad, wk, wvt)


def _ret_kernel(*refs, t, hp, latent, carry, layer, seq_grid=False):
    if latent:
        lg_ref, q_ref, k_ref, v_ref, g_ref, s0f_ref, s0b_ref, o_ref, d_ref = refs
    elif carry:
        lg_ref, q_ref, k_ref, v_ref, g_ref, sf_prev_ref, sb_prev_ref, o_ref, sf_ref, sb_ref, d_ref = refs
    else:
        lg_ref, q_ref, k_ref, v_ref, g_ref, o_ref, sf_ref, sb_ref, d_ref = refs
    hblk = 0 if seq_grid else pl.program_id(0)
    first_seq = pl.program_id(0 if seq_grid else 1) == 0

    @pl.when(first_seq)
    def _():
        diff = (lax.broadcasted_iota(jnp.int32, (t, t), 0) - lax.broadcasted_iota(jnp.int32, (t, t), 1)).astype(F32)
        for hh in range(hp):
            lgf = lg_ref[layer, 0, hblk * hp + hh]
            lgb = lg_ref[layer, 1, hblk * hp + hh]
            d_ref[hh] = jnp.exp(jnp.where(diff >= 0, diff * lgf, -diff * lgb))

    if carry:
        sf_ref[0] = sf_prev_ref[...]
        sb_ref[0] = sb_prev_ref[...]
    pos = lax.broadcasted_iota(jnp.int32, (t, 1), 0).astype(F32)
    for hh in range(hp):
        lgf = lg_ref[layer, 0, hblk * hp + hh]
        lgb = lg_ref[layer, 1, hblk * hp + hh]
        q = q_ref[:, hh * RET_DK:(hh + 1) * RET_DK]
        k = k_ref[:, hh * RET_DK:(hh + 1) * RET_DK]
        v = v_ref[:, hh * RET_DV:(hh + 1) * RET_DV]
        p = (_dot_nt(q, k) * d_ref[hh]).astype(BF16)
        o = _dot(p, v)
        if latent:
            o = o + jnp.exp((pos + 1.0) * lgf) * _dot(q, s0f_ref[hh].astype(BF16))
            o = o + jnp.exp((t - pos) * lgb) * _dot(q, s0b_ref[hh].astype(BF16))
        else:
            kf = k.astype(F32)
            sf = _dot_tn((kf * jnp.exp((t - 1.0 - pos) * lgf)).astype(BF16), v)
            sb = _dot_tn((kf * jnp.exp(pos * lgb)).astype(BF16), v)
            if carry:
                sf_ref[1, hh] = sf
                sb_ref[1, hh] = sb
            else:
                sf_ref[hh] = sf
                sb_ref[hh] = sb
        o_ref[:, hh * RET_DV:(hh + 1) * RET_DV] = (g_ref[:, hh * RET_DV:(hh + 1) * RET_DV] * _norm_rows(o)).astype(BF16)


def _retention(log_g, slab, states, *, layer, latent=True):
    t = DEC_SEQ if latent else SEQ
    hp = 2 if latent else RET_HEADS
    n_seq = DEC_BATCH if latent else BATCH
    row0 = (N_CTX // t) if latent else 0

    def row(name, w):
        assert SLAB_COLS[name][0] % (hp * w) == 0
        first = SLAB_COLS[name][0] // (hp * w)
        return pl.BlockSpec((t, hp * w), lambda h, s: (row0 + s, first + h))

    smem = pl.BlockSpec(memory_space=pltpu.SMEM)
    out_shape = [jax.ShapeDtypeStruct((n_seq * t, RET_HEADS * RET_DV), BF16)]
    out_specs = [pl.BlockSpec((t, hp * RET_DV), lambda h, s: (s, h))]
    in_specs = [smem, row("rq", RET_DK), row("rk", RET_DK), row("rv", RET_DV), row("rg", RET_DV)]
    args = [log_g, slab, slab, slab, slab]
    carry = False
    if latent:
        st = pl.BlockSpec((None, None, hp, RET_DK, RET_DV), lambda h, s: (s, layer, h, 0, 0))
        in_specs += [st, st]
        args += list(states)
    else:
        st = pl.BlockSpec((None, hp, RET_DK, RET_DV), lambda h, s: (s, h, 0, 0))
        if states is None:
            out_specs += [st, st]
            out_shape += [jax.ShapeDtypeStruct((BATCH, RET_HEADS, RET_DK, RET_DV), F32)] * 2
        else:
            carry = True
            in_specs += [st, st]
            args += list(states)
            st2 = pl.BlockSpec((None, DEPTH, hp, RET_DK, RET_DV), lambda h, s: (s, 0, h, 0, 0))
            out_specs += [st2, st2]
            out_shape += [jax.ShapeDtypeStruct((BATCH, DEPTH, RET_HEADS, RET_DK, RET_DV), F32)] * 2

    return pl.pallas_call(
        functools.partial(_ret_kernel, t=t, hp=hp, latent=latent, carry=carry, layer=layer),
        grid=(RET_HEADS // hp, n_seq),
        in_specs=in_specs,
        out_specs=out_specs,
        out_shape=out_shape,
        scratch_shapes=[pltpu.VMEM((hp, t, t), F32)],
        compiler_params=_params(("arbitrary", "arbitrary")),
        name="retention_lat" if latent else "retention_ctx",
    )(*args)


def _attn_kernel(*refs, t, hp, latent):
    if latent:
        q_ref, k_ref, vt_ref, kc_ref, vtc_ref, o_ref = refs
    else:
        q_ref, k_ref, vt_ref, o_ref = refs
    qb = min(ATTN_QBLOCK, t)
    units = [(slice(hh * HEAD_PAD, (hh + 1) * HEAD_PAD), slice(hh * MLA_D_V, (hh + 1) * MLA_D_V),
              slice(b * qb, (b + 1) * qb)) for b in range(t // qb) for hh in range(hp)]
    def scores(unit):
        sl, _, rows = unit
        q = q_ref[rows, sl]
        s = [_dot_nt(k_ref[:, sl], q)]
        if latent:
            s.append(_dot_nt(kc_ref[:, sl], q))
        return s

    def softmax(s):
        m = functools.reduce(jnp.maximum, [jnp.max(x, axis=0, keepdims=True) for x in s])
        e = [jnp.exp2(x - m) for x in s]
        den = functools.reduce(jnp.add, [jnp.sum(x, axis=0, keepdims=True) for x in e])
        return [x.astype(BF16) for x in e], den

    def values(unit, e, den):
        _, vs, _ = unit
        o = _dot(vt_ref[vs, :], e[0])
        if latent:
            o = o + _dot(vtc_ref[vs, :], e[1])
        return o / den

    pairs = [units[u:u + 2] for u in range(0, len(units), 2)]
    s_next = [scores(u) for u in pairs[0]]
    sm_prev = None
    for g in range(len(pairs) + 1):
        s_cur = s_next
        if g + 1 < len(pairs):
            s_next = [scores(u) for u in pairs[g + 1]]
        sm_cur = [softmax(s) for s in s_cur] if g < len(pairs) else None
        if sm_prev is not None:
            (_, vs0, rows), (_, vs1, _) = pairs[g - 1]
            outs = [values(u, e, den) for u, (e, den) in zip(pairs[g - 1], sm_prev)]
            o_ref[rows, vs0.start:vs1.stop] = jnp.concatenate(outs, axis=0).T.astype(BF16)
        sm_prev = sm_cur


def _attention(slab, vt, cache, *, layer, latent=True):
    t = DEC_SEQ if latent else SEQ
    hp = 4 if latent else MLA_HEADS
    n_seq = DEC_BATCH if latent else BATCH
    row0 = (N_CTX // t) if latent else 0

    def row(name):
        assert SLAB_COLS[name][0] % (hp * HEAD_PAD) == 0
        first = SLAB_COLS[name][0] // (hp * HEAD_PAD)
        return pl.BlockSpec((t, hp * HEAD_PAD), lambda s, h: (row0 + s, first + h))

    col = pl.BlockSpec((hp * MLA_D_V, t), lambda s, h: (h, row0 + s))
    in_specs = [row("q"), row("kcat"), col]
    args = [slab, slab, vt]
    if latent:
        in_specs += [pl.BlockSpec((None, PAST_LEN, hp * HEAD_PAD), lambda s, h: (layer, s, h)),
                     pl.BlockSpec((None, hp * MLA_D_V, PAST_LEN), lambda s, h: (layer, h, s))]
        args += list(cache)
    return pl.pallas_call(
        functools.partial(_attn_kernel, t=t, hp=hp, latent=latent),
        grid=(n_seq, MLA_HEADS // hp),
        in_specs=in_specs,
        out_specs=pl.BlockSpec((t, hp * MLA_D_V), lambda s, h: (s, h)),
        out_shape=jax.ShapeDtypeStruct((n_seq * t, MLA_V_W), BF16),
        compiler_params=_params(("arbitrary", "arbitrary")),
        name="attention_lat" if latent else "attention_ctx",
    )(*args)


def _ctx_mixers_kernel(*refs, n_ret_in, carry, layer):
    ret_in = refs[:n_ret_in]
    q_ref, k_ref, vt_ref = refs[n_ret_in:n_ret_in + 3]
    r_ref, sf_ref, sb_ref, m_ref, d_ref = refs[n_ret_in + 3:]
    _ret_kernel(*ret_in, r_ref, sf_ref, sb_ref, d_ref, t=SEQ, hp=RET_HEADS, latent=False, carry=carry, layer=layer,
                seq_grid=True)
    _attn_kernel(q_ref, k_ref, vt_ref, m_ref, t=SEQ, hp=MLA_HEADS, latent=False)


def _ctx_mixers(log_g, slab, vt, states, *, layer):
    def slab_cols(name):
        lo, w = SLAB_COLS[name]
        assert lo % w == 0
        return pl.BlockSpec((SEQ, w), lambda s: (s, lo // w))

    st = pl.BlockSpec((None, RET_HEADS, RET_DK, RET_DV), lambda s: (s, 0, 0, 0))
    in_specs = [pl.BlockSpec(memory_space=pltpu.SMEM)] + [slab_cols(n) for n in ("rq", "rk", "rv", "rg")]
    args = [log_g, slab, slab, slab, slab]
    if states is None:
        st_out = st
        st_shape = jax.ShapeDtypeStruct((BATCH, RET_HEADS, RET_DK, RET_DV), F32)
    else:
        in_specs += [st, st]
        args += list(states)
        st_out = pl.BlockSpec((None, DEPTH, RET_HEADS, RET_DK, RET_DV), lambda s: (s, 0, 0, 0, 0))
        st_shape = jax.ShapeDtypeStruct((BATCH, DEPTH, RET_HEADS, RET_DK, RET_DV), F32)
    n_ret_in = len(in_specs)
    in_specs += [slab_cols("q"), slab_cols("kcat"), pl.BlockSpec((MLA_V_W, SEQ), lambda s: (0, s))]
    args += [slab, slab, vt]
    r, sf, sb, m = pl.pallas_call(
        functools.partial(_ctx_mixers_kernel, n_ret_in=n_ret_in, carry=states is not None, layer=layer),
        grid=(BATCH,),
        in_specs=in_specs,
        out_specs=[pl.BlockSpec((SEQ, RET_HEADS * RET_DV), lambda s: (s, 0)), st_out, st_out,
                   pl.BlockSpec((SEQ, MLA_V_W), lambda s: (s, 0))],
        out_shape=[jax.ShapeDtypeStruct((N_CTX, RET_HEADS * RET_DV), BF16), st_shape, st_shape,
                   jax.ShapeDtypeStruct((N_CTX, MLA_V_W), BF16)],
        scratch_shapes=[pltpu.VMEM((RET_HEADS, SEQ, SEQ), F32)],
        compiler_params=_params(("arbitrary",)),
        name="ctx_mixers",
    )(*args)
    return r, sf, sb, m


def _merge_kernel(x_ref, mod_ref, a_ref, rc_ref, rl_ref, mc_ref, ml_ref, sig_ref, wc_ref, wr_ref, wm_ref, wo_ref,
                  g_ref, b_ref, o_ref, wc_bf, wr_bf, wm_bf, wo_bf, *, tm, layer):
    g_ref, b_ref = (r.at[layer, 1:2, :] for r in (g_ref, b_ref))
    i = pl.program_id(0)
    is_ctx = i < N_WSTEPS + N_CTX // tm

    @pl.when(i < N_WSTEPS)
    def _():
        _stage_chunk(i, wc_ref, wc_bf)
        _stage_chunk(i, wr_ref, wr_bf)
        _stage_chunk(i, wo_ref, wo_bf)
        _stage_chunk(i, wm_ref, wm_bf)

    @pl.when(i >= N_WSTEPS)
    def _():
        for half in range(ROW_HALVES):
            rows = slice(half * tm // ROW_HALVES, (half + 1) * tm // ROW_HALVES)
            x = x_ref[rows, :]
            r = jnp.where(is_ctx, rc_ref[rows, :], rl_ref[rows, :])
            m = jnp.where(is_ctx, mc_ref[rows, :], ml_ref[rows, :])
            merged = sig_ref[rows, 0:D_MODEL] * _dot(a_ref[rows, :], wc_bf[...])
            merged = merged + sig_ref[rows, D_MODEL:2 * D_MODEL] * _dot(r, wr_bf[...])
            merged = merged + sig_ref[rows, 2 * D_MODEL:] * _dot(m, wm_bf[...])
            y = _dot(merged.astype(BF16), wo_bf[...])
            z = DEEPNORM_ALPHA * x + mod_ref[5:6, :] * y
            o_ref[rows, :] = _norm_rows(z) * g_ref[...] + b_ref[...]


def _merge(x, mods_all, a, r_pair, m_pair, sig, wc, wr, wm, wo, ln_g, ln_b, *, layer, tm=512):
    def row(w):
        return _row_spec(tm, w, N_WSTEPS)

    def pair(w):
        return [_ctx_row_spec(tm, w, N_WSTEPS), _lat_row_spec(tm, w, N_WSTEPS)]

    return pl.pallas_call(
        functools.partial(_merge_kernel, tm=tm, layer=layer),
        grid=(N_WSTEPS + N_TOK // tm,),
        in_specs=[row(D_MODEL), _mod_spec(tm, layer, N_WSTEPS), row(CONV_DIM)] + pair(RET_HEADS * RET_DV)
        + pair(MLA_V_W) + [row(GATE_W), _wchunk_spec(wc, layer), _wchunk_spec(wr, layer), _wchunk_spec(wm, layer),
                         _wchunk_spec(wo, layer), _resident(ln_g.shape), _resident(ln_b.shape)],
        out_specs=row(D_MODEL),
        out_shape=jax.ShapeDtypeStruct((N_TOK, D_MODEL), F32),
        scratch_shapes=[pltpu.VMEM((CONV_DIM, D_MODEL), BF16), pltpu.VMEM((RET_HEADS * RET_DV, D_MODEL), BF16),
                        pltpu.VMEM((MLA_V_W, D_MODEL), BF16), pltpu.VMEM((D_MODEL, D_MODEL), BF16)],
        compiler_params=_params(("arbitrary",)),
        name="merge",
    )(x, mods_all, a, *r_pair, *m_pair, sig, wc, wr, wm, wo, ln_g, ln_b)


def _rope_tables():
    f32 = np.float32
    rows = DEC_SEQ // GRID_W
    row_id = np.repeat(np.arange(rows, dtype=f32), GRID_W)
    col_id = np.tile(np.arange(GRID_W, dtype=f32), rows)
    inv_freq = (f32(ROPE_BASE) ** (-np.arange(ROPE_AXIS_HALF, dtype=f32) / f32(ROPE_AXIS_HALF))).astype(f32)
    ang = np.stack([row_id[:, None] * inv_freq, col_id[:, None] * inv_freq], axis=1).astype(f32)
    cos = np.cos(ang).astype(f32)
    sin = np.sin(ang).astype(f32)
    cos32 = np.stack([cos, cos], axis=2).reshape(DEC_SEQ, MLA_D_ROPE)
    zero = np.zeros_like(sin)
    sin_lo32 = np.stack([-sin, zero], axis=2).reshape(DEC_SEQ, MLA_D_ROPE)
    sin_hi32 = np.stack([zero, sin], axis=2).reshape(DEC_SEQ, MLA_D_ROPE)
    tail = HEAD_PAD - ROPE_LANE0 - MLA_D_ROPE
    cos_t = np.concatenate([np.ones((DEC_SEQ, ROPE_LANE0), f32), cos32, np.ones((DEC_SEQ, tail), f32)], axis=1)
    cos_t = np.concatenate([np.ones((DEC_SEQ, HEAD_PAD), f32), cos_t], axis=0)

    def sin_table(s32):
        return np.pad(s32, ((DEC_SEQ, 0), (ROPE_LANE0, tail)))

    return tuple(jnp.asarray(t, dtype=F32) for t in (cos_t, sin_table(sin_lo32), sin_table(sin_hi32)))


def _head_pad_cols(w, width):
    k = w.shape[0]
    w = w.reshape(k, MLA_HEADS, width)
    return jnp.pad(w, ((0, 0), (0, 0), (0, HEAD_PAD - width))).reshape(k, MLA_W)


def kernel(x_prompt, x_sample, cache_mla_ckv, cache_mla_krope, state_ret_fwd, state_ret_bwd, c, c_ctx, ada_w, ada_b, ffn1_w_in, ffn1_w_out, ffn2_w_in, ffn2_w_out, post_ln_g, post_ln_b, mix_w_in, conv_w_dw, conv_b_dw, conv_ln_g, conv_ln_b, conv_w_out, ret_decay_fwd, ret_decay_bwd, ret_w_out, mla_q_norm, mla_w_uq, mla_kv_norm, mla_w_ukv, mla_w_out, mix_w_o):
    assert DEPTH == 2
    cvec = jnp.concatenate([c_ctx[None, :], c, jnp.zeros((N_MOD_ROWS - 1 - DEC_BATCH, D_MODEL), F32)], axis=0)
    mods_all = _ada_mods(cvec, ada_w, ada_b).reshape(DEPTH, N_MOD_ROWS, N_MODS, D_MODEL)
    rope_tabs = _rope_tables()

    w_ukv = mla_w_ukv.reshape(DEPTH, MLA_KV_LORA, MLA_HEADS, MLA_D_NOPE + MLA_D_V)
    wk_all = _head_pad_cols(w_ukv[..., :MLA_D_NOPE].reshape(DEPTH * MLA_KV_LORA, -1), MLA_D_NOPE)
    wk_all = wk_all.reshape(DEPTH, MLA_KV_LORA, MLA_W).astype(BF16)
    wvt_all = jnp.swapaxes(w_ukv[..., MLA_D_NOPE:].reshape(DEPTH, MLA_KV_LORA, MLA_V_W), 1, 2).astype(BF16)
    kr_tail = HEAD_PAD - ROPE_LANE0 - MLA_D_ROPE
    cache_kr_pad = jnp.pad(cache_mla_krope, ((0, 0), (0, 0), (0, 0), (ROPE_LANE0, kr_tail)))
    kcat_c, vt_c = _cache_kv(cache_mla_ckv, cache_kr_pad, wk_all, wvt_all)
    log_g = jnp.stack([jax.nn.log_sigmoid(ret_decay_fwd), jax.nn.log_sigmoid(ret_decay_bwd)], axis=1)
    wt_all = jnp.swapaxes(mix_w_in, 1, 2)
    conv_w = (conv_w_dw, conv_b_dw, conv_ln_g, conv_ln_b)
    ln_g, ln_b = post_ln_g, post_ln_b
    xs = (x_prompt.reshape(N_CTX, D_MODEL), x_sample.reshape(N_LAT, D_MODEL))
    ctx_carry = None
    state_carry = None
    for l in range(DEPTH):
        last = l == DEPTH - 1
        x = _ffn(xs, mods_all, ffn1_w_in, ffn1_w_out, ln_g, ln_b, layer=l, which=0)[0]

        a, slab, vt, ckv, kr = _proj(x, mods_all, rope_tabs, wt_all, mla_w_uq, wk_all, wvt_all, mla_q_norm, mla_kv_norm,
                                     conv_w, ctx_carry, layer=l)
        ctx_carry = (ckv, kr)

        r_ctx, sf, sb, m_ctx = _ctx_mixers(log_g, slab, vt, state_carry, layer=l)
        state_carry = (sf, sb)
        r_lat = _retention(log_g, slab, (state_ret_fwd, state_ret_bwd), layer=l)[0]
        m_lat = _attention(slab, vt, (kcat_c, vt_c), layer=l)

        x = _merge(x, mods_all, a, (r_ctx, r_lat), (m_ctx, m_lat), slab, conv_w_out, ret_w_out, mla_w_out, mix_w_o,
                   ln_g, ln_b, layer=l)
        xs = _ffn((x,), mods_all, ffn2_w_in, ffn2_w_out, ln_g, ln_b, layer=l, which=2, split_out=last)

    y_ctx, y_lat = xs
    return (y_ctx.reshape(BATCH, SEQ, D_MODEL), y_lat.reshape(DEC_BATCH, DEC_SEQ, D_MODEL),
            ctx_carry[0], ctx_carry[1], state_carry[0], state_carry[1])
```

```python
import functools
import math

import jax
import jax.numpy as jnp
import numpy as np
from jax import lax
from jax.experimental import pallas as pl
from jax.experimental.pallas import tpu as pltpu

F32 = jnp.float32
BF16 = jnp.bfloat16

D_MODEL = 1024
BATCH = 16
SEQ = 256
DEPTH = 2
DEC_BATCH = 4
DEC_SEQ = 1024
PAST_LEN = 256
GRID_W = 64
D_FF = 2816
N_MODS = 9
CONV_DIM = 512
CONV_WIDTH = 31
RET_HEADS = 4
RET_DK = 128
RET_DV = 256
MLA_HEADS = 8
MLA_Q_LORA = 512
MLA_KV_LORA = 256
MLA_D_NOPE = 64
MLA_D_ROPE = 32
MLA_D_V = 64
ROPE_AXIS_HALF = MLA_D_ROPE // 4
ROPE_BASE = 10000.0
DEEPNORM_ALPHA = (2 * DEPTH) ** 0.25
LN_EPS = 1e-5
RMS_EPS = 1e-6

N_CTX = BATCH * SEQ
N_LAT = DEC_BATCH * DEC_SEQ
N_TOK = N_CTX + N_LAT
N_MOD_ROWS = 8
HEAD_PAD = 128
ROPE_LANE0 = MLA_D_NOPE
MLA_W = MLA_HEADS * HEAD_PAD
MLA_V_W = MLA_HEADS * MLA_D_V
MAIN_W = 2 * CONV_DIM + 2 * RET_HEADS * RET_DK + 2 * RET_HEADS * RET_DV + MLA_Q_LORA + MLA_KV_LORA
N_BRANCHES = 3
GATE_W = N_BRANCHES * D_MODEL
SLAB_COLS = {}
for _name, _w in (("sig", GATE_W), ("rq", RET_HEADS * RET_DK), ("rk", RET_HEADS * RET_DK), ("rv", RET_HEADS * RET_DV),
                  ("rg", RET_HEADS * RET_DV), ("q", MLA_W), ("kcat", MLA_W)):
    SLAB_COLS[_name] = (sum(w for _, w in SLAB_COLS.values()), _w)
SLAB_W = sum(w for _, w in SLAB_COLS.values())
SUBLANES = 8
CONV_HALO = 16
CONV_BLOCK = 256
VMEM_LIMIT = 56 * 1024 * 1024
N_WSTEPS = 8
PROJ_WSTEPS = 6
MXU_COLS = 256
FFN_CHUNKS = ((0, 6 * MXU_COLS), (6 * MXU_COLS, D_FF))
ROW_HALVES = 2
ATTN_QBLOCK = 256
ATTN_QSCALE = (MLA_D_NOPE + MLA_D_ROPE) ** -0.5 * math.log2(math.e)


def _dot(a, b):
    return jnp.dot(a, b, preferred_element_type=F32)


def _dot_nt(a, b):
    return lax.dot_general(a, b, (((1,), (1,)), ((), ())), preferred_element_type=F32)


def _dot_tn(a, b):
    return lax.dot_general(a, b, (((0,), (0,)), ((), ())), preferred_element_type=F32)


def _sigmoid(x):
    return 1.0 / (1.0 + jnp.exp(-x))


def _norm_rows(z):
    mu = jnp.mean(z, axis=-1, keepdims=True)
    zc = z - mu
    var = jnp.mean(zc * zc, axis=-1, keepdims=True)
    return zc * lax.rsqrt(var + LN_EPS)


def _rms_rows(z):
    return z * lax.rsqrt(jnp.mean(z * z, axis=-1, keepdims=True) + RMS_EPS)


def _resident(shape):
    zeros = (0,) * len(shape)
    return pl.BlockSpec(shape, lambda *_: zeros, pipeline_mode=pl.Buffered(1))


def _layer_resident(w, layer):
    zeros = (0,) * (w.ndim - 1)
    return pl.BlockSpec((None,) + w.shape[1:], lambda *_: (layer,) + zeros, pipeline_mode=pl.Buffered(1))


def _tile(i, n_w, tm):
    return jnp.clip(i - n_w, 0, N_TOK // tm - 1)


def _row_spec(tm, w, n_w):
    return pl.BlockSpec((tm, w), lambda i: (_tile(i, n_w, tm), 0))


def _ctx_row_spec(tm, w, n_w):
    last = N_CTX // tm - 1
    return pl.BlockSpec((tm, w), lambda i: (jnp.minimum(_tile(i, n_w, tm), last), 0))


def _lat_row_spec(tm, w, n_w):
    first = N_CTX // tm
    return pl.BlockSpec((tm, w), lambda i: (jnp.maximum(_tile(i, n_w, tm) - first, 0), 0))


def _mod_spec(tm, layer, n_w):
    n_ctx_tiles = N_CTX // tm
    tiles_per_seq = DEC_SEQ // tm

    def index(i):
        j = _tile(i, n_w, tm)
        return (layer, jnp.where(j < n_ctx_tiles, 0, 1 + (j - n_ctx_tiles) // tiles_per_seq), 0, 0)

    return pl.BlockSpec((None, None, N_MODS, D_MODEL), index)


def _wchunk_spec(w, layer, n_w=N_WSTEPS):
    _, rows, cols = w.shape
    return pl.BlockSpec((None, rows // n_w, cols), lambda i: (layer, jnp.minimum(i, n_w - 1), 0))


def _stage_chunk(i, src_ref, dst_ref):
    rows = src_ref.shape[0]
    dst_ref[pl.ds(pl.multiple_of(i * rows, rows), rows), :] = src_ref[...].astype(BF16)


def _params(semantics):
    return pltpu.CompilerParams(dimension_semantics=semantics, vmem_limit_bytes=VMEM_LIMIT)


ADA_COL_PARTS = 4
ADA_BUFFERS = 3


def _ada_kernel(c_ref, w_hbm, b_ref, o_ref, buf, sem):
    tn = N_MODS * D_MODEL // ADA_COL_PARTS
    parts = [(l, j) for l in range(DEPTH) for j in range(ADA_COL_PARTS)]

    def part_copy(k):
        l, j = parts[k]
        slot = k % ADA_BUFFERS
        return pltpu.make_async_copy(w_hbm.at[l, :, pl.ds(j * tn, tn)], buf.at[slot], sem.at[slot])

    for k in range(min(ADA_BUFFERS, len(parts))):
        part_copy(k).start()
    c = c_ref[...]
    h = (c * _sigmoid(c)).astype(BF16)
    for k, (l, j) in enumerate(parts):
        part_copy(k).wait()
        cols = slice(j * tn, (j + 1) * tn)
        o_ref[l, :, cols] = _dot(h, buf[k % ADA_BUFFERS].astype(BF16)) + b_ref[l:l + 1, cols]
        if k + ADA_BUFFERS < len(parts):
            part_copy(k + ADA_BUFFERS).start()


def _ada_mods(cvec, ada_w, ada_b):
    n_out = N_MODS * D_MODEL
    tn = n_out // ADA_COL_PARTS
    return pl.pallas_call(
        _ada_kernel,
        in_specs=[pl.BlockSpec(memory_space=pltpu.VMEM), pl.BlockSpec(memory_space=pl.ANY),
                  pl.BlockSpec(memory_space=pltpu.VMEM)],
        out_specs=pl.BlockSpec(memory_space=pltpu.VMEM),
        out_shape=jax.ShapeDtypeStruct((DEPTH, N_MOD_ROWS, n_out), F32),
        scratch_shapes=[pltpu.VMEM((ADA_BUFFERS, D_MODEL, tn), F32), pltpu.SemaphoreType.DMA((ADA_BUFFERS,))],
        compiler_params=pltpu.CompilerParams(vmem_limit_bytes=VMEM_LIMIT),
        name="ada_mods",
    )(cvec, ada_w, ada_b)


def _ffn_kernel(*refs, layer, which, tm, n_x, n_out):
    base = N_MODS // 3 * which
    x_refs = refs[:n_x]
    mod_ref, win_ref, wout_ref, g_ref, b_ref = refs[n_x:n_x + 5]
    g_ref, b_ref = (r.at[layer, which:which + 1, :] for r in (g_ref, b_ref))
    o_refs = refs[n_x + 5:n_x + 5 + n_out]
    win_bf, wout_bf = refs[n_x + 5 + n_out:]
    i = pl.program_id(0)
    is_ctx = i < N_WSTEPS + N_CTX // tm

    @pl.when(i < N_WSTEPS)
    def _():
        _stage_chunk(i, win_ref, win_bf)
        _stage_chunk(i, wout_ref, wout_bf)

    @pl.when(i >= N_WSTEPS)
    def _():
        shift = mod_ref[base:base + 1, :]
        scale = mod_ref[base + 1:base + 2, :]
        gate = mod_ref[base + 2:base + 3, :]
        halves = [slice(k * tm // ROW_HALVES, (k + 1) * tm // ROW_HALVES) for k in range(ROW_HALVES)]
        results = []
        for rows in halves:
            if n_x == 2:
                x = jnp.where(is_ctx, x_refs[0][rows, :], x_refs[1][rows, :])
            else:
                x = x_refs[0][rows, :]
            h = (x * (1.0 + scale) + shift).astype(BF16)
            y = None
            for lo, hi in FFN_CHUNKS:
                g = _dot(h, win_bf[:, lo:hi])
                u = _dot(h, win_bf[:, D_FF + lo:D_FF + hi])
                a = (g * _sigmoid(g) * u).astype(BF16)
                yc = _dot(a, wout_bf[lo:hi, :])
                y = yc if y is None else y + yc
            z = DEEPNORM_ALPHA * x + 0.5 * gate * y
            res = _norm_rows(z) * g_ref[...] + b_ref[...]
            if n_out == 1:
                o_refs[0][rows, :] = res
            results.append(res)
        if n_out == 2:
            @pl.when(is_ctx)
            def _():
                for rows, res in zip(halves, results):
                    o_refs[0][rows, :] = res

            @pl.when(jnp.logical_not(is_ctx))
            def _():
                for rows, res in zip(halves, results):
                    o_refs[1][rows, :] = res


def _ffn(xs, mods_all, w_in, w_out, ln_g, ln_b, *, layer, which, split_out=False, tm=512):
    row = _row_spec(tm, D_MODEL, N_WSTEPS)
    pair = [_ctx_row_spec(tm, D_MODEL, N_WSTEPS), _lat_row_spec(tm, D_MODEL, N_WSTEPS)]
    if split_out:
        out_specs = pair
        out_shape = [jax.ShapeDtypeStruct((N_CTX, D_MODEL), F32), jax.ShapeDtypeStruct((N_LAT, D_MODEL), F32)]
    else:
        out_specs = [row]
        out_shape = [jax.ShapeDtypeStruct((N_TOK, D_MODEL), F32)]
    return pl.pallas_call(
        functools.partial(_ffn_kernel, layer=layer, which=which, tm=tm, n_x=len(xs), n_out=len(out_specs)),
        grid=(N_WSTEPS + N_TOK // tm,),
        in_specs=(pair if len(xs) == 2 else [row]) + [
            _mod_spec(tm, layer, N_WSTEPS), _wchunk_spec(w_in, layer), _wchunk_spec(w_out, layer),
            _resident(ln_g.shape), _resident(ln_b.shape)],
        out_specs=out_specs,
        out_shape=out_shape,
        scratch_shapes=[pltpu.VMEM((D_MODEL, 2 * D_FF), BF16), pltpu.VMEM((D_FF, D_MODEL), BF16)],
        compiler_params=_params(("arbitrary",)),
        name="ffn",
    )(*xs, mods_all, w_in, w_out, ln_g, ln_b)


def _conv_stages(prev_halo, main, next_halo, w_ref, b_ref, g_ref, beta_ref, o_ref, pad_ref, shift_ref, acc_ref):
    rows = 32
    lanes = 128
    first = CONV_HALO - CONV_WIDTH // 2

    def setup(after):
        del after
        pad_ref[0:CONV_HALO, :] = prev_halo()
        pad_ref[CONV_HALO:CONV_HALO + CONV_BLOCK, :] = main()
        pad_ref[CONV_HALO + CONV_BLOCK:, :] = next_halo()
        span = shift_ref.shape[1]
        for ph in range(SUBLANES):
            shift_ref[ph] = pad_ref[ph:ph + span, :]

    def taps(c, r, start_from):
        cs = slice(c * lanes, (c + 1) * lanes)
        acc = jnp.broadcast_to(b_ref[:, cs], (rows, lanes)) + start_from
        for j in range(CONV_WIDTH):
            ph = (first + j) % SUBLANES
            start = r * rows + (first + j) - ph
            acc = acc + w_ref[j:j + 1, cs] * shift_ref[ph, start:start + rows, cs]
        acc_ref[r * rows:(r + 1) * rows, cs] = acc

    def finish(after):
        del after
        y = _norm_rows(acc_ref[...]) * g_ref[...] + beta_ref[...]
        o_ref[...] = (y * _sigmoid(y)).astype(BF16)

    def lane_group_half(c, half, after):
        if after is None:
            start_from = jnp.zeros((rows, lanes), F32)
        else:
            bits = lax.bitcast_convert_type(after, jnp.int32)
            zero = lax.shift_right_logical(lax.shift_right_logical(bits, 16), 16).astype(F32)
            start_from = jnp.concatenate([zero] * (rows // SUBLANES), axis=0)
        per_half = CONV_BLOCK // rows // 2
        for r in range(half * per_half, (half + 1) * per_half):
            taps(c, r, start_from)

    chunks = [functools.partial(lane_group_half, c, half) for c in range(CONV_DIM // lanes) for half in range(2)]
    return [setup] + chunks + [finish]


def _proj_kernel(*refs, tm, carry, layer):
    (x_ref, mod_ref, cos_ref, sin_lo_ref, sin_hi_ref, wt_ref, wuq_ref, place_ref, wk_ref, wvt_ref, gq_ref,
     gkv_ref, cw_ref, cb_ref, cg_ref, cbeta_ref) = refs[:16]
    refs = refs[16:]
    gq_ref, gkv_ref, cb_ref, cg_ref, cbeta_ref = (
        r.at[layer:layer + 1, :] for r in (gq_ref, gkv_ref, cb_ref, cg_ref, cbeta_ref))
    if carry:
        ckv_prev_ref, kr_prev_ref = refs[:2]
        refs = refs[2:]
    (a_ref, slab_ref, vt_ref, ckv_ref, kr_ref, glu_ring, pad_ref, shift_ref, acc_ref, wt_bf, wuq_bf) = refs
    sig_ref, rq_ref, rk_ref, rv_ref, rg_ref, q_ref, kcat_ref = (
        slab_ref.at[:, lo:lo + w] for lo, w in (SLAB_COLS[n] for n in ("sig", "rq", "rk", "rv", "rg", "q", "kcat")))
    t = pl.program_id(0) - PROJ_WSTEPS
    n_tiles = N_TOK // tm
    n_ctx_tiles = N_CTX // tm
    tiles_per_seq = DEC_SEQ // tm
    is_ctx = t < n_ctx_tiles
    pos = (t - 1 - n_ctx_tiles) % tiles_per_seq
    conv_latent = t - 1 >= n_ctx_tiles
    has_prev = jnp.logical_and(conv_latent, pos != 0)
    has_next = jnp.logical_and(conv_latent, pos != tiles_per_seq - 1)
    slot = t % 2
    no_halo = jnp.zeros((CONV_HALO, CONV_DIM), F32)

    def conv_prev_tile(next_rows):
        return _conv_stages(
            lambda: jnp.where(has_prev, glu_ring[slot, tm - CONV_HALO:tm, :], no_halo),
            lambda: glu_ring[1 - slot],
            lambda: jnp.where(has_next, next_rows, no_halo),
            cw_ref, cb_ref, cg_ref, cbeta_ref, a_ref, pad_ref, shift_ref, acc_ref)

    @pl.when(t < 0)
    def _():
        _stage_chunk(pl.program_id(0), wt_ref, wt_bf)

    @pl.when(t == 0)
    def _():
        glu_ring[...] = jnp.zeros(glu_ring.shape, F32)
        wuq_bf[...] = _dot(wuq_ref[...].astype(BF16), place_ref[...]).astype(BF16)

    @pl.when(t == n_tiles)
    def _():
        for step in conv_prev_tile(no_halo):
            step(None)

    @pl.when(jnp.logical_and(t >= 0, t < n_tiles))
    def _():
        _proj_tile(x_ref, mod_ref, cos_ref, sin_lo_ref, sin_hi_ref, wt_bf, wuq_bf, wk_ref, wvt_ref, gq_ref,
                   gkv_ref, ckv_prev_ref if carry else None, kr_prev_ref if carry else None,
                   rq_ref, rk_ref, rv_ref, rg_ref, q_ref, kcat_ref, vt_ref, sig_ref, ckv_ref, kr_ref,
                   glu_ring, conv_prev_tile, slot, is_ctx, tm)


def _proj_tile(x_ref, mod_ref, cos_ref, sin_lo_ref, sin_hi_ref, wt_ref, wuq_ref, wk_ref, wvt_ref, gq_ref,
               gkv_ref, ckv_prev_ref, kr_prev_ref,
               rq_ref, rk_ref, rv_ref, rg_ref, q_ref, kcat_ref, vt_ref, sig_ref, ckv_ref, kr_ref,
               glu_ring, conv_prev_tile, slot, is_ctx, tm):
    carry = ckv_prev_ref is not None
    x = x_ref[...]
    u = (x * (1.0 + mod_ref[4:5, :]) + mod_ref[3:4, :]).astype(BF16)
    widths = (CONV_DIM, CONV_DIM, RET_HEADS * RET_DK, RET_HEADS * RET_DK, RET_HEADS * RET_DV, RET_HEADS * RET_DV,
              MLA_Q_LORA, MLA_KV_LORA)
    starts = [sum(widths[:n]) for n in range(len(widths))]

    def proj(n):
        return _dot_nt(u, wt_ref[starts[n]:starts[n] + widths[n], :])

    glu = proj(0) * _sigmoid(proj(1))
    conv_steps = conv_prev_tile(glu[0:CONV_HALO, :])

    def conv(n, result):
        for _ in range(n):
            conv_steps.pop(0)(result[0:SUBLANES, 0:128])

    conv(1, glu)
    glu_ring[slot] = glu

    mq = proj(6)
    mkv = proj(7)
    conv(2, mq)
    cos = cos_ref[...]
    sin_lo = sin_lo_ref[...]
    sin_hi = sin_hi_ref[...]

    def rotary(v):
        up = pltpu.roll(v, HEAD_PAD - ROPE_AXIS_HALF, 1)
        down = pltpu.roll(v, ROPE_AXIS_HALF, 1)
        return v * cos + up * sin_lo + down * sin_hi

    qn = (_rms_rows(mq) * gq_ref[...]).astype(BF16)
    qm = _dot(qn, wuq_ref[...])
    ckv = _rms_rows(mkv) * gkv_ref[...]
    ckvb = ckv.astype(BF16)
    kn = _dot(ckvb, wk_ref[...])
    vt_ref[...] = _dot_nt(wvt_ref[...], ckvb).astype(BF16)
    kr_grp = _dot_nt(u, wt_ref[MAIN_W:MAIN_W + HEAD_PAD, :])
    conv(2, kn)
    lane = lax.broadcasted_iota(jnp.int32, kr_grp.shape, 1)
    in_rope = jnp.logical_and(lane >= ROPE_LANE0, lane < ROPE_LANE0 + MLA_D_ROPE)
    kr = jnp.where(in_rope, pltpu.roll(kr_grp, ROPE_LANE0, 1), 0.0)
    kr_rot = rotary(kr)
    for h in range(MLA_HEADS):
        sl = slice(h * HEAD_PAD, (h + 1) * HEAD_PAD)
        q_ref[:, sl] = (rotary(qm[:, sl]) * ATTN_QSCALE).astype(BF16)
        kcat_ref[:, sl] = (kn[:, sl] + kr_rot).astype(BF16)

    gate0 = MAIN_W + MLA_D_ROPE

    def branch_gate(blk):
        cols = slice(blk * D_MODEL, (blk + 1) * D_MODEL)
        gate = _dot_nt(u, wt_ref[gate0 + cols.start:gate0 + cols.stop, :])
        sig_ref[:, cols] = _sigmoid(gate).astype(BF16)
        return gate

    for blk in range(N_BRANCHES):
        conv(1, branch_gate(blk))
    rg = proj(5)
    rg_ref[...] = (rg * _sigmoid(rg)).astype(BF16)
    conv(2, rg)
    rv_ref[...] = proj(4).astype(BF16)
    rk_ref[...] = (proj(3) * (RET_DK ** -0.5)).astype(BF16)
    rq_ref[...] = proj(2).astype(BF16)

    @pl.when(is_ctx)
    def _():
        seqs = tm // SEQ
        ckv3 = ckv.reshape(seqs, SEQ, MLA_KV_LORA)
        kr3 = kr_grp[:, :MLA_D_ROPE].reshape(seqs, SEQ, MLA_D_ROPE)
        if carry:
            ckv_ref[:, 0] = ckv_prev_ref[...]
            kr_ref[:, 0] = kr_prev_ref[...]
            ckv_ref[:, 1] = ckv3
            kr_ref[:, 1] = kr3
        else:
            ckv_ref[...] = ckv3
            kr_ref[...] = kr3


def _proj(x, mods_all, rope_tabs, wt_all, wuq, wk, wvt, gq, gkv, conv_w, carry, *, layer):
    tm = CONV_BLOCK
    n_w = PROJ_WSTEPS
    qk = MLA_D_NOPE + MLA_D_ROPE
    place_np = np.zeros((MLA_HEADS * qk, MLA_W), np.float32)
    cols = np.arange(MLA_HEADS * qk)
    place_np[cols, cols // qk * HEAD_PAD + cols % qk] = 1.0
    place = jnp.asarray(place_np, dtype=BF16)
    n_tiles = N_TOK // tm
    n_ctx_tiles = N_CTX // tm
    tiles_per_seq = DEC_SEQ // tm
    seqs = tm // SEQ

    def rope_index(i):
        j = _tile(i, n_w, tm)
        return (jnp.where(j < n_ctx_tiles, 0, tiles_per_seq + (j - n_ctx_tiles) % tiles_per_seq), 0)

    def row(w):
        return _row_spec(tm, w, n_w)

    def out(w, dt):
        return jax.ShapeDtypeStruct((N_TOK, w), dt)

    def ctx_seq_spec(*tail):
        zeros = (0,) * len(tail)
        return pl.BlockSpec((seqs,) + tail, lambda i: (jnp.minimum(_tile(i, n_w, tm), n_ctx_tiles - 1),) + zeros)

    rope = pl.BlockSpec((tm, HEAD_PAD), rope_index)
    in_specs = [row(D_MODEL), _mod_spec(tm, layer, n_w), rope, rope, rope,
                _wchunk_spec(wt_all, layer, n_w), _layer_resident(wuq, layer), _resident(place.shape),
                _layer_resident(wk, layer), _layer_resident(wvt, layer), _resident(gq.shape), _resident(gkv.shape),
                _layer_resident(conv_w[0], layer)] + [_resident(w.shape) for w in conv_w[1:]]
    args = [x, mods_all, *rope_tabs, wt_all, wuq, place, wk, wvt, gq, gkv, *conv_w]
    if carry is None:
        ctx_specs = [ctx_seq_spec(SEQ, MLA_KV_LORA), ctx_seq_spec(SEQ, MLA_D_ROPE)]
        ctx_shapes = [jax.ShapeDtypeStruct((BATCH, SEQ, MLA_KV_LORA), F32),
                      jax.ShapeDtypeStruct((BATCH, SEQ, MLA_D_ROPE), F32)]
    else:
        in_specs += [ctx_seq_spec(SEQ, MLA_KV_LORA), ctx_seq_spec(SEQ, MLA_D_ROPE)]
        args += list(carry)
        ctx_specs = [ctx_seq_spec(DEPTH, SEQ, MLA_KV_LORA), ctx_seq_spec(DEPTH, SEQ, MLA_D_ROPE)]
        ctx_shapes = [jax.ShapeDtypeStruct((BATCH, DEPTH, SEQ, MLA_KV_LORA), F32),
                      jax.ShapeDtypeStruct((BATCH, DEPTH, SEQ, MLA_D_ROPE), F32)]
    span = CONV_BLOCK + 2 * CONV_HALO - SUBLANES
    return pl.pallas_call(
        functools.partial(_proj_kernel, tm=tm, carry=carry is not None, layer=layer),
        grid=(n_w + n_tiles + 1,),
        in_specs=in_specs,
        out_specs=[pl.BlockSpec((tm, CONV_DIM), lambda i: (_tile(i, n_w + 1, tm), 0)), row(SLAB_W),
                   pl.BlockSpec((MLA_V_W, tm), lambda i: (0, _tile(i, n_w, tm)))] + ctx_specs,
        out_shape=[out(CONV_DIM, BF16), out(SLAB_W, BF16),
                   jax.ShapeDtypeStruct((MLA_V_W, N_TOK), BF16)] + ctx_shapes,
        scratch_shapes=[pltpu.VMEM((2, tm, CONV_DIM), F32),
                        pltpu.VMEM((CONV_BLOCK + 2 * CONV_HALO, CONV_DIM), F32),
                        pltpu.VMEM((SUBLANES, span, CONV_DIM), F32),
                        pltpu.VMEM((CONV_BLOCK, CONV_DIM), F32),
                        pltpu.VMEM(wt_all.shape[1:], BF16), pltpu.VMEM((MLA_Q_LORA, MLA_W), BF16)],
        compiler_params=_params(("arbitrary",)),
        name="mix_proj",
    )(*args)


def _cache_kv_kernel(ckv_ref, kr_ref, wk_ref, wvt_ref, kcat_ref, vt_ref):
    n = DEC_BATCH * PAST_LEN
    ckvb = ckv_ref[...].reshape(n, MLA_KV_LORA).astype(BF16)
    kn = _dot(ckvb, wk_ref[...])
    vt_ref[...] = _dot_nt(wvt_ref[...], ckvb).astype(BF16)
    kr = kr_ref[...].reshape(n, HEAD_PAD)
    for h in range(MLA_HEADS):
        sl = slice(h * HEAD_PAD, (h + 1) * HEAD_PAD)
        kcat_ref[:, sl] = (kn[:, sl] + kr).astype(BF16)


def _cache_kv(cache_ckv, cache_kr_pad, wk, wvt):
    n = DEC_BATCH * PAST_LEN
    return pl.pallas_call(
        _cache_kv_kernel,
        grid=(DEPTH,),
        in_specs=[pl.BlockSpec((DEC_BATCH, None, PAST_LEN, MLA_KV_LORA), lambda l: (0, l, 0, 0)),
                  pl.BlockSpec((DEC_BATCH, None, PAST_LEN, HEAD_PAD), lambda l: (0, l, 0, 0)),
                  pl.BlockSpec((None, MLA_KV_LORA, MLA_W), lambda l: (l, 0, 0)),
                  pl.BlockSpec((None, MLA_V_W, MLA_KV_LORA), lambda l: (l, 0, 0))],
        out_specs=[pl.BlockSpec((None, n, MLA_W), lambda l: (l, 0, 0)),
                   pl.BlockSpec((None, MLA_V_W, n), lambda l: (l, 0, 0))],
        out_shape=[jax.ShapeDtypeStruct((DEPTH, n, MLA_W), BF16), jax.ShapeDtypeStruct((DEPTH, MLA_V_W, n), BF16)],
        compiler_params=_params(("arbitrary",)),
        name="cache_kv",
    )(cache_ckv, cache_kr_pad, wk, wvt)


def _ret_kernel(*refs, t, hp, latent, carry, layer, seq_grid=False):
    if latent:
        lg_ref, q_ref, k_ref, v_ref, g_ref, s0f_ref, s0b_ref, o_ref, d_ref = refs
    elif carry:
        lg_ref, q_ref, k_ref, v_ref, g_ref, sf_prev_ref, sb_prev_ref, o_ref, sf_ref, sb_ref, d_ref = refs
    else:
        lg_ref, q_ref, k_ref, v_ref, g_ref, o_ref, sf_ref, sb_ref, d_ref = refs
    hblk = 0 if seq_grid else pl.program_id(0)
    first_seq = pl.program_id(0 if seq_grid else 1) == 0

    @pl.when(first_seq)
    def _():
        diff = (lax.broadcasted_iota(jnp.int32, (t, t), 0) - lax.broadcasted_iota(jnp.int32, (t, t), 1)).astype(F32)
        for hh in range(hp):
            lgf = lg_ref[layer, 0, hblk * hp + hh]
            lgb = lg_ref[layer, 1, hblk * hp + hh]
            d_ref[hh] = jnp.exp(jnp.where(diff >= 0, diff * lgf, -diff * lgb))

    if carry:
        sf_ref[0] = sf_prev_ref[...]
        sb_ref[0] = sb_prev_ref[...]
    pos = lax.broadcasted_iota(jnp.int32, (t, 1), 0).astype(F32)
    for hh in range(hp):
        lgf = lg_ref[layer, 0, hblk * hp + hh]
        lgb = lg_ref[layer, 1, hblk * hp + hh]
        q = q_ref[:, hh * RET_DK:(hh + 1) * RET_DK]
        k = k_ref[:, hh * RET_DK:(hh + 1) * RET_DK]
        v = v_ref[:, hh * RET_DV:(hh + 1) * RET_DV]
        p = (_dot_nt(q, k) * d_ref[hh]).astype(BF16)
        o = _dot(p, v)
        if latent:
            o = o + jnp.exp((pos + 1.0) * lgf) * _dot(q, s0f_ref[hh].astype(BF16))
            o = o + jnp.exp((t - pos) * lgb) * _dot(q, s0b_ref[hh].astype(BF16))
        else:
            kf = k.astype(F32)
            sf = _dot_tn((kf * jnp.exp((t - 1.0 - pos) * lgf)).astype(BF16), v)
            sb = _dot_tn((kf * jnp.exp(pos * lgb)).astype(BF16), v)
            if carry:
                sf_ref[1, hh] = sf
                sb_ref[1, hh] = sb
            else:
                sf_ref[hh] = sf
                sb_ref[hh] = sb
        o_ref[:, hh * RET_DV:(hh + 1) * RET_DV] = (g_ref[:, hh * RET_DV:(hh + 1) * RET_DV] * _norm_rows(o)).astype(BF16)


def _attn_kernel(*refs, t, hp, latent):
    if latent:
        q_ref, k_ref, vt_ref, kc_ref, vtc_ref, o_ref = refs
    else:
        q_ref, k_ref, vt_ref, o_ref = refs
    qb = min(ATTN_QBLOCK, t)
    units = [(slice(hh * HEAD_PAD, (hh + 1) * HEAD_PAD), slice(hh * MLA_D_V, (hh + 1) * MLA_D_V),
              slice(b * qb, (b + 1) * qb)) for b in range(t // qb) for hh in range(hp)]
    def scores(unit):
        sl, _, rows = unit
        q = q_ref[rows, sl]
        s = [_dot_nt(k_ref[:, sl], q)]
        if latent:
            s.append(_dot_nt(kc_ref[:, sl], q))
        return s

    def softmax(s):
        m = functools.reduce(jnp.maximum, [jnp.max(x, axis=0, keepdims=True) for x in s])
        e = [jnp.exp2(x - m) for x in s]
        den = functools.reduce(jnp.add, [jnp.sum(x, axis=0, keepdims=True) for x in e])
        return [x.astype(BF16) for x in e], den

    def values(unit, e, den):
        _, vs, _ = unit
        o = _dot(vt_ref[vs, :], e[0])
        if latent:
            o = o + _dot(vtc_ref[vs, :], e[1])
        return o / den

    pairs = [units[u:u + 2] for u in range(0, len(units), 2)]
    s_next = [scores(u) for u in pairs[0]]
    sm_prev = None
    for g in range(len(pairs) + 1):
        s_cur = s_next
        if g + 1 < len(pairs):
            s_next = [scores(u) for u in pairs[g + 1]]
        sm_cur = [softmax(s) for s in s_cur] if g < len(pairs) else None
        if sm_prev is not None:
            (_, vs0, rows), (_, vs1, _) = pairs[g - 1]
            outs = [values(u, e, den) for u, (e, den) in zip(pairs[g - 1], sm_prev)]
            o_ref[rows, vs0.start:vs1.stop] = jnp.concatenate(outs, axis=0).T.astype(BF16)
        sm_prev = sm_cur


def _ctx_mixers_kernel(*refs, n_ret_in, carry, layer):
    ret_in = refs[:n_ret_in]
    q_ref, k_ref, vt_ref = refs[n_ret_in:n_ret_in + 3]
    r_ref, sf_ref, sb_ref, m_ref, d_ref = refs[n_ret_in + 3:]
    _ret_kernel(*ret_in, r_ref, sf_ref, sb_ref, d_ref, t=SEQ, hp=RET_HEADS, latent=False, carry=carry, layer=layer,
                seq_grid=True)
    _attn_kernel(q_ref, k_ref, vt_ref, m_ref, t=SEQ, hp=MLA_HEADS, latent=False)


def _ctx_mixers(log_g, slab, vt, states, *, layer):
    def slab_cols(name):
        lo, w = SLAB_COLS[name]
        assert lo % w == 0
        return pl.BlockSpec((SEQ, w), lambda s: (s, lo // w))

    st = pl.BlockSpec((None, RET_HEADS, RET_DK, RET_DV), lambda s: (s, 0, 0, 0))
    in_specs = [pl.BlockSpec(memory_space=pltpu.SMEM)] + [slab_cols(n) for n in ("rq", "rk", "rv", "rg")]
    args = [log_g, slab, slab, slab, slab]
    if states is None:
        st_out = st
        st_shape = jax.ShapeDtypeStruct((BATCH, RET_HEADS, RET_DK, RET_DV), F32)
    else:
        in_specs += [st, st]
        args += list(states)
        st_out = pl.BlockSpec((None, DEPTH, RET_HEADS, RET_DK, RET_DV), lambda s: (s, 0, 0, 0, 0))
        st_shape = jax.ShapeDtypeStruct((BATCH, DEPTH, RET_HEADS, RET_DK, RET_DV), F32)
    n_ret_in = len(in_specs)
    in_specs += [slab_cols("q"), slab_cols("kcat"), pl.BlockSpec((MLA_V_W, SEQ), lambda s: (0, s))]
    args += [slab, slab, vt]
    r, sf, sb, m = pl.pallas_call(
        functools.partial(_ctx_mixers_kernel, n_ret_in=n_ret_in, carry=states is not None, layer=layer),
        grid=(BATCH,),
        in_specs=in_specs,
        out_specs=[pl.BlockSpec((SEQ, RET_HEADS * RET_DV), lambda s: (s, 0)), st_out, st_out,
                   pl.BlockSpec((SEQ, MLA_V_W), lambda s: (s, 0))],
        out_shape=[jax.ShapeDtypeStruct((N_CTX, RET_HEADS * RET_DV), BF16), st_shape, st_shape,
                   jax.ShapeDtypeStruct((N_CTX, MLA_V_W), BF16)],
        scratch_shapes=[pltpu.VMEM((RET_HEADS, SEQ, SEQ), F32)],
        compiler_params=_params(("arbitrary",)),
        name="ctx_mixers",
    )(*args)
    return r, sf, sb, m


LAT_RET_HP = 2
LAT_ATTN_HP = MLA_HEADS * LAT_RET_HP // RET_HEADS


def _lat_mixers_kernel(*refs, layer):
    ret_in = refs[:7]
    q_ref, k_ref, vt_ref, kc_ref, vtc_ref = refs[7:12]
    r_ref, m_ref, d_ref = refs[12:]
    _ret_kernel(*ret_in, r_ref, d_ref, t=DEC_SEQ, hp=LAT_RET_HP, latent=True, carry=False, layer=layer)
    _attn_kernel(q_ref, k_ref, vt_ref, kc_ref, vtc_ref, m_ref, t=DEC_SEQ, hp=LAT_ATTN_HP, latent=True)


def _lat_mixers(log_g, slab, vt, states, cache, *, layer):
    t = DEC_SEQ
    row0 = N_CTX // t

    def slab_cols(name, w):
        lo = SLAB_COLS[name][0]
        assert lo % w == 0
        return pl.BlockSpec((t, w), lambda h, s: (row0 + s, lo // w + h))

    rk_w, rv_w, qk_w, v_w = LAT_RET_HP * RET_DK, LAT_RET_HP * RET_DV, LAT_ATTN_HP * HEAD_PAD, LAT_ATTN_HP * MLA_D_V
    st = pl.BlockSpec((None, None, LAT_RET_HP, RET_DK, RET_DV), lambda h, s: (s, layer, h, 0, 0))
    in_specs = [pl.BlockSpec(memory_space=pltpu.SMEM), slab_cols("rq", rk_w), slab_cols("rk", rk_w),
                slab_cols("rv", rv_w), slab_cols("rg", rv_w), st, st,
                slab_cols("q", qk_w), slab_cols("kcat", qk_w),
                pl.BlockSpec((v_w, t), lambda h, s: (h, row0 + s)),
                pl.BlockSpec((None, PAST_LEN, qk_w), lambda h, s: (layer, s, h)),
                pl.BlockSpec((None, v_w, PAST_LEN), lambda h, s: (layer, h, s))]
    return pl.pallas_call(
        functools.partial(_lat_mixers_kernel, layer=layer),
        grid=(RET_HEADS // LAT_RET_HP, DEC_BATCH),
        in_specs=in_specs,
        out_specs=[pl.BlockSpec((t, rv_w), lambda h, s: (s, h)), pl.BlockSpec((t, v_w), lambda h, s: (s, h))],
        out_shape=[jax.ShapeDtypeStruct((N_LAT, RET_HEADS * RET_DV), BF16),
                   jax.ShapeDtypeStruct((N_LAT, MLA_V_W), BF16)],
        scratch_shapes=[pltpu.VMEM((LAT_RET_HP, t, t), F32)],
        compiler_params=_params(("arbitrary", "arbitrary")),
        name="lat_mixers",
    )(log_g, slab, slab, slab, slab, *states, slab, slab, vt, *cache)


def _merge_kernel(x_ref, mod_ref, a_ref, rc_ref, rl_ref, mc_ref, ml_ref, sig_ref, wc_ref, wr_ref, wm_ref, wo_ref,
                  g_ref, b_ref, o_ref, wc_bf, wr_bf, wm_bf, wo_bf, *, tm, layer):
    g_ref, b_ref = (r.at[layer, 1:2, :] for r in (g_ref, b_ref))
    i = pl.program_id(0)
    is_ctx = i < N_WSTEPS + N_CTX // tm

    @pl.when(i < N_WSTEPS)
    def _():
        _stage_chunk(i, wc_ref, wc_bf)
        _stage_chunk(i, wr_ref, wr_bf)
        _stage_chunk(i, wo_ref, wo_bf)
        _stage_chunk(i, wm_ref, wm_bf)

    @pl.when(i >= N_WSTEPS)
    def _():
        for half in range(ROW_HALVES):
            rows = slice(half * tm // ROW_HALVES, (half + 1) * tm // ROW_HALVES)
            x = x_ref[rows, :]
            r = jnp.where(is_ctx, rc_ref[rows, :], rl_ref[rows, :])
            m = jnp.where(is_ctx, mc_ref[rows, :], ml_ref[rows, :])
            merged = sig_ref[rows, 0:D_MODEL] * _dot(a_ref[rows, :], wc_bf[...])
            merged = merged + sig_ref[rows, D_MODEL:2 * D_MODEL] * _dot(r, wr_bf[...])
            merged = merged + sig_ref[rows, 2 * D_MODEL:] * _dot(m, wm_bf[...])
            y = _dot(merged.astype(BF16), wo_bf[...])
            z = DEEPNORM_ALPHA * x + mod_ref[5:6, :] * y
            o_ref[rows, :] = _norm_rows(z) * g_ref[...] + b_ref[...]


def _merge(x, mods_all, a, r_pair, m_pair, sig, wc, wr, wm, wo, ln_g, ln_b, *, layer, tm=512):
    def row(w):
        return _row_spec(tm, w, N_WSTEPS)

    def pair(w):
        return [_ctx_row_spec(tm, w, N_WSTEPS), _lat_row_spec(tm, w, N_WSTEPS)]

    return pl.pallas_call(
        functools.partial(_merge_kernel, tm=tm, layer=layer),
        grid=(N_WSTEPS + N_TOK // tm,),
        in_specs=[row(D_MODEL), _mod_spec(tm, layer, N_WSTEPS), row(CONV_DIM)] + pair(RET_HEADS * RET_DV)
        + pair(MLA_V_W) + [row(GATE_W), _wchunk_spec(wc, layer), _wchunk_spec(wr, layer), _wchunk_spec(wm, layer),
                         _wchunk_spec(wo, layer), _resident(ln_g.shape), _resident(ln_b.shape)],
        out_specs=row(D_MODEL),
        out_shape=jax.ShapeDtypeStruct((N_TOK, D_MODEL), F32),
        scratch_shapes=[pltpu.VMEM((CONV_DIM, D_MODEL), BF16), pltpu.VMEM((RET_HEADS * RET_DV, D_MODEL), BF16),
                        pltpu.VMEM((MLA_V_W, D_MODEL), BF16), pltpu.VMEM((D_MODEL, D_MODEL), BF16)],
        compiler_params=_params(("arbitrary",)),
        name="merge",
    )(x, mods_all, a, *r_pair, *m_pair, sig, wc, wr, wm, wo, ln_g, ln_b)


def _rope_tables():
    f32 = np.float32
    rows = DEC_SEQ // GRID_W
    row_id = np.repeat(np.arange(rows, dtype=f32), GRID_W)
    col_id = np.tile(np.arange(GRID_W, dtype=f32), rows)
    inv_freq = (f32(ROPE_BASE) ** (-np.arange(ROPE_AXIS_HALF, dtype=f32) / f32(ROPE_AXIS_HALF))).astype(f32)
    ang = np.stack([row_id[:, None] * inv_freq, col_id[:, None] * inv_freq], axis=1).astype(f32)
    cos = np.cos(ang).astype(f32)
    sin = np.sin(ang).astype(f32)
    cos32 = np.stack([cos, cos], axis=2).reshape(DEC_SEQ, MLA_D_ROPE)
    zero = np.zeros_like(sin)
    sin_lo32 = np.stack([-sin, zero], axis=2).reshape(DEC_SEQ, MLA_D_ROPE)
    sin_hi32 = np.stack([zero, sin], axis=2).reshape(DEC_SEQ, MLA_D_ROPE)
    tail = HEAD_PAD - ROPE_LANE0 - MLA_D_ROPE
    cos_t = np.concatenate([np.ones((DEC_SEQ, ROPE_LANE0), f32), cos32, np.ones((DEC_SEQ, tail), f32)], axis=1)
    cos_t = np.concatenate([np.ones((DEC_SEQ, HEAD_PAD), f32), cos_t], axis=0)

    def sin_table(s32):
        return np.pad(s32, ((DEC_SEQ, 0), (ROPE_LANE0, tail)))

    return tuple(jnp.asarray(t, dtype=F32) for t in (cos_t, sin_table(sin_lo32), sin_table(sin_hi32)))


def _head_pad_cols(w, width):
    k = w.shape[0]
    w = w.reshape(k, MLA_HEADS, width)
    return jnp.pad(w, ((0, 0), (0, 0), (0, HEAD_PAD - width))).reshape(k, MLA_W)


def kernel(x_prompt, x_sample, cache_mla_ckv, cache_mla_krope, state_ret_fwd, state_ret_bwd, c, c_ctx, ada_w, ada_b, ffn1_w_in, ffn1_w_out, ffn2_w_in, ffn2_w_out, post_ln_g, post_ln_b, mix_w_in, conv_w_dw, conv_b_dw, conv_ln_g, conv_ln_b, conv_w_out, ret_decay_fwd, ret_decay_bwd, ret_w_out, mla_q_norm, mla_w_uq, mla_kv_norm, mla_w_ukv, mla_w_out, mix_w_o):
    assert DEPTH == 2
    cvec = jnp.concatenate([c_ctx[None, :], c, jnp.zeros((N_MOD_ROWS - 1 - DEC_BATCH, D_MODEL), F32)], axis=0)
    mods_all = _ada_mods(cvec, ada_w, ada_b).reshape(DEPTH, N_MOD_ROWS, N_MODS, D_MODEL)
    rope_tabs = _rope_tables()

    w_ukv = mla_w_ukv.reshape(DEPTH, MLA_KV_LORA, MLA_HEADS, MLA_D_NOPE + MLA_D_V)
    wk_all = _head_pad_cols(w_ukv[..., :MLA_D_NOPE].reshape(DEPTH * MLA_KV_LORA, -1), MLA_D_NOPE)
    wk_all = wk_all.reshape(DEPTH, MLA_KV_LORA, MLA_W).astype(BF16)
    wvt_all = jnp.swapaxes(w_ukv[..., MLA_D_NOPE:].reshape(DEPTH, MLA_KV_LORA, MLA_V_W), 1, 2).astype(BF16)
    kr_tail = HEAD_PAD - ROPE_LANE0 - MLA_D_ROPE
    cache_kr_pad = jnp.pad(cache_mla_krope, ((0, 0), (0, 0), (0, 0), (ROPE_LANE0, kr_tail)))
    kcat_c, vt_c = _cache_kv(cache_mla_ckv, cache_kr_pad, wk_all, wvt_all)
    log_g = jnp.stack([jax.nn.log_sigmoid(ret_decay_fwd), jax.nn.log_sigmoid(ret_decay_bwd)], axis=1)
    wt_all = jnp.swapaxes(mix_w_in, 1, 2)
    conv_w = (conv_w_dw, conv_b_dw, conv_ln_g, conv_ln_b)
    ln_g, ln_b = post_ln_g, post_ln_b
    xs = (x_prompt.reshape(N_CTX, D_MODEL), x_sample.reshape(N_LAT, D_MODEL))
    ctx_carry = None
    state_carry = None
    for l in range(DEPTH):
        last = l == DEPTH - 1
        x = _ffn(xs, mods_all, ffn1_w_in, ffn1_w_out, ln_g, ln_b, layer=l, which=0)[0]

        a, slab, vt, ckv, kr = _proj(x, mods_all, rope_tabs, wt_all, mla_w_uq, wk_all, wvt_all, mla_q_norm, mla_kv_norm,
                                     conv_w, ctx_carry, layer=l)
        ctx_carry = (ckv, kr)

        r_ctx, sf, sb, m_ctx = _ctx_mixers(log_g, slab, vt, state_carry, layer=l)
        state_carry = (sf, sb)
        r_lat, m_lat = _lat_mixers(log_g, slab, vt, (state_ret_fwd, state_ret_bwd), (kcat_c, vt_c), layer=l)

        x = _merge(x, mods_all, a, (r_ctx, r_lat), (m_ctx, m_lat), slab, conv_w_out, ret_w_out, mla_w_out, mix_w_o,
                   ln_g, ln_b, layer=l)
        xs = _ffn((x,), mods_all, ffn2_w_in, ffn2_w_out, ln_g, ln_b, layer=l, which=2, split_out=last)

    y_ctx, y_lat = xs
    return (y_ctx.reshape(BATCH, SEQ, D_MODEL), y_lat.reshape(DEC_BATCH, DEC_SEQ, D_MODEL),
            ctx_carry[0], ctx_carry[1], state_carry[0], state_carry[1])
```
